```python
import math
import jax
import jax.numpy as jnp
from jax import lax
import numpy as np

D_MODEL = 1024
BATCH = 4
SEQ = 4096
DEPTH = 4

N_A_LAYERS = DEPTH // 2
N_B_LAYERS = DEPTH - N_A_LAYERS
MEM_LEN = 256
MEM_HEADS = 4
MEM_HEAD_DIM = 64
MEM_W = MEM_HEADS * MEM_HEAD_DIM
MIX_W = D_MODEL
MAIN_W = MIX_W - MEM_W
GLA_HEADS = 4
GLA_DV = MAIN_W // GLA_HEADS
GLA_DK = GLA_DV // 2
GLA_RANK = 16
GLA_GATE_NORM = 16.0
GLA_CHUNK = 16
NSA_HEADS = 12
NSA_GROUPS = 4
NSA_HEAD_DIM = MAIN_W // NSA_HEADS
NSA_REP = NSA_HEADS // NSA_GROUPS
CMP_BLOCK = 32
CMP_STRIDE = 16
CMP_HIDDEN = 128
SEL_BLOCK = 64
SEL_TOPK = 16
WINDOW = 512
Q_BLOCK = 64
REL_BUCKETS = 32
REL_MAX_DIST = 128
FFN_DIM = 2816
CONV_WIDTH = 3
EPS = 1e-6
GLA_SIZES = (GLA_HEADS * GLA_DK, GLA_HEADS * GLA_DK, GLA_HEADS * GLA_DV, GLA_HEADS * GLA_DV, GLA_RANK, MEM_W)
GLA_IN = sum(GLA_SIZES)
NSA_SIZES = (NSA_HEADS * NSA_HEAD_DIM, NSA_HEADS * 3, MEM_W)
NSA_IN = sum(NSA_SIZES)
SHARED_KV_W = 6 * NSA_GROUPS * NSA_HEAD_DIM

kernel_name = 'hybrid_gla_nsa_yoco_trunk'


def rmsnorm(x, g):
    xf = x.astype(jnp.float32)
    y = xf * lax.rsqrt(jnp.mean(xf * xf, axis=-1, keepdims=True) + EPS)
    return (y * g.astype(jnp.float32)).astype(x.dtype)


def split_cols(x, sizes):
    out = []
    start = 0
    for n in sizes:
        out.append(x[..., start:start + n])
        start += n
    return out


def rel_bucket(dist):
    dist = jnp.maximum(dist, 0)
    max_exact = REL_BUCKETS // 2
    large = max_exact + (jnp.log(jnp.maximum(dist, 1).astype(jnp.float32) / max_exact)
                         / math.log(REL_MAX_DIST / max_exact) * (REL_BUCKETS - max_exact)).astype(jnp.int32)
    large = jnp.minimum(large, REL_BUCKETS - 1)
    return jnp.where(dist < max_exact, dist, large)


def masked_softmax(s, mask):
    s = jnp.where(mask, s.astype(jnp.float32), -1e30)
    m = jnp.max(s, axis=-1, keepdims=True)
    p = jnp.where(mask, jnp.exp(s - m), 0.0)
    return p / jnp.maximum(jnp.sum(p, axis=-1, keepdims=True), 1e-30)


def gla_mixer(q, k, v, log_a, g_out, out_norm):
    dtype = v.dtype
    B, S = q.shape[0], q.shape[1]
    C = GLA_CHUNK
    N = S // C

    def chunked(t):
        return t.reshape(B, N, C, GLA_HEADS, t.shape[-1]).transpose(0, 3, 1, 2, 4).astype(jnp.float32)

    qc = chunked(q) * GLA_DK ** -0.5
    kc = chunked(k)
    vc = chunked(v)
    b = lax.cumsum(chunked(log_a), axis=3)
    causal = jnp.tril(jnp.ones((C, C), dtype=bool))
    diff = b[..., :, None, :] - b[..., None, :, :]
    decay = jnp.exp(jnp.where(causal[:, :, None], diff, -jnp.inf))
    attn = jnp.einsum('bhnid,bhnjd,bhnijd->bhnij', qc, kc, decay)
    o_intra = jnp.einsum('bhnij,bhnjv->bhniv', attn, vc)
    b_last = b[..., -1:, :]
    q_in = qc * jnp.exp(b)
    k_out = kc * jnp.exp(b_last - b)
    a_last = jnp.exp(b_last[..., 0, :])

    def step(state, inp):
        qi, ki, vi, ai = inp
        o = jnp.einsum('bhcd,bhdv->bhcv', qi, state)
        state = ai[..., None] * state + jnp.einsum('bhcd,bhcv->bhdv', ki, vi)
        return state, o

    xs = (jnp.moveaxis(q_in, 2, 0), jnp.moveaxis(k_out, 2, 0), jnp.moveaxis(vc, 2, 0), jnp.moveaxis(a_last, 2, 0))
    state0 = jnp.zeros((B, GLA_HEADS, GLA_DK, GLA_DV), jnp.float32)
    _, o_inter = lax.scan(step, state0, xs)
    o = o_intra + jnp.moveaxis(o_inter, 0, 2)
    o = o.transpose(0, 2, 3, 1, 4).reshape(B, S, GLA_HEADS, GLA_DV)
    o = rmsnorm(o, out_norm) * jax.nn.silu(g_out.astype(jnp.float32))
    return o.reshape(B, S, GLA_HEADS * GLA_DV).astype(dtype)


def mem_attention(q, mem_kv):
    B, S = q.shape[0], q.shape[1]
    M = mem_kv.shape[1]
    k, v = split_cols(mem_kv, (MEM_W, MEM_W))
    qh = q.reshape(B, S, MEM_HEADS, MEM_HEAD_DIM)
    kh = k.reshape(B, M, MEM_HEADS, MEM_HEAD_DIM)
    vh = v.reshape(B, M, MEM_HEADS, MEM_HEAD_DIM)
    s = jnp.einsum('bshd,bmhd->bhsm', qh, kh).astype(jnp.float32) * MEM_HEAD_DIM ** -0.5
    p = jax.nn.softmax(s, axis=-1)
    o = jnp.einsum('bhsm,bmhd->bshd', p, vh.astype(jnp.float32))
    return o.reshape(B, S, MEM_W).astype(q.dtype)


def shared_kv(h, kv_norm, w_kv, cmp_pos, cmp_w1, cmp_b1, cmp_w2, cmp_b2):
    B, S = h.shape[0], h.shape[1]
    G, Dh = NSA_GROUPS, NSA_HEAD_DIM
    kv = (rmsnorm(h, kv_norm) @ w_kv).reshape(B, S, 6, G, Dh).transpose(2, 0, 3, 1, 4)
    NC = (S - CMP_BLOCK) // CMP_STRIDE + 1
    win = jnp.arange(NC)[:, None] * CMP_STRIDE + jnp.arange(CMP_BLOCK)[None, :]

    def compress(t, j):
        blocks = t[:, :, win] + cmp_pos[j]
        flat = blocks.reshape(B, G, NC, CMP_BLOCK * Dh)
        return jax.nn.silu(flat @ cmp_w1[j] + cmp_b1[j]) @ cmp_w2[j] + cmp_b2[j]

    kc = compress(kv[0], 0)
    vc = compress(kv[1], 1)
    return (kc, vc, kv[2], kv[3], kv[4], kv[5])


def nsa_mixer(q, gates, shared, rel_bias):
    kc, vc, ks, vs, kw, vw = shared
    B, S = q.shape[0], q.shape[1]
    G, R, Dh = NSA_GROUPS, NSA_REP, NSA_HEAD_DIM
    NC = kc.shape[2]
    NSB = S // SEL_BLOCK
    n_sel = min(SEL_TOPK, NSB)
    qg = (q * Dh ** -0.5).reshape(B, S, G, R, Dh).transpose(0, 2, 3, 1, 4)
    gg = gates.reshape(B, S, G, R, 3).transpose(0, 2, 3, 1, 4)
    kb = ks.reshape(B, G, NSB, SEL_BLOCK, Dh)
    vb = vs.reshape(B, G, NSB, SEL_BLOCK, Dh)
    kw_p = jnp.pad(kw, ((0, 0), (0, 0), (WINDOW, 0), (0, 0)))
    vw_p = jnp.pad(vw, ((0, 0), (0, 0), (WINDOW, 0), (0, 0)))
    cmp_start = jnp.arange(NC) * CMP_STRIDE
    cmp_end = cmp_start + CMP_BLOCK - 1
    sel_idx = jnp.arange(NSB)
    sel_start = sel_idx * SEL_BLOCK
    overlap = ((cmp_start[:, None] < sel_start[None, :] + SEL_BLOCK)
               & (cmp_end[:, None] >= sel_start[None, :])).astype(jnp.float32)
    tbl = rel_bias.astype(jnp.float32).reshape(REL_BUCKETS, G, R)
    g_idx = jnp.arange(G)[None, :, None, None]

    def common_bias(dist):
        return tbl[rel_bucket(dist)].transpose(2, 3, 0, 1)

    def group_bias(dist):
        return jnp.moveaxis(tbl[rel_bucket(dist), g_idx], -1, 2)

    def block(c):
        t0 = c * Q_BLOCK
        t = t0 + jnp.arange(Q_BLOCK)
        qb = lax.dynamic_slice_in_dim(qg, t0, Q_BLOCK, axis=3)
        gb = lax.dynamic_slice_in_dim(gg, t0, Q_BLOCK, axis=3)
        d_c = t[:, None] - cmp_end[None, :]
        s_c = jnp.einsum('bgrqd,bgkd->bgrqk', qb, kc).astype(jnp.float32) + common_bias(d_c)
        p_c = masked_softmax(s_c, d_c >= 0)
        o_c = jnp.einsum('bgrqk,bgkd->bgrqd', p_c, vc.astype(jnp.float32))
        imp = jnp.einsum('bgrqk,kj->bgqj', p_c, overlap)
        cur = (t // SEL_BLOCK)[:, None]
        jj = sel_idx[None, :]
        forced = (jj == 0) | (jj == cur) | (jj == cur - 1)
        score = jnp.where(forced, 1e4, jnp.where(jj <= cur, imp, -1.0))
        _, idx = lax.top_k(score, n_sel)
        gather = jax.vmap(jax.vmap(lambda blk, ix: blk[ix]))
        k_sel = gather(kb, idx).reshape(B, G, Q_BLOCK, n_sel * SEL_BLOCK, Dh)
        v_sel = gather(vb, idx).reshape(B, G, Q_BLOCK, n_sel * SEL_BLOCK, Dh)
        pos = (idx[..., None] * SEL_BLOCK + jnp.arange(SEL_BLOCK)).reshape(B, G, Q_BLOCK, n_sel * SEL_BLOCK)
        d_s = t[None, None, :, None] - pos
        s_s = jnp.einsum('bgrqd,bgqkd->bgrqk', qb, k_sel).astype(jnp.float32) + group_bias(d_s)
        p_s = masked_softmax(s_s, (d_s >= 0)[:, :, None])
        o_s = jnp.einsum('bgrqk,bgqkd->bgrqd', p_s, v_sel.astype(jnp.float32))
        kwb = lax.dynamic_slice_in_dim(kw_p, t0, WINDOW + Q_BLOCK, axis=2)
        vwb = lax.dynamic_slice_in_dim(vw_p, t0, WINDOW + Q_BLOCK, axis=2)
        s_pos = t0 - WINDOW + jnp.arange(WINDOW + Q_BLOCK)
        d_w = t[:, None] - s_pos[None, :]
        mask_w = (d_w >= 0) & (d_w < WINDOW) & (s_pos[None, :] >= 0)
        s_w = jnp.einsum('bgrqd,bgkd->bgrqk', qb, kwb).astype(jnp.float32) + common_bias(d_w)
        p_w = masked_softmax(s_w, mask_w)
        o_w = jnp.einsum('bgrqk,bgkd->bgrqd', p_w, vwb.astype(jnp.float32))
        o = gb[..., 0:1] * o_c + gb[..., 1:2] * o_s + gb[..., 2:3] * o_w
        return o.astype(q.dtype)

    out = lax.map(block, jnp.arange(S // Q_BLOCK))
    return out.transpose(1, 0, 4, 2, 3, 5).reshape(B, S, NSA_HEADS * Dh)


def conv_ffn(x, w_up, conv_w, conv_b, w_down):
    S = x.shape[1]
    u = x @ w_up
    u_pad = jnp.pad(u, ((0, 0), (CONV_WIDTH - 1, 0), (0, 0)))
    hc = conv_b
    for j in range(CONV_WIDTH):
        hc = hc + conv_w[j] * u_pad[:, j:j + S]
    a, b = split_cols(hc, (FFN_DIM, FFN_DIM))
    return (jax.nn.silu(a) * b) @ w_down


def setup_inputs(seed: int = 0) -> dict:
    key = jax.random.key(seed)
    ks = jax.random.split(key, 26)
    f32 = jnp.float32

    def nrm(k, shape, scale):
        return jax.random.normal(k, shape, f32) * scale

    def gain(k, shape):
        return 1.0 + 0.02 * jax.random.normal(k, shape, f32)

    res = (2.0 * DEPTH) ** -0.5
    return {
        'x': nrm(ks[0], (BATCH, SEQ, D_MODEL), 1.0),
        'mem': nrm(ks[1], (BATCH, MEM_LEN, D_MODEL), 1.0),
        'norm_mix': gain(ks[2], (DEPTH, D_MODEL)),
        'norm_mem': gain(ks[3], (DEPTH, D_MODEL)),
        'w_mem_kv': nrm(ks[4], (DEPTH, D_MODEL, 2 * MEM_W), D_MODEL ** -0.5),
        'w_out': nrm(ks[5], (DEPTH, MIX_W, D_MODEL), MIX_W ** -0.5 * res),
        'norm_ffn': gain(ks[6], (DEPTH, D_MODEL)),
        'w_up': nrm(ks[7], (DEPTH, D_MODEL, 2 * FFN_DIM), D_MODEL ** -0.5),
        'conv_w': nrm(ks[8], (DEPTH, CONV_WIDTH, 2 * FFN_DIM), CONV_WIDTH ** -0.5),
        'conv_b': nrm(ks[9], (DEPTH, 2 * FFN_DIM), 0.01),
        'w_down': nrm(ks[10], (DEPTH, FFN_DIM, D_MODEL), FFN_DIM ** -0.5 * res),
        'gla_w_in': nrm(ks[11], (N_A_LAYERS, D_MODEL, GLA_IN), D_MODEL ** -0.5),
        'gla_w_gate_up': nrm(ks[12], (N_A_LAYERS, GLA_RANK, GLA_HEADS * GLA_DK), GLA_RANK ** -0.5),
        'gla_b_gate': nrm(ks[13], (N_A_LAYERS, GLA_HEADS * GLA_DK), 0.01),
        'gla_out_norm': gain(ks[14], (N_A_LAYERS, GLA_DV)),
        'nsa_w_in': nrm(ks[15], (N_B_LAYERS, D_MODEL, NSA_IN), D_MODEL ** -0.5),
        'kv_norm': gain(ks[16], (D_MODEL,)),
        'w_kv_shared': nrm(ks[17], (D_MODEL, SHARED_KV_W), D_MODEL ** -0.5),
        'cmp_pos': nrm(ks[18], (2, CMP_BLOCK, NSA_HEAD_DIM), 0.1),
        'cmp_w1': nrm(ks[19], (2, CMP_BLOCK * NSA_HEAD_DIM, CMP_HIDDEN), (CMP_BLOCK * NSA_HEAD_DIM) ** -0.5),
        'cmp_b1': nrm(ks[20], (2, CMP_HIDDEN), 0.01),
        'cmp_w2': nrm(ks[21], (2, CMP_HIDDEN, NSA_HEAD_DIM), CMP_HIDDEN ** -0.5),
        'cmp_b2': nrm(ks[22], (2, NSA_HEAD_DIM), 0.01),
        'rel_bias': nrm(ks[23], (REL_BUCKETS, NSA_HEADS), 0.5),
        'final_norm': gain(ks[24], (D_MODEL,)),
    }


def reference(x, mem, norm_mix, norm_mem, w_mem_kv, w_out, norm_ffn, w_up, conv_w, conv_b, w_down,
              gla_w_in, gla_w_gate_up, gla_b_gate, gla_out_norm, nsa_w_in, kv_norm, w_kv_shared,
              cmp_pos, cmp_w1, cmp_b1, cmp_w2, cmp_b2, rel_bias, final_norm):
    B, S = x.shape[0], x.shape[1]
    h = x
    shared = None
    for i in range(DEPTH):
        if i == N_A_LAYERS:
            shared = shared_kv(h, kv_norm, w_kv_shared, cmp_pos, cmp_w1, cmp_b1, cmp_w2, cmp_b2)
        xn = rmsnorm(h, norm_mix[i])
        mem_kv = rmsnorm(mem, norm_mem[i]) @ w_mem_kv[i]
        if i < N_A_LAYERS:
            q, k, v, g, lr, mq = split_cols(xn @ gla_w_in[i], GLA_SIZES)
            log_a = jax.nn.log_sigmoid((lr @ gla_w_gate_up[i] + gla_b_gate[i]).astype(jnp.float32)) / GLA_GATE_NORM
            main = gla_mixer(q.reshape(B, S, GLA_HEADS, GLA_DK), k.reshape(B, S, GLA_HEADS, GLA_DK),
                             v.reshape(B, S, GLA_HEADS, GLA_DV), log_a.reshape(B, S, GLA_HEADS, GLA_DK),
                             g.reshape(B, S, GLA_HEADS, GLA_DV), gla_out_norm[i])
        else:
            q, gl, mq = split_cols(xn @ nsa_w_in[i - N_A_LAYERS], NSA_SIZES)
            gates = jax.nn.sigmoid(gl.astype(jnp.float32)).reshape(B, S, NSA_HEADS, 3)
            main = nsa_mixer(q.reshape(B, S, NSA_HEADS, NSA_HEAD_DIM), gates, shared, rel_bias)
        mo = mem_attention(mq, mem_kv)
        h = h + jnp.concatenate([main.astype(xn.dtype), mo.astype(xn.dtype)], axis=-1) @ w_out[i]
        h = h + conv_ffn(rmsnorm(h, norm_ffn[i]), w_up[i], conv_w[i], conv_b[i], w_down[i])
    return rmsnorm(h, final_norm)
```

```python
import functools
import math

import numpy as np
import jax
import jax.numpy as jnp
from jax import lax
from jax.experimental import pallas as pl
from jax.experimental.pallas import tpu as pltpu

F32 = jnp.float32
BF16 = jnp.bfloat16

D_MODEL = 1024
DEPTH = 4
N_A_LAYERS = DEPTH // 2
MEM_HEADS = 4
MEM_HEAD_DIM = 64
MEM_W = MEM_HEADS * MEM_HEAD_DIM
MAIN_W = D_MODEL - MEM_W
GLA_HEADS = 4
GLA_DV = MAIN_W // GLA_HEADS
GLA_DK = GLA_DV // 2
GLA_RANK = 16
GLA_GATE_NORM = 16.0
NSA_HEADS = 12
NSA_GROUPS = 4
NSA_HEAD_DIM = MAIN_W // NSA_HEADS
NSA_REP = NSA_HEADS // NSA_GROUPS
CMP_BLOCK = 32
CMP_STRIDE = 16
CMP_HIDDEN = 128
SEL_BLOCK = 64
SEL_TOPK = 16
WINDOW = 512
REL_BUCKETS = 32
REL_MAX_DIST = 128
FFN_DIM = 2816
CONV_WIDTH = 3
EPS = 1e-6

LANE = 128
SUBLANE = 8
VMEM_LIMIT = 56 * 1024 * 1024
GLA_DKP = LANE
GLA_DVP = 2 * LANE
MEM_DP = LANE
NSA_DP = LANE
NEG = -1e30
TQ = 128
ROW_TILE = 512
GLA_CHUNK = 64
FFN_TILE = 256

NT = (((1,), (1,)), ((), ()))
TN = (((0,), (0,)), ((), ()))


def _cparams(*sem):
    return pltpu.CompilerParams(dimension_semantics=sem, vmem_limit_bytes=VMEM_LIMIT)


def _rms(x, g):
    return x * lax.rsqrt(jnp.mean(x * x, axis=-1, keepdims=True) + EPS) * g


def _sigmoid(x):
    return 1.0 / (1.0 + jnp.exp(-x))


def _dot(a, b):
    return jnp.dot(a, b, preferred_element_type=F32)


def _pad_heads(w, nh, d, dp):
    lead = w.shape[:-1]
    w = w.reshape(lead + (nh, d))
    w = jnp.pad(w, [(0, 0)] * len(lead) + [(0, 0), (0, dp - d)])
    return w.reshape(lead + (nh * dp,))


def _pad_head_rows(w, nh, d, dp):
    n = w.shape[-1]
    w = jnp.pad(w.reshape(nh, d, n), ((0, 0), (0, dp - d), (0, 0)))
    return w.reshape(nh * dp, n)


GLA_QW = GLA_HEADS * GLA_DKP
GLA_VW = GLA_HEADS * GLA_DVP
MEM_QW = MEM_HEADS * MEM_DP
GLA_OFF_K = GLA_QW
GLA_OFF_V = 2 * GLA_QW
GLA_OFF_G = GLA_OFF_V + GLA_VW
GLA_OFF_LR = GLA_OFF_G + GLA_VW
GLA_OFF_MQ = GLA_OFF_LR + LANE
GLA_NP = GLA_OFF_MQ + MEM_QW


def _gla_in_body(h_ref, g_ref, w_ref, wg_ref, bg_ref, q_ref, k_ref, v_ref, go_ref, la_ref, mq_ref):
    xn = _rms(h_ref[...], g_ref[...]).astype(BF16)

    def proj(lo, n):
        return _dot(xn, w_ref[:, lo:lo + n])

    q_ref[...] = proj(0, GLA_QW).astype(BF16)
    k_ref[...] = proj(GLA_OFF_K, GLA_QW).astype(BF16)
    for j in range(GLA_VW // GLA_QW):
        v_ref[:, j * GLA_QW:(j + 1) * GLA_QW] = proj(GLA_OFF_V + j * GLA_QW, GLA_QW).astype(BF16)
        go_ref[:, j * GLA_QW:(j + 1) * GLA_QW] = proj(GLA_OFF_G + j * GLA_QW, GLA_QW).astype(BF16)
    lr = proj(GLA_OFF_LR, LANE).astype(BF16)
    z = _dot(lr, wg_ref[...]) + bg_ref[...]
    la_ref[...] = (jnp.minimum(z, 0.0) - jnp.log(1.0 + jnp.exp(-jnp.abs(z)))) * (1.0 / GLA_GATE_NORM)
    mq_ref[...] = proj(GLA_OFF_MQ, MEM_QW).astype(BF16)


def _gla_in_proj(h, g, w, wg, bg):
    t = h.shape[0]
    tm = ROW_TILE
    row = lambda n: pl.BlockSpec((tm, n), lambda i: (i, 0))
    full = lambda a: pl.BlockSpec(a.shape, lambda i: (0,) * a.ndim)
    return pl.pallas_call(
        _gla_in_body,
        grid=(t // tm,),
        in_specs=[row(D_MODEL), full(g), full(w), full(wg), full(bg)],
        out_specs=[row(GLA_QW), row(GLA_QW), row(GLA_VW), row(GLA_VW), row(GLA_QW), row(MEM_QW)],
        out_shape=[jax.ShapeDtypeStruct((t, GLA_QW), BF16), jax.ShapeDtypeStruct((t, GLA_QW), BF16),
                   jax.ShapeDtypeStruct((t, GLA_VW), BF16), jax.ShapeDtypeStruct((t, GLA_VW), BF16),
                   jax.ShapeDtypeStruct((t, GLA_QW), F32), jax.ShapeDtypeStruct((t, MEM_QW), BF16)],
        compiler_params=_cparams("parallel"),
        name="gla_in_proj",
    )(h, g, w, wg, bg)


def _gla_mix_body(q_ref, k_ref, la_ref, v_ref, go_ref, on_ref, o_ref, st_ref, *, cn):
    @pl.when(pl.program_id(1) == 0)
    def _():
        st_ref[...] = jnp.zeros_like(st_ref)

    row = lax.broadcasted_iota(jnp.int32, (cn, cn), 0)
    col = lax.broadcasted_iota(jnp.int32, (cn, cn), 1)
    causal = row >= col
    tril = jnp.where(causal, 1.0, 0.0).astype(BF16)
    scale = GLA_DK ** -0.5
    for hd in range(GLA_HEADS):
        sk = slice(hd * GLA_DKP, (hd + 1) * GLA_DKP)
        sv = slice(hd * GLA_DVP, (hd + 1) * GLA_DVP)
        la = la_ref[:, sk]
        la1 = la.astype(BF16)
        r1 = la - la1.astype(F32)
        la2 = r1.astype(BF16)
        la3 = (r1 - la2.astype(F32)).astype(BF16)
        b = _dot(tril, la1) + _dot(tril, la2) + _dot(tril, la3)
        b_last = b[cn - 1:cn, :]
        q = q_ref[:, sk].astype(F32)
        k = k_ref[:, sk].astype(F32)
        qs = (q * jnp.exp(b) * scale).astype(BF16)
        ks = (k * jnp.exp(-b)).astype(BF16)
        ko = (k * jnp.exp(b_last - b)).astype(BF16)
        v = v_ref[:, sv]
        a = lax.dot_general(qs, ks, NT, preferred_element_type=F32)
        a = jnp.where(causal, a, 0.0).astype(BF16)
        st = st_ref[hd]
        o = _dot(a, v) + lax.dot_general(qs, st.astype(BF16), NT, preferred_element_type=F32)
        st_ref[hd] = st * jnp.exp(b_last) + lax.dot_general(v, ko, TN, preferred_element_type=F32)
        ms = jnp.sum(o * o, axis=-1, keepdims=True) * (1.0 / GLA_DV)
        y = o * lax.rsqrt(ms + EPS) * on_ref[...]
        g = go_ref[:, sv].astype(F32)
        o_ref[:, sv] = (y * (g * _sigmoid(g))).astype(BF16)


def _gla_mixer(q, k, la, v, go, on, batch):
    t = q.shape[0]
    cn = GLA_CHUNK
    nc = t // batch // cn
    row = lambda n: pl.BlockSpec((cn, n), lambda b, c: (b * nc + c, 0))
    return pl.pallas_call(
        functools.partial(_gla_mix_body, cn=cn),
        grid=(batch, nc),
        in_specs=[row(GLA_QW), row(GLA_QW), row(GLA_QW), row(GLA_VW), row(GLA_VW),
                  pl.BlockSpec(on.shape, lambda b, c: (0, 0))],
        out_specs=row(GLA_VW),
        out_shape=jax.ShapeDtypeStruct((t, GLA_VW), BF16),
        scratch_shapes=[pltpu.VMEM((GLA_HEADS, GLA_DVP, GLA_DKP), F32)],
        compiler_params=_cparams("arbitrary", "arbitrary"),
        name="gla_mixer",
    )(q, k, la, v, go, on)


def _mem_kv_body(m_ref, g_ref, w_ref, o_ref):
    xn = _rms(m_ref[...], g_ref[...]).astype(BF16)
    o_ref[...] = _dot(xn, w_ref[...]).astype(BF16)


def _mem_kv_proj(mem2, g, w, batch):
    nl = w.shape[0]
    m = mem2.shape[0] // batch
    n = w.shape[2]
    return pl.pallas_call(
        _mem_kv_body,
        grid=(nl, batch),
        in_specs=[pl.BlockSpec((m, D_MODEL), lambda l, b: (b, 0)),
                  pl.BlockSpec((None, 1, D_MODEL), lambda l, b: (l, 0, 0)),
                  pl.BlockSpec((None, D_MODEL, n), lambda l, b: (l, 0, 0))],
        out_specs=pl.BlockSpec((None, m, n), lambda l, b: (l, b, 0)),
        out_shape=jax.ShapeDtypeStruct((nl, mem2.shape[0], n), BF16),
        compiler_params=_cparams("arbitrary", "arbitrary"),
        name="mem_kv_proj",
    )(mem2, g, w)


def _mem_attn_body(q_ref, kv_ref, o_ref):
    for hd in range(MEM_HEADS):
        sl = slice(hd * MEM_DP, (hd + 1) * MEM_DP)
        sv = slice(MEM_QW + hd * MEM_DP, MEM_QW + (hd + 1) * MEM_DP)
        s = lax.dot_general(q_ref[:, sl], kv_ref[:, sl], NT, preferred_element_type=F32) * MEM_HEAD_DIM ** -0.5
        p = jnp.exp(s - jnp.max(s, axis=-1, keepdims=True))
        l = jnp.sum(p, axis=-1, keepdims=True)
        o_ref[:, sl] = (_dot(p.astype(BF16), kv_ref[:, sv]) / l).astype(BF16)


def _mem_attn(mq, mem_kv, batch):
    t = mq.shape[0]
    tm = ROW_TILE
    nt = t // batch // tm
    m = mem_kv.shape[0] // batch
    return pl.pallas_call(
        _mem_attn_body,
        grid=(batch, nt),
        in_specs=[pl.BlockSpec((tm, MEM_QW), lambda b, i: (b * nt + i, 0)),
                  pl.BlockSpec((m, 2 * MEM_QW), lambda b, i: (b, 0))],
        out_specs=pl.BlockSpec((tm, MEM_QW), lambda b, i: (b * nt + i, 0)),
        out_shape=jax.ShapeDtypeStruct((t, MEM_QW), BF16),
        compiler_params=_cparams("parallel", "parallel"),
        name="mem_attn",
    )(mq, mem_kv)


def _out_proj_body(h_ref, a_ref, m_ref, wa_ref, wm_ref, o_ref):
    o_ref[...] = h_ref[...] + _dot(a_ref[...], wa_ref[...]) + _dot(m_ref[...], wm_ref[...])


def _out_proj(h, main, mo, wa, wm):
    t = h.shape[0]
    tm = ROW_TILE
    row = lambda n: pl.BlockSpec((tm, n), lambda i: (i, 0))
    full = lambda a: pl.BlockSpec(a.shape, lambda i: (0,) * a.ndim)
    return pl.pallas_call(
        _out_proj_body,
        grid=(t // tm,),
        in_specs=[row(D_MODEL), row(main.shape[1]), row(mo.shape[1]), full(wa), full(wm)],
        out_specs=row(D_MODEL),
        out_shape=jax.ShapeDtypeStruct((t, D_MODEL), F32),
        compiler_params=_cparams("parallel"),
        name="out_proj",
    )(h, main, mo, wa, wm)


def _ffn_body(h_ref, hp_ref, g_ref, wup_ref, cw_ref, cb_ref, wdn_ref, o_ref, acc_ref, *, tm, tf):
    h = h_ref[...]
    g = g_ref[...]
    keep = jnp.where(pl.program_id(1) > 0, 1.0, 0.0)
    xe = jnp.concatenate([_rms(hp_ref[...], g) * keep, _rms(h, g)], axis=0).astype(BF16)
    acc_ref[...] = jnp.zeros_like(acc_ref)
    for j in range(FFN_DIM // tf):
        def conv(off):
            u = _dot(xe, wup_ref[:, off:off + tf])
            w = cw_ref[:, off:off + tf]
            out = cb_ref[:, off:off + tf]
            for c in range(CONV_WIDTH):
                lo = SUBLANE - (CONV_WIDTH - 1) + c
                out = out + w[c:c + 1, :] * u[lo:lo + tm, :]
            return out
        a = conv(j * tf)
        b = conv(FFN_DIM + j * tf)
        act = (a * _sigmoid(a) * b).astype(BF16)
        acc_ref[...] += _dot(act, wdn_ref[j * tf:(j + 1) * tf, :])
    o_ref[...] = h + acc_ref[...]


def _conv_ffn(h, g, wup, cw, cb, wdn, batch):
    t = h.shape[0]
    tm = ROW_TILE
    nt = t // batch // tm
    hb = tm // SUBLANE
    full = lambda a: pl.BlockSpec(a.shape, lambda b, i: (0,) * a.ndim)
    return pl.pallas_call(
        functools.partial(_ffn_body, tm=tm, tf=FFN_TILE),
        grid=(batch, nt),
        in_specs=[pl.BlockSpec((tm, D_MODEL), lambda b, i: (b * nt + i, 0)),
                  pl.BlockSpec((SUBLANE, D_MODEL), lambda b, i: (jnp.maximum((b * nt + i) * hb - 1, 0), 0)),
                  full(g), full(wup), full(cw), full(cb), full(wdn)],
        out_specs=pl.BlockSpec((tm, D_MODEL), lambda b, i: (b * nt + i, 0)),
        out_shape=jax.ShapeDtypeStruct((t, D_MODEL), F32),
        scratch_shapes=[pltpu.VMEM((tm, D_MODEL), F32)],
        compiler_params=_cparams("parallel", "parallel"),
        name="conv_ffn",
    )(h, h, g, wup, cw, cb, wdn)


def _final_norm_body(h_ref, g_ref, o_ref):
    o_ref[...] = _rms(h_ref[...], g_ref[...])


def _final_norm(h, g):
    t = h.shape[0]
    tm = ROW_TILE
    return pl.pallas_call(
        _final_norm_body,
        grid=(t // tm,),
        in_specs=[pl.BlockSpec((tm, D_MODEL), lambda i: (i, 0)), pl.BlockSpec(g.shape, lambda i: (0, 0))],
        out_specs=pl.BlockSpec((tm, D_MODEL), lambda i: (i, 0)),
        out_shape=jax.ShapeDtypeStruct((t, D_MODEL), F32),
        compiler_params=_cparams("parallel"),
        name="final_norm",
    )(h, g)


KV_NAT = NSA_GROUPS * NSA_HEAD_DIM


def _kv_proj_body(h_ref, g_ref, w_ref, ck_ref, cv_ref, ks_ref, kw_ref, vsw_ref, *, tm):
    xn = _rms(h_ref[...], g_ref[...]).astype(BF16)
    ck_ref[...] = _dot(xn, w_ref[:, 0:KV_NAT]).astype(BF16)
    cv_ref[...] = _dot(xn, w_ref[:, KV_NAT:2 * KV_NAT]).astype(BF16)
    key = pl.program_id(1) * tm + lax.broadcasted_iota(jnp.int32, (SEL_BLOCK, tm), 1)
    blk = lax.broadcasted_iota(jnp.int32, (SEL_BLOCK, tm), 0)
    e = jnp.where(lax.shift_right_logical(key, int(math.log2(SEL_BLOCK))) == blk, 1.0, 0.0).astype(BF16)
    for gi in range(NSA_GROUPS):
        kk = _dot(xn, w_ref[:, 2 * KV_NAT + gi * LANE:2 * KV_NAT + (gi + 1) * LANE])
        kt = kk.T
        ks_ref[gi, 0:NSA_HEAD_DIM, :] = kt[0:NSA_HEAD_DIM].astype(BF16)
        ks_ref[gi, NSA_HEAD_DIM:, :] = e
        kw_ref[gi] = kt[NSA_HEAD_DIM:].astype(BF16)
        voff = 2 * KV_NAT + NSA_GROUPS * LANE
        vv = _dot(xn, w_ref[:, voff + gi * LANE:voff + (gi + 1) * LANE])
        vsw_ref[gi] = vv.astype(BF16)


def _kv_proj(h, g, w, batch):
    t = h.shape[0]
    s = t // batch
    tm = ROW_TILE
    nt = s // tm
    gr = NSA_GROUPS
    return pl.pallas_call(
        functools.partial(_kv_proj_body, tm=tm),
        grid=(batch, nt),
        in_specs=[pl.BlockSpec((tm, D_MODEL), lambda b, i: (b * nt + i, 0)),
                  pl.BlockSpec(g.shape, lambda b, i: (0, 0)),
                  pl.BlockSpec(w.shape, lambda b, i: (0, 0))],
        out_specs=[pl.BlockSpec((tm, KV_NAT), lambda b, i: (b * nt + i, 0)),
                   pl.BlockSpec((tm, KV_NAT), lambda b, i: (b * nt + i, 0)),
                   pl.BlockSpec((None, gr, 2 * NSA_HEAD_DIM, tm), lambda b, i: (b, 0, 0, i)),
                   pl.BlockSpec((None, gr, NSA_HEAD_DIM, tm), lambda b, i: (b, 0, 0, i)),
                   pl.BlockSpec((None, gr, tm, LANE), lambda b, i: (b, 0, i, 0))],
        out_shape=[jax.ShapeDtypeStruct((t, KV_NAT), BF16), jax.ShapeDtypeStruct((t, KV_NAT), BF16),
                   jax.ShapeDtypeStruct((batch, gr, 2 * NSA_HEAD_DIM, s), BF16),
                   jax.ShapeDtypeStruct((batch, gr, NSA_HEAD_DIM, s), BF16),
                   jax.ShapeDtypeStruct((batch, gr, s, LANE), BF16)],
        compiler_params=_cparams("parallel", "parallel"),
        name="nsa_kv_proj",
    )(h, g, w)


def _compress_body(x_ref, wt_ref, wb_ref, pos_ref, w1_ref, b1_ref, w2_ref, b2_ref, on_ref, ot_ref, *, ncp):
    x = x_ref[...]
    top = _dot(x, wt_ref[...])
    bot = _dot(x, wb_ref[...])
    posb = _dot(pos_ref[...], w1_ref[...])[0:1, :] + b1_ref[...]
    for gi in range(NSA_GROUPS):
        sl = slice(gi * CMP_HIDDEN, (gi + 1) * CMP_HIDDEN)
        hid = top[:, sl] + pltpu.roll(bot[:, sl], ncp - 1, 0) + posb
        hid = (hid * _sigmoid(hid)).astype(BF16)
        out = _dot(hid, w2_ref[...]) + b2_ref[...]
        on_ref[gi] = out.astype(BF16)
        ot_ref[gi] = out.T.astype(BF16)


def _compress(x16, wt, wb, pos, w1, b1, w2, b2, batch):
    ncp = x16.shape[1] // batch
    gr = NSA_GROUPS
    per_j = lambda a: pl.BlockSpec((None,) + a.shape[1:], lambda j, b: (j,) + (0,) * (a.ndim - 1))
    return pl.pallas_call(
        functools.partial(_compress_body, ncp=ncp),
        grid=(2, batch),
        in_specs=[pl.BlockSpec((None, ncp, x16.shape[2]), lambda j, b: (j, b, 0)),
                  per_j(wt), per_j(wb), per_j(pos), per_j(w1), per_j(b1), per_j(w2), per_j(b2)],
        out_specs=[pl.BlockSpec((None, None, gr, ncp, LANE), lambda j, b: (j, b, 0, 0, 0)),
                   pl.BlockSpec((None, None, gr, LANE, ncp), lambda j, b: (j, b, 0, 0, 0))],
        out_shape=[jax.ShapeDtypeStruct((2, batch, gr, ncp, LANE), BF16),
                   jax.ShapeDtypeStruct((2, batch, gr, LANE, ncp), BF16)],
        compiler_params=_cparams("arbitrary", "arbitrary"),
        name="nsa_compress",
    )(x16, wt, wb, pos, w1, b1, w2, b2)


NSA_QW = NSA_HEADS * NSA_DP


def _nsa_in_body(h_ref, g_ref, w_ref, q_ref, gt_ref, mq_ref):
    xn = _rms(h_ref[...], g_ref[...]).astype(BF16)
    for j in range(NSA_QW // 512):
        q_ref[:, j * 512:(j + 1) * 512] = (
            _dot(xn, w_ref[:, j * 512:(j + 1) * 512]) * NSA_HEAD_DIM ** -0.5).astype(BF16)
    gt_ref[...] = _sigmoid(_dot(xn, w_ref[:, NSA_QW:NSA_QW + LANE]))
    mq_ref[...] = _dot(xn, w_ref[:, NSA_QW + LANE:]).astype(BF16)


def _nsa_in_proj(h, g, w):
    t = h.shape[0]
    tm = ROW_TILE
    row = lambda n: pl.BlockSpec((tm, n), lambda i: (i, 0))
    full = lambda a: pl.BlockSpec(a.shape, lambda i: (0,) * a.ndim)
    return pl.pallas_call(
        _nsa_in_body,
        grid=(t // tm,),
        in_specs=[row(D_MODEL), full(g), full(w)],
        out_specs=[row(NSA_QW), row(LANE), row(MEM_QW)],
        out_shape=[jax.ShapeDtypeStruct((t, NSA_QW), BF16), jax.ShapeDtypeStruct((t, LANE), F32),
                   jax.ShapeDtypeStruct((t, MEM_QW), BF16)],
        compiler_params=_cparams("parallel"),
        name="nsa_in_proj",
    )(h, g, w)


def _nsa_attn_body(q_ref, gt_ref, kct_ref, vc_ref, ks_ref, kw_ref, vsw_ref, ov_ref, tz_ref, cb_ref,
                   o_ref, sc_ref, *, nsb, ncp, n_sel):
    n = pl.program_id(1)
    t0 = n * TQ
    tq = t0 + lax.broadcasted_iota(jnp.int32, (TQ, ncp), 0)
    kc = lax.broadcasted_iota(jnp.int32, (TQ, ncp), 1)
    cmask = tq >= kc * CMP_STRIDE + (CMP_BLOCK - 1)
    cshift = lax.rem(n * (TQ // CMP_STRIDE), ncp)
    jj = lax.broadcasted_iota(jnp.int32, (nsb, TQ), 0)
    cur = lax.shift_right_logical(t0 + lax.broadcasted_iota(jnp.int32, (nsb, TQ), 1), int(math.log2(SEL_BLOCK)))
    forced = (jj == 0) | (jj == cur) | (jj == cur - 1)
    lane64 = lax.broadcasted_iota(jnp.int32, (TQ, LANE), 1) >= NSA_HEAD_DIM

    for gi in range(NSA_GROUPS):
        heads = [gi * NSA_REP + r for r in range(NSA_REP)]
        qs = [q_ref[:, h * NSA_DP:(h + 1) * NSA_DP] for h in heads]

        oc = []
        psum = jnp.zeros((TQ, ncp), F32)
        for r, h in enumerate(heads):
            s = _dot(qs[r], kct_ref[gi]) + pltpu.roll(cb_ref[h], cshift, 1)
            s = jnp.where(cmask, s, NEG)
            p = jnp.where(cmask, jnp.exp(s - jnp.max(s, axis=-1, keepdims=True)), 0.0)
            p = p / jnp.maximum(jnp.sum(p, axis=-1, keepdims=True), 1e-30)
            psum = psum + p
            oc.append(_dot(p.astype(BF16), vc_ref[gi]))
        p1 = psum.astype(BF16)
        p2 = (psum - p1.astype(F32)).astype(BF16)
        imp = _dot(p1, ov_ref[...]) + _dot(p2, ov_ref[...])
        imp_t = imp.T[0:nsb, :]
        score = jnp.where(forced, 1e4, jnp.where(jj <= cur, imp_t, -1.0))
        sc_ref[...] = score

        def rank_step(i, cnt):
            rowv = sc_ref[pl.ds(i, 1), :]
            before = (rowv > score) | ((rowv == score) & (i < jj))
            return cnt + jnp.where(before, 1.0, 0.0)
        cnt = lax.fori_loop(0, nsb, rank_step, jnp.zeros((nsb, TQ), F32))
        selneg_t = jnp.where(cnt < n_sel, 0.0, NEG)
        if nsb < SEL_BLOCK:
            selneg_t = jnp.concatenate([selneg_t, jnp.zeros((SEL_BLOCK - nsb, TQ), F32)], axis=0)
        selneg = jnp.concatenate([jnp.zeros((TQ, NSA_HEAD_DIM), F32), selneg_t.T], axis=1).astype(BF16)
        qa = [jnp.where(lane64, selneg, qs[r]) for r in range(NSA_REP)]

        def sel_step(kt, carry):
            koff = pl.multiple_of(kt * TQ, TQ)
            ka = ks_ref[gi, :, pl.ds(koff, TQ)]
            vv = vsw_ref[gi, pl.ds(koff, TQ), :]
            ti = jnp.minimum(n - kt, 2)
            out = []
            for r, h in enumerate(heads):
                m, l, acc = carry[r]
                s = _dot(qa[r], ka) + tz_ref[h, ti]
                m2 = jnp.maximum(m, jnp.max(s, axis=-1, keepdims=True))
                al = jnp.exp(m - m2)
                p = jnp.exp(s - m2)
                l2 = al * l + jnp.sum(p, axis=-1, keepdims=True)
                acc2 = al * acc + _dot(p.astype(BF16), vv)
                out.append((m2, l2, acc2))
            return tuple(out)

        init = tuple((jnp.full((TQ, 1), NEG, F32), jnp.zeros((TQ, 1), F32), jnp.zeros((TQ, LANE), F32))
                     for _ in range(NSA_REP))
        sel = lax.fori_loop(0, n + 1, sel_step, init)

        nwt = WINDOW // TQ

        def win_step(kt, carry):
            koff = pl.multiple_of(kt * TQ, TQ)
            kk = kw_ref[gi, :, pl.ds(koff, TQ)]
            vv = vsw_ref[gi, pl.ds(koff, TQ), :]
            d = n - kt
            ti = jnp.where(d == nwt, 3, jnp.minimum(d, 2))
            out = []
            for r, h in enumerate(heads):
                m, l, acc = carry[r]
                s = _dot(qs[r][:, 0:NSA_HEAD_DIM], kk) + tz_ref[h, ti]
                m2 = jnp.maximum(m, jnp.max(s, axis=-1, keepdims=True))
                al = jnp.exp(m - m2)
                p = jnp.exp(s - m2)
                l2 = al * l + jnp.sum(p, axis=-1, keepdims=True)
                acc2 = al * acc + _dot(p.astype(BF16), vv)
                out.append((m2, l2, acc2))
            return tuple(out)

        win = lax.fori_loop(jnp.maximum(n - nwt, 0), n + 1, win_step, init)

        for r, h in enumerate(heads):
            gcol = lambda c: gt_ref[:, h * 3 + c:h * 3 + c + 1]
            o_s = sel[r][2] / sel[r][1]
            o_w = pltpu.roll(win[r][2] / win[r][1], NSA_HEAD_DIM, 1)
            o = gcol(0) * oc[r] + gcol(1) * o_s + gcol(2) * o_w
            o_ref[:, h * NSA_DP:(h + 1) * NSA_DP] = o.astype(BF16)


def _nsa_attn(q, gates, kct, vc, ksel, kwin, vsw, ov, tz, cb, batch):
    t = q.shape[0]
    s = t // batch
    nt = s // TQ
    nsb = s // SEL_BLOCK
    ncp = kct.shape[-1]
    gr = NSA_GROUPS
    per_b = lambda a: pl.BlockSpec((None,) + a.shape[1:], lambda b, i: (b,) + (0,) * (a.ndim - 1))
    full = lambda a: pl.BlockSpec(a.shape, lambda b, i: (0,) * a.ndim)
    return pl.pallas_call(
        functools.partial(_nsa_attn_body, nsb=nsb, ncp=ncp, n_sel=min(SEL_TOPK, nsb)),
        grid=(batch, nt),
        in_specs=[pl.BlockSpec((TQ, NSA_QW), lambda b, i: (b * nt + i, 0)),
                  pl.BlockSpec((TQ, LANE), lambda b, i: (b * nt + i, 0)),
                  per_b(kct), per_b(vc), per_b(ksel), per_b(kwin), per_b(vsw),
                  full(ov), full(tz), full(cb)],
        out_specs=pl.BlockSpec((TQ, NSA_QW), lambda b, i: (b * nt + i, 0)),
        out_shape=jax.ShapeDtypeStruct((t, NSA_QW), BF16),
        scratch_shapes=[pltpu.VMEM((nsb, TQ), F32)],
        compiler_params=_cparams("parallel", "arbitrary"),
        name="nsa_attn",
    )(q, gates, kct, vc, ksel, kwin, vsw, ov, tz, cb)


def _rel_bucket_np(dist):
    dist = np.maximum(dist, 0)
    max_exact = REL_BUCKETS // 2
    ratio = np.log(np.maximum(dist, 1).astype(np.float32) / np.float32(max_exact)) / np.float32(
        math.log(REL_MAX_DIST / max_exact))
    large = max_exact + (ratio * np.float32(REL_BUCKETS - max_exact)).astype(np.int32)
    large = np.minimum(large, REL_BUCKETS - 1)
    return np.where(dist < max_exact, dist, large).astype(np.int32)


def _bias_tables(rel_bias, ncp):
    q = np.arange(TQ)[:, None]
    k = np.arange(TQ)[None, :]
    tbl = rel_bias.astype(F32)
    far = jnp.broadcast_to(tbl[REL_BUCKETS - 1][:, None, None], (NSA_HEADS, TQ, TQ))
    t0 = jnp.where(jnp.asarray(k <= q)[None], jnp.moveaxis(tbl[_rel_bucket_np(q - k)], -1, 0), NEG)
    t1 = jnp.moveaxis(tbl[_rel_bucket_np(TQ + q - k)], -1, 0)
    t3 = jnp.where(jnp.asarray(k > q)[None], far, NEG)
    tz = jnp.stack([t0, t1, far, t3], axis=1)
    kk = np.arange(ncp)[None, :]
    m = np.where(kk < ncp // 2, -kk, ncp - kk)
    d = CMP_STRIDE * m + q - (CMP_BLOCK - 1)
    idx = np.where((d >= 0) & (d < REL_MAX_DIST), _rel_bucket_np(d), REL_BUCKETS - 1)
    cb = jnp.moveaxis(tbl[idx], -1, 0)
    return tz, cb


def _overlap_table(s, ncp):
    nsb = s // SEL_BLOCK
    nc = (s - CMP_BLOCK) // CMP_STRIDE + 1
    cs = np.arange(ncp) * CMP_STRIDE
    ce = cs + CMP_BLOCK - 1
    ss = np.arange(LANE) * SEL_BLOCK
    ov = (cs[:, None] < ss[None, :] + SEL_BLOCK) & (ce[:, None] >= ss[None, :])
    ov &= (np.arange(ncp) < nc)[:, None] & (np.arange(LANE) < nsb)[None, :]
    return jnp.asarray(ov, dtype=BF16)


def kernel(x, mem, norm_mix, norm_mem, w_mem_kv, w_out, norm_ffn, w_up, conv_w, conv_b, w_down,
           gla_w_in, gla_w_gate_up, gla_b_gate, gla_out_norm, nsa_w_in, kv_norm, w_kv_shared,
           cmp_pos, cmp_w1, cmp_b1, cmp_w2, cmp_b2, rel_bias, final_norm):
    batch, seq = x.shape[0], x.shape[1]
    t = batch * seq
    h = x.reshape(t, D_MODEL)
    row = lambda v: v.reshape(1, -1).astype(F32)

    wk, wv = w_mem_kv[..., :MEM_W], w_mem_kv[..., MEM_W:]
    w_mkv = jnp.concatenate([_pad_heads(wk, MEM_HEADS, MEM_HEAD_DIM, MEM_DP),
                             _pad_heads(wv, MEM_HEADS, MEM_HEAD_DIM, MEM_DP)], axis=-1).astype(BF16)
    mem_kv_all = _mem_kv_proj(mem.reshape(-1, D_MODEL), norm_mem.reshape(DEPTH, 1, D_MODEL), w_mkv, batch)

    shared = None
    for i in range(DEPTH):
        w_o = w_out[i]
        w_o_mem = _pad_head_rows(w_o[MAIN_W:], MEM_HEADS, MEM_HEAD_DIM, MEM_DP).astype(BF16)
        if i < N_A_LAYERS:
            wi = gla_w_in[i]
            c0 = GLA_HEADS * GLA_DK
            c1 = 2 * c0
            c2 = c1 + GLA_HEADS * GLA_DV
            c3 = c2 + GLA_HEADS * GLA_DV
            c4 = c3 + GLA_RANK
            w_all = jnp.concatenate([
                _pad_heads(wi[:, :c0], GLA_HEADS, GLA_DK, GLA_DKP),
                _pad_heads(wi[:, c0:c1], GLA_HEADS, GLA_DK, GLA_DKP),
                _pad_heads(wi[:, c1:c2], GLA_HEADS, GLA_DV, GLA_DVP),
                _pad_heads(wi[:, c2:c3], GLA_HEADS, GLA_DV, GLA_DVP),
                _pad_heads(wi[:, c3:c4], 1, GLA_RANK, LANE),
                _pad_heads(wi[:, c4:], MEM_HEADS, MEM_HEAD_DIM, MEM_DP)], axis=1).astype(BF16)
            wg = jnp.pad(_pad_heads(gla_w_gate_up[i], GLA_HEADS, GLA_DK, GLA_DKP),
                         ((0, LANE - GLA_RANK), (0, 0))).astype(BF16)
            bg = row(_pad_heads(gla_b_gate[i], GLA_HEADS, GLA_DK, GLA_DKP))
            q, k, v, go, la, mq = _gla_in_proj(h, row(norm_mix[i]), w_all, wg, bg)
            on = row(jnp.pad(gla_out_norm[i], (0, GLA_DVP - GLA_DV)))
            main = _gla_mixer(q, k, la, v, go, on, batch)
            w_o_main = _pad_head_rows(w_o[:MAIN_W], GLA_HEADS, GLA_DV, GLA_DVP).astype(BF16)
        else:
            if shared is None:
                ncp = seq // CMP_STRIDE
                gr, dh = NSA_GROUPS, NSA_HEAD_DIM
                wkv = w_kv_shared.reshape(D_MODEL, 6, gr, dh)
                pair = lambda a, b: jnp.concatenate([wkv[:, a], wkv[:, b]], axis=-1).reshape(D_MODEL, gr * 2 * dh)
                w_kv = jnp.concatenate([wkv[:, 0].reshape(D_MODEL, KV_NAT), wkv[:, 1].reshape(D_MODEL, KV_NAT),
                                        pair(2, 4), pair(3, 5)], axis=1).astype(BF16)
                ck, cv, ksel, kwin, vsw = _kv_proj(h, row(kv_norm), w_kv, batch)
                x16 = jnp.stack([ck.reshape(batch * ncp, CMP_STRIDE * KV_NAT),
                                 cv.reshape(batch * ncp, CMP_STRIDE * KV_NAT)])
                w1 = cmp_w1.reshape(2, 2, CMP_STRIDE, dh, CMP_HIDDEN)
                eye = jnp.eye(gr, dtype=F32)
                w1x = jnp.einsum('jhldc,gk->jhlgdkc', w1, eye).reshape(2, 2, CMP_STRIDE * KV_NAT, gr * CMP_HIDDEN)
                w1x = w1x.astype(BF16)
                pos8 = jnp.broadcast_to(cmp_pos.reshape(2, 1, CMP_BLOCK * dh), (2, SUBLANE, CMP_BLOCK * dh)).astype(BF16)
                w2p = jnp.pad(cmp_w2, ((0, 0), (0, 0), (0, LANE - dh))).astype(BF16)
                b2p = jnp.pad(cmp_b2, ((0, 0), (0, LANE - dh))).reshape(2, 1, LANE).astype(F32)
                cnat, ctr = _compress(x16, w1x[:, 0], w1x[:, 1], pos8, cmp_w1.astype(BF16),
                                      cmp_b1.reshape(2, 1, CMP_HIDDEN).astype(F32), w2p, b2p, batch)
                tz, cb = _bias_tables(rel_bias, ncp)
                ov = _overlap_table(seq, ncp)
                shared = (ctr[0], cnat[1], ksel, kwin, vsw, ov, tz, cb)
            wi = nsa_w_in[i - N_A_LAYERS]
            c0 = NSA_HEADS * NSA_HEAD_DIM
            c1 = c0 + NSA_HEADS * 3
            w_all = jnp.concatenate([
                _pad_heads(wi[:, :c0], NSA_HEADS, NSA_HEAD_DIM, NSA_DP),
                _pad_heads(wi[:, c0:c1], 1, NSA_HEADS * 3, LANE),
                _pad_heads(wi[:, c1:], MEM_HEADS, MEM_HEAD_DIM, MEM_DP)], axis=1).astype(BF16)
            q, gates, mq = _nsa_in_proj(h, row(norm_mix[i]), w_all)
            main = _nsa_attn(q, gates, *shared, batch)
            w_o_main = _pad_head_rows(w_o[:MAIN_W], NSA_HEADS, NSA_HEAD_DIM, NSA_DP).astype(BF16)
        mo = _mem_attn(mq, mem_kv_all[i], batch)
        h = _out_proj(h, main, mo, w_o_main, w_o_mem)
        h = _conv_ffn(h, row(norm_ffn[i]), w_up[i].astype(BF16), conv_w[i].astype(F32), row(conv_b[i]),
                      w_down[i].astype(BF16), batch)
    return _final_norm(h, row(final_norm)).reshape(batch, seq, D_MODEL)
```

```python
import functools
import math

import numpy as np
import jax
import jax.numpy as jnp
from jax import lax
from jax.experimental import pallas as pl
from jax.experimental.pallas import tpu as pltpu

F32 = jnp.float32
BF16 = jnp.bfloat16

D_MODEL = 1024
DEPTH = 4
N_A_LAYERS = DEPTH // 2
MEM_HEADS = 4
MEM_HEAD_DIM = 64
MEM_W = MEM_HEADS * MEM_HEAD_DIM
MAIN_W = D_MODEL - MEM_W
GLA_HEADS = 4
GLA_DV = MAIN_W // GLA_HEADS
GLA_DK = GLA_DV // 2
GLA_RANK = 16
GLA_GATE_NORM = 16.0
NSA_HEADS = 12
NSA_GROUPS = 4
NSA_HEAD_DIM = MAIN_W // NSA_HEADS
NSA_REP = NSA_HEADS // NSA_GROUPS
CMP_BLOCK = 32
CMP_STRIDE = 16
CMP_HIDDEN = 128
SEL_BLOCK = 64
SEL_TOPK = 16
WINDOW = 512
REL_BUCKETS = 32
REL_MAX_DIST = 128
FFN_DIM = 2816
CONV_WIDTH = 3
EPS = 1e-6

LANE = 128
SUBLANE = 8
VMEM_LIMIT = 56 * 1024 * 1024
GLA_DKP = LANE
GLA_DVP = 2 * LANE
MEM_DP = LANE
NEG = -1e30
TQ = 128
ROW_TILE = 512
GLA_CHUNK = 64
FFN_TILE = 256

NT = (((1,), (1,)), ((), ()))
TN = (((0,), (0,)), ((), ()))


def _cparams(*sem):
    return pltpu.CompilerParams(dimension_semantics=sem, vmem_limit_bytes=VMEM_LIMIT)


def _rms(x, g):
    return x * lax.rsqrt(jnp.mean(x * x, axis=-1, keepdims=True) + EPS) * g


def _sigmoid(x):
    return 1.0 / (1.0 + jnp.exp(-x))


def _dot(a, b):
    return jnp.dot(a, b, preferred_element_type=F32)


def _pad_heads(w, nh, d, dp):
    lead = w.shape[:-1]
    w = w.reshape(lead + (nh, d))
    w = jnp.pad(w, [(0, 0)] * len(lead) + [(0, 0), (0, dp - d)])
    return w.reshape(lead + (nh * dp,))


def _pad_head_rows(w, nh, d, dp):
    n = w.shape[-1]
    w = jnp.pad(w.reshape(nh, d, n), ((0, 0), (0, dp - d), (0, 0)))
    return w.reshape(nh * dp, n)


GLA_QW = GLA_HEADS * GLA_DKP
GLA_VW = GLA_HEADS * GLA_DVP
MEM_QW = MEM_HEADS * MEM_DP
GLA_OFF_K = GLA_QW
GLA_OFF_V = 2 * GLA_QW
GLA_OFF_G = GLA_OFF_V + GLA_VW
GLA_OFF_LR = GLA_OFF_G + GLA_VW
GLA_OFF_MQ = GLA_OFF_LR + LANE
GLA_NP = GLA_OFF_MQ + MEM_QW


def _gla_in_body(h_ref, g_ref, w_ref, wg_ref, bg_ref, q_ref, k_ref, v_ref, go_ref, la_ref, mq_ref):
    xn = _rms(h_ref[...], g_ref[...]).astype(BF16)

    def proj(lo, n):
        return _dot(xn, w_ref[:, lo:lo + n])

    q_ref[...] = proj(0, GLA_QW).astype(BF16)
    k_ref[...] = proj(GLA_OFF_K, GLA_QW).astype(BF16)
    for j in range(GLA_VW // GLA_QW):
        v_ref[:, j * GLA_QW:(j + 1) * GLA_QW] = proj(GLA_OFF_V + j * GLA_QW, GLA_QW).astype(BF16)
        go_ref[:, j * GLA_QW:(j + 1) * GLA_QW] = proj(GLA_OFF_G + j * GLA_QW, GLA_QW).astype(BF16)
    lr = proj(GLA_OFF_LR, LANE).astype(BF16)
    z = _dot(lr, wg_ref[...]) + bg_ref[...]
    la_ref[...] = (jnp.minimum(z, 0.0) - jnp.log(1.0 + jnp.exp(-jnp.abs(z)))) * (1.0 / GLA_GATE_NORM)
    mq_ref[...] = proj(GLA_OFF_MQ, MEM_QW).astype(BF16)


def _gla_in_proj(h, g, w, wg, bg):
    t = h.shape[0]
    tm = ROW_TILE
    row = lambda n: pl.BlockSpec((tm, n), lambda i: (i, 0))
    full = lambda a: pl.BlockSpec(a.shape, lambda i: (0,) * a.ndim)
    return pl.pallas_call(
        _gla_in_body,
        grid=(t // tm,),
        in_specs=[row(D_MODEL), full(g), full(w), full(wg), full(bg)],
        out_specs=[row(GLA_QW), row(GLA_QW), row(GLA_VW), row(GLA_VW), row(GLA_QW), row(MEM_QW)],
        out_shape=[jax.ShapeDtypeStruct((t, GLA_QW), BF16), jax.ShapeDtypeStruct((t, GLA_QW), BF16),
                   jax.ShapeDtypeStruct((t, GLA_VW), BF16), jax.ShapeDtypeStruct((t, GLA_VW), BF16),
                   jax.ShapeDtypeStruct((t, GLA_QW), F32), jax.ShapeDtypeStruct((t, MEM_QW), BF16)],
        compiler_params=_cparams("parallel"),
        name="gla_in_proj",
    )(h, g, w, wg, bg)


def _gla_mix_body(q_ref, k_ref, la_ref, v_ref, go_ref, on_ref, o_ref, st_ref, *, cn):
    @pl.when(pl.program_id(1) == 0)
    def _():
        st_ref[...] = jnp.zeros_like(st_ref)

    row = lax.broadcasted_iota(jnp.int32, (cn, cn), 0)
    col = lax.broadcasted_iota(jnp.int32, (cn, cn), 1)
    causal = row >= col
    tril = jnp.where(causal, 1.0, 0.0).astype(BF16)
    scale = GLA_DK ** -0.5
    for hd in range(GLA_HEADS):
        sk = slice(hd * GLA_DKP, (hd + 1) * GLA_DKP)
        sv = slice(hd * GLA_DVP, (hd + 1) * GLA_DVP)
        la = la_ref[:, sk]
        la1 = la.astype(BF16)
        r1 = la - la1.astype(F32)
        la2 = r1.astype(BF16)
        la3 = (r1 - la2.astype(F32)).astype(BF16)
        b = _dot(tril, la1) + _dot(tril, la2) + _dot(tril, la3)
        b_last = b[cn - 1:cn, :]
        q = q_ref[:, sk].astype(F32)
        k = k_ref[:, sk].astype(F32)
        qs = (q * jnp.exp(b) * scale).astype(BF16)
        ks = (k * jnp.exp(-b)).astype(BF16)
        ko = (k * jnp.exp(b_last - b)).astype(BF16)
        v = v_ref[:, sv]
        a = lax.dot_general(qs, ks, NT, preferred_element_type=F32)
        a = jnp.where(causal, a, 0.0).astype(BF16)
        st = st_ref[hd]
        o = _dot(a, v) + lax.dot_general(qs, st.astype(BF16), NT, preferred_element_type=F32)
        st_ref[hd] = st * jnp.exp(b_last) + lax.dot_general(v, ko, TN, preferred_element_type=F32)
        ms = jnp.sum(o * o, axis=-1, keepdims=True) * (1.0 / GLA_DV)
        y = o * lax.rsqrt(ms + EPS) * on_ref[...]
        g = go_ref[:, sv].astype(F32)
        o_ref[:, sv] = (y * (g * _sigmoid(g))).astype(BF16)


def _gla_mixer(q, k, la, v, go, on, batch):
    t = q.shape[0]
    cn = GLA_CHUNK
    nc = t // batch // cn
    row = lambda n: pl.BlockSpec((cn, n), lambda b, c: (b * nc + c, 0))
    return pl.pallas_call(
        functools.partial(_gla_mix_body, cn=cn),
        grid=(batch, nc),
        in_specs=[row(GLA_QW), row(GLA_QW), row(GLA_QW), row(GLA_VW), row(GLA_VW),
                  pl.BlockSpec(on.shape, lambda b, c: (0, 0))],
        out_specs=row(GLA_VW),
        out_shape=jax.ShapeDtypeStruct((t, GLA_VW), BF16),
        scratch_shapes=[pltpu.VMEM((GLA_HEADS, GLA_DVP, GLA_DKP), F32)],
        compiler_params=_cparams("arbitrary", "arbitrary"),
        name="gla_mixer",
    )(q, k, la, v, go, on)


def _mem_kv_body(m_ref, g_ref, w_ref, o_ref):
    xn = _rms(m_ref[...], g_ref[...]).astype(BF16)
    o_ref[...] = _dot(xn, w_ref[...]).astype(BF16)


def _mem_kv_proj(mem2, g, w, batch):
    nl = w.shape[0]
    m = mem2.shape[0] // batch
    n = w.shape[2]
    return pl.pallas_call(
        _mem_kv_body,
        grid=(nl, batch),
        in_specs=[pl.BlockSpec((m, D_MODEL), lambda l, b: (b, 0)),
                  pl.BlockSpec((None, 1, D_MODEL), lambda l, b: (l, 0, 0)),
                  pl.BlockSpec((None, D_MODEL, n), lambda l, b: (l, 0, 0))],
        out_specs=pl.BlockSpec((None, m, n), lambda l, b: (l, b, 0)),
        out_shape=jax.ShapeDtypeStruct((nl, mem2.shape[0], n), BF16),
        compiler_params=_cparams("arbitrary", "arbitrary"),
        name="mem_kv_proj",
    )(mem2, g, w)


def _mem_attn_body(q_ref, kv_ref, o_ref):
    for hd in range(MEM_HEADS):
        sl = slice(hd * MEM_DP, (hd + 1) * MEM_DP)
        sv = slice(MEM_QW + hd * MEM_DP, MEM_QW + (hd + 1) * MEM_DP)
        s = lax.dot_general(q_ref[:, sl], kv_ref[:, sl], NT, preferred_element_type=F32) * MEM_HEAD_DIM ** -0.5
        p = jnp.exp(s - jnp.max(s, axis=-1, keepdims=True))
        l = jnp.sum(p, axis=-1, keepdims=True)
        o_ref[:, sl] = (_dot(p.astype(BF16), kv_ref[:, sv]) / l).astype(BF16)


def _mem_attn(mq, mem_kv, batch):
    t = mq.shape[0]
    tm = ROW_TILE
    nt = t // batch // tm
    m = mem_kv.shape[0] // batch
    return pl.pallas_call(
        _mem_attn_body,
        grid=(batch, nt),
        in_specs=[pl.BlockSpec((tm, MEM_QW), lambda b, i: (b * nt + i, 0)),
                  pl.BlockSpec((m, 2 * MEM_QW), lambda b, i: (b, 0))],
        out_specs=pl.BlockSpec((tm, MEM_QW), lambda b, i: (b * nt + i, 0)),
        out_shape=jax.ShapeDtypeStruct((t, MEM_QW), BF16),
        compiler_params=_cparams("parallel", "parallel"),
        name="mem_attn",
    )(mq, mem_kv)


def _out_proj_body(h_ref, a_ref, m_ref, wa_ref, wm_ref, o_ref):
    o_ref[...] = h_ref[...] + _dot(a_ref[...], wa_ref[...]) + _dot(m_ref[...], wm_ref[...])


def _out_proj(h, main, mo, wa, wm):
    t = h.shape[0]
    tm = ROW_TILE
    row = lambda n: pl.BlockSpec((tm, n), lambda i: (i, 0))
    full = lambda a: pl.BlockSpec(a.shape, lambda i: (0,) * a.ndim)
    return pl.pallas_call(
        _out_proj_body,
        grid=(t // tm,),
        in_specs=[row(D_MODEL), row(main.shape[1]), row(mo.shape[1]), full(wa), full(wm)],
        out_specs=row(D_MODEL),
        out_shape=jax.ShapeDtypeStruct((t, D_MODEL), F32),
        compiler_params=_cparams("parallel"),
        name="out_proj",
    )(h, main, mo, wa, wm)


def _ffn_body(h_ref, hp_ref, g_ref, wup_ref, cw_ref, cb_ref, wdn_ref, o_ref, acc_ref, *, tm, tf):
    h = h_ref[...]
    g = g_ref[...]
    keep = jnp.where(pl.program_id(1) > 0, 1.0, 0.0)
    xe = jnp.concatenate([_rms(hp_ref[...], g) * keep, _rms(h, g)], axis=0).astype(BF16)
    acc_ref[...] = jnp.zeros_like(acc_ref)
    for j in range(FFN_DIM // tf):
        def conv(off):
            u = _dot(xe, wup_ref[:, off:off + tf])
            w = cw_ref[:, off:off + tf]
            out = cb_ref[:, off:off + tf]
            for c in range(CONV_WIDTH):
                lo = SUBLANE - (CONV_WIDTH - 1) + c
                out = out + w[c:c + 1, :] * u[lo:lo + tm, :]
            return out
        a = conv(j * tf)
        b = conv(FFN_DIM + j * tf)
        act = (a * _sigmoid(a) * b).astype(BF16)
        acc_ref[...] += _dot(act, wdn_ref[j * tf:(j + 1) * tf, :])
    o_ref[...] = h + acc_ref[...]


def _conv_ffn(h, g, wup, cw, cb, wdn, batch):
    t = h.shape[0]
    tm = ROW_TILE
    nt = t // batch // tm
    hb = tm // SUBLANE
    full = lambda a: pl.BlockSpec(a.shape, lambda b, i: (0,) * a.ndim)
    return pl.pallas_call(
        functools.partial(_ffn_body, tm=tm, tf=FFN_TILE),
        grid=(batch, nt),
        in_specs=[pl.BlockSpec((tm, D_MODEL), lambda b, i: (b * nt + i, 0)),
                  pl.BlockSpec((SUBLANE, D_MODEL), lambda b, i: (jnp.maximum((b * nt + i) * hb - 1, 0), 0)),
                  full(g), full(wup), full(cw), full(cb), full(wdn)],
        out_specs=pl.BlockSpec((tm, D_MODEL), lambda b, i: (b * nt + i, 0)),
        out_shape=jax.ShapeDtypeStruct((t, D_MODEL), F32),
        scratch_shapes=[pltpu.VMEM((tm, D_MODEL), F32)],
        compiler_params=_cparams("parallel", "parallel"),
        name="conv_ffn",
    )(h, h, g, wup, cw, cb, wdn)


def _final_norm_body(h_ref, g_ref, o_ref):
    o_ref[...] = _rms(h_ref[...], g_ref[...])


def _final_norm(h, g):
    t = h.shape[0]
    tm = ROW_TILE
    return pl.pallas_call(
        _final_norm_body,
        grid=(t // tm,),
        in_specs=[pl.BlockSpec((tm, D_MODEL), lambda i: (i, 0)), pl.BlockSpec(g.shape, lambda i: (0, 0))],
        out_specs=pl.BlockSpec((tm, D_MODEL), lambda i: (i, 0)),
        out_shape=jax.ShapeDtypeStruct((t, D_MODEL), F32),
        compiler_params=_cparams("parallel"),
        name="final_norm",
    )(h, g)


KV_NAT = NSA_GROUPS * NSA_HEAD_DIM


def _kv_proj_body(h_ref, g_ref, w_ref, ck_ref, cv_ref, ks_ref, kw_ref, vt_ref, *, tm):
    xn = _rms(h_ref[...], g_ref[...]).astype(BF16)
    ck_ref[...] = _dot(xn, w_ref[:, 0:KV_NAT]).astype(BF16)
    cv_ref[...] = _dot(xn, w_ref[:, KV_NAT:2 * KV_NAT]).astype(BF16)
    key = pl.program_id(1) * tm + lax.broadcasted_iota(jnp.int32, (tm, LANE), 0)
    lane = lax.broadcasted_iota(jnp.int32, (tm, LANE), 1)
    onehot = jnp.where(lax.shift_right_logical(key, int(math.log2(SEL_BLOCK))) == lane - NSA_HEAD_DIM, 1.0, 0.0)
    low = lane < NSA_HEAD_DIM
    for gi in range(NSA_GROUPS):
        slot = lambda n: _dot(xn, w_ref[:, 2 * KV_NAT + (n * NSA_GROUPS + gi) * LANE:
                                           2 * KV_NAT + (n * NSA_GROUPS + gi + 1) * LANE])
        ks_ref[gi] = jnp.where(low, slot(0), onehot).astype(BF16)
        kw_ref[gi] = jnp.where(low, slot(1), 0.0).astype(BF16)
        vt_ref[gi] = slot(2).T.astype(BF16)


def _kv_proj(h, g, w, batch):
    t = h.shape[0]
    s = t // batch
    tm = ROW_TILE
    nt = s // tm
    gr = NSA_GROUPS
    return pl.pallas_call(
        functools.partial(_kv_proj_body, tm=tm),
        grid=(batch, nt),
        in_specs=[pl.BlockSpec((tm, D_MODEL), lambda b, i: (b * nt + i, 0)),
                  pl.BlockSpec(g.shape, lambda b, i: (0, 0)),
                  pl.BlockSpec(w.shape, lambda b, i: (0, 0))],
        out_specs=[pl.BlockSpec((tm, KV_NAT), lambda b, i: (b * nt + i, 0)),
                   pl.BlockSpec((tm, KV_NAT), lambda b, i: (b * nt + i, 0)),
                   pl.BlockSpec((None, gr, tm, LANE), lambda b, i: (b, 0, i, 0)),
                   pl.BlockSpec((None, gr, tm, LANE), lambda b, i: (b, 0, i, 0)),
                   pl.BlockSpec((None, gr, LANE, tm), lambda b, i: (b, 0, 0, i))],
        out_shape=[jax.ShapeDtypeStruct((t, KV_NAT), BF16), jax.ShapeDtypeStruct((t, KV_NAT), BF16),
                   jax.ShapeDtypeStruct((batch, gr, s, LANE), BF16),
                   jax.ShapeDtypeStruct((batch, gr, s, LANE), BF16),
                   jax.ShapeDtypeStruct((batch, gr, LANE, s), BF16)],
        compiler_params=_cparams("parallel", "parallel"),
        name="nsa_kv_proj",
    )(h, g, w)


def _compress_body(x_ref, wt_ref, wb_ref, pos_ref, w1_ref, b1_ref, w2_ref, b2_ref, on_ref, ot_ref, *, ncp):
    x = x_ref[...]
    top = _dot(x, wt_ref[...])
    bot = _dot(x, wb_ref[...])
    posb = _dot(pos_ref[...], w1_ref[...])[0:1, :] + b1_ref[...]
    for gi in range(NSA_GROUPS):
        sl = slice(gi * CMP_HIDDEN, (gi + 1) * CMP_HIDDEN)
        hid = top[:, sl] + pltpu.roll(bot[:, sl], ncp - 1, 0) + posb
        hid = (hid * _sigmoid(hid)).astype(BF16)
        out = _dot(hid, w2_ref[...]) + b2_ref[...]
        on_ref[gi] = out.astype(BF16)
        ot_ref[gi] = out.T.astype(BF16)


def _compress(x16, wt, wb, pos, w1, b1, w2, b2, batch):
    ncp = x16.shape[1] // batch
    gr = NSA_GROUPS
    per_j = lambda a: pl.BlockSpec((None,) + a.shape[1:], lambda j, b: (j,) + (0,) * (a.ndim - 1))
    return pl.pallas_call(
        functools.partial(_compress_body, ncp=ncp),
        grid=(2, batch),
        in_specs=[pl.BlockSpec((None, ncp, x16.shape[2]), lambda j, b: (j, b, 0)),
                  per_j(wt), per_j(wb), per_j(pos), per_j(w1), per_j(b1), per_j(w2), per_j(b2)],
        out_specs=[pl.BlockSpec((None, None, gr, ncp, LANE), lambda j, b: (j, b, 0, 0, 0)),
                   pl.BlockSpec((None, None, gr, LANE, ncp), lambda j, b: (j, b, 0, 0, 0))],
        out_shape=[jax.ShapeDtypeStruct((2, batch, gr, ncp, LANE), BF16),
                   jax.ShapeDtypeStruct((2, batch, gr, LANE, ncp), BF16)],
        compiler_params=_cparams("arbitrary", "arbitrary"),
        name="nsa_compress",
    )(x16, wt, wb, pos, w1, b1, w2, b2)


NSA_QW = NSA_HEADS * NSA_HEAD_DIM
G3 = NSA_REP * TQ


def _nsa_in_body(h_ref, g_ref, w_ref, qt_ref, gt_ref, mq_ref):
    xn = _rms(h_ref[...], g_ref[...]).astype(BF16)
    for j in range(NSA_QW // LANE):
        y = _dot(xn, w_ref[:, j * LANE:(j + 1) * LANE]) * NSA_HEAD_DIM ** -0.5
        qt_ref[j * LANE:(j + 1) * LANE, :] = y.T.astype(BF16)
    gt_ref[...] = _sigmoid(_dot(xn, w_ref[:, NSA_QW:NSA_QW + LANE])).T
    mq_ref[...] = _dot(xn, w_ref[:, NSA_QW + LANE:]).astype(BF16)


def _nsa_in_proj(h, g, w):
    t = h.shape[0]
    tm = ROW_TILE
    row = lambda n: pl.BlockSpec((tm, n), lambda i: (i, 0))
    col = lambda n: pl.BlockSpec((n, tm), lambda i: (0, i))
    full = lambda a: pl.BlockSpec(a.shape, lambda i: (0,) * a.ndim)
    return pl.pallas_call(
        _nsa_in_body,
        grid=(t // tm,),
        in_specs=[row(D_MODEL), full(g), full(w)],
        out_specs=[col(NSA_QW), col(LANE), row(MEM_QW)],
        out_shape=[jax.ShapeDtypeStruct((NSA_QW, t), BF16), jax.ShapeDtypeStruct((LANE, t), F32),
                   jax.ShapeDtypeStruct((t, MEM_QW), BF16)],
        compiler_params=_cparams("parallel"),
        name="nsa_in_proj",
    )(h, g, w)


def _nsa_attn_body(qt_ref, gt_ref, kc_ref, vct_ref, ks_ref, kw_ref, vt_ref, ovt_ref, tz_ref, cb_ref,
                   o_ref, sc_ref, acc_ref, ot_ref, *, nsb, ncp, n_sel):
    n = pl.program_id(1)
    t0 = n * TQ
    dh = NSA_HEAD_DIM
    tq = t0 + (lax.broadcasted_iota(jnp.int32, (ncp, G3), 1) & (TQ - 1))
    kc = lax.broadcasted_iota(jnp.int32, (ncp, G3), 0)
    cmask = tq >= kc * CMP_STRIDE + (CMP_BLOCK - 1)
    cstart = pl.multiple_of(ncp - lax.rem(n * (TQ // CMP_STRIDE), ncp), SUBLANE)
    jj = lax.broadcasted_iota(jnp.int32, (nsb, TQ), 0)
    cur = lax.shift_right_logical(t0 + lax.broadcasted_iota(jnp.int32, (nsb, TQ), 1), int(math.log2(SEL_BLOCK)))
    forced = (jj == 0) | (jj == cur) | (jj == cur - 1)
    zpad = jnp.zeros((dh, G3), BF16)

    def flash(k_ref, gi, lo, hi, tz_index, qa):
        acc_ref[...] = jnp.zeros_like(acc_ref)

        def step(kt, carry):
            m, l = carry
            koff = pl.multiple_of(kt * TQ, TQ)
            s = _dot(k_ref[gi, pl.ds(koff, TQ), :], qa) + tz_ref[gi, tz_index(n - kt)]
            m2 = jnp.maximum(m, jnp.max(s, axis=0, keepdims=True))
            al = jnp.exp(m - m2)
            p = jnp.exp(s - m2)
            acc_ref[...] = al * acc_ref[...] + _dot(vt_ref[gi, :, pl.ds(koff, TQ)], p.astype(BF16))
            return m2, al * l + jnp.sum(p, axis=0, keepdims=True)

        return lax.fori_loop(lo, hi, step, (jnp.full((1, G3), NEG, F32), jnp.zeros((1, G3), F32)))

    for gi in range(NSA_GROUPS):
        h0 = gi * NSA_REP
        q3 = jnp.concatenate([qt_ref[(h0 + r) * dh:(h0 + r + 1) * dh, :] for r in range(NSA_REP)], axis=1)

        s = _dot(kc_ref[gi], jnp.concatenate([q3, zpad], axis=0)) + cb_ref[gi, pl.ds(cstart, ncp), :]
        s = jnp.where(cmask, s, NEG)
        p = jnp.where(cmask, jnp.exp(s - jnp.max(s, axis=0, keepdims=True)), 0.0)
        p = p * (1.0 / jnp.maximum(jnp.sum(p, axis=0, keepdims=True), 1e-30))
        oc = _dot(vct_ref[gi], p.astype(BF16))
        psum = p[:, 0:TQ]
        for r in range(1, NSA_REP):
            psum = psum + p[:, r * TQ:(r + 1) * TQ]
        p1 = psum.astype(BF16)
        p2 = (psum - p1.astype(F32)).astype(BF16)
        imp = _dot(ovt_ref[...], p1) + _dot(ovt_ref[...], p2)
        score = jnp.where(forced, 1e4, jnp.where(jj <= cur, imp[0:nsb], -1.0))
        sc_ref[...] = score

        def rank_step(i, cnt):
            rowv = sc_ref[pl.ds(i, 1), :]
            before = (rowv > score) | ((rowv == score) & (i < jj))
            return cnt + jnp.where(before, 1.0, 0.0)
        cnt = lax.fori_loop(0, nsb, rank_step, jnp.zeros((nsb, TQ), F32))
        selneg = jnp.where(cnt < n_sel, 0.0, NEG).astype(BF16)
        if nsb < SEL_BLOCK:
            selneg = jnp.concatenate([selneg, jnp.zeros((SEL_BLOCK - nsb, TQ), BF16)], axis=0)
        qa = jnp.concatenate([q3, jnp.concatenate([selneg] * NSA_REP, axis=1)], axis=0)

        m_s, l_s = flash(ks_ref, gi, 0, n + 1, lambda d: jnp.minimum(d, 2), qa)
        o_s = acc_ref[0:dh, :] * (1.0 / l_s)
        nwt = WINDOW // TQ
        m_w, l_w = flash(kw_ref, gi, jnp.maximum(n - nwt, 0), n + 1,
                         lambda d: jnp.where(d == nwt, 3, jnp.minimum(d, 2)), qa)
        o_w = acc_ref[dh:2 * dh, :] * (1.0 / l_w)

        for r in range(NSA_REP):
            h = h0 + r
            gate = lambda c: gt_ref[pl.ds(h * 3 + c, 1), :]
            sl = slice(r * TQ, (r + 1) * TQ)
            ot_ref[h * dh:(h + 1) * dh, :] = gate(0) * oc[0:dh, sl] + gate(1) * o_s[:, sl] + gate(2) * o_w[:, sl]

    for j in range(NSA_QW // LANE):
        o_ref[:, j * LANE:(j + 1) * LANE] = ot_ref[j * LANE:(j + 1) * LANE, :].T.astype(BF16)


def _nsa_attn(qt, gt, kc, vct, ksel, kwin, vt, ovt, tz, cb, batch):
    t = qt.shape[1]
    s = t // batch
    nt = s // TQ
    nsb = s // SEL_BLOCK
    ncp = kc.shape[-2]
    per_b = lambda a: pl.BlockSpec((None,) + a.shape[1:], lambda b, i: (b,) + (0,) * (a.ndim - 1))
    full = lambda a: pl.BlockSpec(a.shape, lambda b, i: (0,) * a.ndim)
    return pl.pallas_call(
        functools.partial(_nsa_attn_body, nsb=nsb, ncp=ncp, n_sel=min(SEL_TOPK, nsb)),
        grid=(batch, nt),
        in_specs=[pl.BlockSpec((NSA_QW, TQ), lambda b, i: (0, b * nt + i)),
                  pl.BlockSpec((LANE, TQ), lambda b, i: (0, b * nt + i)),
                  per_b(kc), per_b(vct), per_b(ksel), per_b(kwin), per_b(vt),
                  full(ovt), full(tz), full(cb)],
        out_specs=pl.BlockSpec((TQ, NSA_QW), lambda b, i: (b * nt + i, 0)),
        out_shape=jax.ShapeDtypeStruct((t, NSA_QW), BF16),
        scratch_shapes=[pltpu.VMEM((nsb, TQ), F32), pltpu.VMEM((LANE, G3), F32), pltpu.VMEM((NSA_QW, TQ), F32)],
        compiler_params=_cparams("parallel", "arbitrary"),
        name="nsa_attn",
    )(qt, gt, kc, vct, ksel, kwin, vt, ovt, tz, cb)


def _rel_bucket_np(dist):
    dist = np.maximum(dist, 0)
    max_exact = REL_BUCKETS // 2
    ratio = np.log(np.maximum(dist, 1).astype(np.float32) / np.float32(max_exact)) / np.float32(
        math.log(REL_MAX_DIST / max_exact))
    large = max_exact + (ratio * np.float32(REL_BUCKETS - max_exact)).astype(np.int32)
    large = np.minimum(large, REL_BUCKETS - 1)
    return np.where(dist < max_exact, dist, large).astype(np.int32)


def _group_lanes(a):
    hh, r, c = a.shape
    return a.reshape(NSA_GROUPS, NSA_REP, r, c).transpose(0, 2, 1, 3).reshape(NSA_GROUPS, r, NSA_REP * c)


def _bias_tables(rel_bias, ncp):
    k = np.arange(TQ)[:, None]
    q = np.arange(TQ)[None, :]
    tbl = rel_bias.astype(F32)
    far = jnp.broadcast_to(tbl[REL_BUCKETS - 1][:, None, None], (NSA_HEADS, TQ, TQ))
    t0 = jnp.where(jnp.asarray(k <= q)[None], jnp.moveaxis(tbl[_rel_bucket_np(q - k)], -1, 0), NEG)
    t1 = jnp.moveaxis(tbl[_rel_bucket_np(TQ + q - k)], -1, 0)
    t3 = jnp.where(jnp.asarray(k > q)[None], far, NEG)
    tz = jnp.stack([_group_lanes(x) for x in (t0, t1, far, t3)], axis=1)
    kk = np.arange(ncp)[:, None]
    m = np.where(kk < ncp // 2, -kk, ncp - kk)
    d = CMP_STRIDE * m + q - (CMP_BLOCK - 1)
    idx = np.where((d >= 0) & (d < REL_MAX_DIST), _rel_bucket_np(d), REL_BUCKETS - 1)
    cb = _group_lanes(jnp.moveaxis(tbl[idx], -1, 0))
    return tz, jnp.concatenate([cb, cb], axis=1)


def _overlap_table(s, ncp):
    nsb = s // SEL_BLOCK
    nc = (s - CMP_BLOCK) // CMP_STRIDE + 1
    cs = np.arange(ncp) * CMP_STRIDE
    ce = cs + CMP_BLOCK - 1
    ss = np.arange(SEL_BLOCK) * SEL_BLOCK
    ov = (cs[None, :] < ss[:, None] + SEL_BLOCK) & (ce[None, :] >= ss[:, None])
    ov &= (np.arange(ncp) < nc)[None, :] & (np.arange(SEL_BLOCK) < nsb)[:, None]
    return jnp.asarray(ov, dtype=BF16)


def kernel(x, mem, norm_mix, norm_mem, w_mem_kv, w_out, norm_ffn, w_up, conv_w, conv_b, w_down,
           gla_w_in, gla_w_gate_up, gla_b_gate, gla_out_norm, nsa_w_in, kv_norm, w_kv_shared,
           cmp_pos, cmp_w1, cmp_b1, cmp_w2, cmp_b2, rel_bias, final_norm):
    batch, seq = x.shape[0], x.shape[1]
    t = batch * seq
    h = x.reshape(t, D_MODEL)
    row = lambda v: v.reshape(1, -1).astype(F32)

    wk, wv = w_mem_kv[..., :MEM_W], w_mem_kv[..., MEM_W:]
    w_mkv = jnp.concatenate([_pad_heads(wk, MEM_HEADS, MEM_HEAD_DIM, MEM_DP),
                             _pad_heads(wv, MEM_HEADS, MEM_HEAD_DIM, MEM_DP)], axis=-1).astype(BF16)
    mem_kv_all = _mem_kv_proj(mem.reshape(-1, D_MODEL), norm_mem.reshape(DEPTH, 1, D_MODEL), w_mkv, batch)

    shared = None
    for i in range(DEPTH):
        w_o = w_out[i]
        w_o_mem = _pad_head_rows(w_o[MAIN_W:], MEM_HEADS, MEM_HEAD_DIM, MEM_DP).astype(BF16)
        if i < N_A_LAYERS:
            wi = gla_w_in[i]
            c0 = GLA_HEADS * GLA_DK
            c1 = 2 * c0
            c2 = c1 + GLA_HEADS * GLA_DV
            c3 = c2 + GLA_HEADS * GLA_DV
            c4 = c3 + GLA_RANK
            w_all = jnp.concatenate([
                _pad_heads(wi[:, :c0], GLA_HEADS, GLA_DK, GLA_DKP),
                _pad_heads(wi[:, c0:c1], GLA_HEADS, GLA_DK, GLA_DKP),
                _pad_heads(wi[:, c1:c2], GLA_HEADS, GLA_DV, GLA_DVP),
                _pad_heads(wi[:, c2:c3], GLA_HEADS, GLA_DV, GLA_DVP),
                _pad_heads(wi[:, c3:c4], 1, GLA_RANK, LANE),
                _pad_heads(wi[:, c4:], MEM_HEADS, MEM_HEAD_DIM, MEM_DP)], axis=1).astype(BF16)
            wg = jnp.pad(_pad_heads(gla_w_gate_up[i], GLA_HEADS, GLA_DK, GLA_DKP),
                         ((0, LANE - GLA_RANK), (0, 0))).astype(BF16)
            bg = row(_pad_heads(gla_b_gate[i], GLA_HEADS, GLA_DK, GLA_DKP))
            q, k, v, go, la, mq = _gla_in_proj(h, row(norm_mix[i]), w_all, wg, bg)
            on = row(jnp.pad(gla_out_norm[i], (0, GLA_DVP - GLA_DV)))
            main = _gla_mixer(q, k, la, v, go, on, batch)
            w_o_main = _pad_head_rows(w_o[:MAIN_W], GLA_HEADS, GLA_DV, GLA_DVP).astype(BF16)
        else:
            if shared is None:
                ncp = seq // CMP_STRIDE
                gr, dh = NSA_GROUPS, NSA_HEAD_DIM
                wkv = w_kv_shared.reshape(D_MODEL, 6, gr, dh)
                pair = lambda a, b: jnp.concatenate([wkv[:, a], wkv[:, b]], axis=-1).reshape(D_MODEL, gr * 2 * dh)
                slot = lambda a: _pad_heads(wkv[:, a].reshape(D_MODEL, KV_NAT), gr, dh, LANE)
                w_kv = jnp.concatenate([wkv[:, 0].reshape(D_MODEL, KV_NAT), wkv[:, 1].reshape(D_MODEL, KV_NAT),
                                        slot(2), slot(4), pair(3, 5)], axis=1).astype(BF16)
                ck, cv, ksel, kwin, vt = _kv_proj(h, row(kv_norm), w_kv, batch)
                x16 = jnp.stack([ck.reshape(batch * ncp, CMP_STRIDE * KV_NAT),
                                 cv.reshape(batch * ncp, CMP_STRIDE * KV_NAT)])
                w1 = cmp_w1.reshape(2, 2, CMP_STRIDE, dh, CMP_HIDDEN)
                eye = jnp.eye(gr, dtype=F32)
                w1x = jnp.einsum('jhldc,gk->jhlgdkc', w1, eye).reshape(2, 2, CMP_STRIDE * KV_NAT, gr * CMP_HIDDEN)
                w1x = w1x.astype(BF16)
                pos8 = jnp.broadcast_to(cmp_pos.reshape(2, 1, CMP_BLOCK * dh), (2, SUBLANE, CMP_BLOCK * dh)).astype(BF16)
                w2p = jnp.pad(cmp_w2, ((0, 0), (0, 0), (0, LANE - dh))).astype(BF16)
                b2p = jnp.pad(cmp_b2, ((0, 0), (0, LANE - dh))).reshape(2, 1, LANE).astype(F32)
                cnat, ctr = _compress(x16, w1x[:, 0], w1x[:, 1], pos8, cmp_w1.astype(BF16),
                                      cmp_b1.reshape(2, 1, CMP_HIDDEN).astype(F32), w2p, b2p, batch)
                tz, cb = _bias_tables(rel_bias, ncp)
                ov = _overlap_table(seq, ncp)
                shared = (cnat[0], ctr[1], ksel, kwin, vt, ov, tz, cb)
            wi = nsa_w_in[i - N_A_LAYERS]
            c0 = NSA_HEADS * NSA_HEAD_DIM
            c1 = c0 + NSA_HEADS * 3
            w_all = jnp.concatenate([
                wi[:, :c0],
                _pad_heads(wi[:, c0:c1], 1, NSA_HEADS * 3, LANE),
                _pad_heads(wi[:, c1:], MEM_HEADS, MEM_HEAD_DIM, MEM_DP)], axis=1).astype(BF16)
            q, gates, mq = _nsa_in_proj(h, row(norm_mix[i]), w_all)
            main = _nsa_attn(q, gates, *shared, batch)
            w_o_main = w_o[:MAIN_W].astype(BF16)
        mo = _mem_attn(mq, mem_kv_all[i], batch)
        h = _out_proj(h, main, mo, w_o_main, w_o_mem)
        h = _conv_ffn(h, row(norm_ffn[i]), w_up[i].astype(BF16), conv_w[i].astype(F32), row(conv_b[i]),
                      w_down[i].astype(BF16), batch)
    return _final_norm(h, row(final_norm)).reshape(batch, seq, D_MODEL)
```

```python
import functools
import math

import numpy as np
import jax
import jax.numpy as jnp
from jax import lax
from jax.experimental import pallas as pl
from jax.experimental.pallas import tpu as pltpu

F32 = jnp.float32
BF16 = jnp.bfloat16

D_MODEL = 1024
DEPTH = 4
N_A_LAYERS = DEPTH // 2
MEM_HEADS = 4
MEM_HEAD_DIM = 64
MEM_W = MEM_HEADS * MEM_HEAD_DIM
MAIN_W = D_MODEL - MEM_W
GLA_HEADS = 4
GLA_DV = MAIN_W // GLA_HEADS
GLA_DK = GLA_DV // 2
GLA_RANK = 16
GLA_GATE_NORM = 16.0
NSA_HEADS = 12
NSA_GROUPS = 4
NSA_HEAD_DIM = MAIN_W // NSA_HEADS
NSA_REP = NSA_HEADS // NSA_GROUPS
CMP_BLOCK = 32
CMP_STRIDE = 16
CMP_HIDDEN = 128
SEL_BLOCK = 64
SEL_TOPK = 16
WINDOW = 512
REL_BUCKETS = 32
REL_MAX_DIST = 128
FFN_DIM = 2816
CONV_WIDTH = 3
EPS = 1e-6

LANE = 128
SUBLANE = 8
VMEM_LIMIT = 56 * 1024 * 1024
GLA_DKP = LANE
GLA_DVP = 2 * LANE
MEM_DP = LANE
NEG = -1e30
TQ = 128
ROW_TILE = 512
GLA_CHUNK = 64
FFN_TILE = 256
SEL_TILES = 4
RANK_UNROLL = 4
LOG2E = math.log2(math.e)

NT = (((1,), (1,)), ((), ()))
TN = (((0,), (0,)), ((), ()))


def _cparams(*sem):
    return pltpu.CompilerParams(dimension_semantics=sem, vmem_limit_bytes=VMEM_LIMIT)


def _rms(x, g):
    return x * lax.rsqrt(jnp.mean(x * x, axis=-1, keepdims=True) + EPS) * g


def _sigmoid(x):
    return 1.0 / (1.0 + jnp.exp(-x))


def _dot(a, b):
    return jnp.dot(a, b, preferred_element_type=F32)


def _pad_heads(w, nh, d, dp):
    lead = w.shape[:-1]
    w = w.reshape(lead + (nh, d))
    w = jnp.pad(w, [(0, 0)] * len(lead) + [(0, 0), (0, dp - d)])
    return w.reshape(lead + (nh * dp,))


def _pad_head_rows(w, nh, d, dp):
    n = w.shape[-1]
    w = jnp.pad(w.reshape(nh, d, n), ((0, 0), (0, dp - d), (0, 0)))
    return w.reshape(nh * dp, n)


GLA_QW = GLA_HEADS * GLA_DKP
GLA_VW = GLA_HEADS * GLA_DVP
MEM_QW = MEM_HEADS * MEM_DP
GLA_OFF_K = GLA_QW
GLA_OFF_V = 2 * GLA_QW
GLA_OFF_G = GLA_OFF_V + GLA_VW
GLA_OFF_LR = GLA_OFF_G + GLA_VW
GLA_OFF_MQ = GLA_OFF_LR + LANE
GLA_NP = GLA_OFF_MQ + MEM_QW


def _gla_in_body(h_ref, g_ref, w_ref, wg_ref, bg_ref, q_ref, k_ref, v_ref, go_ref, la_ref, mq_ref):
    xn = _rms(h_ref[...], g_ref[...]).astype(BF16)

    def proj(lo, n):
        return _dot(xn, w_ref[:, lo:lo + n])

    q_ref[...] = proj(0, GLA_QW).astype(BF16)
    k_ref[...] = proj(GLA_OFF_K, GLA_QW).astype(BF16)
    for j in range(GLA_VW // GLA_QW):
        v_ref[:, j * GLA_QW:(j + 1) * GLA_QW] = proj(GLA_OFF_V + j * GLA_QW, GLA_QW).astype(BF16)
        go_ref[:, j * GLA_QW:(j + 1) * GLA_QW] = proj(GLA_OFF_G + j * GLA_QW, GLA_QW).astype(BF16)
    lr = proj(GLA_OFF_LR, LANE).astype(BF16)
    z = _dot(lr, wg_ref[...]) + bg_ref[...]
    la_ref[...] = (jnp.minimum(z, 0.0) - jnp.log(1.0 + jnp.exp(-jnp.abs(z)))) * (1.0 / GLA_GATE_NORM)
    mq_ref[...] = proj(GLA_OFF_MQ, MEM_QW).astype(BF16)


def _gla_in_proj(h, g, w, wg, bg):
    t = h.shape[0]
    tm = ROW_TILE
    row = lambda n: pl.BlockSpec((tm, n), lambda i: (i, 0))
    full = lambda a: pl.BlockSpec(a.shape, lambda i: (0,) * a.ndim)
    return pl.pallas_call(
        _gla_in_body,
        grid=(t // tm,),
        in_specs=[row(D_MODEL), full(g), full(w), full(wg), full(bg)],
        out_specs=[row(GLA_QW), row(GLA_QW), row(GLA_VW), row(GLA_VW), row(GLA_QW), row(MEM_QW)],
        out_shape=[jax.ShapeDtypeStruct((t, GLA_QW), BF16), jax.ShapeDtypeStruct((t, GLA_QW), BF16),
                   jax.ShapeDtypeStruct((t, GLA_VW), BF16), jax.ShapeDtypeStruct((t, GLA_VW), BF16),
                   jax.ShapeDtypeStruct((t, GLA_QW), F32), jax.ShapeDtypeStruct((t, MEM_QW), BF16)],
        compiler_params=_cparams("parallel"),
        name="gla_in_proj",
    )(h, g, w, wg, bg)


def _gla_mix_body(q_ref, k_ref, la_ref, v_ref, go_ref, on_ref, o_ref, st_ref, *, cn):
    @pl.when(pl.program_id(1) == 0)
    def _():
        st_ref[...] = jnp.zeros_like(st_ref)

    row = lax.broadcasted_iota(jnp.int32, (cn, cn), 0)
    col = lax.broadcasted_iota(jnp.int32, (cn, cn), 1)
    causal = row >= col
    tril = jnp.where(causal, 1.0, 0.0).astype(BF16)
    scale = GLA_DK ** -0.5
    for hd in range(GLA_HEADS):
        sk = slice(hd * GLA_DKP, (hd + 1) * GLA_DKP)
        sv = slice(hd * GLA_DVP, (hd + 1) * GLA_DVP)
        la = la_ref[:, sk]
        la1 = la.astype(BF16)
        r1 = la - la1.astype(F32)
        la2 = r1.astype(BF16)
        la3 = (r1 - la2.astype(F32)).astype(BF16)
        b = _dot(tril, la1) + _dot(tril, la2) + _dot(tril, la3)
        b_last = b[cn - 1:cn, :]
        q = q_ref[:, sk].astype(F32)
        k = k_ref[:, sk].astype(F32)
        qs = (q * jnp.exp(b) * scale).astype(BF16)
        ks = (k * jnp.exp(-b)).astype(BF16)
        ko = (k * jnp.exp(b_last - b)).astype(BF16)
        v = v_ref[:, sv]
        a = lax.dot_general(qs, ks, NT, preferred_element_type=F32)
        a = jnp.where(causal, a, 0.0).astype(BF16)
        st = st_ref[hd]
        o = _dot(a, v) + lax.dot_general(qs, st.astype(BF16), NT, preferred_element_type=F32)
        st_ref[hd] = st * jnp.exp(b_last) + lax.dot_general(v, ko, TN, preferred_element_type=F32)
        ms = jnp.sum(o * o, axis=-1, keepdims=True) * (1.0 / GLA_DV)
        y = o * lax.rsqrt(ms + EPS) * on_ref[...]
        g = go_ref[:, sv].astype(F32)
        o_ref[:, sv] = (y * (g * _sigmoid(g))).astype(BF16)


def _gla_mixer(q, k, la, v, go, on, batch):
    t = q.shape[0]
    cn = GLA_CHUNK
    nc = t // batch // cn
    row = lambda n: pl.BlockSpec((cn, n), lambda b, c: (b * nc + c, 0))
    return pl.pallas_call(
        functools.partial(_gla_mix_body, cn=cn),
        grid=(batch, nc),
        in_specs=[row(GLA_QW), row(GLA_QW), row(GLA_QW), row(GLA_VW), row(GLA_VW),
                  pl.BlockSpec(on.shape, lambda b, c: (0, 0))],
        out_specs=row(GLA_VW),
        out_shape=jax.ShapeDtypeStruct((t, GLA_VW), BF16),
        scratch_shapes=[pltpu.VMEM((GLA_HEADS, GLA_DVP, GLA_DKP), F32)],
        compiler_params=_cparams("arbitrary", "arbitrary"),
        name="gla_mixer",
    )(q, k, la, v, go, on)


def _mem_kv_body(m_ref, g_ref, w_ref, o_ref):
    xn = _rms(m_ref[...], g_ref[...]).astype(BF16)
    o_ref[...] = _dot(xn, w_ref[...]).astype(BF16)


def _mem_kv_proj(mem2, g, w, batch):
    nl = w.shape[0]
    m = mem2.shape[0] // batch
    n = w.shape[2]
    return pl.pallas_call(
        _mem_kv_body,
        grid=(nl, batch),
        in_specs=[pl.BlockSpec((m, D_MODEL), lambda l, b: (b, 0)),
                  pl.BlockSpec((None, 1, D_MODEL), lambda l, b: (l, 0, 0)),
                  pl.BlockSpec((None, D_MODEL, n), lambda l, b: (l, 0, 0))],
        out_specs=pl.BlockSpec((None, m, n), lambda l, b: (l, b, 0)),
        out_shape=jax.ShapeDtypeStruct((nl, mem2.shape[0], n), BF16),
        compiler_params=_cparams("arbitrary", "arbitrary"),
        name="mem_kv_proj",
    )(mem2, g, w)


def _mem_attn_body(q_ref, kv_ref, o_ref):
    for hd in range(MEM_HEADS):
        sl = slice(hd * MEM_DP, (hd + 1) * MEM_DP)
        sv = slice(MEM_QW + hd * MEM_DP, MEM_QW + (hd + 1) * MEM_DP)
        s = lax.dot_general(q_ref[:, sl], kv_ref[:, sl], NT, preferred_element_type=F32) * MEM_HEAD_DIM ** -0.5
        p = jnp.exp(s - jnp.max(s, axis=-1, keepdims=True))
        l = jnp.sum(p, axis=-1, keepdims=True)
        o_ref[:, sl] = (_dot(p.astype(BF16), kv_ref[:, sv]) / l).astype(BF16)


def _mem_attn(mq, mem_kv, batch):
    t = mq.shape[0]
    tm = ROW_TILE
    nt = t // batch // tm
    m = mem_kv.shape[0] // batch
    return pl.pallas_call(
        _mem_attn_body,
        grid=(batch, nt),
        in_specs=[pl.BlockSpec((tm, MEM_QW), lambda b, i: (b * nt + i, 0)),
                  pl.BlockSpec((m, 2 * MEM_QW), lambda b, i: (b, 0))],
        out_specs=pl.BlockSpec((tm, MEM_QW), lambda b, i: (b * nt + i, 0)),
        out_shape=jax.ShapeDtypeStruct((t, MEM_QW), BF16),
        compiler_params=_cparams("parallel", "parallel"),
        name="mem_attn",
    )(mq, mem_kv)


def _out_proj_body(h_ref, a_ref, m_ref, wa_ref, wm_ref, o_ref):
    o_ref[...] = h_ref[...] + _dot(a_ref[...], wa_ref[...]) + _dot(m_ref[...], wm_ref[...])


def _out_proj(h, main, mo, wa, wm):
    t = h.shape[0]
    tm = ROW_TILE
    row = lambda n: pl.BlockSpec((tm, n), lambda i: (i, 0))
    full = lambda a: pl.BlockSpec(a.shape, lambda i: (0,) * a.ndim)
    return pl.pallas_call(
        _out_proj_body,
        grid=(t // tm,),
        in_specs=[row(D_MODEL), row(main.shape[1]), row(mo.shape[1]), full(wa), full(wm)],
        out_specs=row(D_MODEL),
        out_shape=jax.ShapeDtypeStruct((t, D_MODEL), F32),
        compiler_params=_cparams("parallel"),
        name="out_proj",
    )(h, main, mo, wa, wm)


def _ffn_body(h_ref, hp_ref, g_ref, wup_ref, cw_ref, cb_ref, wdn_ref, o_ref, acc_ref, *, tm, tf):
    h = h_ref[...]
    g = g_ref[...]
    keep = jnp.where(pl.program_id(1) > 0, 1.0, 0.0)
    xe = jnp.concatenate([_rms(hp_ref[...], g) * keep, _rms(h, g)], axis=0).astype(BF16)
    acc_ref[...] = jnp.zeros_like(acc_ref)
    for j in range(FFN_DIM // tf):
        def conv(off):
            u = _dot(xe, wup_ref[:, off:off + tf])
            w = cw_ref[:, off:off + tf]
            out = cb_ref[:, off:off + tf]
            for c in range(CONV_WIDTH):
                lo = SUBLANE - (CONV_WIDTH - 1) + c
                out = out + w[c:c + 1, :] * u[lo:lo + tm, :]
            return out
        a = conv(j * tf)
        b = conv(FFN_DIM + j * tf)
        act = (a * _sigmoid(a) * b).astype(BF16)
        acc_ref[...] += _dot(act, wdn_ref[j * tf:(j + 1) * tf, :])
    o_ref[...] = h + acc_ref[...]


def _conv_ffn(h, g, wup, cw, cb, wdn, batch):
    t = h.shape[0]
    tm = ROW_TILE
    nt = t // batch // tm
    hb = tm // SUBLANE
    full = lambda a: pl.BlockSpec(a.shape, lambda b, i: (0,) * a.ndim)
    return pl.pallas_call(
        functools.partial(_ffn_body, tm=tm, tf=FFN_TILE),
        grid=(batch, nt),
        in_specs=[pl.BlockSpec((tm, D_MODEL), lambda b, i: (b * nt + i, 0)),
                  pl.BlockSpec((SUBLANE, D_MODEL), lambda b, i: (jnp.maximum((b * nt + i) * hb - 1, 0), 0)),
                  full(g), full(wup), full(cw), full(cb), full(wdn)],
        out_specs=pl.BlockSpec((tm, D_MODEL), lambda b, i: (b * nt + i, 0)),
        out_shape=jax.ShapeDtypeStruct((t, D_MODEL), F32),
        scratch_shapes=[pltpu.VMEM((tm, D_MODEL), F32)],
        compiler_params=_cparams("parallel", "parallel"),
        name="conv_ffn",
    )(h, h, g, wup, cw, cb, wdn)


def _final_norm_body(h_ref, g_ref, o_ref):
    o_ref[...] = _rms(h_ref[...], g_ref[...])


def _final_norm(h, g):
    t = h.shape[0]
    tm = ROW_TILE
    return pl.pallas_call(
        _final_norm_body,
        grid=(t // tm,),
        in_specs=[pl.BlockSpec((tm, D_MODEL), lambda i: (i, 0)), pl.BlockSpec(g.shape, lambda i: (0, 0))],
        out_specs=pl.BlockSpec((tm, D_MODEL), lambda i: (i, 0)),
        out_shape=jax.ShapeDtypeStruct((t, D_MODEL), F32),
        compiler_params=_cparams("parallel"),
        name="final_norm",
    )(h, g)


KV_NAT = NSA_GROUPS * NSA_HEAD_DIM


def _kv_proj_body(h_ref, g_ref, w_ref, ck_ref, cv_ref, ks_ref, kw_ref, vt_ref, *, tm):
    xn = _rms(h_ref[...], g_ref[...]).astype(BF16)
    ck_ref[...] = _dot(xn, w_ref[:, 0:KV_NAT]).astype(BF16)
    cv_ref[...] = _dot(xn, w_ref[:, KV_NAT:2 * KV_NAT]).astype(BF16)
    key = pl.program_id(1) * tm + lax.broadcasted_iota(jnp.int32, (tm, LANE), 0)
    lane = lax.broadcasted_iota(jnp.int32, (tm, LANE), 1)
    onehot = jnp.where(lax.shift_right_logical(key, int(math.log2(SEL_BLOCK))) == lane - NSA_HEAD_DIM, 1.0, 0.0)
    low = lane < NSA_HEAD_DIM
    for gi in range(NSA_GROUPS):
        slot = lambda n: _dot(xn, w_ref[:, 2 * KV_NAT + (n * NSA_GROUPS + gi) * LANE:
                                           2 * KV_NAT + (n * NSA_GROUPS + gi + 1) * LANE])
        ks_ref[gi] = jnp.where(low, slot(0), onehot).astype(BF16)
        kw_ref[gi] = jnp.where(low, slot(1), 0.0).astype(BF16)
        vt_ref[gi] = slot(2).T.astype(BF16)


def _kv_proj(h, g, w, batch):
    t = h.shape[0]
    s = t // batch
    tm = ROW_TILE
    nt = s // tm
    gr = NSA_GROUPS
    return pl.pallas_call(
        functools.partial(_kv_proj_body, tm=tm),
        grid=(batch, nt),
        in_specs=[pl.BlockSpec((tm, D_MODEL), lambda b, i: (b * nt + i, 0)),
                  pl.BlockSpec(g.shape, lambda b, i: (0, 0)),
                  pl.BlockSpec(w.shape, lambda b, i: (0, 0))],
        out_specs=[pl.BlockSpec((tm, KV_NAT), lambda b, i: (b * nt + i, 0)),
                   pl.BlockSpec((tm, KV_NAT), lambda b, i: (b * nt + i, 0)),
                   pl.BlockSpec((None, gr, tm, LANE), lambda b, i: (b, 0, i, 0)),
                   pl.BlockSpec((None, gr, tm, LANE), lambda b, i: (b, 0, i, 0)),
                   pl.BlockSpec((None, gr, LANE, tm), lambda b, i: (b, 0, 0, i))],
        out_shape=[jax.ShapeDtypeStruct((t, KV_NAT), BF16), jax.ShapeDtypeStruct((t, KV_NAT), BF16),
                   jax.ShapeDtypeStruct((batch, gr, s, LANE), BF16),
                   jax.ShapeDtypeStruct((batch, gr, s, LANE), BF16),
                   jax.ShapeDtypeStruct((batch, gr, LANE, s), BF16)],
        compiler_params=_cparams("parallel", "parallel"),
        name="nsa_kv_proj",
    )(h, g, w)


def _compress_body(x_ref, wt_ref, wb_ref, pos_ref, w1_ref, b1_ref, w2_ref, b2_ref, on_ref, ot_ref, *, ncp):
    x = x_ref[...]
    top = _dot(x, wt_ref[...])
    bot = _dot(x, wb_ref[...])
    posb = _dot(pos_ref[...], w1_ref[...])[0:1, :] + b1_ref[...]
    for gi in range(NSA_GROUPS):
        sl = slice(gi * CMP_HIDDEN, (gi + 1) * CMP_HIDDEN)
        hid = top[:, sl] + pltpu.roll(bot[:, sl], ncp - 1, 0) + posb
        hid = (hid * _sigmoid(hid)).astype(BF16)
        out = _dot(hid, w2_ref[...]) + b2_ref[...]
        on_ref[gi] = out.astype(BF16)
        ot_ref[gi] = out.T.astype(BF16)


def _compress(x16, wt, wb, pos, w1, b1, w2, b2, batch):
    ncp = x16.shape[1] // batch
    gr = NSA_GROUPS
    per_j = lambda a: pl.BlockSpec((None,) + a.shape[1:], lambda j, b: (j,) + (0,) * (a.ndim - 1))
    return pl.pallas_call(
        functools.partial(_compress_body, ncp=ncp),
        grid=(2, batch),
        in_specs=[pl.BlockSpec((None, ncp, x16.shape[2]), lambda j, b: (j, b, 0)),
                  per_j(wt), per_j(wb), per_j(pos), per_j(w1), per_j(b1), per_j(w2), per_j(b2)],
        out_specs=[pl.BlockSpec((None, None, gr, ncp, LANE), lambda j, b: (j, b, 0, 0, 0)),
                   pl.BlockSpec((None, None, gr, LANE, ncp), lambda j, b: (j, b, 0, 0, 0))],
        out_shape=[jax.ShapeDtypeStruct((2, batch, gr, ncp, LANE), BF16),
                   jax.ShapeDtypeStruct((2, batch, gr, LANE, ncp), BF16)],
        compiler_params=_cparams("arbitrary", "arbitrary"),
        name="nsa_compress",
    )(x16, wt, wb, pos, w1, b1, w2, b2)


NSA_QW = NSA_HEADS * NSA_HEAD_DIM
G3 = NSA_REP * TQ


def _nsa_in_body(h_ref, g_ref, w_ref, qt_ref, gt_ref, mq_ref):
    xn = _rms(h_ref[...], g_ref[...]).astype(BF16)
    for j in range(NSA_QW // LANE):
        y = _dot(xn, w_ref[:, j * LANE:(j + 1) * LANE]) * (NSA_HEAD_DIM ** -0.5 * LOG2E)
        qt_ref[j * LANE:(j + 1) * LANE, :] = y.T.astype(BF16)
    gt_ref[...] = _sigmoid(_dot(xn, w_ref[:, NSA_QW:NSA_QW + LANE])).T
    mq_ref[...] = _dot(xn, w_ref[:, NSA_QW + LANE:]).astype(BF16)


def _nsa_in_proj(h, g, w):
    t = h.shape[0]
    tm = ROW_TILE
    row = lambda n: pl.BlockSpec((tm, n), lambda i: (i, 0))
    col = lambda n: pl.BlockSpec((n, tm), lambda i: (0, i))
    full = lambda a: pl.BlockSpec(a.shape, lambda i: (0,) * a.ndim)
    return pl.pallas_call(
        _nsa_in_body,
        grid=(t // tm,),
        in_specs=[row(D_MODEL), full(g), full(w)],
        out_specs=[col(NSA_QW), col(LANE), row(MEM_QW)],
        out_shape=[jax.ShapeDtypeStruct((NSA_QW, t), BF16), jax.ShapeDtypeStruct((LANE, t), F32),
                   jax.ShapeDtypeStruct((t, MEM_QW), BF16)],
        compiler_params=_cparams("parallel"),
        name="nsa_in_proj",
    )(h, g, w)


def _nsa_attn_body(qt_ref, gt_ref, kc_ref, vct_ref, ks_ref, kw_ref, vt_ref, ovt_ref, tz_ref, cb_ref,
                   o_ref, sc_ref, qa_ref, acc_ref, ot_ref, *, nsb, ncp, n_sel):
    n = pl.program_id(1)
    t0 = n * TQ
    dh = NSA_HEAD_DIM
    tq = t0 + (lax.broadcasted_iota(jnp.int32, (ncp, G3), 1) & (TQ - 1))
    kc = lax.broadcasted_iota(jnp.int32, (ncp, G3), 0)
    cmask = tq >= kc * CMP_STRIDE + (CMP_BLOCK - 1)
    cstart = pl.multiple_of(ncp - lax.rem(n * (TQ // CMP_STRIDE), ncp), SUBLANE)
    jj = lax.broadcasted_iota(jnp.int32, (nsb, TQ), 0)
    cur = lax.shift_right_logical(t0 + lax.broadcasted_iota(jnp.int32, (nsb, TQ), 1), int(math.log2(SEL_BLOCK)))
    forced = (jj == 0) | (jj == cur) | (jj == cur - 1)
    zpad = jnp.zeros((dh, G3), BF16)
    nwt = WINDOW // TQ

    def scores(k_ref, gi, first, count, tz_index, qa):
        koff = pl.multiple_of(first * TQ, TQ)
        s = _dot(k_ref[gi, pl.ds(koff, count * TQ), :], qa)
        return [s[i * TQ:(i + 1) * TQ] + tz_ref[gi, tz_index(n - (first + i))] for i in range(count)]

    def col_max(parts):
        mx = parts[0]
        for x in parts[1:]:
            mx = jnp.maximum(mx, x)
        return jnp.max(mx, axis=0, keepdims=True)

    def probs(parts, m):
        ps = [jnp.exp2(x - m) for x in parts]
        tot = ps[0]
        for x in ps[1:]:
            tot = tot + x
        return jnp.concatenate([x.astype(BF16) for x in ps], axis=0), jnp.sum(tot, axis=0, keepdims=True)

    def values(gi, first, count, p):
        koff = pl.multiple_of(first * TQ, TQ)
        return _dot(vt_ref[gi, :, pl.ds(koff, count * TQ)], p)

    sel_index = lambda d: jnp.where(d < 0, 2, jnp.minimum(d, 2))
    win_index = lambda d: jnp.where(d < 0, 4, jnp.where(d == nwt, 3, jnp.minimum(d, 2)))

    groups = range(NSA_GROUPS)
    gate = lambda h, c: gt_ref[pl.ds(h * 3 + c, 1), :]
    head_lanes = lambda r: slice(r * TQ, (r + 1) * TQ)

    q3s, scs = [], []
    for gi in groups:
        h0 = gi * NSA_REP
        q3 = jnp.concatenate([qt_ref[(h0 + r) * dh:(h0 + r + 1) * dh, :] for r in range(NSA_REP)], axis=1)
        s = _dot(kc_ref[gi], jnp.concatenate([q3, zpad], axis=0)) + cb_ref[gi, pl.ds(cstart, ncp), :]
        s = jnp.where(cmask, s, NEG)
        p = jnp.where(cmask, jnp.exp2(s - jnp.max(s, axis=0, keepdims=True)), 0.0)
        p = p * (1.0 / jnp.maximum(jnp.sum(p, axis=0, keepdims=True), 1e-30))
        oc = _dot(vct_ref[gi], p.astype(BF16))
        for r in range(NSA_REP):
            ot_ref[(h0 + r) * dh:(h0 + r + 1) * dh, :] = gate(h0 + r, 0) * oc[0:dh, head_lanes(r)]
        psum = p[:, 0:TQ]
        for r in range(1, NSA_REP):
            psum = psum + p[:, head_lanes(r)]
        p1 = psum.astype(BF16)
        p2 = (psum - p1.astype(F32)).astype(BF16)
        imp = _dot(ovt_ref[...], p1) + _dot(ovt_ref[...], p2)
        score = jnp.where(forced, 1e4, jnp.where(jj <= cur, imp[0:nsb], -1.0))
        sc_ref[gi] = score
        q3s.append(q3)
        scs.append(score)

    def rank_step(i4, cnts):
        cnts = list(cnts)
        for u in range(RANK_UNROLL):
            i = i4 * RANK_UNROLL + u
            for gi in groups:
                rowv = sc_ref[gi, pl.ds(i, 1), :]
                before = (rowv > scs[gi]) | ((rowv == scs[gi]) & (i < jj))
                cnts[gi] = cnts[gi] + jnp.where(before, 1.0, 0.0)
        return tuple(cnts)
    rank_trips = jnp.minimum((2 * n + 2 + RANK_UNROLL - 1) // RANK_UNROLL, nsb // RANK_UNROLL)
    cnts = lax.fori_loop(0, rank_trips, rank_step, tuple(jnp.zeros((nsb, TQ), F32) for _ in groups))
    for gi in groups:
        selneg = jnp.where((cnts[gi] < n_sel) & (jj <= cur), 0.0, NEG).astype(BF16)
        if nsb < SEL_BLOCK:
            selneg = jnp.concatenate([selneg, jnp.zeros((SEL_BLOCK - nsb, TQ), BF16)], axis=0)
        qa_ref[gi] = jnp.concatenate([q3s[gi], jnp.concatenate([selneg] * NSA_REP, axis=1)], axis=0)

    acc_ref[...] = jnp.zeros_like(acc_ref)

    def sel_step(it, carry):
        out = []
        for gi in groups:
            m, l = carry[gi]
            parts = scores(ks_ref, gi, it * SEL_TILES, SEL_TILES, sel_index, qa_ref[gi])
            m2 = jnp.maximum(m, col_max(parts))
            al = jnp.exp2(m - m2)
            p, psum_k = probs(parts, m2)
            acc_ref[gi] = al * acc_ref[gi] + values(gi, it * SEL_TILES, SEL_TILES, p)
            out.append((m2, al * l + psum_k))
        return tuple(out)

    ml = lax.fori_loop(0, n // SEL_TILES + 1, sel_step,
                       tuple((jnp.full((1, G3), NEG, F32), jnp.zeros((1, G3), F32)) for _ in groups))

    wfirst = jnp.maximum(n - nwt, 0)
    for gi in groups:
        o_s = acc_ref[gi, 0:dh, :] * (1.0 / ml[gi][1])
        parts = scores(kw_ref, gi, wfirst, nwt + 1, win_index, qa_ref[gi])
        p, l_w = probs(parts, col_max(parts))
        o_w = values(gi, wfirst, nwt + 1, p)[dh:2 * dh, :] * (1.0 / l_w)
        for r in range(NSA_REP):
            h = gi * NSA_REP + r
            ot_ref[h * dh:(h + 1) * dh, :] += gate(h, 1) * o_s[:, head_lanes(r)] + gate(h, 2) * o_w[:, head_lanes(r)]

    for j in range(NSA_QW // LANE):
        o_ref[:, j * LANE:(j + 1) * LANE] = ot_ref[j * LANE:(j + 1) * LANE, :].T.astype(BF16)


def _nsa_attn(qt, gt, kc, vct, ksel, kwin, vt, ovt, tz, cb, batch):
    t = qt.shape[1]
    s = t // batch
    nt = s // TQ
    nsb = s // SEL_BLOCK
    ncp = kc.shape[-2]
    per_b = lambda a: pl.BlockSpec((None,) + a.shape[1:], lambda b, i: (b,) + (0,) * (a.ndim - 1))
    full = lambda a: pl.BlockSpec(a.shape, lambda b, i: (0,) * a.ndim)
    return pl.pallas_call(
        functools.partial(_nsa_attn_body, nsb=nsb, ncp=ncp, n_sel=min(SEL_TOPK, nsb)),
        grid=(batch, nt),
        in_specs=[pl.BlockSpec((NSA_QW, TQ), lambda b, i: (0, b * nt + i)),
                  pl.BlockSpec((LANE, TQ), lambda b, i: (0, b * nt + i)),
                  per_b(kc), per_b(vct), per_b(ksel), per_b(kwin), per_b(vt),
                  full(ovt), full(tz), full(cb)],
        out_specs=pl.BlockSpec((TQ, NSA_QW), lambda b, i: (b * nt + i, 0)),
        out_shape=jax.ShapeDtypeStruct((t, NSA_QW), BF16),
        scratch_shapes=[pltpu.VMEM((NSA_GROUPS, nsb, TQ), F32), pltpu.VMEM((NSA_GROUPS, LANE, G3), BF16),
                        pltpu.VMEM((NSA_GROUPS, LANE, G3), F32), pltpu.VMEM((NSA_QW, TQ), F32)],
        compiler_params=_cparams("parallel", "arbitrary"),
        name="nsa_attn",
    )(qt, gt, kc, vct, ksel, kwin, vt, ovt, tz, cb)


def _rel_bucket_np(dist):
    dist = np.maximum(dist, 0)
    max_exact = REL_BUCKETS // 2
    ratio = np.log(np.maximum(dist, 1).astype(np.float32) / np.float32(max_exact)) / np.float32(
        math.log(REL_MAX_DIST / max_exact))
    large = max_exact + (ratio * np.float32(REL_BUCKETS - max_exact)).astype(np.int32)
    large = np.minimum(large, REL_BUCKETS - 1)
    return np.where(dist < max_exact, dist, large).astype(np.int32)


def _group_lanes(a):
    hh, r, c = a.shape
    return a.reshape(NSA_GROUPS, NSA_REP, r, c).transpose(0, 2, 1, 3).reshape(NSA_GROUPS, r, NSA_REP * c)


def _bias_tables(rel_bias, ncp):
    k = np.arange(TQ)[:, None]
    q = np.arange(TQ)[None, :]
    tbl = rel_bias.astype(F32)
    far = jnp.broadcast_to(tbl[REL_BUCKETS - 1][:, None, None], (NSA_HEADS, TQ, TQ))
    t0 = jnp.where(jnp.asarray(k <= q)[None], jnp.moveaxis(tbl[_rel_bucket_np(q - k)], -1, 0), NEG)
    t1 = jnp.moveaxis(tbl[_rel_bucket_np(TQ + q - k)], -1, 0)
    t3 = jnp.where(jnp.asarray(k > q)[None], far, NEG)
    tz = jnp.stack([_group_lanes(x) for x in (t0, t1, far, t3, jnp.full_like(far, NEG))], axis=1)
    kk = np.arange(ncp)[:, None]
    m = np.where(kk < ncp // 2, -kk, ncp - kk)
    d = CMP_STRIDE * m + q - (CMP_BLOCK - 1)
    idx = np.where((d >= 0) & (d < REL_MAX_DIST), _rel_bucket_np(d), REL_BUCKETS - 1)
    cb = _group_lanes(jnp.moveaxis(tbl[idx], -1, 0))
    return tz * LOG2E, jnp.concatenate([cb, cb], axis=1) * LOG2E


def _overlap_table(s, ncp):
    nsb = s // SEL_BLOCK
    nc = (s - CMP_BLOCK) // CMP_STRIDE + 1
    cs = np.arange(ncp) * CMP_STRIDE
    ce = cs + CMP_BLOCK - 1
    ss = np.arange(SEL_BLOCK) * SEL_BLOCK
    ov = (cs[None, :] < ss[:, None] + SEL_BLOCK) & (ce[None, :] >= ss[:, None])
    ov &= (np.arange(ncp) < nc)[None, :] & (np.arange(SEL_BLOCK) < nsb)[:, None]
    return jnp.asarray(ov, dtype=BF16)


def kernel(x, mem, norm_mix, norm_mem, w_mem_kv, w_out, norm_ffn, w_up, conv_w, conv_b, w_down,
           gla_w_in, gla_w_gate_up, gla_b_gate, gla_out_norm, nsa_w_in, kv_norm, w_kv_shared,
           cmp_pos, cmp_w1, cmp_b1, cmp_w2, cmp_b2, rel_bias, final_norm):
    batch, seq = x.shape[0], x.shape[1]
    t = batch * seq
    h = x.reshape(t, D_MODEL)
    row = lambda v: v.reshape(1, -1).astype(F32)

    wk, wv = w_mem_kv[..., :MEM_W], w_mem_kv[..., MEM_W:]
    w_mkv = jnp.concatenate([_pad_heads(wk, MEM_HEADS, MEM_HEAD_DIM, MEM_DP),
                             _pad_heads(wv, MEM_HEADS, MEM_HEAD_DIM, MEM_DP)], axis=-1).astype(BF16)
    mem_kv_all = _mem_kv_proj(mem.reshape(-1, D_MODEL), norm_mem.reshape(DEPTH, 1, D_MODEL), w_mkv, batch)

    shared = None
    for i in range(DEPTH):
        w_o = w_out[i]
        w_o_mem = _pad_head_rows(w_o[MAIN_W:], MEM_HEADS, MEM_HEAD_DIM, MEM_DP).astype(BF16)
        if i < N_A_LAYERS:
            wi = gla_w_in[i]
            c0 = GLA_HEADS * GLA_DK
            c1 = 2 * c0
            c2 = c1 + GLA_HEADS * GLA_DV
            c3 = c2 + GLA_HEADS * GLA_DV
            c4 = c3 + GLA_RANK
            w_all = jnp.concatenate([
                _pad_heads(wi[:, :c0], GLA_HEADS, GLA_DK, GLA_DKP),
                _pad_heads(wi[:, c0:c1], GLA_HEADS, GLA_DK, GLA_DKP),
                _pad_heads(wi[:, c1:c2], GLA_HEADS, GLA_DV, GLA_DVP),
                _pad_heads(wi[:, c2:c3], GLA_HEADS, GLA_DV, GLA_DVP),
                _pad_heads(wi[:, c3:c4], 1, GLA_RANK, LANE),
                _pad_heads(wi[:, c4:], MEM_HEADS, MEM_HEAD_DIM, MEM_DP)], axis=1).astype(BF16)
            wg = jnp.pad(_pad_heads(gla_w_gate_up[i], GLA_HEADS, GLA_DK, GLA_DKP),
                         ((0, LANE - GLA_RANK), (0, 0))).astype(BF16)
            bg = row(_pad_heads(gla_b_gate[i], GLA_HEADS, GLA_DK, GLA_DKP))
            q, k, v, go, la, mq = _gla_in_proj(h, row(norm_mix[i]), w_all, wg, bg)
            on = row(jnp.pad(gla_out_norm[i], (0, GLA_DVP - GLA_DV)))
            main = _gla_mixer(q, k, la, v, go, on, batch)
            w_o_main = _pad_head_rows(w_o[:MAIN_W], GLA_HEADS, GLA_DV, GLA_DVP).astype(BF16)
        else:
            if shared is None:
                ncp = seq // CMP_STRIDE
                gr, dh = NSA_GROUPS, NSA_HEAD_DIM
                wkv = w_kv_shared.reshape(D_MODEL, 6, gr, dh)
                pair = lambda a, b: jnp.concatenate([wkv[:, a], wkv[:, b]], axis=-1).reshape(D_MODEL, gr * 2 * dh)
                slot = lambda a: _pad_heads(wkv[:, a].reshape(D_MODEL, KV_NAT), gr, dh, LANE)
                w_kv = jnp.concatenate([wkv[:, 0].reshape(D_MODEL, KV_NAT), wkv[:, 1].reshape(D_MODEL, KV_NAT),
                                        slot(2), slot(4), pair(3, 5)], axis=1).astype(BF16)
                ck, cv, ksel, kwin, vt = _kv_proj(h, row(kv_norm), w_kv, batch)
                x16 = jnp.stack([ck.reshape(batch * ncp, CMP_STRIDE * KV_NAT),
                                 cv.reshape(batch * ncp, CMP_STRIDE * KV_NAT)])
                w1 = cmp_w1.reshape(2, 2, CMP_STRIDE, dh, CMP_HIDDEN)
                eye = jnp.eye(gr, dtype=F32)
                w1x = jnp.einsum('jhldc,gk->jhlgdkc', w1, eye).reshape(2, 2, CMP_STRIDE * KV_NAT, gr * CMP_HIDDEN)
                w1x = w1x.astype(BF16)
                pos8 = jnp.broadcast_to(cmp_pos.reshape(2, 1, CMP_BLOCK * dh), (2, SUBLANE, CMP_BLOCK * dh)).astype(BF16)
                w2p = jnp.pad(cmp_w2, ((0, 0), (0, 0), (0, LANE - dh))).astype(BF16)
                b2p = jnp.pad(cmp_b2, ((0, 0), (0, LANE - dh))).reshape(2, 1, LANE).astype(F32)
                cnat, ctr = _compress(x16, w1x[:, 0], w1x[:, 1], pos8, cmp_w1.astype(BF16),
                                      cmp_b1.reshape(2, 1, CMP_HIDDEN).astype(F32), w2p, b2p, batch)
                tz, cb = _bias_tables(rel_bias, ncp)
                ov = _overlap_table(seq, ncp)
                shared = (cnat[0], ctr[1], ksel, kwin, vt, ov, tz, cb)
            wi = nsa_w_in[i - N_A_LAYERS]
            c0 = NSA_HEADS * NSA_HEAD_DIM
            c1 = c0 + NSA_HEADS * 3
            w_all = jnp.concatenate([
                wi[:, :c0],
                _pad_heads(wi[:, c0:c1], 1, NSA_HEADS * 3, LANE),
                _pad_heads(wi[:, c1:], MEM_HEADS, MEM_HEAD_DIM, MEM_DP)], axis=1).astype(BF16)
            q, gates, mq = _nsa_in_proj(h, row(norm_mix[i]), w_all)
            main = _nsa_attn(q, gates, *shared, batch)
            w_o_main = w_o[:MAIN_W].astype(BF16)
        mo = _mem_attn(mq, mem_kv_all[i], batch)
        h = _out_proj(h, main, mo, w_o_main, w_o_mem)
        h = _conv_ffn(h, row(norm_ffn[i]), w_up[i].astype(BF16), conv_w[i].astype(F32), row(conv_b[i]),
                      w_down[i].astype(BF16), batch)
    return _final_norm(h, row(final_norm)).reshape(batch, seq, D_MODEL)
```

```python
import functools
import math

import numpy as np
import jax
import jax.numpy as jnp
from jax import lax
from jax.experimental import pallas as pl
from jax.experimental.pallas import tpu as pltpu

F32 = jnp.float32
BF16 = jnp.bfloat16

D_MODEL = 1024
DEPTH = 4
N_A_LAYERS = DEPTH // 2
MEM_HEADS = 4
MEM_HEAD_DIM = 64
MEM_W = MEM_HEADS * MEM_HEAD_DIM
MAIN_W = D_MODEL - MEM_W
GLA_HEADS = 4
GLA_DV = MAIN_W // GLA_HEADS
GLA_DK = GLA_DV // 2
GLA_RANK = 16
GLA_GATE_NORM = 16.0
NSA_HEADS = 12
NSA_GROUPS = 4
NSA_HEAD_DIM = MAIN_W // NSA_HEADS
NSA_REP = NSA_HEADS // NSA_GROUPS
CMP_BLOCK = 32
CMP_STRIDE = 16
CMP_HIDDEN = 128
SEL_BLOCK = 64
SEL_TOPK = 16
WINDOW = 512
REL_BUCKETS = 32
REL_MAX_DIST = 128
FFN_DIM = 2816
CONV_WIDTH = 3
EPS = 1e-6

LANE = 128
SUBLANE = 8
VMEM_LIMIT = 56 * 1024 * 1024
GLA_DKP = LANE
GLA_DVP = 2 * LANE
MEM_DP = LANE
NEG = -1e30
TQ = 128
ROW_TILE = 512
GLA_CHUNK = 64
FFN_TILE = 256
SEL_TILES = 4
RANK_UNROLL = 4
LOG2E = math.log2(math.e)

NT = (((1,), (1,)), ((), ()))
TN = (((0,), (0,)), ((), ()))


def _cparams(*sem):
    return pltpu.CompilerParams(dimension_semantics=sem, vmem_limit_bytes=VMEM_LIMIT)


def _rms(x, g):
    return x * lax.rsqrt(jnp.mean(x * x, axis=-1, keepdims=True) + EPS) * g


def _sigmoid(x):
    return 1.0 / (1.0 + jnp.exp(-x))


def _dot(a, b):
    return jnp.dot(a, b, preferred_element_type=F32)


def _pad_heads(w, nh, d, dp):
    lead = w.shape[:-1]
    w = w.reshape(lead + (nh, d))
    w = jnp.pad(w, [(0, 0)] * len(lead) + [(0, 0), (0, dp - d)])
    return w.reshape(lead + (nh * dp,))


def _pad_head_rows(w, nh, d, dp):
    n = w.shape[-1]
    w = jnp.pad(w.reshape(nh, d, n), ((0, 0), (0, dp - d), (0, 0)))
    return w.reshape(nh * dp, n)


GLA_QW = GLA_HEADS * GLA_DKP
GLA_VW = GLA_HEADS * GLA_DVP
MEM_QW = MEM_HEADS * MEM_DP
GLA_OFF_K = GLA_QW
GLA_OFF_V = 2 * GLA_QW
GLA_OFF_G = GLA_OFF_V + GLA_VW
GLA_OFF_LR = GLA_OFF_G + GLA_VW
GLA_OFF_MQ = GLA_OFF_LR + LANE
GLA_NP = GLA_OFF_MQ + MEM_QW


def _gla_in_body(h_ref, g_ref, w_ref, wg_ref, bg_ref, q_ref, k_ref, v_ref, go_ref, la_ref, mq_ref):
    xn = _rms(h_ref[...], g_ref[...]).astype(BF16)

    def proj(lo, n):
        return _dot(xn, w_ref[:, lo:lo + n])

    q_ref[...] = proj(0, GLA_QW).astype(BF16)
    k_ref[...] = proj(GLA_OFF_K, GLA_QW).astype(BF16)
    for j in range(GLA_VW // GLA_QW):
        v_ref[:, j * GLA_QW:(j + 1) * GLA_QW] = proj(GLA_OFF_V + j * GLA_QW, GLA_QW).astype(BF16)
        go_ref[:, j * GLA_QW:(j + 1) * GLA_QW] = proj(GLA_OFF_G + j * GLA_QW, GLA_QW).astype(BF16)
    lr = proj(GLA_OFF_LR, LANE).astype(BF16)
    z = _dot(lr, wg_ref[...]) + bg_ref[...]
    la_ref[...] = (jnp.minimum(z, 0.0) - jnp.log(1.0 + jnp.exp(-jnp.abs(z)))) * (1.0 / GLA_GATE_NORM)
    mq_ref[...] = proj(GLA_OFF_MQ, MEM_QW).astype(BF16)


def _gla_in_proj(h, g, w, wg, bg):
    t = h.shape[0]
    tm = ROW_TILE
    row = lambda n: pl.BlockSpec((tm, n), lambda i: (i, 0))
    full = lambda a: pl.BlockSpec(a.shape, lambda i: (0,) * a.ndim)
    return pl.pallas_call(
        _gla_in_body,
        grid=(t // tm,),
        in_specs=[row(D_MODEL), full(g), full(w), full(wg), full(bg)],
        out_specs=[row(GLA_QW), row(GLA_QW), row(GLA_VW), row(GLA_VW), row(GLA_QW), row(MEM_QW)],
        out_shape=[jax.ShapeDtypeStruct((t, GLA_QW), BF16), jax.ShapeDtypeStruct((t, GLA_QW), BF16),
                   jax.ShapeDtypeStruct((t, GLA_VW), BF16), jax.ShapeDtypeStruct((t, GLA_VW), BF16),
                   jax.ShapeDtypeStruct((t, GLA_QW), F32), jax.ShapeDtypeStruct((t, MEM_QW), BF16)],
        compiler_params=_cparams("parallel"),
        name="gla_in_proj",
    )(h, g, w, wg, bg)


def _gla_mix_body(q_ref, k_ref, la_ref, v_ref, go_ref, on_ref, o_ref, st_ref, *, cn):
    @pl.when(pl.program_id(1) == 0)
    def _():
        st_ref[...] = jnp.zeros_like(st_ref)

    row = lax.broadcasted_iota(jnp.int32, (cn, cn), 0)
    col = lax.broadcasted_iota(jnp.int32, (cn, cn), 1)
    causal = row >= col
    tril = jnp.where(causal, 1.0, 0.0).astype(BF16)
    scale = GLA_DK ** -0.5
    for hd in range(GLA_HEADS):
        sk = slice(hd * GLA_DKP, (hd + 1) * GLA_DKP)
        sv = slice(hd * GLA_DVP, (hd + 1) * GLA_DVP)
        la = la_ref[:, sk]
        la1 = la.astype(BF16)
        r1 = la - la1.astype(F32)
        la2 = r1.astype(BF16)
        la3 = (r1 - la2.astype(F32)).astype(BF16)
        b = _dot(tril, la1) + _dot(tril, la2) + _dot(tril, la3)
        b_last = b[cn - 1:cn, :]
        q = q_ref[:, sk].astype(F32)
        k = k_ref[:, sk].astype(F32)
        qs = (q * jnp.exp(b) * scale).astype(BF16)
        ks = (k * jnp.exp(-b)).astype(BF16)
        ko = (k * jnp.exp(b_last - b)).astype(BF16)
        v = v_ref[:, sv]
        a = lax.dot_general(qs, ks, NT, preferred_element_type=F32)
        a = jnp.where(causal, a, 0.0).astype(BF16)
        st = st_ref[hd]
        o = _dot(a, v) + lax.dot_general(qs, st.astype(BF16), NT, preferred_element_type=F32)
        st_ref[hd] = st * jnp.exp(b_last) + lax.dot_general(v, ko, TN, preferred_element_type=F32)
        ms = jnp.sum(o * o, axis=-1, keepdims=True) * (1.0 / GLA_DV)
        y = o * lax.rsqrt(ms + EPS) * on_ref[...]
        g = go_ref[:, sv].astype(F32)
        o_ref[:, sv] = (y * (g * _sigmoid(g))).astype(BF16)


def _gla_mixer(q, k, la, v, go, on, batch):
    t = q.shape[0]
    cn = GLA_CHUNK
    nc = t // batch // cn
    row = lambda n: pl.BlockSpec((cn, n), lambda b, c: (b * nc + c, 0))
    return pl.pallas_call(
        functools.partial(_gla_mix_body, cn=cn),
        grid=(batch, nc),
        in_specs=[row(GLA_QW), row(GLA_QW), row(GLA_QW), row(GLA_VW), row(GLA_VW),
                  pl.BlockSpec(on.shape, lambda b, c: (0, 0))],
        out_specs=row(GLA_VW),
        out_shape=jax.ShapeDtypeStruct((t, GLA_VW), BF16),
        scratch_shapes=[pltpu.VMEM((GLA_HEADS, GLA_DVP, GLA_DKP), F32)],
        compiler_params=_cparams("arbitrary", "arbitrary"),
        name="gla_mixer",
    )(q, k, la, v, go, on)


def _mem_kv_body(m_ref, g_ref, w_ref, o_ref):
    xn = _rms(m_ref[...], g_ref[...]).astype(BF16)
    o_ref[...] = _dot(xn, w_ref[...]).astype(BF16)


def _mem_kv_proj(mem2, g, w, batch):
    nl = w.shape[0]
    m = mem2.shape[0] // batch
    n = w.shape[2]
    return pl.pallas_call(
        _mem_kv_body,
        grid=(nl, batch),
        in_specs=[pl.BlockSpec((m, D_MODEL), lambda l, b: (b, 0)),
                  pl.BlockSpec((None, 1, D_MODEL), lambda l, b: (l, 0, 0)),
                  pl.BlockSpec((None, D_MODEL, n), lambda l, b: (l, 0, 0))],
        out_specs=pl.BlockSpec((None, m, n), lambda l, b: (l, b, 0)),
        out_shape=jax.ShapeDtypeStruct((nl, mem2.shape[0], n), BF16),
        compiler_params=_cparams("arbitrary", "arbitrary"),
        name="mem_kv_proj",
    )(mem2, g, w)


def _mem_attn_body(q_ref, kv_ref, o_ref):
    for hd in range(MEM_HEADS):
        sl = slice(hd * MEM_DP, (hd + 1) * MEM_DP)
        sv = slice(MEM_QW + hd * MEM_DP, MEM_QW + (hd + 1) * MEM_DP)
        s = lax.dot_general(q_ref[:, sl], kv_ref[:, sl], NT, preferred_element_type=F32) * MEM_HEAD_DIM ** -0.5
        p = jnp.exp(s - jnp.max(s, axis=-1, keepdims=True))
        l = jnp.sum(p, axis=-1, keepdims=True)
        o_ref[:, sl] = (_dot(p.astype(BF16), kv_ref[:, sv]) / l).astype(BF16)


def _mem_attn(mq, mem_kv, batch):
    t = mq.shape[0]
    tm = ROW_TILE
    nt = t // batch // tm
    m = mem_kv.shape[0] // batch
    return pl.pallas_call(
        _mem_attn_body,
        grid=(batch, nt),
        in_specs=[pl.BlockSpec((tm, MEM_QW), lambda b, i: (b * nt + i, 0)),
                  pl.BlockSpec((m, 2 * MEM_QW), lambda b, i: (b, 0))],
        out_specs=pl.BlockSpec((tm, MEM_QW), lambda b, i: (b * nt + i, 0)),
        out_shape=jax.ShapeDtypeStruct((t, MEM_QW), BF16),
        compiler_params=_cparams("parallel", "parallel"),
        name="mem_attn",
    )(mq, mem_kv)


def _out_proj_body(h_ref, a_ref, m_ref, wa_ref, wm_ref, o_ref):
    o_ref[...] = h_ref[...] + _dot(a_ref[...], wa_ref[...]) + _dot(m_ref[...], wm_ref[...])


def _out_proj(h, main, mo, wa, wm):
    t = h.shape[0]
    tm = ROW_TILE
    row = lambda n: pl.BlockSpec((tm, n), lambda i: (i, 0))
    full = lambda a: pl.BlockSpec(a.shape, lambda i: (0,) * a.ndim)
    return pl.pallas_call(
        _out_proj_body,
        grid=(t // tm,),
        in_specs=[row(D_MODEL), row(main.shape[1]), row(mo.shape[1]), full(wa), full(wm)],
        out_specs=row(D_MODEL),
        out_shape=jax.ShapeDtypeStruct((t, D_MODEL), F32),
        compiler_params=_cparams("parallel"),
        name="out_proj",
    )(h, main, mo, wa, wm)


def _ffn_body(h_ref, hp_ref, g_ref, wup_ref, cw_ref, cb_ref, wdn_ref, o_ref, acc_ref, *, tm, tf):
    h = h_ref[...]
    g = g_ref[...]
    keep = jnp.where(pl.program_id(1) > 0, 1.0, 0.0)
    xe = jnp.concatenate([_rms(hp_ref[...], g) * keep, _rms(h, g)], axis=0).astype(BF16)
    acc_ref[...] = jnp.zeros_like(acc_ref)
    for j in range(FFN_DIM // tf):
        def conv(off):
            u = _dot(xe, wup_ref[:, off:off + tf])
            w = cw_ref[:, off:off + tf]
            out = cb_ref[:, off:off + tf]
            for c in range(CONV_WIDTH):
                lo = SUBLANE - (CONV_WIDTH - 1) + c
                out = out + w[c:c + 1, :] * u[lo:lo + tm, :]
            return out
        a = conv(j * tf)
        b = conv(FFN_DIM + j * tf)
        act = (a * _sigmoid(a) * b).astype(BF16)
        acc_ref[...] += _dot(act, wdn_ref[j * tf:(j + 1) * tf, :])
    o_ref[...] = h + acc_ref[...]


def _conv_ffn(h, g, wup, cw, cb, wdn, batch):
    t = h.shape[0]
    tm = ROW_TILE
    nt = t // batch // tm
    hb = tm // SUBLANE
    full = lambda a: pl.BlockSpec(a.shape, lambda b, i: (0,) * a.ndim)
    return pl.pallas_call(
        functools.partial(_ffn_body, tm=tm, tf=FFN_TILE),
        grid=(batch, nt),
        in_specs=[pl.BlockSpec((tm, D_MODEL), lambda b, i: (b * nt + i, 0)),
                  pl.BlockSpec((SUBLANE, D_MODEL), lambda b, i: (jnp.maximum((b * nt + i) * hb - 1, 0), 0)),
                  full(g), full(wup), full(cw), full(cb), full(wdn)],
        out_specs=pl.BlockSpec((tm, D_MODEL), lambda b, i: (b * nt + i, 0)),
        out_shape=jax.ShapeDtypeStruct((t, D_MODEL), F32),
        scratch_shapes=[pltpu.VMEM((tm, D_MODEL), F32)],
        compiler_params=_cparams("parallel", "parallel"),
        name="conv_ffn",
    )(h, h, g, wup, cw, cb, wdn)


def _final_norm_body(h_ref, g_ref, o_ref):
    o_ref[...] = _rms(h_ref[...], g_ref[...])


def _final_norm(h, g):
    t = h.shape[0]
    tm = ROW_TILE
    return pl.pallas_call(
        _final_norm_body,
        grid=(t // tm,),
        in_specs=[pl.BlockSpec((tm, D_MODEL), lambda i: (i, 0)), pl.BlockSpec(g.shape, lambda i: (0, 0))],
        out_specs=pl.BlockSpec((tm, D_MODEL), lambda i: (i, 0)),
        out_shape=jax.ShapeDtypeStruct((t, D_MODEL), F32),
        compiler_params=_cparams("parallel"),
        name="final_norm",
    )(h, g)


KV_NAT = NSA_GROUPS * NSA_HEAD_DIM


def _kv_proj_body(h_ref, g_ref, w_ref, ck_ref, cv_ref, ks_ref, kw_ref, vs_ref, vw_ref, *, tm):
    xn = _rms(h_ref[...], g_ref[...]).astype(BF16)
    ck_ref[...] = _dot(xn, w_ref[:, 0:KV_NAT]).astype(BF16)
    cv_ref[...] = _dot(xn, w_ref[:, KV_NAT:2 * KV_NAT]).astype(BF16)
    key = pl.program_id(1) * tm + lax.broadcasted_iota(jnp.int32, (tm, LANE), 0)
    lane = lax.broadcasted_iota(jnp.int32, (tm, LANE), 1)
    onehot = jnp.where(lax.shift_right_logical(key, int(math.log2(SEL_BLOCK))) == lane - NSA_HEAD_DIM, 1.0, 0.0)
    low = lane < NSA_HEAD_DIM
    for gi in range(NSA_GROUPS):
        slot = lambda n: _dot(xn, w_ref[:, 2 * KV_NAT + (n * NSA_GROUPS + gi) * LANE:
                                           2 * KV_NAT + (n * NSA_GROUPS + gi + 1) * LANE])
        ks_ref[gi] = jnp.where(low, slot(0), onehot).astype(BF16)
        kw_ref[gi] = jnp.where(low, slot(1), 0.0).astype(BF16)
        vt = slot(2).T.astype(BF16)
        ones = jnp.ones((NSA_HEAD_DIM, tm), BF16)
        vs_ref[gi] = jnp.concatenate([vt[0:NSA_HEAD_DIM], ones], axis=0)
        vw_ref[gi] = jnp.concatenate([vt[NSA_HEAD_DIM:], ones], axis=0)


def _kv_proj(h, g, w, batch):
    t = h.shape[0]
    s = t // batch
    tm = ROW_TILE
    nt = s // tm
    gr = NSA_GROUPS
    return pl.pallas_call(
        functools.partial(_kv_proj_body, tm=tm),
        grid=(batch, nt),
        in_specs=[pl.BlockSpec((tm, D_MODEL), lambda b, i: (b * nt + i, 0)),
                  pl.BlockSpec(g.shape, lambda b, i: (0, 0)),
                  pl.BlockSpec(w.shape, lambda b, i: (0, 0))],
        out_specs=[pl.BlockSpec((tm, KV_NAT), lambda b, i: (b * nt + i, 0)),
                   pl.BlockSpec((tm, KV_NAT), lambda b, i: (b * nt + i, 0)),
                   pl.BlockSpec((None, gr, tm, LANE), lambda b, i: (b, 0, i, 0)),
                   pl.BlockSpec((None, gr, tm, LANE), lambda b, i: (b, 0, i, 0)),
                   pl.BlockSpec((None, gr, LANE, tm), lambda b, i: (b, 0, 0, i)),
                   pl.BlockSpec((None, gr, LANE, tm), lambda b, i: (b, 0, 0, i))],
        out_shape=[jax.ShapeDtypeStruct((t, KV_NAT), BF16), jax.ShapeDtypeStruct((t, KV_NAT), BF16),
                   jax.ShapeDtypeStruct((batch, gr, s, LANE), BF16),
                   jax.ShapeDtypeStruct((batch, gr, s, LANE), BF16),
                   jax.ShapeDtypeStruct((batch, gr, LANE, s), BF16),
                   jax.ShapeDtypeStruct((batch, gr, LANE, s), BF16)],
        compiler_params=_cparams("parallel", "parallel"),
        name="nsa_kv_proj",
    )(h, g, w)


def _compress_body(x_ref, wt_ref, wb_ref, pos_ref, w1_ref, b1_ref, w2_ref, b2_ref, on_ref, ot_ref, *, ncp):
    x = x_ref[...]
    top = _dot(x, wt_ref[...])
    bot = _dot(x, wb_ref[...])
    posb = _dot(pos_ref[...], w1_ref[...])[0:1, :] + b1_ref[...]
    for gi in range(NSA_GROUPS):
        sl = slice(gi * CMP_HIDDEN, (gi + 1) * CMP_HIDDEN)
        hid = top[:, sl] + pltpu.roll(bot[:, sl], ncp - 1, 0) + posb
        hid = (hid * _sigmoid(hid)).astype(BF16)
        out = _dot(hid, w2_ref[...]) + b2_ref[...]
        on_ref[gi] = out.astype(BF16)
        ot_ref[gi] = out.T.astype(BF16)


def _compress(x16, wt, wb, pos, w1, b1, w2, b2, batch):
    ncp = x16.shape[1] // batch
    gr = NSA_GROUPS
    per_j = lambda a: pl.BlockSpec((None,) + a.shape[1:], lambda j, b: (j,) + (0,) * (a.ndim - 1))
    return pl.pallas_call(
        functools.partial(_compress_body, ncp=ncp),
        grid=(2, batch),
        in_specs=[pl.BlockSpec((None, ncp, x16.shape[2]), lambda j, b: (j, b, 0)),
                  per_j(wt), per_j(wb), per_j(pos), per_j(w1), per_j(b1), per_j(w2), per_j(b2)],
        out_specs=[pl.BlockSpec((None, None, gr, ncp, LANE), lambda j, b: (j, b, 0, 0, 0)),
                   pl.BlockSpec((None, None, gr, LANE, ncp), lambda j, b: (j, b, 0, 0, 0))],
        out_shape=[jax.ShapeDtypeStruct((2, batch, gr, ncp, LANE), BF16),
                   jax.ShapeDtypeStruct((2, batch, gr, LANE, ncp), BF16)],
        compiler_params=_cparams("arbitrary", "arbitrary"),
        name="nsa_compress",
    )(x16, wt, wb, pos, w1, b1, w2, b2)


NSA_QW = NSA_HEADS * NSA_HEAD_DIM
G3 = NSA_REP * TQ


def _nsa_in_body(h_ref, g_ref, w_ref, qt_ref, gt_ref, mq_ref):
    xn = _rms(h_ref[...], g_ref[...]).astype(BF16)
    for j in range(NSA_QW // LANE):
        y = _dot(xn, w_ref[:, j * LANE:(j + 1) * LANE]) * (NSA_HEAD_DIM ** -0.5 * LOG2E)
        qt_ref[j * LANE:(j + 1) * LANE, :] = y.T.astype(BF16)
    gt_ref[...] = _sigmoid(_dot(xn, w_ref[:, NSA_QW:NSA_QW + LANE])).T
    mq_ref[...] = _dot(xn, w_ref[:, NSA_QW + LANE:]).astype(BF16)


def _nsa_in_proj(h, g, w):
    t = h.shape[0]
    tm = ROW_TILE
    row = lambda n: pl.BlockSpec((tm, n), lambda i: (i, 0))
    col = lambda n: pl.BlockSpec((n, tm), lambda i: (0, i))
    full = lambda a: pl.BlockSpec(a.shape, lambda i: (0,) * a.ndim)
    return pl.pallas_call(
        _nsa_in_body,
        grid=(t // tm,),
        in_specs=[row(D_MODEL), full(g), full(w)],
        out_specs=[col(NSA_QW), col(LANE), row(MEM_QW)],
        out_shape=[jax.ShapeDtypeStruct((NSA_QW, t), BF16), jax.ShapeDtypeStruct((LANE, t), F32),
                   jax.ShapeDtypeStruct((t, MEM_QW), BF16)],
        compiler_params=_cparams("parallel"),
        name="nsa_in_proj",
    )(h, g, w)


def _nsa_attn_body(qt_ref, gt_ref, kc_ref, vct_ref, ks_ref, kw_ref, vs_ref, vw_ref, ovt_ref, tz_ref, cb_ref,
                   o_ref, sc_ref, qa_ref, acc_ref, ot_ref, s0_ref, s1_ref, bm0_ref, bm1_ref, m_ref, *, nsb, ncp, n_sel):
    n = pl.program_id(1)
    t0 = n * TQ
    dh = NSA_HEAD_DIM
    tq = t0 + (lax.broadcasted_iota(jnp.int32, (ncp, G3), 1) & (TQ - 1))
    kc = lax.broadcasted_iota(jnp.int32, (ncp, G3), 0)
    cmask = tq >= kc * CMP_STRIDE + (CMP_BLOCK - 1)
    cstart = pl.multiple_of(ncp - lax.rem(n * (TQ // CMP_STRIDE), ncp), SUBLANE)
    jj = lax.broadcasted_iota(jnp.int32, (nsb, TQ), 0)
    cur = lax.shift_right_logical(t0 + lax.broadcasted_iota(jnp.int32, (nsb, TQ), 1), int(math.log2(SEL_BLOCK)))
    forced = (jj == 0) | (jj == cur) | (jj == cur - 1)
    zpad = jnp.zeros((dh, G3), BF16)
    nwt = WINDOW // TQ

    def scores(k_ref, gi, first, count, tz_index, qa):
        koff = pl.multiple_of(first * TQ, TQ)
        s = _dot(k_ref[gi, pl.ds(koff, count * TQ), :], qa)
        return [s[i * TQ:(i + 1) * TQ] + tz_ref[gi, tz_index(n - (first + i))] for i in range(count)]

    def col_max(parts):
        mx = parts[0]
        for x in parts[1:]:
            mx = jnp.maximum(mx, x)
        return jnp.max(mx, axis=0, keepdims=True)

    def probs(parts, m):
        return jnp.concatenate([jnp.exp2(x - m).astype(BF16) for x in parts], axis=0)

    def values(v_ref, gi, first, count, p):
        koff = pl.multiple_of(first * TQ, TQ)
        return _dot(v_ref[gi, :, pl.ds(koff, count * TQ)], p)

    sel_index = lambda d: jnp.where(d < 0, 2, jnp.minimum(d, 2))
    win_index = lambda d: jnp.where(d < 0, 4, jnp.where(d == nwt, 3, jnp.minimum(d, 2)))

    groups = range(NSA_GROUPS)
    gate = lambda h, c: gt_ref[pl.ds(h * 3 + c, 1), :]
    head_lanes = lambda r: slice(r * TQ, (r + 1) * TQ)

    q3s, scs = [], []
    for gi in groups:
        h0 = gi * NSA_REP
        q3 = jnp.concatenate([qt_ref[(h0 + r) * dh:(h0 + r + 1) * dh, :] for r in range(NSA_REP)], axis=1)
        s = _dot(kc_ref[gi], jnp.concatenate([q3, zpad], axis=0)) + cb_ref[gi, pl.ds(cstart, ncp), :]
        s = jnp.where(cmask, s, NEG)
        p = jnp.where(cmask, jnp.exp2(s - jnp.max(s, axis=0, keepdims=True)), 0.0)
        p = p * (1.0 / jnp.maximum(jnp.sum(p, axis=0, keepdims=True), 1e-30))
        oc = _dot(vct_ref[gi], p.astype(BF16))
        for r in range(NSA_REP):
            ot_ref[(h0 + r) * dh:(h0 + r + 1) * dh, :] = gate(h0 + r, 0) * oc[0:dh, head_lanes(r)]
        psum = p[:, 0:TQ]
        for r in range(1, NSA_REP):
            psum = psum + p[:, head_lanes(r)]
        p1 = psum.astype(BF16)
        p2 = (psum - p1.astype(F32)).astype(BF16)
        imp = _dot(ovt_ref[...], p1) + _dot(ovt_ref[...], p2)
        score = jnp.where(forced, 1e4, jnp.where(jj <= cur, imp[0:nsb], -1.0))
        sc_ref[gi] = score
        q3s.append(q3)
        scs.append(score)

    def rank_step(i4, cnts):
        cnts = list(cnts)
        for u in range(RANK_UNROLL):
            i = i4 * RANK_UNROLL + u
            for gi in groups:
                rowv = sc_ref[gi, pl.ds(i, 1), :]
                before = (rowv > scs[gi]) | ((rowv == scs[gi]) & (i < jj))
                cnts[gi] = cnts[gi] + jnp.where(before, 1.0, 0.0)
        return tuple(cnts)
    rank_trips = jnp.minimum((2 * n + 2 + RANK_UNROLL - 1) // RANK_UNROLL, nsb // RANK_UNROLL)
    cnts = lax.fori_loop(0, rank_trips, rank_step, tuple(jnp.zeros((nsb, TQ), F32) for _ in groups))
    for gi in groups:
        selneg = jnp.where((cnts[gi] < n_sel) & (jj <= cur), 0.0, NEG).astype(BF16)
        if nsb < SEL_BLOCK:
            selneg = jnp.concatenate([selneg, jnp.zeros((SEL_BLOCK - nsb, TQ), BF16)], axis=0)
        qa_ref[gi] = jnp.concatenate([q3s[gi], jnp.concatenate([selneg] * NSA_REP, axis=1)], axis=0)

    acc_ref[...] = jnp.zeros_like(acc_ref)
    m_ref[...] = jnp.full(m_ref.shape, NEG, F32)
    gshift = int(math.log2(NSA_GROUPS))
    s_refs, bm_refs = (s0_ref, s1_ref), (bm0_ref, bm1_ref)

    def score_item(j, slot):
        gi, it = j & (NSA_GROUPS - 1), lax.shift_right_logical(j, gshift)
        parts = scores(ks_ref, gi, it * SEL_TILES, SEL_TILES, sel_index, qa_ref[gi])
        for i, x in enumerate(parts):
            s_refs[slot][i * TQ:(i + 1) * TQ, :] = x
        bm_refs[slot][0:1, :] = col_max(parts)

    def finish_item(j, slot):
        gi, it = j & (NSA_GROUPS - 1), lax.shift_right_logical(j, gshift)
        m = m_ref[gi, 0:1, :]
        m2 = jnp.maximum(m, bm_refs[slot][0:1, :])
        p = probs([s_refs[slot][i * TQ:(i + 1) * TQ, :] for i in range(SEL_TILES)], m2)
        acc_ref[gi] = jnp.exp2(m - m2) * acc_ref[gi] + values(vs_ref, gi, it * SEL_TILES, SEL_TILES, p)
        m_ref[gi, 0:1, :] = m2

    n_items = (n // SEL_TILES + 1) * NSA_GROUPS
    score_item(0, 0)

    def sel_step(jp, carry):
        j = NSA_GROUPS * jp
        for u in range(NSA_GROUPS):
            nxt = j + u + 1 if u + 1 < NSA_GROUPS else jnp.minimum(j + u + 1, n_items - 1)
            score_item(nxt, (u + 1) % 2)
            finish_item(j + u, u % 2)
        return carry

    lax.fori_loop(0, n // SEL_TILES + 1, sel_step, 0)

    wfirst = jnp.maximum(n - nwt, 0)
    for gi in groups:
        o_s = acc_ref[gi, 0:dh, :] * (1.0 / acc_ref[gi, dh:dh + 1, :])
        parts = scores(kw_ref, gi, wfirst, nwt + 1, win_index, qa_ref[gi])
        ow = values(vw_ref, gi, wfirst, nwt + 1, probs(parts, col_max(parts)))
        o_w = ow[0:dh, :] * (1.0 / ow[dh:dh + 1, :])
        for r in range(NSA_REP):
            h = gi * NSA_REP + r
            ot_ref[h * dh:(h + 1) * dh, :] += gate(h, 1) * o_s[:, head_lanes(r)] + gate(h, 2) * o_w[:, head_lanes(r)]

    for j in range(NSA_QW // LANE):
        o_ref[:, j * LANE:(j + 1) * LANE] = ot_ref[j * LANE:(j + 1) * LANE, :].T.astype(BF16)


def _nsa_attn(qt, gt, kc, vct, ksel, kwin, vsel, vwin, ovt, tz, cb, batch):
    t = qt.shape[1]
    s = t // batch
    nt = s // TQ
    nsb = s // SEL_BLOCK
    ncp = kc.shape[-2]
    per_b = lambda a: pl.BlockSpec((None,) + a.shape[1:], lambda b, i: (b,) + (0,) * (a.ndim - 1))
    full = lambda a: pl.BlockSpec(a.shape, lambda b, i: (0,) * a.ndim)
    return pl.pallas_call(
        functools.partial(_nsa_attn_body, nsb=nsb, ncp=ncp, n_sel=min(SEL_TOPK, nsb)),
        grid=(batch, nt),
        in_specs=[pl.BlockSpec((NSA_QW, TQ), lambda b, i: (0, b * nt + i)),
                  pl.BlockSpec((LANE, TQ), lambda b, i: (0, b * nt + i)),
                  per_b(kc), per_b(vct), per_b(ksel), per_b(kwin), per_b(vsel), per_b(vwin),
                  full(ovt), full(tz), full(cb)],
        out_specs=pl.BlockSpec((TQ, NSA_QW), lambda b, i: (b * nt + i, 0)),
        out_shape=jax.ShapeDtypeStruct((t, NSA_QW), BF16),
        scratch_shapes=[pltpu.VMEM((NSA_GROUPS, nsb, TQ), F32), pltpu.VMEM((NSA_GROUPS, LANE, G3), BF16),
                        pltpu.VMEM((NSA_GROUPS, LANE, G3), F32), pltpu.VMEM((NSA_QW, TQ), F32),
                        pltpu.VMEM((SEL_TILES * TQ, G3), F32), pltpu.VMEM((SEL_TILES * TQ, G3), F32),
                        pltpu.VMEM((SUBLANE, G3), F32), pltpu.VMEM((SUBLANE, G3), F32),
                        pltpu.VMEM((NSA_GROUPS, SUBLANE, G3), F32)],
        compiler_params=_cparams("parallel", "arbitrary"),
        name="nsa_attn",
    )(qt, gt, kc, vct, ksel, kwin, vsel, vwin, ovt, tz, cb)


def _rel_bucket_np(dist):
    dist = np.maximum(dist, 0)
    max_exact = REL_BUCKETS // 2
    ratio = np.log(np.maximum(dist, 1).astype(np.float32) / np.float32(max_exact)) / np.float32(
        math.log(REL_MAX_DIST / max_exact))
    large = max_exact + (ratio * np.float32(REL_BUCKETS - max_exact)).astype(np.int32)
    large = np.minimum(large, REL_BUCKETS - 1)
    return np.where(dist < max_exact, dist, large).astype(np.int32)


def _group_lanes(a):
    hh, r, c = a.shape
    return a.reshape(NSA_GROUPS, NSA_REP, r, c).transpose(0, 2, 1, 3).reshape(NSA_GROUPS, r, NSA_REP * c)


def _bias_tables(rel_bias, ncp):
    k = np.arange(TQ)[:, None]
    q = np.arange(TQ)[None, :]
    tbl = rel_bias.astype(F32)

    def lookup(idx):
        onehot = (jnp.asarray(idx.reshape(1, -1)) == jnp.arange(REL_BUCKETS)[:, None]).astype(F32)
        out = jnp.dot(tbl.T, onehot, precision=lax.Precision.HIGHEST)
        return out.reshape((NSA_HEADS,) + idx.shape)

    far = jnp.broadcast_to(tbl[REL_BUCKETS - 1][:, None, None], (NSA_HEADS, TQ, TQ))
    t0 = jnp.where(jnp.asarray(k <= q)[None], lookup(_rel_bucket_np(q - k)), NEG)
    t1 = lookup(_rel_bucket_np(TQ + q - k))
    t3 = jnp.where(jnp.asarray(k > q)[None], far, NEG)
    tz = jnp.stack([_group_lanes(x) for x in (t0, t1, far, t3, jnp.full_like(far, NEG))], axis=1)
    kk = np.arange(ncp)[:, None]
    m = np.where(kk < ncp // 2, -kk, ncp - kk)
    d = CMP_STRIDE * m + q - (CMP_BLOCK - 1)
    idx = np.where((d >= 0) & (d < REL_MAX_DIST), _rel_bucket_np(d), REL_BUCKETS - 1)
    cb = _group_lanes(lookup(idx))
    return tz * LOG2E, jnp.concatenate([cb, cb], axis=1) * LOG2E


def _overlap_table(s, ncp):
    nsb = s // SEL_BLOCK
    nc = (s - CMP_BLOCK) // CMP_STRIDE + 1
    cs = np.arange(ncp) * CMP_STRIDE
    ce = cs + CMP_BLOCK - 1
    ss = np.arange(SEL_BLOCK) * SEL_BLOCK
    ov = (cs[None, :] < ss[:, None] + SEL_BLOCK) & (ce[None, :] >= ss[:, None])
    ov &= (np.arange(ncp) < nc)[None, :] & (np.arange(SEL_BLOCK) < nsb)[:, None]
    return jnp.asarray(ov, dtype=BF16)


def kernel(x, mem, norm_mix, norm_mem, w_mem_kv, w_out, norm_ffn, w_up, conv_w, conv_b, w_down,
           gla_w_in, gla_w_gate_up, gla_b_gate, gla_out_norm, nsa_w_in, kv_norm, w_kv_shared,
           cmp_pos, cmp_w1, cmp_b1, cmp_w2, cmp_b2, rel_bias, final_norm):
    batch, seq = x.shape[0], x.shape[1]
    t = batch * seq
    h = x.reshape(t, D_MODEL)
    row = lambda v: v.reshape(1, -1).astype(F32)

    wk, wv = w_mem_kv[..., :MEM_W], w_mem_kv[..., MEM_W:]
    w_mkv = jnp.concatenate([_pad_heads(wk, MEM_HEADS, MEM_HEAD_DIM, MEM_DP),
                             _pad_heads(wv, MEM_HEADS, MEM_HEAD_DIM, MEM_DP)], axis=-1).astype(BF16)
    mem_kv_all = _mem_kv_proj(mem.reshape(-1, D_MODEL), norm_mem.reshape(DEPTH, 1, D_MODEL), w_mkv, batch)

    shared = None
    for i in range(DEPTH):
        w_o = w_out[i]
        w_o_mem = _pad_head_rows(w_o[MAIN_W:], MEM_HEADS, MEM_HEAD_DIM, MEM_DP).astype(BF16)
        if i < N_A_LAYERS:
            wi = gla_w_in[i]
            c0 = GLA_HEADS * GLA_DK
            c1 = 2 * c0
            c2 = c1 + GLA_HEADS * GLA_DV
            c3 = c2 + GLA_HEADS * GLA_DV
            c4 = c3 + GLA_RANK
            w_all = jnp.concatenate([
                _pad_heads(wi[:, :c0], GLA_HEADS, GLA_DK, GLA_DKP),
                _pad_heads(wi[:, c0:c1], GLA_HEADS, GLA_DK, GLA_DKP),
                _pad_heads(wi[:, c1:c2], GLA_HEADS, GLA_DV, GLA_DVP),
                _pad_heads(wi[:, c2:c3], GLA_HEADS, GLA_DV, GLA_DVP),
                _pad_heads(wi[:, c3:c4], 1, GLA_RANK, LANE),
                _pad_heads(wi[:, c4:], MEM_HEADS, MEM_HEAD_DIM, MEM_DP)], axis=1).astype(BF16)
            wg = jnp.pad(_pad_heads(gla_w_gate_up[i], GLA_HEADS, GLA_DK, GLA_DKP),
                         ((0, LANE - GLA_RANK), (0, 0))).astype(BF16)
            bg = row(_pad_heads(gla_b_gate[i], GLA_HEADS, GLA_DK, GLA_DKP))
            q, k, v, go, la, mq = _gla_in_proj(h, row(norm_mix[i]), w_all, wg, bg)
            on = row(jnp.pad(gla_out_norm[i], (0, GLA_DVP - GLA_DV)))
            main = _gla_mixer(q, k, la, v, go, on, batch)
            w_o_main = _pad_head_rows(w_o[:MAIN_W], GLA_HEADS, GLA_DV, GLA_DVP).astype(BF16)
        else:
            if shared is None:
                ncp = seq // CMP_STRIDE
                gr, dh = NSA_GROUPS, NSA_HEAD_DIM
                wkv = w_kv_shared.reshape(D_MODEL, 6, gr, dh)
                pair = lambda a, b: jnp.concatenate([wkv[:, a], wkv[:, b]], axis=-1).reshape(D_MODEL, gr * 2 * dh)
                slot = lambda a: _pad_heads(wkv[:, a].reshape(D_MODEL, KV_NAT), gr, dh, LANE)
                w_kv = jnp.concatenate([wkv[:, 0].reshape(D_MODEL, KV_NAT), wkv[:, 1].reshape(D_MODEL, KV_NAT),
                                        slot(2), slot(4), pair(3, 5)], axis=1).astype(BF16)
                ck, cv, ksel, kwin, vsel, vwin = _kv_proj(h, row(kv_norm), w_kv, batch)
                x16 = jnp.stack([ck.reshape(batch * ncp, CMP_STRIDE * KV_NAT),
                                 cv.reshape(batch * ncp, CMP_STRIDE * KV_NAT)])
                w1 = cmp_w1.reshape(2, 2, CMP_STRIDE, dh, CMP_HIDDEN)
                eye = jnp.eye(gr, dtype=F32)
                w1x = jnp.einsum('jhldc,gk->jhlgdkc', w1, eye).reshape(2, 2, CMP_STRIDE * KV_NAT, gr * CMP_HIDDEN)
                w1x = w1x.astype(BF16)
                pos8 = jnp.broadcast_to(cmp_pos.reshape(2, 1, CMP_BLOCK * dh), (2, SUBLANE, CMP_BLOCK * dh)).astype(BF16)
                w2p = jnp.pad(cmp_w2, ((0, 0), (0, 0), (0, LANE - dh))).astype(BF16)
                b2p = jnp.pad(cmp_b2, ((0, 0), (0, LANE - dh))).reshape(2, 1, LANE).astype(F32)
                cnat, ctr = _compress(x16, w1x[:, 0], w1x[:, 1], pos8, cmp_w1.astype(BF16),
                                      cmp_b1.reshape(2, 1, CMP_HIDDEN).astype(F32), w2p, b2p, batch)
                tz, cb = _bias_tables(rel_bias, ncp)
                ov = _overlap_table(seq, ncp)
                shared = (cnat[0], ctr[1], ksel, kwin, vsel, vwin, ov, tz, cb)
            wi = nsa_w_in[i - N_A_LAYERS]
            c0 = NSA_HEADS * NSA_HEAD_DIM
            c1 = c0 + NSA_HEADS * 3
            w_all = jnp.concatenate([
                wi[:, :c0],
                _pad_heads(wi[:, c0:c1], 1, NSA_HEADS * 3, LANE),
                _pad_heads(wi[:, c1:], MEM_HEADS, MEM_HEAD_DIM, MEM_DP)], axis=1).astype(BF16)
            q, gates, mq = _nsa_in_proj(h, row(norm_mix[i]), w_all)
            main = _nsa_attn(q, gates, *shared, batch)
            w_o_main = w_o[:MAIN_W].astype(BF16)
        mo = _mem_attn(mq, mem_kv_all[i], batch)
        h = _out_proj(h, main, mo, w_o_main, w_o_mem)
        h = _conv_ffn(h, row(norm_ffn[i]), w_up[i].astype(BF16), conv_w[i].astype(F32), row(conv_b[i]),
                      w_down[i].astype(BF16), batch)
    return _final_norm(h, row(final_norm)).reshape(batch, seq, D_MODEL)
```

```python
import functools
import math

import numpy as np
import jax
import jax.numpy as jnp
from jax import lax
from jax.experimental import pallas as pl
from jax.experimental.pallas import tpu as pltpu

F32 = jnp.float32
BF16 = jnp.bfloat16

D_MODEL = 1024
DEPTH = 4
N_A_LAYERS = DEPTH // 2
MEM_HEADS = 4
MEM_HEAD_DIM = 64
MEM_W = MEM_HEADS * MEM_HEAD_DIM
MAIN_W = D_MODEL - MEM_W
GLA_HEADS = 4
GLA_DV = MAIN_W // GLA_HEADS
GLA_DK = GLA_DV // 2
GLA_RANK = 16
GLA_GATE_NORM = 16.0
NSA_HEADS = 12
NSA_GROUPS = 4
NSA_HEAD_DIM = MAIN_W // NSA_HEADS
NSA_REP = NSA_HEADS // NSA_GROUPS
CMP_BLOCK = 32
CMP_STRIDE = 16
CMP_HIDDEN = 128
SEL_BLOCK = 64
SEL_TOPK = 16
WINDOW = 512
REL_BUCKETS = 32
REL_MAX_DIST = 128
FFN_DIM = 2816
CONV_WIDTH = 3
EPS = 1e-6

LANE = 128
SUBLANE = 8
VMEM_LIMIT = 56 * 1024 * 1024
GLA_DKP = LANE
GLA_DVP = 2 * LANE
MEM_DP = LANE
NEG = -1e30
TQ = 128
ROW_TILE = 512
GLA_CHUNK = 64
FFN_TILE = 256
SEL_TILES = 4
V_ROWS = 64 + 16
RANK_UNROLL = 4
LOG2E = math.log2(math.e)

NT = (((1,), (1,)), ((), ()))
TN = (((0,), (0,)), ((), ()))


def _cparams(*sem):
    return pltpu.CompilerParams(dimension_semantics=sem, vmem_limit_bytes=VMEM_LIMIT)


def _rms(x, g):
    return x * lax.rsqrt(jnp.mean(x * x, axis=-1, keepdims=True) + EPS) * g


def _sigmoid(x):
    return 1.0 / (1.0 + jnp.exp(-x))


def _dot(a, b):
    return jnp.dot(a, b, preferred_element_type=F32)


def _pad_heads(w, nh, d, dp):
    lead = w.shape[:-1]
    w = w.reshape(lead + (nh, d))
    w = jnp.pad(w, [(0, 0)] * len(lead) + [(0, 0), (0, dp - d)])
    return w.reshape(lead + (nh * dp,))


def _pad_head_rows(w, nh, d, dp):
    n = w.shape[-1]
    w = jnp.pad(w.reshape(nh, d, n), ((0, 0), (0, dp - d), (0, 0)))
    return w.reshape(nh * dp, n)


GLA_QW = GLA_HEADS * GLA_DKP
GLA_VW = GLA_HEADS * GLA_DVP
MEM_QW = MEM_HEADS * MEM_DP
GLA_OFF_K = GLA_QW
GLA_OFF_V = 2 * GLA_QW
GLA_OFF_G = GLA_OFF_V + GLA_VW
GLA_OFF_LR = GLA_OFF_G + GLA_VW
GLA_OFF_MQ = GLA_OFF_LR + LANE
GLA_NP = GLA_OFF_MQ + MEM_QW


def _gla_in_body(h_ref, g_ref, w_ref, wg_ref, bg_ref, q_ref, k_ref, v_ref, go_ref, la_ref, mq_ref):
    xn = _rms(h_ref[...], g_ref[...]).astype(BF16)

    def proj(lo, n):
        return _dot(xn, w_ref[:, lo:lo + n])

    q_ref[...] = proj(0, GLA_QW).astype(BF16)
    k_ref[...] = proj(GLA_OFF_K, GLA_QW).astype(BF16)
    for j in range(GLA_VW // GLA_QW):
        v_ref[:, j * GLA_QW:(j + 1) * GLA_QW] = proj(GLA_OFF_V + j * GLA_QW, GLA_QW).astype(BF16)
        go_ref[:, j * GLA_QW:(j + 1) * GLA_QW] = proj(GLA_OFF_G + j * GLA_QW, GLA_QW).astype(BF16)
    lr = proj(GLA_OFF_LR, LANE).astype(BF16)
    z = _dot(lr, wg_ref[...]) + bg_ref[...]
    la_ref[...] = (jnp.minimum(z, 0.0) - jnp.log(1.0 + jnp.exp(-jnp.abs(z)))) * (1.0 / GLA_GATE_NORM)
    mq_ref[...] = proj(GLA_OFF_MQ, MEM_QW).astype(BF16)


def _gla_in_proj(h, g, w, wg, bg):
    t = h.shape[0]
    tm = ROW_TILE
    row = lambda n: pl.BlockSpec((tm, n), lambda i: (i, 0))
    full = lambda a: pl.BlockSpec(a.shape, lambda i: (0,) * a.ndim)
    return pl.pallas_call(
        _gla_in_body,
        grid=(t // tm,),
        in_specs=[row(D_MODEL), full(g), full(w), full(wg), full(bg)],
        out_specs=[row(GLA_QW), row(GLA_QW), row(GLA_VW), row(GLA_VW), row(GLA_QW), row(MEM_QW)],
        out_shape=[jax.ShapeDtypeStruct((t, GLA_QW), BF16), jax.ShapeDtypeStruct((t, GLA_QW), BF16),
                   jax.ShapeDtypeStruct((t, GLA_VW), BF16), jax.ShapeDtypeStruct((t, GLA_VW), BF16),
                   jax.ShapeDtypeStruct((t, GLA_QW), F32), jax.ShapeDtypeStruct((t, MEM_QW), BF16)],
        compiler_params=_cparams("parallel"),
        name="gla_in_proj",
    )(h, g, w, wg, bg)


def _gla_mix_body(q_ref, k_ref, la_ref, v_ref, go_ref, on_ref, o_ref, st_ref, *, cn, batch):
    @pl.when(pl.program_id(0) == 0)
    def _():
        st_ref[...] = jnp.zeros_like(st_ref)

    row = lax.broadcasted_iota(jnp.int32, (cn, cn), 0)
    col = lax.broadcasted_iota(jnp.int32, (cn, cn), 1)
    causal = row >= col
    tril = jnp.where(causal, 1.0, 0.0).astype(BF16)
    scale = GLA_DK ** -0.5
    chains = [(bi, hd) for bi in range(batch) for hd in range(GLA_HEADS)]
    sk = lambda hd: slice(hd * GLA_DKP, (hd + 1) * GLA_DKP)
    sv = lambda hd: slice(hd * GLA_DVP, (hd + 1) * GLA_DVP)
    dg = lambda x, y, dims: lax.dot_general(x, y, dims, preferred_element_type=F32)

    def cum_decay(bi, hd):
        la = la_ref[bi, :, sk(hd)]
        la1 = la.astype(BF16)
        r1 = la - la1.astype(F32)
        la2 = r1.astype(BF16)
        la3 = (r1 - la2.astype(F32)).astype(BF16)
        return _dot(tril, la1) + _dot(tril, la2) + _dot(tril, la3)

    bs = [cum_decay(bi, hd) for bi, hd in chains]
    scaled = []
    for (bi, hd), b in zip(chains, bs):
        b_last = b[cn - 1:cn, :]
        q = q_ref[bi, :, sk(hd)].astype(F32)
        k = k_ref[bi, :, sk(hd)].astype(F32)
        scaled.append(((q * jnp.exp(b) * scale).astype(BF16), (k * jnp.exp(-b)).astype(BF16),
                       (k * jnp.exp(b_last - b)).astype(BF16), jnp.exp(b_last)))
    intra = [jnp.where(causal, dg(qs, ks, NT), 0.0).astype(BF16) for qs, ks, _, _ in scaled]
    outs = []
    for i, (bi, hd) in enumerate(chains):
        qs, _, ko, a_last = scaled[i]
        v = v_ref[bi, :, sv(hd)]
        st = st_ref[i]
        outs.append(_dot(intra[i], v) + dg(qs, st.astype(BF16), NT))
        st_ref[i] = st * a_last + dg(v, ko, TN)
    for (bi, hd), o in zip(chains, outs):
        ms = jnp.sum(o * o, axis=-1, keepdims=True) * (1.0 / GLA_DV)
        y = o * lax.rsqrt(ms + EPS) * on_ref[...]
        g = go_ref[bi, :, sv(hd)].astype(F32)
        o_ref[bi, :, sv(hd)] = (y * (g * _sigmoid(g))).astype(BF16)


def _gla_mixer(q, k, la, v, go, on, batch):
    t = q.shape[0]
    s = t // batch
    cn = GLA_CHUNK
    seq = lambda a: a.reshape(batch, s, a.shape[1])
    blk = lambda n: pl.BlockSpec((batch, cn, n), lambda c: (0, c, 0))
    out = pl.pallas_call(
        functools.partial(_gla_mix_body, cn=cn, batch=batch),
        grid=(s // cn,),
        in_specs=[blk(GLA_QW), blk(GLA_QW), blk(GLA_QW), blk(GLA_VW), blk(GLA_VW),
                  pl.BlockSpec(on.shape, lambda c: (0, 0))],
        out_specs=blk(GLA_VW),
        out_shape=jax.ShapeDtypeStruct((batch, s, GLA_VW), BF16),
        scratch_shapes=[pltpu.VMEM((batch * GLA_HEADS, GLA_DVP, GLA_DKP), F32)],
        compiler_params=_cparams("arbitrary"),
        name="gla_mixer",
    )(seq(q), seq(k), seq(la), seq(v), seq(go), on)
    return out.reshape(t, GLA_VW)


def _mem_kv_body(m_ref, g_ref, w_ref, o_ref):
    xn = _rms(m_ref[...], g_ref[...]).astype(BF16)
    o_ref[...] = _dot(xn, w_ref[...]).astype(BF16)


def _mem_kv_proj(mem2, g, w, batch):
    nl = w.shape[0]
    m = mem2.shape[0] // batch
    n = w.shape[2]
    return pl.pallas_call(
        _mem_kv_body,
        grid=(nl, batch),
        in_specs=[pl.BlockSpec((m, D_MODEL), lambda l, b: (b, 0)),
                  pl.BlockSpec((None, 1, D_MODEL), lambda l, b: (l, 0, 0)),
                  pl.BlockSpec((None, D_MODEL, n), lambda l, b: (l, 0, 0))],
        out_specs=pl.BlockSpec((None, m, n), lambda l, b: (l, b, 0)),
        out_shape=jax.ShapeDtypeStruct((nl, mem2.shape[0], n), BF16),
        compiler_params=_cparams("arbitrary", "arbitrary"),
        name="mem_kv_proj",
    )(mem2, g, w)


def _mem_attn_body(q_ref, kv_ref, o_ref):
    for hd in range(MEM_HEADS):
        sl = slice(hd * MEM_DP, (hd + 1) * MEM_DP)
        sv = slice(MEM_QW + hd * MEM_DP, MEM_QW + (hd + 1) * MEM_DP)
        s = lax.dot_general(q_ref[:, sl], kv_ref[:, sl], NT, preferred_element_type=F32) * MEM_HEAD_DIM ** -0.5
        p = jnp.exp(s - jnp.max(s, axis=-1, keepdims=True))
        l = jnp.sum(p, axis=-1, keepdims=True)
        o_ref[:, sl] = (_dot(p.astype(BF16), kv_ref[:, sv]) / l).astype(BF16)


def _mem_attn(mq, mem_kv, batch):
    t = mq.shape[0]
    tm = ROW_TILE
    nt = t // batch // tm
    m = mem_kv.shape[0] // batch
    return pl.pallas_call(
        _mem_attn_body,
        grid=(batch, nt),
        in_specs=[pl.BlockSpec((tm, MEM_QW), lambda b, i: (b * nt + i, 0)),
                  pl.BlockSpec((m, 2 * MEM_QW), lambda b, i: (b, 0))],
        out_specs=pl.BlockSpec((tm, MEM_QW), lambda b, i: (b * nt + i, 0)),
        out_shape=jax.ShapeDtypeStruct((t, MEM_QW), BF16),
        compiler_params=_cparams("parallel", "parallel"),
        name="mem_attn",
    )(mq, mem_kv)


def _out_proj_body(h_ref, a_ref, m_ref, wa_ref, wm_ref, o_ref):
    o_ref[...] = h_ref[...] + _dot(a_ref[...], wa_ref[...]) + _dot(m_ref[...], wm_ref[...])


def _out_proj(h, main, mo, wa, wm):
    t = h.shape[0]
    tm = ROW_TILE
    row = lambda n: pl.BlockSpec((tm, n), lambda i: (i, 0))
    full = lambda a: pl.BlockSpec(a.shape, lambda i: (0,) * a.ndim)
    return pl.pallas_call(
        _out_proj_body,
        grid=(t // tm,),
        in_specs=[row(D_MODEL), row(main.shape[1]), row(mo.shape[1]), full(wa), full(wm)],
        out_specs=row(D_MODEL),
        out_shape=jax.ShapeDtypeStruct((t, D_MODEL), F32),
        compiler_params=_cparams("parallel"),
        name="out_proj",
    )(h, main, mo, wa, wm)


def _ffn_body(h_ref, hp_ref, g_ref, wup_ref, cw_ref, cb_ref, wdn_ref, o_ref, acc_ref, *, tm, tf):
    h = h_ref[...]
    g = g_ref[...]
    keep = jnp.where(pl.program_id(1) > 0, 1.0, 0.0)
    xe = jnp.concatenate([_rms(hp_ref[...], g) * keep, _rms(h, g)], axis=0).astype(BF16)
    acc_ref[...] = jnp.zeros_like(acc_ref)
    for j in range(FFN_DIM // tf):
        def conv(off):
            u = _dot(xe, wup_ref[:, off:off + tf])
            w = cw_ref[:, off:off + tf]
            out = cb_ref[:, off:off + tf]
            for c in range(CONV_WIDTH):
                lo = SUBLANE - (CONV_WIDTH - 1) + c
                out = out + w[c:c + 1, :] * u[lo:lo + tm, :]
            return out
        a = conv(j * tf)
        b = conv(FFN_DIM + j * tf)
        act = (a * _sigmoid(a) * b).astype(BF16)
        acc_ref[...] += _dot(act, wdn_ref[j * tf:(j + 1) * tf, :])
    o_ref[...] = h + acc_ref[...]


def _conv_ffn(h, g, wup, cw, cb, wdn, batch):
    t = h.shape[0]
    tm = ROW_TILE
    nt = t // batch // tm
    hb = tm // SUBLANE
    full = lambda a: pl.BlockSpec(a.shape, lambda b, i: (0,) * a.ndim)
    return pl.pallas_call(
        functools.partial(_ffn_body, tm=tm, tf=FFN_TILE),
        grid=(batch, nt),
        in_specs=[pl.BlockSpec((tm, D_MODEL), lambda b, i: (b * nt + i, 0)),
                  pl.BlockSpec((SUBLANE, D_MODEL), lambda b, i: (jnp.maximum((b * nt + i) * hb - 1, 0), 0)),
                  full(g), full(wup), full(cw), full(cb), full(wdn)],
        out_specs=pl.BlockSpec((tm, D_MODEL), lambda b, i: (b * nt + i, 0)),
        out_shape=jax.ShapeDtypeStruct((t, D_MODEL), F32),
        scratch_shapes=[pltpu.VMEM((tm, D_MODEL), F32)],
        compiler_params=_cparams("parallel", "parallel"),
        name="conv_ffn",
    )(h, h, g, wup, cw, cb, wdn)


def _final_norm_body(h_ref, g_ref, o_ref):
    o_ref[...] = _rms(h_ref[...], g_ref[...])


def _final_norm(h, g):
    t = h.shape[0]
    tm = ROW_TILE
    return pl.pallas_call(
        _final_norm_body,
        grid=(t // tm,),
        in_specs=[pl.BlockSpec((tm, D_MODEL), lambda i: (i, 0)), pl.BlockSpec(g.shape, lambda i: (0, 0))],
        out_specs=pl.BlockSpec((tm, D_MODEL), lambda i: (i, 0)),
        out_shape=jax.ShapeDtypeStruct((t, D_MODEL), F32),
        compiler_params=_cparams("parallel"),
        name="final_norm",
    )(h, g)


KV_NAT = NSA_GROUPS * NSA_HEAD_DIM


def _kv_proj_body(h_ref, g_ref, w_ref, ck_ref, cv_ref, ks_ref, kw_ref, vs_ref, vw_ref, *, tm):
    xn = _rms(h_ref[...], g_ref[...]).astype(BF16)
    ck_ref[...] = _dot(xn, w_ref[:, 0:KV_NAT]).astype(BF16)
    cv_ref[...] = _dot(xn, w_ref[:, KV_NAT:2 * KV_NAT]).astype(BF16)
    key = pl.program_id(1) * tm + lax.broadcasted_iota(jnp.int32, (tm, LANE), 0)
    lane = lax.broadcasted_iota(jnp.int32, (tm, LANE), 1)
    onehot = jnp.where(lax.shift_right_logical(key, int(math.log2(SEL_BLOCK))) == lane - NSA_HEAD_DIM, 1.0, 0.0)
    low = lane < NSA_HEAD_DIM
    for gi in range(NSA_GROUPS):
        slot = lambda n: _dot(xn, w_ref[:, 2 * KV_NAT + (n * NSA_GROUPS + gi) * LANE:
                                           2 * KV_NAT + (n * NSA_GROUPS + gi + 1) * LANE])
        ks_ref[gi] = jnp.where(low, slot(0), onehot).astype(BF16)
        kw_ref[gi] = jnp.where(low, slot(1), 0.0).astype(BF16)
        vt = slot(2).T.astype(BF16)
        ones = jnp.ones((V_ROWS - NSA_HEAD_DIM, tm), BF16)
        vs_ref[gi] = jnp.concatenate([vt[0:NSA_HEAD_DIM], ones], axis=0)
        vw_ref[gi] = jnp.concatenate([vt[NSA_HEAD_DIM:], ones], axis=0)


def _kv_proj(h, g, w, batch):
    t = h.shape[0]
    s = t // batch
    tm = ROW_TILE
    nt = s // tm
    gr = NSA_GROUPS
    return pl.pallas_call(
        functools.partial(_kv_proj_body, tm=tm),
        grid=(batch, nt),
        in_specs=[pl.BlockSpec((tm, D_MODEL), lambda b, i: (b * nt + i, 0)),
                  pl.BlockSpec(g.shape, lambda b, i: (0, 0)),
                  pl.BlockSpec(w.shape, lambda b, i: (0, 0))],
        out_specs=[pl.BlockSpec((tm, KV_NAT), lambda b, i: (b * nt + i, 0)),
                   pl.BlockSpec((tm, KV_NAT), lambda b, i: (b * nt + i, 0)),
                   pl.BlockSpec((None, gr, tm, LANE), lambda b, i: (b, 0, i, 0)),
                   pl.BlockSpec((None, gr, tm, LANE), lambda b, i: (b, 0, i, 0)),
                   pl.BlockSpec((None, gr, V_ROWS, tm), lambda b, i: (b, 0, 0, i)),
                   pl.BlockSpec((None, gr, V_ROWS, tm), lambda b, i: (b, 0, 0, i))],
        out_shape=[jax.ShapeDtypeStruct((t, KV_NAT), BF16), jax.ShapeDtypeStruct((t, KV_NAT), BF16),
                   jax.ShapeDtypeStruct((batch, gr, s, LANE), BF16),
                   jax.ShapeDtypeStruct((batch, gr, s, LANE), BF16),
                   jax.ShapeDtypeStruct((batch, gr, V_ROWS, s), BF16),
                   jax.ShapeDtypeStruct((batch, gr, V_ROWS, s), BF16)],
        compiler_params=_cparams("parallel", "parallel"),
        name="nsa_kv_proj",
    )(h, g, w)


def _compress_body(x_ref, wt_ref, wb_ref, pos_ref, w1_ref, b1_ref, w2_ref, b2_ref, on_ref, ot_ref, *, ncp):
    x = x_ref[...]
    top = _dot(x, wt_ref[...])
    bot = _dot(x, wb_ref[...])
    posb = _dot(pos_ref[...], w1_ref[...])[0:1, :] + b1_ref[...]
    for gi in range(NSA_GROUPS):
        sl = slice(gi * CMP_HIDDEN, (gi + 1) * CMP_HIDDEN)
        hid = top[:, sl] + pltpu.roll(bot[:, sl], ncp - 1, 0) + posb
        hid = (hid * _sigmoid(hid)).astype(BF16)
        out = _dot(hid, w2_ref[...]) + b2_ref[...]
        on_ref[gi] = out.astype(BF16)
        ot_ref[gi] = out.T.astype(BF16)


def _compress(x16, wt, wb, pos, w1, b1, w2, b2, batch):
    ncp = x16.shape[1] // batch
    gr = NSA_GROUPS
    per_j = lambda a: pl.BlockSpec((None,) + a.shape[1:], lambda j, b: (j,) + (0,) * (a.ndim - 1))
    return pl.pallas_call(
        functools.partial(_compress_body, ncp=ncp),
        grid=(2, batch),
        in_specs=[pl.BlockSpec((None, ncp, x16.shape[2]), lambda j, b: (j, b, 0)),
                  per_j(wt), per_j(wb), per_j(pos), per_j(w1), per_j(b1), per_j(w2), per_j(b2)],
        out_specs=[pl.BlockSpec((None, None, gr, ncp, LANE), lambda j, b: (j, b, 0, 0, 0)),
                   pl.BlockSpec((None, None, gr, LANE, ncp), lambda j, b: (j, b, 0, 0, 0))],
        out_shape=[jax.ShapeDtypeStruct((2, batch, gr, ncp, LANE), BF16),
                   jax.ShapeDtypeStruct((2, batch, gr, LANE, ncp), BF16)],
        compiler_params=_cparams("arbitrary", "arbitrary"),
        name="nsa_compress",
    )(x16, wt, wb, pos, w1, b1, w2, b2)


NSA_QW = NSA_HEADS * NSA_HEAD_DIM
G3 = NSA_REP * TQ


def _nsa_in_body(h_ref, g_ref, w_ref, qt_ref, gt_ref, mq_ref):
    xn = _rms(h_ref[...], g_ref[...]).astype(BF16)
    for j in range(NSA_QW // LANE):
        y = _dot(xn, w_ref[:, j * LANE:(j + 1) * LANE]) * (NSA_HEAD_DIM ** -0.5 * LOG2E)
        qt_ref[j * LANE:(j + 1) * LANE, :] = y.T.astype(BF16)
    gt_ref[...] = _sigmoid(_dot(xn, w_ref[:, NSA_QW:NSA_QW + LANE])).T
    mq_ref[...] = _dot(xn, w_ref[:, NSA_QW + LANE:]).astype(BF16)


def _nsa_in_proj(h, g, w):
    t = h.shape[0]
    tm = ROW_TILE
    row = lambda n: pl.BlockSpec((tm, n), lambda i: (i, 0))
    col = lambda n: pl.BlockSpec((n, tm), lambda i: (0, i))
    full = lambda a: pl.BlockSpec(a.shape, lambda i: (0,) * a.ndim)
    return pl.pallas_call(
        _nsa_in_body,
        grid=(t // tm,),
        in_specs=[row(D_MODEL), full(g), full(w)],
        out_specs=[col(NSA_QW), col(LANE), row(MEM_QW)],
        out_shape=[jax.ShapeDtypeStruct((NSA_QW, t), BF16), jax.ShapeDtypeStruct((LANE, t), F32),
                   jax.ShapeDtypeStruct((t, MEM_QW), BF16)],
        compiler_params=_cparams("parallel"),
        name="nsa_in_proj",
    )(h, g, w)


def _nsa_attn_body(qt_ref, gt_ref, kc_ref, vct_ref, ks_ref, kw_ref, vs_ref, vw_ref, ovt_ref, tz_ref, cb_ref,
                   o_ref, sc_ref, qa_ref, acc_ref, ot_ref, s0_ref, s1_ref, s2_ref, s3_ref, *, nsb, ncp, n_sel):
    n = pl.program_id(1)
    t0 = n * TQ
    dh = NSA_HEAD_DIM
    tq = t0 + (lax.broadcasted_iota(jnp.int32, (ncp, G3), 1) & (TQ - 1))
    kc = lax.broadcasted_iota(jnp.int32, (ncp, G3), 0)
    cmask = tq >= kc * CMP_STRIDE + (CMP_BLOCK - 1)
    cstart = pl.multiple_of(ncp - lax.rem(n * (TQ // CMP_STRIDE), ncp), SUBLANE)
    jj = lax.broadcasted_iota(jnp.int32, (nsb, TQ), 0)
    cur = lax.shift_right_logical(t0 + lax.broadcasted_iota(jnp.int32, (nsb, TQ), 1), int(math.log2(SEL_BLOCK)))
    forced = (jj == 0) | (jj == cur) | (jj == cur - 1)
    zpad = jnp.zeros((dh, G3), BF16)
    nwt = WINDOW // TQ

    def scores(k_ref, gi, first, count, tz_index, qa):
        koff = pl.multiple_of(first * TQ, TQ)
        s = _dot(k_ref[gi, pl.ds(koff, count * TQ), :], qa)
        return [s[i * TQ:(i + 1) * TQ] + tz_ref[gi, tz_index(n - (first + i))] for i in range(count)]

    def col_max(parts):
        mx = parts[0]
        for x in parts[1:]:
            mx = jnp.maximum(mx, x)
        return jnp.max(mx, axis=0, keepdims=True)

    def probs(parts, m):
        return jnp.concatenate([jnp.exp2(x - m).astype(BF16) for x in parts], axis=0)

    def values(v_ref, gi, first, count, p):
        koff = pl.multiple_of(first * TQ, TQ)
        return _dot(v_ref[gi, :, pl.ds(koff, count * TQ)], p)

    sel_index = lambda d: jnp.where(d < 0, 2, jnp.minimum(d, 2))
    win_index = lambda d: jnp.where(d < 0, 4, jnp.where(d == nwt, 3, jnp.minimum(d, 2)))

    groups = range(NSA_GROUPS)
    gate = lambda h, c: gt_ref[pl.ds(h * 3 + c, 1), :]
    head_lanes = lambda r: slice(r * TQ, (r + 1) * TQ)

    q3s = [jnp.concatenate([qt_ref[(gi * NSA_REP + r) * dh:(gi * NSA_REP + r + 1) * dh, :] for r in range(NSA_REP)],
                           axis=1) for gi in groups]
    cs = [jnp.where(cmask, _dot(kc_ref[gi], jnp.concatenate([q3s[gi], zpad], axis=0))
                    + cb_ref[gi, pl.ds(cstart, ncp), :], NEG) for gi in groups]
    cps = [jnp.where(cmask, jnp.exp2(s - jnp.max(s, axis=0, keepdims=True)), 0.0) for s in cs]
    cps = [p * (1.0 / jnp.maximum(jnp.sum(p, axis=0, keepdims=True), 1e-30)) for p in cps]
    ocs = [_dot(vct_ref[gi], cps[gi].astype(BF16)) for gi in groups]
    scs = []
    for gi in groups:
        psum = cps[gi][:, 0:TQ]
        for r in range(1, NSA_REP):
            psum = psum + cps[gi][:, head_lanes(r)]
        p1 = psum.astype(BF16)
        p2 = (psum - p1.astype(F32)).astype(BF16)
        imp = _dot(ovt_ref[...], p1) + _dot(ovt_ref[...], p2)
        score = jnp.where(forced, 1e4, jnp.where(jj <= cur, imp[0:nsb], -1.0))
        scs.append(jnp.where(score < 0.0, -1, lax.bitcast_convert_type(score, jnp.int32)))
    for gi in groups:
        sc_ref[gi] = scs[gi]
        for r in range(NSA_REP):
            h = gi * NSA_REP + r
            ot_ref[h * dh:(h + 1) * dh, :] = gate(h, 0) * ocs[gi][0:dh, head_lanes(r)]

    scs1 = [k + 1 for k in scs]

    def rank_step(i4, cnts):
        cnts = list(cnts)
        for u in range(RANK_UNROLL):
            i = i4 * RANK_UNROLL + u
            lower = i < jj
            for gi in groups:
                rowk = sc_ref[gi, pl.ds(i, 1), :]
                before = rowk >= jnp.where(lower, scs[gi], scs1[gi])
                cnts[gi] = cnts[gi] + jnp.where(before, 1, 0)
        return tuple(cnts)
    rank_trips = jnp.minimum((2 * n + 2 + RANK_UNROLL - 1) // RANK_UNROLL, nsb // RANK_UNROLL)
    cnts = lax.fori_loop(0, rank_trips, rank_step, tuple(jnp.zeros((nsb, TQ), jnp.int32) for _ in groups))
    for gi in groups:
        selneg = jnp.where((cnts[gi] < n_sel) & (jj <= cur), 0.0, NEG).astype(BF16)
        if nsb < SEL_BLOCK:
            selneg = jnp.concatenate([selneg, jnp.zeros((SEL_BLOCK - nsb, TQ), BF16)], axis=0)
        qa_ref[gi] = jnp.concatenate([q3s[gi], jnp.concatenate([selneg] * NSA_REP, axis=1)], axis=0)

    acc_ref[...] = jnp.zeros_like(acc_ref)
    half = NSA_GROUPS // 2
    s_refs = ((s0_ref, s1_ref), (s2_ref, s3_ref))

    def score_half(it, hb):
        plist = [scores(ks_ref, hb * half + u, it * SEL_TILES, SEL_TILES, sel_index, qa_ref[hb * half + u])
                 for u in range(half)]
        for u, parts in enumerate(plist):
            for i, x in enumerate(parts):
                s_refs[hb][u][i * TQ:(i + 1) * TQ, :] = x
        return [col_max(parts) for parts in plist]

    def finish_half(it, hb, ms, bms):
        m2s = [jnp.maximum(ms[u], bms[u]) for u in range(half)]
        ps = [probs([s_refs[hb][u][i * TQ:(i + 1) * TQ, :] for i in range(SEL_TILES)], m2s[u]) for u in range(half)]
        vals = [values(vs_ref, hb * half + u, it * SEL_TILES, SEL_TILES, ps[u]) for u in range(half)]
        for u in range(half):
            gi = hb * half + u
            acc_ref[gi] = jnp.exp2(ms[u] - m2s[u]) * acc_ref[gi] + vals[u]
        return m2s

    def sel_trip(it, carry, score_next):
        ms0, ms1, bm0 = carry
        bm1 = score_half(it, 1)
        ms0 = finish_half(it, 0, ms0, bm0)
        if score_next:
            bm0 = score_half(it + 1, 0)
        ms1 = finish_half(it, 1, ms1, bm1)
        return ms0, ms1, bm0

    neg = [jnp.full((1, G3), NEG, F32) for _ in range(half)]
    carry = (neg, neg, score_half(0, 0))
    carry = lax.fori_loop(0, n // SEL_TILES, lambda it, c: sel_trip(it, c, True), carry)
    sel_trip(n // SEL_TILES, carry, False)

    wfirst = jnp.maximum(n - nwt, 0)
    wparts = [scores(kw_ref, gi, wfirst, nwt + 1, win_index, qa_ref[gi]) for gi in groups]
    wps = [probs(parts, col_max(parts)) for parts in wparts]
    ows = [values(vw_ref, gi, wfirst, nwt + 1, wps[gi]) for gi in groups]
    for gi in groups:
        o_s = acc_ref[gi, 0:dh, :] * (1.0 / acc_ref[gi, dh:dh + 1, :])
        o_w = ows[gi][0:dh, :] * (1.0 / ows[gi][dh:dh + 1, :])
        for r in range(NSA_REP):
            h = gi * NSA_REP + r
            ot_ref[h * dh:(h + 1) * dh, :] += gate(h, 1) * o_s[:, head_lanes(r)] + gate(h, 2) * o_w[:, head_lanes(r)]

    for j in range(NSA_QW // LANE):
        o_ref[:, j * LANE:(j + 1) * LANE] = ot_ref[j * LANE:(j + 1) * LANE, :].T.astype(BF16)


def _nsa_attn(qt, gt, kc, vct, ksel, kwin, vsel, vwin, ovt, tz, cb, batch):
    t = qt.shape[1]
    s = t // batch
    nt = s // TQ
    nsb = s // SEL_BLOCK
    ncp = kc.shape[-2]
    per_b = lambda a: pl.BlockSpec((None,) + a.shape[1:], lambda b, i: (b,) + (0,) * (a.ndim - 1))
    full = lambda a: pl.BlockSpec(a.shape, lambda b, i: (0,) * a.ndim)
    return pl.pallas_call(
        functools.partial(_nsa_attn_body, nsb=nsb, ncp=ncp, n_sel=min(SEL_TOPK, nsb)),
        grid=(batch, nt),
        in_specs=[pl.BlockSpec((NSA_QW, TQ), lambda b, i: (0, b * nt + i)),
                  pl.BlockSpec((LANE, TQ), lambda b, i: (0, b * nt + i)),
                  per_b(kc), per_b(vct), per_b(ksel), per_b(kwin), per_b(vsel), per_b(vwin),
                  full(ovt), full(tz), full(cb)],
        out_specs=pl.BlockSpec((TQ, NSA_QW), lambda b, i: (b * nt + i, 0)),
        out_shape=jax.ShapeDtypeStruct((t, NSA_QW), BF16),
        scratch_shapes=[pltpu.VMEM((NSA_GROUPS, nsb, TQ), jnp.int32), pltpu.VMEM((NSA_GROUPS, LANE, G3), BF16),
                        pltpu.VMEM((NSA_GROUPS, V_ROWS, G3), F32), pltpu.VMEM((NSA_QW, TQ), F32),
                        ] + [pltpu.VMEM((SEL_TILES * TQ, G3), F32)] * NSA_GROUPS,
        compiler_params=_cparams("parallel", "arbitrary"),
        name="nsa_attn",
    )(qt, gt, kc, vct, ksel, kwin, vsel, vwin, ovt, tz, cb)


def _rel_bucket_np(dist):
    dist = np.maximum(dist, 0)
    max_exact = REL_BUCKETS // 2
    ratio = np.log(np.maximum(dist, 1).astype(np.float32) / np.float32(max_exact)) / np.float32(
        math.log(REL_MAX_DIST / max_exact))
    large = max_exact + (ratio * np.float32(REL_BUCKETS - max_exact)).astype(np.int32)
    large = np.minimum(large, REL_BUCKETS - 1)
    return np.where(dist < max_exact, dist, large).astype(np.int32)


def _group_lanes(a):
    hh, r, c = a.shape
    return a.reshape(NSA_GROUPS, NSA_REP, r, c).transpose(0, 2, 1, 3).reshape(NSA_GROUPS, r, NSA_REP * c)


def _bias_tables(rel_bias, ncp):
    k = np.arange(TQ)[:, None]
    q = np.arange(TQ)[None, :]
    tbl = rel_bias.astype(F32)

    def lookup(idx):
        onehot = (jnp.asarray(idx.reshape(1, -1)) == jnp.arange(REL_BUCKETS)[:, None]).astype(F32)
        out = jnp.dot(tbl.T, onehot, precision=lax.Precision.HIGHEST)
        return out.reshape((NSA_HEADS,) + idx.shape)

    far = jnp.broadcast_to(tbl[REL_BUCKETS - 1][:, None, None], (NSA_HEADS, TQ, TQ))
    t0 = jnp.where(jnp.asarray(k <= q)[None], lookup(_rel_bucket_np(q - k)), NEG)
    t1 = lookup(_rel_bucket_np(TQ + q - k))
    t3 = jnp.where(jnp.asarray(k > q)[None], far, NEG)
    tz = jnp.stack([_group_lanes(x) for x in (t0, t1, far, t3, jnp.full_like(far, NEG))], axis=1)
    kk = np.arange(ncp)[:, None]
    m = np.where(kk < ncp // 2, -kk, ncp - kk)
    d = CMP_STRIDE * m + q - (CMP_BLOCK - 1)
    idx = np.where((d >= 0) & (d < REL_MAX_DIST), _rel_bucket_np(d), REL_BUCKETS - 1)
    cb = _group_lanes(lookup(idx))
    return tz * LOG2E, jnp.concatenate([cb, cb], axis=1) * LOG2E


def _overlap_table(s, ncp):
    nsb = s // SEL_BLOCK
    nc = (s - CMP_BLOCK) // CMP_STRIDE + 1
    cs = np.arange(ncp) * CMP_STRIDE
    ce = cs + CMP_BLOCK - 1
    ss = np.arange(SEL_BLOCK) * SEL_BLOCK
    ov = (cs[None, :] < ss[:, None] + SEL_BLOCK) & (ce[None, :] >= ss[:, None])
    ov &= (np.arange(ncp) < nc)[None, :] & (np.arange(SEL_BLOCK) < nsb)[:, None]
    return jnp.asarray(ov, dtype=BF16)


def kernel(x, mem, norm_mix, norm_mem, w_mem_kv, w_out, norm_ffn, w_up, conv_w, conv_b, w_down,
           gla_w_in, gla_w_gate_up, gla_b_gate, gla_out_norm, nsa_w_in, kv_norm, w_kv_shared,
           cmp_pos, cmp_w1, cmp_b1, cmp_w2, cmp_b2, rel_bias, final_norm):
    batch, seq = x.shape[0], x.shape[1]
    t = batch * seq
    h = x.reshape(t, D_MODEL)
    row = lambda v: v.reshape(1, -1).astype(F32)

    wk, wv = w_mem_kv[..., :MEM_W], w_mem_kv[..., MEM_W:]
    w_mkv = jnp.concatenate([_pad_heads(wk, MEM_HEADS, MEM_HEAD_DIM, MEM_DP),
                             _pad_heads(wv, MEM_HEADS, MEM_HEAD_DIM, MEM_DP)], axis=-1).astype(BF16)
    mem_kv_all = _mem_kv_proj(mem.reshape(-1, D_MODEL), norm_mem.reshape(DEPTH, 1, D_MODEL), w_mkv, batch)

    shared = None
    for i in range(DEPTH):
        w_o = w_out[i]
        w_o_mem = _pad_head_rows(w_o[MAIN_W:], MEM_HEADS, MEM_HEAD_DIM, MEM_DP).astype(BF16)
        if i < N_A_LAYERS:
            wi = gla_w_in[i]
            c0 = GLA_HEADS * GLA_DK
            c1 = 2 * c0
            c2 = c1 + GLA_HEADS * GLA_DV
            c3 = c2 + GLA_HEADS * GLA_DV
            c4 = c3 + GLA_RANK
            w_all = jnp.concatenate([
                _pad_heads(wi[:, :c0], GLA_HEADS, GLA_DK, GLA_DKP),
                _pad_heads(wi[:, c0:c1], GLA_HEADS, GLA_DK, GLA_DKP),
                _pad_heads(wi[:, c1:c2], GLA_HEADS, GLA_DV, GLA_DVP),
                _pad_heads(wi[:, c2:c3], GLA_HEADS, GLA_DV, GLA_DVP),
                _pad_heads(wi[:, c3:c4], 1, GLA_RANK, LANE),
                _pad_heads(wi[:, c4:], MEM_HEADS, MEM_HEAD_DIM, MEM_DP)], axis=1).astype(BF16)
            wg = jnp.pad(_pad_heads(gla_w_gate_up[i], GLA_HEADS, GLA_DK, GLA_DKP),
                         ((0, LANE - GLA_RANK), (0, 0))).astype(BF16)
            bg = row(_pad_heads(gla_b_gate[i], GLA_HEADS, GLA_DK, GLA_DKP))
            q, k, v, go, la, mq = _gla_in_proj(h, row(norm_mix[i]), w_all, wg, bg)
            on = row(jnp.pad(gla_out_norm[i], (0, GLA_DVP - GLA_DV)))
            main = _gla_mixer(q, k, la, v, go, on, batch)
            w_o_main = _pad_head_rows(w_o[:MAIN_W], GLA_HEADS, GLA_DV, GLA_DVP).astype(BF16)
        else:
            if shared is None:
                ncp = seq // CMP_STRIDE
                gr, dh = NSA_GROUPS, NSA_HEAD_DIM
                wkv = w_kv_shared.reshape(D_MODEL, 6, gr, dh)
                pair = lambda a, b: jnp.concatenate([wkv[:, a], wkv[:, b]], axis=-1).reshape(D_MODEL, gr * 2 * dh)
                slot = lambda a: _pad_heads(wkv[:, a].reshape(D_MODEL, KV_NAT), gr, dh, LANE)
                w_kv = jnp.concatenate([wkv[:, 0].reshape(D_MODEL, KV_NAT), wkv[:, 1].reshape(D_MODEL, KV_NAT),
                                        slot(2), slot(4), pair(3, 5)], axis=1).astype(BF16)
                ck, cv, ksel, kwin, vsel, vwin = _kv_proj(h, row(kv_norm), w_kv, batch)
                x16 = jnp.stack([ck.reshape(batch * ncp, CMP_STRIDE * KV_NAT),
                                 cv.reshape(batch * ncp, CMP_STRIDE * KV_NAT)])
                w1 = cmp_w1.reshape(2, 2, CMP_STRIDE, dh, CMP_HIDDEN)
                eye = jnp.eye(gr, dtype=F32)
                w1x = jnp.einsum('jhldc,gk->jhlgdkc', w1, eye).reshape(2, 2, CMP_STRIDE * KV_NAT, gr * CMP_HIDDEN)
                w1x = w1x.astype(BF16)
                pos8 = jnp.broadcast_to(cmp_pos.reshape(2, 1, CMP_BLOCK * dh), (2, SUBLANE, CMP_BLOCK * dh)).astype(BF16)
                w2p = jnp.pad(cmp_w2, ((0, 0), (0, 0), (0, LANE - dh))).astype(BF16)
                b2p = jnp.pad(cmp_b2, ((0, 0), (0, LANE - dh))).reshape(2, 1, LANE).astype(F32)
                cnat, ctr = _compress(x16, w1x[:, 0], w1x[:, 1], pos8, cmp_w1.astype(BF16),
                                      cmp_b1.reshape(2, 1, CMP_HIDDEN).astype(F32), w2p, b2p, batch)
                tz, cb = _bias_tables(rel_bias, ncp)
                ov = _overlap_table(seq, ncp)
                shared = (cnat[0], ctr[1], ksel, kwin, vsel, vwin, ov, tz, cb)
            wi = nsa_w_in[i - N_A_LAYERS]
            c0 = NSA_HEADS * NSA_HEAD_DIM
            c1 = c0 + NSA_HEADS * 3
            w_all = jnp.concatenate([
                wi[:, :c0],
                _pad_heads(wi[:, c0:c1], 1, NSA_HEADS * 3, LANE),
                _pad_heads(wi[:, c1:], MEM_HEADS, MEM_HEAD_DIM, MEM_DP)], axis=1).astype(BF16)
            q, gates, mq = _nsa_in_proj(h, row(norm_mix[i]), w_all)
            main = _nsa_attn(q, gates, *shared, batch)
            w_o_main = w_o[:MAIN_W].astype(BF16)
        mo = _mem_attn(mq, mem_kv_all[i], batch)
        h = _out_proj(h, main, mo, w_o_main, w_o_mem)
        h = _conv_ffn(h, row(norm_ffn[i]), w_up[i].astype(BF16), conv_w[i].astype(F32), row(conv_b[i]),
                      w_down[i].astype(BF16), batch)
    return _final_norm(h, row(final_norm)).reshape(batch, seq, D_MODEL)
```

```python
import functools
import math

import numpy as np
import jax
import jax.numpy as jnp
from jax import lax
from jax.experimental import pallas as pl
from jax.experimental.pallas import tpu as pltpu

F32 = jnp.float32
BF16 = jnp.bfloat16

D_MODEL = 1024
DEPTH = 4
N_A_LAYERS = DEPTH // 2
MEM_HEADS = 4
MEM_HEAD_DIM = 64
MEM_W = MEM_HEADS * MEM_HEAD_DIM
MAIN_W = D_MODEL - MEM_W
GLA_HEADS = 4
GLA_DV = MAIN_W // GLA_HEADS
GLA_DK = GLA_DV // 2
GLA_RANK = 16
GLA_GATE_NORM = 16.0
NSA_HEADS = 12
NSA_GROUPS = 4
NSA_HEAD_DIM = MAIN_W // NSA_HEADS
NSA_REP = NSA_HEADS // NSA_GROUPS
CMP_BLOCK = 32
CMP_STRIDE = 16
CMP_HIDDEN = 128
SEL_BLOCK = 64
SEL_TOPK = 16
WINDOW = 512
REL_BUCKETS = 32
REL_MAX_DIST = 128
FFN_DIM = 2816
CONV_WIDTH = 3
EPS = 1e-6

LANE = 128
SUBLANE = 8
VMEM_LIMIT = 56 * 1024 * 1024
GLA_DKP = LANE
GLA_DVP = 2 * LANE
MEM_DP = LANE
NEG = -1e30
TQ = 128
ROW_TILE = 512
GLA_CHUNK = 64
FFN_TILE = 256
SEL_TILES = 4
V_ROWS = 64 + 16
RANK_UNROLL = 4
LOG2E = math.log2(math.e)

NT = (((1,), (1,)), ((), ()))
TN = (((0,), (0,)), ((), ()))


def _cparams(*sem):
    return pltpu.CompilerParams(dimension_semantics=sem, vmem_limit_bytes=VMEM_LIMIT)


def _rms(x, g):
    return x * lax.rsqrt(jnp.mean(x * x, axis=-1, keepdims=True) + EPS) * g


def _sigmoid(x):
    return 1.0 / (1.0 + jnp.exp(-x))


def _dot(a, b):
    return jnp.dot(a, b, preferred_element_type=F32)


def _pad_heads(w, nh, d, dp):
    lead = w.shape[:-1]
    w = w.reshape(lead + (nh, d))
    w = jnp.pad(w, [(0, 0)] * len(lead) + [(0, 0), (0, dp - d)])
    return w.reshape(lead + (nh * dp,))


def _pad_head_rows(w, nh, d, dp):
    n = w.shape[-1]
    w = jnp.pad(w.reshape(nh, d, n), ((0, 0), (0, dp - d), (0, 0)))
    return w.reshape(nh * dp, n)


GLA_QW = GLA_HEADS * GLA_DKP
GLA_VW = GLA_HEADS * GLA_DVP
MEM_QW = MEM_HEADS * MEM_DP
GLA_OFF_K = GLA_QW
GLA_OFF_V = 2 * GLA_QW
GLA_OFF_G = GLA_OFF_V + GLA_VW
GLA_OFF_LR = GLA_OFF_G + GLA_VW
GLA_OFF_MQ = GLA_OFF_LR + LANE
GLA_NP = GLA_OFF_MQ + MEM_QW


def _gla_in_body(h_ref, g_ref, w_ref, wg_ref, bg_ref, q_ref, k_ref, v_ref, go_ref, la_ref, mq_ref):
    xn = _rms(h_ref[...], g_ref[...]).astype(BF16)

    def proj(lo, n):
        return _dot(xn, w_ref[:, lo:lo + n])

    q_ref[...] = proj(0, GLA_QW).astype(BF16)
    k_ref[...] = proj(GLA_OFF_K, GLA_QW).astype(BF16)
    for j in range(GLA_VW // GLA_QW):
        v_ref[:, j * GLA_QW:(j + 1) * GLA_QW] = proj(GLA_OFF_V + j * GLA_QW, GLA_QW).astype(BF16)
        go_ref[:, j * GLA_QW:(j + 1) * GLA_QW] = proj(GLA_OFF_G + j * GLA_QW, GLA_QW).astype(BF16)
    lr = proj(GLA_OFF_LR, LANE).astype(BF16)
    z = _dot(lr, wg_ref[...]) + bg_ref[...]
    la_ref[...] = (jnp.minimum(z, 0.0) - jnp.log(1.0 + jnp.exp(-jnp.abs(z)))) * (1.0 / GLA_GATE_NORM)
    mq_ref[...] = proj(GLA_OFF_MQ, MEM_QW).astype(BF16)


def _gla_in_proj(h, g, w, wg, bg):
    t = h.shape[0]
    tm = ROW_TILE
    row = lambda n: pl.BlockSpec((tm, n), lambda i: (i, 0))
    full = lambda a: pl.BlockSpec(a.shape, lambda i: (0,) * a.ndim)
    return pl.pallas_call(
        _gla_in_body,
        grid=(t // tm,),
        in_specs=[row(D_MODEL), full(g), full(w), full(wg), full(bg)],
        out_specs=[row(GLA_QW), row(GLA_QW), row(GLA_VW), row(GLA_VW), row(GLA_QW), row(MEM_QW)],
        out_shape=[jax.ShapeDtypeStruct((t, GLA_QW), BF16), jax.ShapeDtypeStruct((t, GLA_QW), BF16),
                   jax.ShapeDtypeStruct((t, GLA_VW), BF16), jax.ShapeDtypeStruct((t, GLA_VW), BF16),
                   jax.ShapeDtypeStruct((t, GLA_QW), F32), jax.ShapeDtypeStruct((t, MEM_QW), BF16)],
        compiler_params=_cparams("parallel"),
        name="gla_in_proj",
    )(h, g, w, wg, bg)


def _gla_mix_body(q_ref, k_ref, la_ref, v_ref, go_ref, on_ref, o_ref, st_ref, *, cn, batch):
    @pl.when(pl.program_id(0) == 0)
    def _():
        st_ref[...] = jnp.zeros_like(st_ref)

    row = lax.broadcasted_iota(jnp.int32, (cn, cn), 0)
    col = lax.broadcasted_iota(jnp.int32, (cn, cn), 1)
    causal = row >= col
    tril = jnp.where(causal, 1.0, 0.0).astype(BF16)
    scale = GLA_DK ** -0.5
    chains = [(bi, hd) for bi in range(batch) for hd in range(GLA_HEADS)]
    sk = lambda hd: slice(hd * GLA_DKP, (hd + 1) * GLA_DKP)
    sv = lambda hd: slice(hd * GLA_DVP, (hd + 1) * GLA_DVP)
    dg = lambda x, y, dims: lax.dot_general(x, y, dims, preferred_element_type=F32)

    def cum_decay(bi, hd):
        la = la_ref[bi, :, sk(hd)]
        la1 = la.astype(BF16)
        r1 = la - la1.astype(F32)
        la2 = r1.astype(BF16)
        la3 = (r1 - la2.astype(F32)).astype(BF16)
        return _dot(tril, la1) + _dot(tril, la2) + _dot(tril, la3)

    bs = [cum_decay(bi, hd) for bi, hd in chains]
    scaled = []
    for (bi, hd), b in zip(chains, bs):
        b_last = b[cn - 1:cn, :]
        q = q_ref[bi, :, sk(hd)].astype(F32)
        k = k_ref[bi, :, sk(hd)].astype(F32)
        scaled.append(((q * jnp.exp(b) * scale).astype(BF16), (k * jnp.exp(-b)).astype(BF16),
                       (k * jnp.exp(b_last - b)).astype(BF16), jnp.exp(b_last)))
    intra = [jnp.where(causal, dg(qs, ks, NT), 0.0).astype(BF16) for qs, ks, _, _ in scaled]
    outs = []
    for i, (bi, hd) in enumerate(chains):
        qs, _, ko, a_last = scaled[i]
        v = v_ref[bi, :, sv(hd)]
        st = st_ref[i]
        outs.append(_dot(intra[i], v) + dg(qs, st.astype(BF16), NT))
        st_ref[i] = st * a_last + dg(v, ko, TN)
    for (bi, hd), o in zip(chains, outs):
        ms = jnp.sum(o * o, axis=-1, keepdims=True) * (1.0 / GLA_DV)
        y = o * lax.rsqrt(ms + EPS) * on_ref[...]
        g = go_ref[bi, :, sv(hd)].astype(F32)
        o_ref[bi, :, sv(hd)] = (y * (g * _sigmoid(g))).astype(BF16)


def _gla_mixer(q, k, la, v, go, on, batch):
    t = q.shape[0]
    s = t // batch
    cn = GLA_CHUNK
    seq = lambda a: a.reshape(batch, s, a.shape[1])
    blk = lambda n: pl.BlockSpec((batch, cn, n), lambda c: (0, c, 0))
    out = pl.pallas_call(
        functools.partial(_gla_mix_body, cn=cn, batch=batch),
        grid=(s // cn,),
        in_specs=[blk(GLA_QW), blk(GLA_QW), blk(GLA_QW), blk(GLA_VW), blk(GLA_VW),
                  pl.BlockSpec(on.shape, lambda c: (0, 0))],
        out_specs=blk(GLA_VW),
        out_shape=jax.ShapeDtypeStruct((batch, s, GLA_VW), BF16),
        scratch_shapes=[pltpu.VMEM((batch * GLA_HEADS, GLA_DVP, GLA_DKP), F32)],
        compiler_params=_cparams("arbitrary"),
        name="gla_mixer",
    )(seq(q), seq(k), seq(la), seq(v), seq(go), on)
    return out.reshape(t, GLA_VW)


def _mem_kv_body(m_ref, g_ref, w_ref, o_ref):
    xn = _rms(m_ref[...], g_ref[...]).astype(BF16)
    o_ref[...] = _dot(xn, w_ref[...]).astype(BF16)


def _mem_kv_proj(mem2, g, w, batch):
    nl = w.shape[0]
    m = mem2.shape[0] // batch
    n = w.shape[2]
    return pl.pallas_call(
        _mem_kv_body,
        grid=(nl, batch),
        in_specs=[pl.BlockSpec((m, D_MODEL), lambda l, b: (b, 0)),
                  pl.BlockSpec((None, 1, D_MODEL), lambda l, b: (l, 0, 0)),
                  pl.BlockSpec((None, D_MODEL, n), lambda l, b: (l, 0, 0))],
        out_specs=pl.BlockSpec((None, m, n), lambda l, b: (l, b, 0)),
        out_shape=jax.ShapeDtypeStruct((nl, mem2.shape[0], n), BF16),
        compiler_params=_cparams("arbitrary", "arbitrary"),
        name="mem_kv_proj",
    )(mem2, g, w)


def _mem_attn_body(q_ref, kv_ref, o_ref):
    for hd in range(MEM_HEADS):
        sl = slice(hd * MEM_DP, (hd + 1) * MEM_DP)
        sv = slice(MEM_QW + hd * MEM_DP, MEM_QW + (hd + 1) * MEM_DP)
        s = lax.dot_general(q_ref[:, sl], kv_ref[:, sl], NT, preferred_element_type=F32) * MEM_HEAD_DIM ** -0.5
        p = jnp.exp(s - jnp.max(s, axis=-1, keepdims=True))
        l = jnp.sum(p, axis=-1, keepdims=True)
        o_ref[:, sl] = (_dot(p.astype(BF16), kv_ref[:, sv]) / l).astype(BF16)


def _mem_attn(mq, mem_kv, batch):
    t = mq.shape[0]
    tm = ROW_TILE
    nt = t // batch // tm
    m = mem_kv.shape[0] // batch
    return pl.pallas_call(
        _mem_attn_body,
        grid=(batch, nt),
        in_specs=[pl.BlockSpec((tm, MEM_QW), lambda b, i: (b * nt + i, 0)),
                  pl.BlockSpec((m, 2 * MEM_QW), lambda b, i: (b, 0))],
        out_specs=pl.BlockSpec((tm, MEM_QW), lambda b, i: (b * nt + i, 0)),
        out_shape=jax.ShapeDtypeStruct((t, MEM_QW), BF16),
        compiler_params=_cparams("parallel", "parallel"),
        name="mem_attn",
    )(mq, mem_kv)


def _out_proj_body(h_ref, a_ref, m_ref, wa_ref, wm_ref, o_ref):
    o_ref[...] = h_ref[...] + _dot(a_ref[...], wa_ref[...]) + _dot(m_ref[...], wm_ref[...])


def _out_proj(h, main, mo, wa, wm):
    t = h.shape[0]
    tm = ROW_TILE
    row = lambda n: pl.BlockSpec((tm, n), lambda i: (i, 0))
    full = lambda a: pl.BlockSpec(a.shape, lambda i: (0,) * a.ndim)
    return pl.pallas_call(
        _out_proj_body,
        grid=(t // tm,),
        in_specs=[row(D_MODEL), row(main.shape[1]), row(mo.shape[1]), full(wa), full(wm)],
        out_specs=row(D_MODEL),
        out_shape=jax.ShapeDtypeStruct((t, D_MODEL), F32),
        compiler_params=_cparams("parallel"),
        name="out_proj",
    )(h, main, mo, wa, wm)


def _ffn_body(h_ref, hp_ref, g_ref, wup_ref, cw_ref, cb_ref, wdn_ref, o_ref, act_ref, *, tm, tf):
    g = g_ref[...]
    nr = tm // SUBLANE
    interleave = lambda x: x.reshape(SUBLANE, nr, x.shape[-1]).swapaxes(0, 1).reshape(tm, x.shape[-1])
    deinterleave = lambda x: x.reshape(nr, SUBLANE, x.shape[-1]).swapaxes(0, 1).reshape(tm, x.shape[-1])
    h = interleave(h_ref[...])
    x = _rms(h, g).astype(BF16)
    keep = jnp.where(pl.program_id(1) > 0, 1.0, 0.0)
    x_prev = (_rms(hp_ref[...], g) * keep).astype(BF16)
    first = lax.broadcasted_iota(jnp.int32, (SUBLANE, tf), 0) == 0
    nchunk = FFN_DIM // tf

    def up(j):
        cols = [pl.ds(off + j * tf, tf) for off in (0, FFN_DIM)]
        return tuple((_dot(x, wup_ref[:, c]), _dot(x_prev, wup_ref[:, c])) for c in cols)

    def conv(u, u_prev, off):
        w = cw_ref[:, off:off + tf]
        wrap = lambda r, k: jnp.where(first, u_prev[SUBLANE - k:SUBLANE - k + 1, :],
                                      pltpu.roll(u[r * SUBLANE:(r + 1) * SUBLANE, :], 1, 0))
        back1 = jnp.concatenate([wrap(nr - 1, 1), u[0:tm - SUBLANE, :]], axis=0)
        back2 = jnp.concatenate([wrap(nr - 2, 2), wrap(nr - 1, 1), u[0:tm - 2 * SUBLANE, :]], axis=0)
        return cb_ref[:, off:off + tf] + w[0:1, :] * back2 + w[1:2, :] * back1 + w[2:3, :] * u

    u_next = up(0)
    for j in range(nchunk):
        (ua, ua_prev), (ub, ub_prev) = u_next
        if j + 1 < nchunk:
            u_next = up(j + 1)
        a = conv(ua, ua_prev, j * tf)
        b = conv(ub, ub_prev, FFN_DIM + j * tf)
        act_ref[:, j * tf:(j + 1) * tf] = (a * _sigmoid(a) * b).astype(BF16)
    o_ref[...] = deinterleave(h + _dot(act_ref[...], wdn_ref[...]))


def _conv_ffn(h, g, wup, cw, cb, wdn, batch):
    t = h.shape[0]
    tm = ROW_TILE
    nt = t // batch // tm
    hb = tm // SUBLANE
    full = lambda a: pl.BlockSpec(a.shape, lambda b, i: (0,) * a.ndim)
    return pl.pallas_call(
        functools.partial(_ffn_body, tm=tm, tf=FFN_TILE),
        grid=(batch, nt),
        in_specs=[pl.BlockSpec((tm, D_MODEL), lambda b, i: (b * nt + i, 0)),
                  pl.BlockSpec((SUBLANE, D_MODEL), lambda b, i: (jnp.maximum((b * nt + i) * hb - 1, 0), 0)),
                  full(g), full(wup), full(cw), full(cb), full(wdn)],
        out_specs=pl.BlockSpec((tm, D_MODEL), lambda b, i: (b * nt + i, 0)),
        out_shape=jax.ShapeDtypeStruct((t, D_MODEL), F32),
        scratch_shapes=[pltpu.VMEM((tm, FFN_DIM), BF16)],
        compiler_params=_cparams("parallel", "parallel"),
        name="conv_ffn",
    )(h, h, g, wup, cw, cb, wdn)


def _final_norm_body(h_ref, g_ref, o_ref):
    o_ref[...] = _rms(h_ref[...], g_ref[...])


def _final_norm(h, g):
    t = h.shape[0]
    tm = ROW_TILE
    return pl.pallas_call(
        _final_norm_body,
        grid=(t // tm,),
        in_specs=[pl.BlockSpec((tm, D_MODEL), lambda i: (i, 0)), pl.BlockSpec(g.shape, lambda i: (0, 0))],
        out_specs=pl.BlockSpec((tm, D_MODEL), lambda i: (i, 0)),
        out_shape=jax.ShapeDtypeStruct((t, D_MODEL), F32),
        compiler_params=_cparams("parallel"),
        name="final_norm",
    )(h, g)


KV_NAT = NSA_GROUPS * NSA_HEAD_DIM


def _kv_proj_body(h_ref, g_ref, w_ref, ck_ref, cv_ref, ks_ref, kw_ref, vs_ref, vw_ref, *, tm):
    xn = _rms(h_ref[...], g_ref[...]).astype(BF16)
    ck_ref[...] = _dot(xn, w_ref[:, 0:KV_NAT]).astype(BF16)
    cv_ref[...] = _dot(xn, w_ref[:, KV_NAT:2 * KV_NAT]).astype(BF16)
    key = pl.program_id(1) * tm + lax.broadcasted_iota(jnp.int32, (tm, LANE), 0)
    lane = lax.broadcasted_iota(jnp.int32, (tm, LANE), 1)
    onehot = jnp.where(lax.shift_right_logical(key, int(math.log2(SEL_BLOCK))) == lane - NSA_HEAD_DIM, 1.0, 0.0)
    low = lane < NSA_HEAD_DIM
    for gi in range(NSA_GROUPS):
        slot = lambda n: _dot(xn, w_ref[:, 2 * KV_NAT + (n * NSA_GROUPS + gi) * LANE:
                                           2 * KV_NAT + (n * NSA_GROUPS + gi + 1) * LANE])
        ks_ref[gi] = jnp.where(low, slot(0), onehot).astype(BF16)
        kw_ref[gi] = jnp.where(low, slot(1), 0.0).astype(BF16)
        vt = slot(2).T.astype(BF16)
        ones = jnp.ones((V_ROWS - NSA_HEAD_DIM, tm), BF16)
        vs_ref[gi] = jnp.concatenate([vt[0:NSA_HEAD_DIM], ones], axis=0)
        vw_ref[gi] = jnp.concatenate([vt[NSA_HEAD_DIM:], ones], axis=0)


def _kv_proj(h, g, w, batch):
    t = h.shape[0]
    s = t // batch
    tm = ROW_TILE
    nt = s // tm
    gr = NSA_GROUPS
    return pl.pallas_call(
        functools.partial(_kv_proj_body, tm=tm),
        grid=(batch, nt),
        in_specs=[pl.BlockSpec((tm, D_MODEL), lambda b, i: (b * nt + i, 0)),
                  pl.BlockSpec(g.shape, lambda b, i: (0, 0)),
                  pl.BlockSpec(w.shape, lambda b, i: (0, 0))],
        out_specs=[pl.BlockSpec((tm, KV_NAT), lambda b, i: (b * nt + i, 0)),
                   pl.BlockSpec((tm, KV_NAT), lambda b, i: (b * nt + i, 0)),
                   pl.BlockSpec((None, gr, tm, LANE), lambda b, i: (b, 0, i, 0)),
                   pl.BlockSpec((None, gr, tm, LANE), lambda b, i: (b, 0, i, 0)),
                   pl.BlockSpec((None, gr, V_ROWS, tm), lambda b, i: (b, 0, 0, i)),
                   pl.BlockSpec((None, gr, V_ROWS, tm), lambda b, i: (b, 0, 0, i))],
        out_shape=[jax.ShapeDtypeStruct((t, KV_NAT), BF16), jax.ShapeDtypeStruct((t, KV_NAT), BF16),
                   jax.ShapeDtypeStruct((batch, gr, s, LANE), BF16),
                   jax.ShapeDtypeStruct((batch, gr, s, LANE), BF16),
                   jax.ShapeDtypeStruct((batch, gr, V_ROWS, s), BF16),
                   jax.ShapeDtypeStruct((batch, gr, V_ROWS, s), BF16)],
        compiler_params=_cparams("parallel", "parallel"),
        name="nsa_kv_proj",
    )(h, g, w)


def _compress_body(x_ref, wt_ref, wb_ref, pos_ref, w1_ref, b1_ref, w2_ref, b2_ref, on_ref, ot_ref, *, ncp):
    x = x_ref[...]
    top = _dot(x, wt_ref[...])
    bot = _dot(x, wb_ref[...])
    posb = _dot(pos_ref[...], w1_ref[...])[0:1, :] + b1_ref[...]
    for gi in range(NSA_GROUPS):
        sl = slice(gi * CMP_HIDDEN, (gi + 1) * CMP_HIDDEN)
        hid = top[:, sl] + pltpu.roll(bot[:, sl], ncp - 1, 0) + posb
        hid = (hid * _sigmoid(hid)).astype(BF16)
        out = _dot(hid, w2_ref[...]) + b2_ref[...]
        on_ref[gi] = out.astype(BF16)
        ot_ref[gi] = out.T.astype(BF16)


def _compress(x16, wt, wb, pos, w1, b1, w2, b2, batch):
    ncp = x16.shape[1] // batch
    gr = NSA_GROUPS
    per_j = lambda a: pl.BlockSpec((None,) + a.shape[1:], lambda j, b: (j,) + (0,) * (a.ndim - 1))
    return pl.pallas_call(
        functools.partial(_compress_body, ncp=ncp),
        grid=(2, batch),
        in_specs=[pl.BlockSpec((None, ncp, x16.shape[2]), lambda j, b: (j, b, 0)),
                  per_j(wt), per_j(wb), per_j(pos), per_j(w1), per_j(b1), per_j(w2), per_j(b2)],
        out_specs=[pl.BlockSpec((None, None, gr, ncp, LANE), lambda j, b: (j, b, 0, 0, 0)),
                   pl.BlockSpec((None, None, gr, LANE, ncp), lambda j, b: (j, b, 0, 0, 0))],
        out_shape=[jax.ShapeDtypeStruct((2, batch, gr, ncp, LANE), BF16),
                   jax.ShapeDtypeStruct((2, batch, gr, LANE, ncp), BF16)],
        compiler_params=_cparams("arbitrary", "arbitrary"),
        name="nsa_compress",
    )(x16, wt, wb, pos, w1, b1, w2, b2)


NSA_QW = NSA_HEADS * NSA_HEAD_DIM
G3 = NSA_REP * TQ


def _nsa_in_body(h_ref, g_ref, w_ref, qt_ref, gt_ref, mq_ref):
    xn = _rms(h_ref[...], g_ref[...]).astype(BF16)
    for j in range(NSA_QW // LANE):
        y = _dot(xn, w_ref[:, j * LANE:(j + 1) * LANE]) * (NSA_HEAD_DIM ** -0.5 * LOG2E)
        qt_ref[j * LANE:(j + 1) * LANE, :] = y.T.astype(BF16)
    gt_ref[...] = _sigmoid(_dot(xn, w_ref[:, NSA_QW:NSA_QW + LANE])).T
    mq_ref[...] = _dot(xn, w_ref[:, NSA_QW + LANE:]).astype(BF16)


def _nsa_in_proj(h, g, w):
    t = h.shape[0]
    tm = ROW_TILE
    row = lambda n: pl.BlockSpec((tm, n), lambda i: (i, 0))
    col = lambda n: pl.BlockSpec((n, tm), lambda i: (0, i))
    full = lambda a: pl.BlockSpec(a.shape, lambda i: (0,) * a.ndim)
    return pl.pallas_call(
        _nsa_in_body,
        grid=(t // tm,),
        in_specs=[row(D_MODEL), full(g), full(w)],
        out_specs=[col(NSA_QW), col(LANE), row(MEM_QW)],
        out_shape=[jax.ShapeDtypeStruct((NSA_QW, t), BF16), jax.ShapeDtypeStruct((LANE, t), F32),
                   jax.ShapeDtypeStruct((t, MEM_QW), BF16)],
        compiler_params=_cparams("parallel"),
        name="nsa_in_proj",
    )(h, g, w)


def _nsa_attn_body(qt_ref, gt_ref, kc_ref, vct_ref, ks_ref, kw_ref, vs_ref, vw_ref, ovt_ref, tz_ref, cb_ref,
                   o_ref, sc_ref, qa_ref, acc_ref, ot_ref, s0_ref, s1_ref, s2_ref, s3_ref, *, nsb, ncp, n_sel):
    n = pl.program_id(1)
    t0 = n * TQ
    dh = NSA_HEAD_DIM
    tq = t0 + (lax.broadcasted_iota(jnp.int32, (ncp, G3), 1) & (TQ - 1))
    kc = lax.broadcasted_iota(jnp.int32, (ncp, G3), 0)
    cmask = tq >= kc * CMP_STRIDE + (CMP_BLOCK - 1)
    cstart = pl.multiple_of(ncp - lax.rem(n * (TQ // CMP_STRIDE), ncp), SUBLANE)
    jj = lax.broadcasted_iota(jnp.int32, (nsb, TQ), 0)
    cur = lax.shift_right_logical(t0 + lax.broadcasted_iota(jnp.int32, (nsb, TQ), 1), int(math.log2(SEL_BLOCK)))
    forced = (jj == 0) | (jj == cur) | (jj == cur - 1)
    zpad = jnp.zeros((dh, G3), BF16)
    nwt = WINDOW // TQ

    def scores(k_ref, gi, first, count, tz_index, qa):
        koff = pl.multiple_of(first * TQ, TQ)
        s = _dot(k_ref[gi, pl.ds(koff, count * TQ), :], qa)
        return [s[i * TQ:(i + 1) * TQ] + tz_ref[gi, tz_index(n - (first + i))] for i in range(count)]

    def col_max(parts):
        mx = parts[0]
        for x in parts[1:]:
            mx = jnp.maximum(mx, x)
        return jnp.max(mx, axis=0, keepdims=True)

    def probs(parts, m):
        return jnp.concatenate([jnp.exp2(x - m).astype(BF16) for x in parts], axis=0)

    def values(v_ref, gi, first, count, p):
        koff = pl.multiple_of(first * TQ, TQ)
        return _dot(v_ref[gi, :, pl.ds(koff, count * TQ)], p)

    sel_index = lambda d: jnp.where(d < 0, 2, jnp.minimum(d, 2))
    win_index = lambda d: jnp.where(d < 0, 4, jnp.where(d == nwt, 3, jnp.minimum(d, 2)))

    groups = range(NSA_GROUPS)
    gate = lambda h, c: gt_ref[pl.ds(h * 3 + c, 1), :]
    head_lanes = lambda r: slice(r * TQ, (r + 1) * TQ)

    q3s = [jnp.concatenate([qt_ref[(gi * NSA_REP + r) * dh:(gi * NSA_REP + r + 1) * dh, :] for r in range(NSA_REP)],
                           axis=1) for gi in groups]
    cs = [jnp.where(cmask, _dot(kc_ref[gi], jnp.concatenate([q3s[gi], zpad], axis=0))
                    + cb_ref[gi, pl.ds(cstart, ncp), :], NEG) for gi in groups]
    cps = [jnp.where(cmask, jnp.exp2(s - jnp.max(s, axis=0, keepdims=True)), 0.0) for s in cs]
    cps = [p * (1.0 / jnp.maximum(jnp.sum(p, axis=0, keepdims=True), 1e-30)) for p in cps]
    ocs = [_dot(vct_ref[gi], cps[gi].astype(BF16)) for gi in groups]
    scs = []
    for gi in groups:
        psum = cps[gi][:, 0:TQ]
        for r in range(1, NSA_REP):
            psum = psum + cps[gi][:, head_lanes(r)]
        p1 = psum.astype(BF16)
        p2 = (psum - p1.astype(F32)).astype(BF16)
        imp = _dot(ovt_ref[...], p1) + _dot(ovt_ref[...], p2)
        score = jnp.where(forced, 1e4, jnp.where(jj <= cur, imp[0:nsb], -1.0))
        scs.append(jnp.where(score < 0.0, -1, lax.bitcast_convert_type(score, jnp.int32)))
    for gi in groups:
        sc_ref[gi] = scs[gi]
        for r in range(NSA_REP):
            h = gi * NSA_REP + r
            ot_ref[h * dh:(h + 1) * dh, :] = gate(h, 0) * ocs[gi][0:dh, head_lanes(r)]

    scs1 = [k + 1 for k in scs]

    def rank_step(i4, cnts):
        cnts = list(cnts)
        for u in range(RANK_UNROLL):
            i = i4 * RANK_UNROLL + u
            lower = i < jj
            for gi in groups:
                rowk = sc_ref[gi, pl.ds(i, 1), :]
                before = rowk >= jnp.where(lower, scs[gi], scs1[gi])
                cnts[gi] = cnts[gi] + jnp.where(before, 1, 0)
        return tuple(cnts)
    rank_trips = jnp.minimum((2 * n + 2 + RANK_UNROLL - 1) // RANK_UNROLL, nsb // RANK_UNROLL)
    cnts = lax.fori_loop(0, rank_trips, rank_step, tuple(jnp.zeros((nsb, TQ), jnp.int32) for _ in groups))
    for gi in groups:
        selneg = jnp.where((cnts[gi] < n_sel) & (jj <= cur), 0.0, NEG).astype(BF16)
        if nsb < SEL_BLOCK:
            selneg = jnp.concatenate([selneg, jnp.zeros((SEL_BLOCK - nsb, TQ), BF16)], axis=0)
        qa_ref[gi] = jnp.concatenate([q3s[gi], jnp.concatenate([selneg] * NSA_REP, axis=1)], axis=0)

    acc_ref[...] = jnp.zeros_like(acc_ref)
    half = NSA_GROUPS // 2
    s_refs = ((s0_ref, s1_ref), (s2_ref, s3_ref))

    def score_half(it, hb):
        plist = [scores(ks_ref, hb * half + u, it * SEL_TILES, SEL_TILES, sel_index, qa_ref[hb * half + u])
                 for u in range(half)]
        for u, parts in enumerate(plist):
            for i, x in enumerate(parts):
                s_refs[hb][u][i * TQ:(i + 1) * TQ, :] = x
        return [col_max(parts) for parts in plist]

    def finish_half(it, hb, ms, bms):
        m2s = [jnp.maximum(ms[u], bms[u]) for u in range(half)]
        ps = [probs([s_refs[hb][u][i * TQ:(i + 1) * TQ, :] for i in range(SEL_TILES)], m2s[u]) for u in range(half)]
        vals = [values(vs_ref, hb * half + u, it * SEL_TILES, SEL_TILES, ps[u]) for u in range(half)]
        for u in range(half):
            gi = hb * half + u
            acc_ref[gi] = jnp.exp2(ms[u] - m2s[u]) * acc_ref[gi] + vals[u]
        return m2s

    def sel_trip(it, carry, score_next):
        ms0, ms1, bm0 = carry
        bm1 = score_half(it, 1)
        ms0 = finish_half(it, 0, ms0, bm0)
        if score_next:
            bm0 = score_half(it + 1, 0)
        ms1 = finish_half(it, 1, ms1, bm1)
        return ms0, ms1, bm0

    neg = [jnp.full((1, G3), NEG, F32) for _ in range(half)]
    carry = (neg, neg, score_half(0, 0))
    carry = lax.fori_loop(0, n // SEL_TILES, lambda it, c: sel_trip(it, c, True), carry)
    sel_trip(n // SEL_TILES, carry, False)

    wfirst = jnp.maximum(n - nwt, 0)
    wparts = [scores(kw_ref, gi, wfirst, nwt + 1, win_index, qa_ref[gi]) for gi in groups]
    wps = [probs(parts, col_max(parts)) for parts in wparts]
    ows = [values(vw_ref, gi, wfirst, nwt + 1, wps[gi]) for gi in groups]
    for gi in groups:
        o_s = acc_ref[gi, 0:dh, :] * (1.0 / acc_ref[gi, dh:dh + 1, :])
        o_w = ows[gi][0:dh, :] * (1.0 / ows[gi][dh:dh + 1, :])
        for r in range(NSA_REP):
            h = gi * NSA_REP + r
            ot_ref[h * dh:(h + 1) * dh, :] += gate(h, 1) * o_s[:, head_lanes(r)] + gate(h, 2) * o_w[:, head_lanes(r)]

    for j in range(NSA_QW // LANE):
        o_ref[:, j * LANE:(j + 1) * LANE] = ot_ref[j * LANE:(j + 1) * LANE, :].T.astype(BF16)


def _nsa_attn(qt, gt, kc, vct, ksel, kwin, vsel, vwin, ovt, tz, cb, batch):
    t = qt.shape[1]
    s = t // batch
    nt = s // TQ
    nsb = s // SEL_BLOCK
    ncp = kc.shape[-2]
    per_b = lambda a: pl.BlockSpec((None,) + a.shape[1:], lambda b, i: (b,) + (0,) * (a.ndim - 1))
    full = lambda a: pl.BlockSpec(a.shape, lambda b, i: (0,) * a.ndim)
    return pl.pallas_call(
        functools.partial(_nsa_attn_body, nsb=nsb, ncp=ncp, n_sel=min(SEL_TOPK, nsb)),
        grid=(batch, nt),
        in_specs=[pl.BlockSpec((NSA_QW, TQ), lambda b, i: (0, b * nt + i)),
                  pl.BlockSpec((LANE, TQ), lambda b, i: (0, b * nt + i)),
                  per_b(kc), per_b(vct), per_b(ksel), per_b(kwin), per_b(vsel), per_b(vwin),
                  full(ovt), full(tz), full(cb)],
        out_specs=pl.BlockSpec((TQ, NSA_QW), lambda b, i: (b * nt + i, 0)),
        out_shape=jax.ShapeDtypeStruct((t, NSA_QW), BF16),
        scratch_shapes=[pltpu.VMEM((NSA_GROUPS, nsb, TQ), jnp.int32), pltpu.VMEM((NSA_GROUPS, LANE, G3), BF16),
                        pltpu.VMEM((NSA_GROUPS, V_ROWS, G3), F32), pltpu.VMEM((NSA_QW, TQ), F32),
                        ] + [pltpu.VMEM((SEL_TILES * TQ, G3), F32)] * NSA_GROUPS,
        compiler_params=_cparams("parallel", "arbitrary"),
        name="nsa_attn",
    )(qt, gt, kc, vct, ksel, kwin, vsel, vwin, ovt, tz, cb)


def _rel_bucket_np(dist):
    dist = np.maximum(dist, 0)
    max_exact = REL_BUCKETS // 2
    ratio = np.log(np.maximum(dist, 1).astype(np.float32) / np.float32(max_exact)) / np.float32(
        math.log(REL_MAX_DIST / max_exact))
    large = max_exact + (ratio * np.float32(REL_BUCKETS - max_exact)).astype(np.int32)
    large = np.minimum(large, REL_BUCKETS - 1)
    return np.where(dist < max_exact, dist, large).astype(np.int32)


def _group_lanes(a):
    hh, r, c = a.shape
    return a.reshape(NSA_GROUPS, NSA_REP, r, c).transpose(0, 2, 1, 3).reshape(NSA_GROUPS, r, NSA_REP * c)


def _bias_tables(rel_bias, ncp):
    k = np.arange(TQ)[:, None]
    q = np.arange(TQ)[None, :]
    tbl = rel_bias.astype(F32)

    def lookup(idx):
        onehot = (jnp.asarray(idx.reshape(1, -1)) == jnp.arange(REL_BUCKETS)[:, None]).astype(F32)
        out = jnp.dot(tbl.T, onehot, precision=lax.Precision.HIGHEST)
        return out.reshape((NSA_HEADS,) + idx.shape)

    far = jnp.broadcast_to(tbl[REL_BUCKETS - 1][:, None, None], (NSA_HEADS, TQ, TQ))
    t0 = jnp.where(jnp.asarray(k <= q)[None], lookup(_rel_bucket_np(q - k)), NEG)
    t1 = lookup(_rel_bucket_np(TQ + q - k))
    t3 = jnp.where(jnp.asarray(k > q)[None], far, NEG)
    tz = jnp.stack([_group_lanes(x) for x in (t0, t1, far, t3, jnp.full_like(far, NEG))], axis=1)
    kk = np.arange(ncp)[:, None]
    m = np.where(kk < ncp // 2, -kk, ncp - kk)
    d = CMP_STRIDE * m + q - (CMP_BLOCK - 1)
    idx = np.where((d >= 0) & (d < REL_MAX_DIST), _rel_bucket_np(d), REL_BUCKETS - 1)
    cb = _group_lanes(lookup(idx))
    return tz * LOG2E, jnp.concatenate([cb, cb], axis=1) * LOG2E


def _overlap_table(s, ncp):
    nsb = s // SEL_BLOCK
    nc = (s - CMP_BLOCK) // CMP_STRIDE + 1
    cs = np.arange(ncp) * CMP_STRIDE
    ce = cs + CMP_BLOCK - 1
    ss = np.arange(SEL_BLOCK) * SEL_BLOCK
    ov = (cs[None, :] < ss[:, None] + SEL_BLOCK) & (ce[None, :] >= ss[:, None])
    ov &= (np.arange(ncp) < nc)[None, :] & (np.arange(SEL_BLOCK) < nsb)[:, None]
    return jnp.asarray(ov, dtype=BF16)


def kernel(x, mem, norm_mix, norm_mem, w_mem_kv, w_out, norm_ffn, w_up, conv_w, conv_b, w_down,
           gla_w_in, gla_w_gate_up, gla_b_gate, gla_out_norm, nsa_w_in, kv_norm, w_kv_shared,
           cmp_pos, cmp_w1, cmp_b1, cmp_w2, cmp_b2, rel_bias, final_norm):
    batch, seq = x.shape[0], x.shape[1]
    t = batch * seq
    h = x.reshape(t, D_MODEL)
    row = lambda v: v.reshape(1, -1).astype(F32)

    wk, wv = w_mem_kv[..., :MEM_W], w_mem_kv[..., MEM_W:]
    w_mkv = jnp.concatenate([_pad_heads(wk, MEM_HEADS, MEM_HEAD_DIM, MEM_DP),
                             _pad_heads(wv, MEM_HEADS, MEM_HEAD_DIM, MEM_DP)], axis=-1).astype(BF16)
    mem_kv_all = _mem_kv_proj(mem.reshape(-1, D_MODEL), norm_mem.reshape(DEPTH, 1, D_MODEL), w_mkv, batch)

    shared = None
    for i in range(DEPTH):
        w_o = w_out[i]
        w_o_mem = _pad_head_rows(w_o[MAIN_W:], MEM_HEADS, MEM_HEAD_DIM, MEM_DP).astype(BF16)
        if i < N_A_LAYERS:
            wi = gla_w_in[i]
            c0 = GLA_HEADS * GLA_DK
            c1 = 2 * c0
            c2 = c1 + GLA_HEADS * GLA_DV
            c3 = c2 + GLA_HEADS * GLA_DV
            c4 = c3 + GLA_RANK
            w_all = jnp.concatenate([
                _pad_heads(wi[:, :c0], GLA_HEADS, GLA_DK, GLA_DKP),
                _pad_heads(wi[:, c0:c1], GLA_HEADS, GLA_DK, GLA_DKP),
                _pad_heads(wi[:, c1:c2], GLA_HEADS, GLA_DV, GLA_DVP),
                _pad_heads(wi[:, c2:c3], GLA_HEADS, GLA_DV, GLA_DVP),
                _pad_heads(wi[:, c3:c4], 1, GLA_RANK, LANE),
                _pad_heads(wi[:, c4:], MEM_HEADS, MEM_HEAD_DIM, MEM_DP)], axis=1).astype(BF16)
            wg = jnp.pad(_pad_heads(gla_w_gate_up[i], GLA_HEADS, GLA_DK, GLA_DKP),
                         ((0, LANE - GLA_RANK), (0, 0))).astype(BF16)
            bg = row(_pad_heads(gla_b_gate[i], GLA_HEADS, GLA_DK, GLA_DKP))
            q, k, v, go, la, mq = _gla_in_proj(h, row(norm_mix[i]), w_all, wg, bg)
            on = row(jnp.pad(gla_out_norm[i], (0, GLA_DVP - GLA_DV)))
            main = _gla_mixer(q, k, la, v, go, on, batch)
            w_o_main = _pad_head_rows(w_o[:MAIN_W], GLA_HEADS, GLA_DV, GLA_DVP).astype(BF16)
        else:
            if shared is None:
                ncp = seq // CMP_STRIDE
                gr, dh = NSA_GROUPS, NSA_HEAD_DIM
                wkv = w_kv_shared.reshape(D_MODEL, 6, gr, dh)
                pair = lambda a, b: jnp.concatenate([wkv[:, a], wkv[:, b]], axis=-1).reshape(D_MODEL, gr * 2 * dh)
                slot = lambda a: _pad_heads(wkv[:, a].reshape(D_MODEL, KV_NAT), gr, dh, LANE)
                w_kv = jnp.concatenate([wkv[:, 0].reshape(D_MODEL, KV_NAT), wkv[:, 1].reshape(D_MODEL, KV_NAT),
                                        slot(2), slot(4), pair(3, 5)], axis=1).astype(BF16)
                ck, cv, ksel, kwin, vsel, vwin = _kv_proj(h, row(kv_norm), w_kv, batch)
                x16 = jnp.stack([ck.reshape(batch * ncp, CMP_STRIDE * KV_NAT),
                                 cv.reshape(batch * ncp, CMP_STRIDE * KV_NAT)])
                w1 = cmp_w1.reshape(2, 2, CMP_STRIDE, dh, CMP_HIDDEN)
                eye = jnp.eye(gr, dtype=F32)
                w1x = jnp.einsum('jhldc,gk->jhlgdkc', w1, eye).reshape(2, 2, CMP_STRIDE * KV_NAT, gr * CMP_HIDDEN)
                w1x = w1x.astype(BF16)
                pos8 = jnp.broadcast_to(cmp_pos.reshape(2, 1, CMP_BLOCK * dh), (2, SUBLANE, CMP_BLOCK * dh)).astype(BF16)
                w2p = jnp.pad(cmp_w2, ((0, 0), (0, 0), (0, LANE - dh))).astype(BF16)
                b2p = jnp.pad(cmp_b2, ((0, 0), (0, LANE - dh))).reshape(2, 1, LANE).astype(F32)
                cnat, ctr = _compress(x16, w1x[:, 0], w1x[:, 1], pos8, cmp_w1.astype(BF16),
                                      cmp_b1.reshape(2, 1, CMP_HIDDEN).astype(F32), w2p, b2p, batch)
                tz, cb = _bias_tables(rel_bias, ncp)
                ov = _overlap_table(seq, ncp)
                shared = (cnat[0], ctr[1], ksel, kwin, vsel, vwin, ov, tz, cb)
            wi = nsa_w_in[i - N_A_LAYERS]
            c0 = NSA_HEADS * NSA_HEAD_DIM
            c1 = c0 + NSA_HEADS * 3
            w_all = jnp.concatenate([
                wi[:, :c0],
                _pad_heads(wi[:, c0:c1], 1, NSA_HEADS * 3, LANE),
                _pad_heads(wi[:, c1:], MEM_HEADS, MEM_HEAD_DIM, MEM_DP)], axis=1).astype(BF16)
            q, gates, mq = _nsa_in_proj(h, row(norm_mix[i]), w_all)
            main = _nsa_attn(q, gates, *shared, batch)
            w_o_main = w_o[:MAIN_W].astype(BF16)
        mo = _mem_attn(mq, mem_kv_all[i], batch)
        h = _out_proj(h, main, mo, w_o_main, w_o_mem)
        h = _conv_ffn(h, row(norm_ffn[i]), w_up[i].astype(BF16), conv_w[i].astype(F32), row(conv_b[i]),
                      w_down[i].astype(BF16), batch)
    return _final_norm(h, row(final_norm)).reshape(batch, seq, D_MODEL)
```

```python
import functools
import math

import numpy as np
import jax
import jax.numpy as jnp
from jax import lax
from jax.experimental import pallas as pl
from jax.experimental.pallas import tpu as pltpu

F32 = jnp.float32
BF16 = jnp.bfloat16

D_MODEL = 1024
DEPTH = 4
N_A_LAYERS = DEPTH // 2
MEM_HEADS = 4
MEM_HEAD_DIM = 64
MEM_W = MEM_HEADS * MEM_HEAD_DIM
MAIN_W = D_MODEL - MEM_W
GLA_HEADS = 4
GLA_DV = MAIN_W // GLA_HEADS
GLA_DK = GLA_DV // 2
GLA_RANK = 16
GLA_GATE_NORM = 16.0
NSA_HEADS = 12
NSA_GROUPS = 4
NSA_HEAD_DIM = MAIN_W // NSA_HEADS
NSA_REP = NSA_HEADS // NSA_GROUPS
CMP_BLOCK = 32
CMP_STRIDE = 16
CMP_HIDDEN = 128
SEL_BLOCK = 64
SEL_TOPK = 16
WINDOW = 512
REL_BUCKETS = 32
REL_MAX_DIST = 128
FFN_DIM = 2816
CONV_WIDTH = 3
EPS = 1e-6

LANE = 128
SUBLANE = 8
VMEM_LIMIT = 56 * 1024 * 1024
GLA_DKP = LANE
GLA_DVP = 2 * LANE
MEM_DP = LANE
NEG = -1e30
TQ = 128
ROW_TILE = 512
GLA_CHUNK = 64
FFN_TILE = 256
SEL_TILES = 4
V_ROWS = 64 + 16
RANK_UNROLL = 4
LOG2E = math.log2(math.e)

NT = (((1,), (1,)), ((), ()))
TN = (((0,), (0,)), ((), ()))


def _cparams(*sem):
    return pltpu.CompilerParams(dimension_semantics=sem, vmem_limit_bytes=VMEM_LIMIT)


def _rms(x, g):
    return x * lax.rsqrt(jnp.mean(x * x, axis=-1, keepdims=True) + EPS) * g


def _sigmoid(x):
    return 1.0 / (1.0 + jnp.exp(-x))


def _dot(a, b):
    return jnp.dot(a, b, preferred_element_type=F32)


def _resident(a):
    return pl.BlockSpec(a.shape, lambda *_: (0,) * a.ndim, pipeline_mode=pl.Buffered(1))


def _pad_heads(w, nh, d, dp):
    lead = w.shape[:-1]
    w = w.reshape(lead + (nh, d))
    w = jnp.pad(w, [(0, 0)] * len(lead) + [(0, 0), (0, dp - d)])
    return w.reshape(lead + (nh * dp,))


def _pad_head_rows(w, nh, d, dp):
    n = w.shape[-1]
    w = jnp.pad(w.reshape(nh, d, n), ((0, 0), (0, dp - d), (0, 0)))
    return w.reshape(nh * dp, n)


GLA_QW = GLA_HEADS * GLA_DKP
GLA_VW = GLA_HEADS * GLA_DVP
MEM_QW = MEM_HEADS * MEM_DP
GLA_OFF_K = GLA_QW
GLA_OFF_V = 2 * GLA_QW
GLA_OFF_G = GLA_OFF_V + GLA_VW
GLA_OFF_LR = GLA_OFF_G + GLA_VW
GLA_OFF_MQ = GLA_OFF_LR + LANE
GLA_NP = GLA_OFF_MQ + MEM_QW


def _gla_in_body(h_ref, g_ref, w_ref, wg_ref, bg_ref, q_ref, k_ref, v_ref, go_ref, la_ref, mq_ref):
    xn = _rms(h_ref[...], g_ref[...]).astype(BF16)

    def proj(lo, n):
        return _dot(xn, w_ref[:, lo:lo + n])

    q_ref[...] = proj(0, GLA_QW).astype(BF16)
    k_ref[...] = proj(GLA_OFF_K, GLA_QW).astype(BF16)
    for j in range(GLA_VW // GLA_QW):
        v_ref[:, j * GLA_QW:(j + 1) * GLA_QW] = proj(GLA_OFF_V + j * GLA_QW, GLA_QW).astype(BF16)
        go_ref[:, j * GLA_QW:(j + 1) * GLA_QW] = proj(GLA_OFF_G + j * GLA_QW, GLA_QW).astype(BF16)
    lr = proj(GLA_OFF_LR, LANE).astype(BF16)
    z = _dot(lr, wg_ref[...]) + bg_ref[...]
    la_ref[...] = (jnp.minimum(z, 0.0) - jnp.log(1.0 + jnp.exp(-jnp.abs(z)))) * (1.0 / GLA_GATE_NORM)
    mq_ref[...] = proj(GLA_OFF_MQ, MEM_QW).astype(BF16)


def _gla_in_proj(h, g, w, wg, bg):
    t = h.shape[0]
    tm = ROW_TILE
    row = lambda n: pl.BlockSpec((tm, n), lambda i: (i, 0))
    full = lambda a: pl.BlockSpec(a.shape, lambda i: (0,) * a.ndim)
    return pl.pallas_call(
        _gla_in_body,
        grid=(t // tm,),
        in_specs=[row(D_MODEL), full(g), full(w), full(wg), full(bg)],
        out_specs=[row(GLA_QW), row(GLA_QW), row(GLA_VW), row(GLA_VW), row(GLA_QW), row(MEM_QW)],
        out_shape=[jax.ShapeDtypeStruct((t, GLA_QW), BF16), jax.ShapeDtypeStruct((t, GLA_QW), BF16),
                   jax.ShapeDtypeStruct((t, GLA_VW), BF16), jax.ShapeDtypeStruct((t, GLA_VW), BF16),
                   jax.ShapeDtypeStruct((t, GLA_QW), F32), jax.ShapeDtypeStruct((t, MEM_QW), BF16)],
        compiler_params=_cparams("parallel"),
        name="gla_in_proj",
    )(h, g, w, wg, bg)


def _gla_mix_body(q_ref, k_ref, la_ref, v_ref, go_ref, on_ref, o_ref, st_ref, *, cn, batch):
    @pl.when(pl.program_id(0) == 0)
    def _():
        st_ref[...] = jnp.zeros_like(st_ref)

    row = lax.broadcasted_iota(jnp.int32, (cn, cn), 0)
    col = lax.broadcasted_iota(jnp.int32, (cn, cn), 1)
    causal = row >= col
    tril = jnp.where(causal, 1.0, 0.0).astype(BF16)
    scale = GLA_DK ** -0.5
    chains = [(bi, hd) for bi in range(batch) for hd in range(GLA_HEADS)]
    sk = lambda hd: slice(hd * GLA_DKP, (hd + 1) * GLA_DKP)
    sv = lambda hd: slice(hd * GLA_DVP, (hd + 1) * GLA_DVP)
    dg = lambda x, y, dims: lax.dot_general(x, y, dims, preferred_element_type=F32)

    def cum_decay(bi, hd):
        la = la_ref[bi, :, sk(hd)]
        la1 = la.astype(BF16)
        r1 = la - la1.astype(F32)
        la2 = r1.astype(BF16)
        la3 = (r1 - la2.astype(F32)).astype(BF16)
        return _dot(tril, la1) + _dot(tril, la2) + _dot(tril, la3)

    bs = [cum_decay(bi, hd) for bi, hd in chains]
    scaled = []
    for (bi, hd), b in zip(chains, bs):
        b_last = b[cn - 1:cn, :]
        q = q_ref[bi, :, sk(hd)].astype(F32)
        k = k_ref[bi, :, sk(hd)].astype(F32)
        scaled.append(((q * jnp.exp(b) * scale).astype(BF16), (k * jnp.exp(-b)).astype(BF16),
                       (k * jnp.exp(b_last - b)).astype(BF16), jnp.exp(b_last)))
    intra = [jnp.where(causal, dg(qs, ks, NT), 0.0).astype(BF16) for qs, ks, _, _ in scaled]
    outs = []
    for i, (bi, hd) in enumerate(chains):
        qs, _, ko, a_last = scaled[i]
        v = v_ref[bi, :, sv(hd)]
        st = st_ref[i]
        outs.append(_dot(intra[i], v) + dg(qs, st.astype(BF16), NT))
        st_ref[i] = st * a_last + dg(v, ko, TN)
    for (bi, hd), o in zip(chains, outs):
        ms = jnp.sum(o * o, axis=-1, keepdims=True) * (1.0 / GLA_DV)
        y = o * lax.rsqrt(ms + EPS) * on_ref[...]
        g = go_ref[bi, :, sv(hd)].astype(F32)
        o_ref[bi, :, sv(hd)] = (y * (g * _sigmoid(g))).astype(BF16)


def _gla_mixer(q, k, la, v, go, on, batch):
    t = q.shape[0]
    s = t // batch
    cn = GLA_CHUNK
    seq = lambda a: a.reshape(batch, s, a.shape[1])
    blk = lambda n: pl.BlockSpec((batch, cn, n), lambda c: (0, c, 0))
    out = pl.pallas_call(
        functools.partial(_gla_mix_body, cn=cn, batch=batch),
        grid=(s // cn,),
        in_specs=[blk(GLA_QW), blk(GLA_QW), blk(GLA_QW), blk(GLA_VW), blk(GLA_VW),
                  pl.BlockSpec(on.shape, lambda c: (0, 0))],
        out_specs=blk(GLA_VW),
        out_shape=jax.ShapeDtypeStruct((batch, s, GLA_VW), BF16),
        scratch_shapes=[pltpu.VMEM((batch * GLA_HEADS, GLA_DVP, GLA_DKP), F32)],
        compiler_params=_cparams("arbitrary"),
        name="gla_mixer",
    )(seq(q), seq(k), seq(la), seq(v), seq(go), on)
    return out.reshape(t, GLA_VW)


def _mem_kv_body(m_ref, g_ref, w_ref, o_ref):
    xn = _rms(m_ref[...], g_ref[...]).astype(BF16)
    o_ref[...] = _dot(xn, w_ref[...]).astype(BF16)


def _mem_kv_proj(mem2, g, w, batch):
    nl = w.shape[0]
    m = mem2.shape[0] // batch
    n = w.shape[2]
    return pl.pallas_call(
        _mem_kv_body,
        grid=(nl, batch),
        in_specs=[pl.BlockSpec((m, D_MODEL), lambda l, b: (b, 0)),
                  pl.BlockSpec((None, 1, D_MODEL), lambda l, b: (l, 0, 0)),
                  pl.BlockSpec((None, D_MODEL, n), lambda l, b: (l, 0, 0))],
        out_specs=pl.BlockSpec((None, m, n), lambda l, b: (l, b, 0)),
        out_shape=jax.ShapeDtypeStruct((nl, mem2.shape[0], n), BF16),
        compiler_params=_cparams("arbitrary", "arbitrary"),
        name="mem_kv_proj",
    )(mem2, g, w)


def _mem_attn_body(q_ref, kv_ref, o_ref):
    for hd in range(MEM_HEADS):
        sl = slice(hd * MEM_DP, (hd + 1) * MEM_DP)
        sv = slice(MEM_QW + hd * MEM_DP, MEM_QW + (hd + 1) * MEM_DP)
        s = lax.dot_general(q_ref[:, sl], kv_ref[:, sl], NT, preferred_element_type=F32) * MEM_HEAD_DIM ** -0.5
        p = jnp.exp(s - jnp.max(s, axis=-1, keepdims=True))
        l = jnp.sum(p, axis=-1, keepdims=True)
        o_ref[:, sl] = (_dot(p.astype(BF16), kv_ref[:, sv]) / l).astype(BF16)


def _mem_attn(mq, mem_kv, batch):
    t = mq.shape[0]
    tm = ROW_TILE
    nt = t // batch // tm
    m = mem_kv.shape[0] // batch
    return pl.pallas_call(
        _mem_attn_body,
        grid=(batch, nt),
        in_specs=[pl.BlockSpec((tm, MEM_QW), lambda b, i: (b * nt + i, 0)),
                  pl.BlockSpec((m, 2 * MEM_QW), lambda b, i: (b, 0))],
        out_specs=pl.BlockSpec((tm, MEM_QW), lambda b, i: (b * nt + i, 0)),
        out_shape=jax.ShapeDtypeStruct((t, MEM_QW), BF16),
        compiler_params=_cparams("parallel", "parallel"),
        name="mem_attn",
    )(mq, mem_kv)


BF16_ROWS = 2 * SUBLANE


def _mix_ffn_body(h_ref, hp_ref, a_ref, ap_ref, m_ref, mp_ref, wa_ref, wm_ref, g_ref, wup_ref, cw_ref, cb_ref,
                  wdn_ref, o_ref, act_ref, *, tm, tf):
    g = g_ref[...]
    nr = tm // SUBLANE
    interleave = lambda x: x.reshape(SUBLANE, nr, x.shape[-1]).swapaxes(0, 1).reshape(tm, x.shape[-1])
    deinterleave = lambda x: x.reshape(nr, SUBLANE, x.shape[-1]).swapaxes(0, 1).reshape(tm, x.shape[-1])
    h = interleave(h_ref[...] + _dot(a_ref[...], wa_ref[...]) + _dot(m_ref[...], wm_ref[...]))
    x = _rms(h, g).astype(BF16)
    keep = jnp.where(pl.program_id(1) > 0, 1.0, 0.0)
    h_prev = (hp_ref[...] + _dot(ap_ref[...], wa_ref[...]) + _dot(mp_ref[...], wm_ref[...]))[BF16_ROWS - SUBLANE:]
    x_prev = (_rms(h_prev, g) * keep).astype(BF16)
    first = lax.broadcasted_iota(jnp.int32, (SUBLANE, tf), 0) == 0
    nchunk = FFN_DIM // tf

    def up(j):
        cols = [pl.ds(off + j * tf, tf) for off in (0, FFN_DIM)]
        return tuple((_dot(x, wup_ref[:, c]), _dot(x_prev, wup_ref[:, c])) for c in cols)

    def conv(u, u_prev, off):
        w = cw_ref[:, off:off + tf]
        wrap = lambda r, k: jnp.where(first, u_prev[SUBLANE - k:SUBLANE - k + 1, :],
                                      pltpu.roll(u[r * SUBLANE:(r + 1) * SUBLANE, :], 1, 0))
        back1 = jnp.concatenate([wrap(nr - 1, 1), u[0:tm - SUBLANE, :]], axis=0)
        back2 = jnp.concatenate([wrap(nr - 2, 2), wrap(nr - 1, 1), u[0:tm - 2 * SUBLANE, :]], axis=0)
        return cb_ref[:, off:off + tf] + w[0:1, :] * back2 + w[1:2, :] * back1 + w[2:3, :] * u

    u_next = up(0)
    for j in range(nchunk):
        (ua, ua_prev), (ub, ub_prev) = u_next
        if j + 1 < nchunk:
            u_next = up(j + 1)
        a = conv(ua, ua_prev, j * tf)
        b = conv(ub, ub_prev, FFN_DIM + j * tf)
        act_ref[:, j * tf:(j + 1) * tf] = (a * _sigmoid(a) * b).astype(BF16)
    o_ref[...] = deinterleave(h + _dot(act_ref[...], wdn_ref[...]))


def _mix_ffn(h, main, mo, wa, wm, g, wup, cw, cb, wdn, batch):
    t = h.shape[0]
    tm = ROW_TILE
    nt = t // batch // tm
    hb = tm // BF16_ROWS
    cur = lambda n: pl.BlockSpec((tm, n), lambda b, i: (b * nt + i, 0))
    prev = lambda n: pl.BlockSpec((BF16_ROWS, n), lambda b, i: (jnp.maximum((b * nt + i) * hb - 1, 0), 0))
    return pl.pallas_call(
        functools.partial(_mix_ffn_body, tm=tm, tf=FFN_TILE),
        grid=(batch, nt),
        in_specs=[cur(D_MODEL), prev(D_MODEL), cur(main.shape[1]), prev(main.shape[1]),
                  cur(mo.shape[1]), prev(mo.shape[1]), _resident(wa), _resident(wm), _resident(g),
                  _resident(wup), _resident(cw), _resident(cb), _resident(wdn)],
        out_specs=cur(D_MODEL),
        out_shape=jax.ShapeDtypeStruct((t, D_MODEL), F32),
        scratch_shapes=[pltpu.VMEM((tm, FFN_DIM), BF16)],
        compiler_params=_cparams("parallel", "parallel"),
        name="mix_ffn",
    )(h, h, main, main, mo, mo, wa, wm, g, wup, cw, cb, wdn)


def _final_norm_body(h_ref, g_ref, o_ref):
    o_ref[...] = _rms(h_ref[...], g_ref[...])


def _final_norm(h, g):
    t = h.shape[0]
    tm = ROW_TILE
    return pl.pallas_call(
        _final_norm_body,
        grid=(t // tm,),
        in_specs=[pl.BlockSpec((tm, D_MODEL), lambda i: (i, 0)), pl.BlockSpec(g.shape, lambda i: (0, 0))],
        out_specs=pl.BlockSpec((tm, D_MODEL), lambda i: (i, 0)),
        out_shape=jax.ShapeDtypeStruct((t, D_MODEL), F32),
        compiler_params=_cparams("parallel"),
        name="final_norm",
    )(h, g)


KV_NAT = NSA_GROUPS * NSA_HEAD_DIM


def _kv_proj_body(h_ref, g_ref, w_ref, ck_ref, cv_ref, ks_ref, kw_ref, vs_ref, vw_ref, *, tm):
    xn = _rms(h_ref[...], g_ref[...]).astype(BF16)
    ck_ref[...] = _dot(xn, w_ref[:, 0:KV_NAT]).astype(BF16)
    cv_ref[...] = _dot(xn, w_ref[:, KV_NAT:2 * KV_NAT]).astype(BF16)
    key = pl.program_id(1) * tm + lax.broadcasted_iota(jnp.int32, (tm, LANE), 0)
    lane = lax.broadcasted_iota(jnp.int32, (tm, LANE), 1)
    onehot = jnp.where(lax.shift_right_logical(key, int(math.log2(SEL_BLOCK))) == lane - NSA_HEAD_DIM, 1.0, 0.0)
    low = lane < NSA_HEAD_DIM
    for gi in range(NSA_GROUPS):
        slot = lambda n: _dot(xn, w_ref[:, 2 * KV_NAT + (n * NSA_GROUPS + gi) * LANE:
                                           2 * KV_NAT + (n * NSA_GROUPS + gi + 1) * LANE])
        ks_ref[gi] = jnp.where(low, slot(0), onehot).astype(BF16)
        kw_ref[gi] = jnp.where(low, slot(1), 0.0).astype(BF16)
        vt = slot(2).T.astype(BF16)
        ones = jnp.ones((V_ROWS - NSA_HEAD_DIM, tm), BF16)
        vs_ref[gi] = jnp.concatenate([vt[0:NSA_HEAD_DIM], ones], axis=0)
        vw_ref[gi] = jnp.concatenate([vt[NSA_HEAD_DIM:], ones], axis=0)


def _kv_proj(h, g, w, batch):
    t = h.shape[0]
    s = t // batch
    tm = ROW_TILE
    nt = s // tm
    gr = NSA_GROUPS
    return pl.pallas_call(
        functools.partial(_kv_proj_body, tm=tm),
        grid=(batch, nt),
        in_specs=[pl.BlockSpec((tm, D_MODEL), lambda b, i: (b * nt + i, 0)),
                  pl.BlockSpec(g.shape, lambda b, i: (0, 0)),
                  pl.BlockSpec(w.shape, lambda b, i: (0, 0))],
        out_specs=[pl.BlockSpec((tm, KV_NAT), lambda b, i: (b * nt + i, 0)),
                   pl.BlockSpec((tm, KV_NAT), lambda b, i: (b * nt + i, 0)),
                   pl.BlockSpec((None, gr, tm, LANE), lambda b, i: (b, 0, i, 0)),
                   pl.BlockSpec((None, gr, tm, LANE), lambda b, i: (b, 0, i, 0)),
                   pl.BlockSpec((None, gr, V_ROWS, tm), lambda b, i: (b, 0, 0, i)),
                   pl.BlockSpec((None, gr, V_ROWS, tm), lambda b, i: (b, 0, 0, i))],
        out_shape=[jax.ShapeDtypeStruct((t, KV_NAT), BF16), jax.ShapeDtypeStruct((t, KV_NAT), BF16),
                   jax.ShapeDtypeStruct((batch, gr, s, LANE), BF16),
                   jax.ShapeDtypeStruct((batch, gr, s, LANE), BF16),
                   jax.ShapeDtypeStruct((batch, gr, V_ROWS, s), BF16),
                   jax.ShapeDtypeStruct((batch, gr, V_ROWS, s), BF16)],
        compiler_params=_cparams("parallel", "parallel"),
        name="nsa_kv_proj",
    )(h, g, w)


def _compress_body(x_ref, wt_ref, wb_ref, pos_ref, w1_ref, b1_ref, w2_ref, b2_ref, on_ref, ot_ref, *, ncp):
    x = x_ref[...]
    top = _dot(x, wt_ref[...])
    bot = _dot(x, wb_ref[...])
    posb = _dot(pos_ref[...], w1_ref[...])[0:1, :] + b1_ref[...]
    for gi in range(NSA_GROUPS):
        sl = slice(gi * CMP_HIDDEN, (gi + 1) * CMP_HIDDEN)
        hid = top[:, sl] + pltpu.roll(bot[:, sl], ncp - 1, 0) + posb
        hid = (hid * _sigmoid(hid)).astype(BF16)
        out = _dot(hid, w2_ref[...]) + b2_ref[...]
        on_ref[gi] = out.astype(BF16)
        ot_ref[gi] = out.T.astype(BF16)


def _compress(x16, wt, wb, pos, w1, b1, w2, b2, batch):
    ncp = x16.shape[1] // batch
    gr = NSA_GROUPS
    per_j = lambda a: pl.BlockSpec((None,) + a.shape[1:], lambda j, b: (j,) + (0,) * (a.ndim - 1))
    return pl.pallas_call(
        functools.partial(_compress_body, ncp=ncp),
        grid=(2, batch),
        in_specs=[pl.BlockSpec((None, ncp, x16.shape[2]), lambda j, b: (j, b, 0)),
                  per_j(wt), per_j(wb), per_j(pos), per_j(w1), per_j(b1), per_j(w2), per_j(b2)],
        out_specs=[pl.BlockSpec((None, None, gr, ncp, LANE), lambda j, b: (j, b, 0, 0, 0)),
                   pl.BlockSpec((None, None, gr, LANE, ncp), lambda j, b: (j, b, 0, 0, 0))],
        out_shape=[jax.ShapeDtypeStruct((2, batch, gr, ncp, LANE), BF16),
                   jax.ShapeDtypeStruct((2, batch, gr, LANE, ncp), BF16)],
        compiler_params=_cparams("arbitrary", "arbitrary"),
        name="nsa_compress",
    )(x16, wt, wb, pos, w1, b1, w2, b2)


NSA_QW = NSA_HEADS * NSA_HEAD_DIM
G3 = NSA_REP * TQ


def _nsa_in_body(h_ref, g_ref, w_ref, qt_ref, gt_ref, mq_ref):
    xn = _rms(h_ref[...], g_ref[...]).astype(BF16)
    for j in range(NSA_QW // LANE):
        y = _dot(xn, w_ref[:, j * LANE:(j + 1) * LANE]) * (NSA_HEAD_DIM ** -0.5 * LOG2E)
        qt_ref[j * LANE:(j + 1) * LANE, :] = y.T.astype(BF16)
    gt_ref[...] = _sigmoid(_dot(xn, w_ref[:, NSA_QW:NSA_QW + LANE])).T
    mq_ref[...] = _dot(xn, w_ref[:, NSA_QW + LANE:]).astype(BF16)


def _nsa_in_proj(h, g, w):
    t = h.shape[0]
    tm = ROW_TILE
    row = lambda n: pl.BlockSpec((tm, n), lambda i: (i, 0))
    col = lambda n: pl.BlockSpec((n, tm), lambda i: (0, i))
    full = lambda a: pl.BlockSpec(a.shape, lambda i: (0,) * a.ndim)
    return pl.pallas_call(
        _nsa_in_body,
        grid=(t // tm,),
        in_specs=[row(D_MODEL), full(g), full(w)],
        out_specs=[col(NSA_QW), col(LANE), row(MEM_QW)],
        out_shape=[jax.ShapeDtypeStruct((NSA_QW, t), BF16), jax.ShapeDtypeStruct((LANE, t), F32),
                   jax.ShapeDtypeStruct((t, MEM_QW), BF16)],
        compiler_params=_cparams("parallel"),
        name="nsa_in_proj",
    )(h, g, w)


def _nsa_attn_body(qt_ref, gt_ref, kc_ref, vct_ref, ks_ref, kw_ref, vs_ref, vw_ref, ovt_ref, tz_ref, cb_ref,
                   o_ref, sc_ref, qa_ref, acc_ref, ot_ref, s0_ref, s1_ref, s2_ref, s3_ref, *, nsb, ncp, n_sel):
    n = pl.program_id(1)
    t0 = n * TQ
    dh = NSA_HEAD_DIM
    tq = t0 + (lax.broadcasted_iota(jnp.int32, (ncp, G3), 1) & (TQ - 1))
    kc = lax.broadcasted_iota(jnp.int32, (ncp, G3), 0)
    cmask = tq >= kc * CMP_STRIDE + (CMP_BLOCK - 1)
    cstart = pl.multiple_of(ncp - lax.rem(n * (TQ // CMP_STRIDE), ncp), SUBLANE)
    jj = lax.broadcasted_iota(jnp.int32, (nsb, TQ), 0)
    cur = lax.shift_right_logical(t0 + lax.broadcasted_iota(jnp.int32, (nsb, TQ), 1), int(math.log2(SEL_BLOCK)))
    forced = (jj == 0) | (jj == cur) | (jj == cur - 1)
    zpad = jnp.zeros((dh, G3), BF16)
    nwt = WINDOW // TQ

    def scores(k_ref, gi, first, count, tz_index, qa):
        koff = pl.multiple_of(first * TQ, TQ)
        s = _dot(k_ref[gi, pl.ds(koff, count * TQ), :], qa)
        return [s[i * TQ:(i + 1) * TQ] + tz_ref[gi, tz_index(n - (first + i))] for i in range(count)]

    def col_max(parts):
        mx = parts[0]
        for x in parts[1:]:
            mx = jnp.maximum(mx, x)
        return jnp.max(mx, axis=0, keepdims=True)

    def probs(parts, m):
        return jnp.concatenate([jnp.exp2(x - m).astype(BF16) for x in parts], axis=0)

    def values(v_ref, gi, first, count, p):
        koff = pl.multiple_of(first * TQ, TQ)
        return _dot(v_ref[gi, :, pl.ds(koff, count * TQ)], p)

    sel_index = lambda d: jnp.where(d < 0, 2, jnp.minimum(d, 2))
    win_index = lambda d: jnp.where(d < 0, 4, jnp.where(d == nwt, 3, jnp.minimum(d, 2)))

    groups = range(NSA_GROUPS)
    gate = lambda h, c: gt_ref[pl.ds(h * 3 + c, 1), :]
    head_lanes = lambda r: slice(r * TQ, (r + 1) * TQ)

    q3s = [jnp.concatenate([qt_ref[(gi * NSA_REP + r) * dh:(gi * NSA_REP + r + 1) * dh, :] for r in range(NSA_REP)],
                           axis=1) for gi in groups]
    cs = [jnp.where(cmask, _dot(kc_ref[gi], jnp.concatenate([q3s[gi], zpad], axis=0))
                    + cb_ref[gi, pl.ds(cstart, ncp), :], NEG) for gi in groups]
    cps = [jnp.where(cmask, jnp.exp2(s - jnp.max(s, axis=0, keepdims=True)), 0.0) for s in cs]
    cps = [p * (1.0 / jnp.maximum(jnp.sum(p, axis=0, keepdims=True), 1e-30)) for p in cps]
    ocs = [_dot(vct_ref[gi], cps[gi].astype(BF16)) for gi in groups]
    scs = []
    for gi in groups:
        psum = cps[gi][:, 0:TQ]
        for r in range(1, NSA_REP):
            psum = psum + cps[gi][:, head_lanes(r)]
        p1 = psum.astype(BF16)
        p2 = (psum - p1.astype(F32)).astype(BF16)
        imp = _dot(ovt_ref[...], p1) + _dot(ovt_ref[...], p2)
        score = jnp.where(forced, 1e4, jnp.where(jj <= cur, imp[0:nsb], -1.0))
        scs.append(jnp.where(score < 0.0, -1, lax.bitcast_convert_type(score, jnp.int32)))
    for gi in groups:
        sc_ref[gi] = scs[gi]
        for r in range(NSA_REP):
            h = gi * NSA_REP + r
            ot_ref[h * dh:(h + 1) * dh, :] = gate(h, 0) * ocs[gi][0:dh, head_lanes(r)]

    scs1 = [k + 1 for k in scs]

    def rank_step(i4, cnts):
        cnts = list(cnts)
        for u in range(RANK_UNROLL):
            i = i4 * RANK_UNROLL + u
            lower = i < jj
            for gi in groups:
                rowk = sc_ref[gi, pl.ds(i, 1), :]
                before = rowk >= jnp.where(lower, scs[gi], scs1[gi])
                cnts[gi] = cnts[gi] + jnp.where(before, 1, 0)
        return tuple(cnts)
    rank_trips = jnp.minimum((2 * n + 2 + RANK_UNROLL - 1) // RANK_UNROLL, nsb // RANK_UNROLL)
    cnts = lax.fori_loop(0, rank_trips, rank_step, tuple(jnp.zeros((nsb, TQ), jnp.int32) for _ in groups))
    for gi in groups:
        selneg = jnp.where((cnts[gi] < n_sel) & (jj <= cur), 0.0, NEG).astype(BF16)
        if nsb < SEL_BLOCK:
            selneg = jnp.concatenate([selneg, jnp.zeros((SEL_BLOCK - nsb, TQ), BF16)], axis=0)
        qa_ref[gi] = jnp.concatenate([q3s[gi], jnp.concatenate([selneg] * NSA_REP, axis=1)], axis=0)

    acc_ref[...] = jnp.zeros_like(acc_ref)
    half = NSA_GROUPS // 2
    s_refs = ((s0_ref, s1_ref), (s2_ref, s3_ref))

    def score_half(it, hb):
        plist = [scores(ks_ref, hb * half + u, it * SEL_TILES, SEL_TILES, sel_index, qa_ref[hb * half + u])
                 for u in range(half)]
        for u, parts in enumerate(plist):
            for i, x in enumerate(parts):
                s_refs[hb][u][i * TQ:(i + 1) * TQ, :] = x
        return [col_max(parts) for parts in plist]

    def finish_half(it, hb, ms, bms):
        m2s = [jnp.maximum(ms[u], bms[u]) for u in range(half)]
        ps = [probs([s_refs[hb][u][i * TQ:(i + 1) * TQ, :] for i in range(SEL_TILES)], m2s[u]) for u in range(half)]
        vals = [values(vs_ref, hb * half + u, it * SEL_TILES, SEL_TILES, ps[u]) for u in range(half)]
        for u in range(half):
            gi = hb * half + u
            acc_ref[gi] = jnp.exp2(ms[u] - m2s[u]) * acc_ref[gi] + vals[u]
        return m2s

    def sel_trip(it, carry, score_next):
        ms0, ms1, bm0 = carry
        bm1 = score_half(it, 1)
        ms0 = finish_half(it, 0, ms0, bm0)
        if score_next:
            bm0 = score_half(it + 1, 0)
        ms1 = finish_half(it, 1, ms1, bm1)
        return ms0, ms1, bm0

    neg = [jnp.full((1, G3), NEG, F32) for _ in range(half)]
    carry = (neg, neg, score_half(0, 0))
    carry = lax.fori_loop(0, n // SEL_TILES, lambda it, c: sel_trip(it, c, True), carry)
    sel_trip(n // SEL_TILES, carry, False)

    wfirst = jnp.maximum(n - nwt, 0)
    wparts = [scores(kw_ref, gi, wfirst, nwt + 1, win_index, qa_ref[gi]) for gi in groups]
    wps = [probs(parts, col_max(parts)) for parts in wparts]
    ows = [values(vw_ref, gi, wfirst, nwt + 1, wps[gi]) for gi in groups]
    for gi in groups:
        o_s = acc_ref[gi, 0:dh, :] * (1.0 / acc_ref[gi, dh:dh + 1, :])
        o_w = ows[gi][0:dh, :] * (1.0 / ows[gi][dh:dh + 1, :])
        for r in range(NSA_REP):
            h = gi * NSA_REP + r
            ot_ref[h * dh:(h + 1) * dh, :] += gate(h, 1) * o_s[:, head_lanes(r)] + gate(h, 2) * o_w[:, head_lanes(r)]

    for j in range(NSA_QW // LANE):
        o_ref[:, j * LANE:(j + 1) * LANE] = ot_ref[j * LANE:(j + 1) * LANE, :].T.astype(BF16)


def _nsa_attn(qt, gt, kc, vct, ksel, kwin, vsel, vwin, ovt, tz, cb, batch):
    t = qt.shape[1]
    s = t // batch
    nt = s // TQ
    nsb = s // SEL_BLOCK
    ncp = kc.shape[-2]
    per_b = lambda a: pl.BlockSpec((None,) + a.shape[1:], lambda b, i: (b,) + (0,) * (a.ndim - 1))
    full = lambda a: pl.BlockSpec(a.shape, lambda b, i: (0,) * a.ndim)
    return pl.pallas_call(
        functools.partial(_nsa_attn_body, nsb=nsb, ncp=ncp, n_sel=min(SEL_TOPK, nsb)),
        grid=(batch, nt),
        in_specs=[pl.BlockSpec((NSA_QW, TQ), lambda b, i: (0, b * nt + i)),
                  pl.BlockSpec((LANE, TQ), lambda b, i: (0, b * nt + i)),
                  per_b(kc), per_b(vct), per_b(ksel), per_b(kwin), per_b(vsel), per_b(vwin),
                  full(ovt), full(tz), full(cb)],
        out_specs=pl.BlockSpec((TQ, NSA_QW), lambda b, i: (b * nt + i, 0)),
        out_shape=jax.ShapeDtypeStruct((t, NSA_QW), BF16),
        scratch_shapes=[pltpu.VMEM((NSA_GROUPS, nsb, TQ), jnp.int32), pltpu.VMEM((NSA_GROUPS, LANE, G3), BF16),
                        pltpu.VMEM((NSA_GROUPS, V_ROWS, G3), F32), pltpu.VMEM((NSA_QW, TQ), F32),
                        ] + [pltpu.VMEM((SEL_TILES * TQ, G3), F32)] * NSA_GROUPS,
        compiler_params=_cparams("parallel", "arbitrary"),
        name="nsa_attn",
    )(qt, gt, kc, vct, ksel, kwin, vsel, vwin, ovt, tz, cb)


def _rel_bucket_np(dist):
    dist = np.maximum(dist, 0)
    max_exact = REL_BUCKETS // 2
    ratio = np.log(np.maximum(dist, 1).astype(np.float32) / np.float32(max_exact)) / np.float32(
        math.log(REL_MAX_DIST / max_exact))
    large = max_exact + (ratio * np.float32(REL_BUCKETS - max_exact)).astype(np.int32)
    large = np.minimum(large, REL_BUCKETS - 1)
    return np.where(dist < max_exact, dist, large).astype(np.int32)


def _group_lanes(a):
    hh, r, c = a.shape
    return a.reshape(NSA_GROUPS, NSA_REP, r, c).transpose(0, 2, 1, 3).reshape(NSA_GROUPS, r, NSA_REP * c)


def _bias_tables(rel_bias, ncp):
    k = np.arange(TQ)[:, None]
    q = np.arange(TQ)[None, :]
    tbl = rel_bias.astype(F32)

    def lookup(idx):
        onehot = (jnp.asarray(idx.reshape(1, -1)) == jnp.arange(REL_BUCKETS)[:, None]).astype(F32)
        out = jnp.dot(tbl.T, onehot, precision=lax.Precision.HIGHEST)
        return out.reshape((NSA_HEADS,) + idx.shape)

    far = jnp.broadcast_to(tbl[REL_BUCKETS - 1][:, None, None], (NSA_HEADS, TQ, TQ))
    t0 = jnp.where(jnp.asarray(k <= q)[None], lookup(_rel_bucket_np(q - k)), NEG)
    t1 = lookup(_rel_bucket_np(TQ + q - k))
    t3 = jnp.where(jnp.asarray(k > q)[None], far, NEG)
    tz = jnp.stack([_group_lanes(x) for x in (t0, t1, far, t3, jnp.full_like(far, NEG))], axis=1)
    kk = np.arange(ncp)[:, None]
    m = np.where(kk < ncp // 2, -kk, ncp - kk)
    d = CMP_STRIDE * m + q - (CMP_BLOCK - 1)
    idx = np.where((d >= 0) & (d < REL_MAX_DIST), _rel_bucket_np(d), REL_BUCKETS - 1)
    cb = _group_lanes(lookup(idx))
    return tz * LOG2E, jnp.concatenate([cb, cb], axis=1) * LOG2E


def _overlap_table(s, ncp):
    nsb = s // SEL_BLOCK
    nc = (s - CMP_BLOCK) // CMP_STRIDE + 1
    cs = np.arange(ncp) * CMP_STRIDE
    ce = cs + CMP_BLOCK - 1
    ss = np.arange(SEL_BLOCK) * SEL_BLOCK
    ov = (cs[None, :] < ss[:, None] + SEL_BLOCK) & (ce[None, :] >= ss[:, None])
    ov &= (np.arange(ncp) < nc)[None, :] & (np.arange(SEL_BLOCK) < nsb)[:, None]
    return jnp.asarray(ov, dtype=BF16)


def kernel(x, mem, norm_mix, norm_mem, w_mem_kv, w_out, norm_ffn, w_up, conv_w, conv_b, w_down,
           gla_w_in, gla_w_gate_up, gla_b_gate, gla_out_norm, nsa_w_in, kv_norm, w_kv_shared,
           cmp_pos, cmp_w1, cmp_b1, cmp_w2, cmp_b2, rel_bias, final_norm):
    batch, seq = x.shape[0], x.shape[1]
    t = batch * seq
    h = x.reshape(t, D_MODEL)
    row = lambda v: v.reshape(1, -1).astype(F32)

    wk, wv = w_mem_kv[..., :MEM_W], w_mem_kv[..., MEM_W:]
    w_mkv = jnp.concatenate([_pad_heads(wk, MEM_HEADS, MEM_HEAD_DIM, MEM_DP),
                             _pad_heads(wv, MEM_HEADS, MEM_HEAD_DIM, MEM_DP)], axis=-1).astype(BF16)
    mem_kv_all = _mem_kv_proj(mem.reshape(-1, D_MODEL), norm_mem.reshape(DEPTH, 1, D_MODEL), w_mkv, batch)

    shared = None
    for i in range(DEPTH):
        w_o = w_out[i]
        w_o_mem = _pad_head_rows(w_o[MAIN_W:], MEM_HEADS, MEM_HEAD_DIM, MEM_DP).astype(BF16)
        if i < N_A_LAYERS:
            wi = gla_w_in[i]
            c0 = GLA_HEADS * GLA_DK
            c1 = 2 * c0
            c2 = c1 + GLA_HEADS * GLA_DV
            c3 = c2 + GLA_HEADS * GLA_DV
            c4 = c3 + GLA_RANK
            w_all = jnp.concatenate([
                _pad_heads(wi[:, :c0], GLA_HEADS, GLA_DK, GLA_DKP),
                _pad_heads(wi[:, c0:c1], GLA_HEADS, GLA_DK, GLA_DKP),
                _pad_heads(wi[:, c1:c2], GLA_HEADS, GLA_DV, GLA_DVP),
                _pad_heads(wi[:, c2:c3], GLA_HEADS, GLA_DV, GLA_DVP),
                _pad_heads(wi[:, c3:c4], 1, GLA_RANK, LANE),
                _pad_heads(wi[:, c4:], MEM_HEADS, MEM_HEAD_DIM, MEM_DP)], axis=1).astype(BF16)
            wg = jnp.pad(_pad_heads(gla_w_gate_up[i], GLA_HEADS, GLA_DK, GLA_DKP),
                         ((0, LANE - GLA_RANK), (0, 0))).astype(BF16)
            bg = row(_pad_heads(gla_b_gate[i], GLA_HEADS, GLA_DK, GLA_DKP))
            q, k, v, go, la, mq = _gla_in_proj(h, row(norm_mix[i]), w_all, wg, bg)
            on = row(jnp.pad(gla_out_norm[i], (0, GLA_DVP - GLA_DV)))
            main = _gla_mixer(q, k, la, v, go, on, batch)
            w_o_main = _pad_head_rows(w_o[:MAIN_W], GLA_HEADS, GLA_DV, GLA_DVP).astype(BF16)
        else:
            if shared is None:
                ncp = seq // CMP_STRIDE
                gr, dh = NSA_GROUPS, NSA_HEAD_DIM
                wkv = w_kv_shared.reshape(D_MODEL, 6, gr, dh)
                pair = lambda a, b: jnp.concatenate([wkv[:, a], wkv[:, b]], axis=-1).reshape(D_MODEL, gr * 2 * dh)
                slot = lambda a: _pad_heads(wkv[:, a].reshape(D_MODEL, KV_NAT), gr, dh, LANE)
                w_kv = jnp.concatenate([wkv[:, 0].reshape(D_MODEL, KV_NAT), wkv[:, 1].reshape(D_MODEL, KV_NAT),
                                        slot(2), slot(4), pair(3, 5)], axis=1).astype(BF16)
                ck, cv, ksel, kwin, vsel, vwin = _kv_proj(h, row(kv_norm), w_kv, batch)
                x16 = jnp.stack([ck.reshape(batch * ncp, CMP_STRIDE * KV_NAT),
                                 cv.reshape(batch * ncp, CMP_STRIDE * KV_NAT)])
                w1 = cmp_w1.reshape(2, 2, CMP_STRIDE, dh, CMP_HIDDEN)
                eye = jnp.eye(gr, dtype=F32)
                w1x = jnp.einsum('jhldc,gk->jhlgdkc', w1, eye).reshape(2, 2, CMP_STRIDE * KV_NAT, gr * CMP_HIDDEN)
                w1x = w1x.astype(BF16)
                pos8 = jnp.broadcast_to(cmp_pos.reshape(2, 1, CMP_BLOCK * dh), (2, SUBLANE, CMP_BLOCK * dh)).astype(BF16)
                w2p = jnp.pad(cmp_w2, ((0, 0), (0, 0), (0, LANE - dh))).astype(BF16)
                b2p = jnp.pad(cmp_b2, ((0, 0), (0, LANE - dh))).reshape(2, 1, LANE).astype(F32)
                cnat, ctr = _compress(x16, w1x[:, 0], w1x[:, 1], pos8, cmp_w1.astype(BF16),
                                      cmp_b1.reshape(2, 1, CMP_HIDDEN).astype(F32), w2p, b2p, batch)
                tz, cb = _bias_tables(rel_bias, ncp)
                ov = _overlap_table(seq, ncp)
                shared = (cnat[0], ctr[1], ksel, kwin, vsel, vwin, ov, tz, cb)
            wi = nsa_w_in[i - N_A_LAYERS]
            c0 = NSA_HEADS * NSA_HEAD_DIM
            c1 = c0 + NSA_HEADS * 3
            w_all = jnp.concatenate([
                wi[:, :c0],
                _pad_heads(wi[:, c0:c1], 1, NSA_HEADS * 3, LANE),
                _pad_heads(wi[:, c1:], MEM_HEADS, MEM_HEAD_DIM, MEM_DP)], axis=1).astype(BF16)
            q, gates, mq = _nsa_in_proj(h, row(norm_mix[i]), w_all)
            main = _nsa_attn(q, gates, *shared, batch)
            w_o_main = w_o[:MAIN_W].astype(BF16)
        mo = _mem_attn(mq, mem_kv_all[i], batch)
        h = _mix_ffn(h, main, mo, w_o_main, w_o_mem, row(norm_ffn[i]), w_up[i].astype(BF16), conv_w[i].astype(F32),
                     row(conv_b[i]), w_down[i].astype(BF16), batch)
    return _final_norm(h, row(final_norm)).reshape(batch, seq, D_MODEL)
```

```python
import functools
import math

import numpy as np
import jax
import jax.numpy as jnp
from jax import lax
from jax.experimental import pallas as pl
from jax.experimental.pallas import tpu as pltpu

F32 = jnp.float32
BF16 = jnp.bfloat16

D_MODEL = 1024
DEPTH = 4
N_A_LAYERS = DEPTH // 2
MEM_HEADS = 4
MEM_HEAD_DIM = 64
MEM_W = MEM_HEADS * MEM_HEAD_DIM
MAIN_W = D_MODEL - MEM_W
GLA_HEADS = 4
GLA_DV = MAIN_W // GLA_HEADS
GLA_DK = GLA_DV // 2
GLA_RANK = 16
GLA_GATE_NORM = 16.0
NSA_HEADS = 12
NSA_GROUPS = 4
NSA_HEAD_DIM = MAIN_W // NSA_HEADS
NSA_REP = NSA_HEADS // NSA_GROUPS
CMP_BLOCK = 32
CMP_STRIDE = 16
CMP_HIDDEN = 128
SEL_BLOCK = 64
SEL_TOPK = 16
WINDOW = 512
REL_BUCKETS = 32
REL_MAX_DIST = 128
FFN_DIM = 2816
CONV_WIDTH = 3
EPS = 1e-6

LANE = 128
SUBLANE = 8
VMEM_LIMIT = 56 * 1024 * 1024
GLA_DKP = LANE
GLA_DVP = 2 * LANE
MEM_DP = LANE
NEG = -1e30
TQ = 128
ROW_TILE = 512
GLA_CHUNK = 64
FFN_TILE = 256
SEL_TILES = 4
V_ROWS = 64 + 16
RANK_UNROLL = 4
LOG2E = math.log2(math.e)

NT = (((1,), (1,)), ((), ()))
TN = (((0,), (0,)), ((), ()))


def _cparams(*sem):
    return pltpu.CompilerParams(dimension_semantics=sem, vmem_limit_bytes=VMEM_LIMIT)


def _rms(x, g):
    return x * lax.rsqrt(jnp.mean(x * x, axis=-1, keepdims=True) + EPS) * g


def _sigmoid(x):
    return 1.0 / (1.0 + jnp.exp(-x))


def _dot(a, b):
    return jnp.dot(a, b, preferred_element_type=F32)


def _resident(a):
    return pl.BlockSpec(a.shape, lambda *_: (0,) * a.ndim, pipeline_mode=pl.Buffered(1))


def _pad_heads(w, nh, d, dp):
    lead = w.shape[:-1]
    w = w.reshape(lead + (nh, d))
    w = jnp.pad(w, [(0, 0)] * len(lead) + [(0, 0), (0, dp - d)])
    return w.reshape(lead + (nh * dp,))


def _pad_head_rows(w, nh, d, dp):
    n = w.shape[-1]
    w = jnp.pad(w.reshape(nh, d, n), ((0, 0), (0, dp - d), (0, 0)))
    return w.reshape(nh * dp, n)


def _mem_attention(q, kv_ref):
    outs = []
    for hd in range(MEM_HEADS):
        sl = slice(hd * MEM_DP, (hd + 1) * MEM_DP)
        sv = slice(MEM_QW + hd * MEM_DP, MEM_QW + (hd + 1) * MEM_DP)
        s = lax.dot_general(q[:, sl], kv_ref[:, sl], NT, preferred_element_type=F32) * MEM_HEAD_DIM ** -0.5
        p = jnp.exp(s - jnp.max(s, axis=-1, keepdims=True))
        l = jnp.sum(p, axis=-1, keepdims=True)
        outs.append((_dot(p.astype(BF16), kv_ref[:, sv]) / l).astype(BF16))
    return jnp.concatenate(outs, axis=1)


GLA_QW = GLA_HEADS * GLA_DKP
GLA_VW = GLA_HEADS * GLA_DVP
MEM_QW = MEM_HEADS * MEM_DP
GLA_OFF_K = GLA_QW
GLA_OFF_V = 2 * GLA_QW
GLA_OFF_G = GLA_OFF_V + GLA_VW
GLA_OFF_LR = GLA_OFF_G + GLA_VW
GLA_OFF_MQ = GLA_OFF_LR + LANE
GLA_NP = GLA_OFF_MQ + MEM_QW


def _gla_in_body(h_ref, g_ref, w_ref, wg_ref, bg_ref, kv_ref, q_ref, k_ref, v_ref, go_ref, la_ref, mo_ref):
    xn = _rms(h_ref[...], g_ref[...]).astype(BF16)

    def proj(lo, n):
        return _dot(xn, w_ref[:, lo:lo + n])

    q_ref[...] = proj(0, GLA_QW).astype(BF16)
    k_ref[...] = proj(GLA_OFF_K, GLA_QW).astype(BF16)
    for j in range(GLA_VW // GLA_QW):
        v_ref[:, j * GLA_QW:(j + 1) * GLA_QW] = proj(GLA_OFF_V + j * GLA_QW, GLA_QW).astype(BF16)
        go_ref[:, j * GLA_QW:(j + 1) * GLA_QW] = proj(GLA_OFF_G + j * GLA_QW, GLA_QW).astype(BF16)
    lr = proj(GLA_OFF_LR, LANE).astype(BF16)
    z = _dot(lr, wg_ref[...]) + bg_ref[...]
    la_ref[...] = (jnp.minimum(z, 0.0) - jnp.log(1.0 + jnp.exp(-jnp.abs(z)))) * (1.0 / GLA_GATE_NORM)
    mo_ref[...] = _mem_attention(proj(GLA_OFF_MQ, MEM_QW).astype(BF16), kv_ref)


def _mem_kv_spec(mem_kv, t, tm, batch):
    nt = t // batch // tm
    return pl.BlockSpec((mem_kv.shape[0] // batch, mem_kv.shape[1]), lambda i: (i // nt, 0))


def _gla_in_proj(h, g, w, wg, bg, mem_kv, batch):
    t = h.shape[0]
    tm = ROW_TILE
    row = lambda n: pl.BlockSpec((tm, n), lambda i: (i, 0))
    return pl.pallas_call(
        _gla_in_body,
        grid=(t // tm,),
        in_specs=[row(D_MODEL), _resident(g), _resident(w), _resident(wg), _resident(bg),
                  _mem_kv_spec(mem_kv, t, tm, batch)],
        out_specs=[row(GLA_QW), row(GLA_QW), row(GLA_VW), row(GLA_VW), row(GLA_QW), row(MEM_QW)],
        out_shape=[jax.ShapeDtypeStruct((t, GLA_QW), BF16), jax.ShapeDtypeStruct((t, GLA_QW), BF16),
                   jax.ShapeDtypeStruct((t, GLA_VW), BF16), jax.ShapeDtypeStruct((t, GLA_VW), BF16),
                   jax.ShapeDtypeStruct((t, GLA_QW), F32), jax.ShapeDtypeStruct((t, MEM_QW), BF16)],
        compiler_params=_cparams("parallel"),
        name="gla_in_proj",
    )(h, g, w, wg, bg, mem_kv)


def _gla_mix_body(q_ref, k_ref, la_ref, v_ref, go_ref, on_ref, o_ref, st_ref, *, cn, batch):
    @pl.when(pl.program_id(0) == 0)
    def _():
        st_ref[...] = jnp.zeros_like(st_ref)

    row = lax.broadcasted_iota(jnp.int32, (cn, cn), 0)
    col = lax.broadcasted_iota(jnp.int32, (cn, cn), 1)
    causal = row >= col
    tril = jnp.where(causal, 1.0, 0.0).astype(BF16)
    scale = GLA_DK ** -0.5
    chains = [(bi, hd) for bi in range(batch) for hd in range(GLA_HEADS)]
    sk = lambda hd: slice(hd * GLA_DKP, (hd + 1) * GLA_DKP)
    sv = lambda hd: slice(hd * GLA_DVP, (hd + 1) * GLA_DVP)
    dg = lambda x, y, dims: lax.dot_general(x, y, dims, preferred_element_type=F32)

    def cum_decay(bi, hd):
        la = la_ref[bi, :, sk(hd)]
        la1 = la.astype(BF16)
        r1 = la - la1.astype(F32)
        la2 = r1.astype(BF16)
        la3 = (r1 - la2.astype(F32)).astype(BF16)
        return _dot(tril, la1) + _dot(tril, la2) + _dot(tril, la3)

    bs = [cum_decay(bi, hd) for bi, hd in chains]
    scaled = []
    for (bi, hd), b in zip(chains, bs):
        b_last = b[cn - 1:cn, :]
        q = q_ref[bi, :, sk(hd)].astype(F32)
        k = k_ref[bi, :, sk(hd)].astype(F32)
        scaled.append(((q * jnp.exp(b) * scale).astype(BF16), (k * jnp.exp(-b)).astype(BF16),
                       (k * jnp.exp(b_last - b)).astype(BF16), jnp.exp(b_last)))
    intra = [jnp.where(causal, dg(qs, ks, NT), 0.0).astype(BF16) for qs, ks, _, _ in scaled]
    outs = []
    for i, (bi, hd) in enumerate(chains):
        qs, _, ko, a_last = scaled[i]
        v = v_ref[bi, :, sv(hd)]
        st = st_ref[i]
        outs.append(_dot(intra[i], v) + dg(qs, st.astype(BF16), NT))
        st_ref[i] = st * a_last + dg(v, ko, TN)
    for (bi, hd), o in zip(chains, outs):
        ms = jnp.sum(o * o, axis=-1, keepdims=True) * (1.0 / GLA_DV)
        y = o * lax.rsqrt(ms + EPS) * on_ref[...]
        g = go_ref[bi, :, sv(hd)].astype(F32)
        o_ref[bi, :, sv(hd)] = (y * (g * _sigmoid(g))).astype(BF16)


def _gla_mixer(q, k, la, v, go, on, batch):
    t = q.shape[0]
    s = t // batch
    cn = GLA_CHUNK
    seq = lambda a: a.reshape(batch, s, a.shape[1])
    blk = lambda n: pl.BlockSpec((batch, cn, n), lambda c: (0, c, 0))
    out = pl.pallas_call(
        functools.partial(_gla_mix_body, cn=cn, batch=batch),
        grid=(s // cn,),
        in_specs=[blk(GLA_QW), blk(GLA_QW), blk(GLA_QW), blk(GLA_VW), blk(GLA_VW),
                  pl.BlockSpec(on.shape, lambda c: (0, 0))],
        out_specs=blk(GLA_VW),
        out_shape=jax.ShapeDtypeStruct((batch, s, GLA_VW), BF16),
        scratch_shapes=[pltpu.VMEM((batch * GLA_HEADS, GLA_DVP, GLA_DKP), F32)],
        compiler_params=_cparams("arbitrary"),
        name="gla_mixer",
    )(seq(q), seq(k), seq(la), seq(v), seq(go), on)
    return out.reshape(t, GLA_VW)


def _mem_kv_body(m_ref, g_ref, w_ref, o_ref):
    xn = _rms(m_ref[...], g_ref[...]).astype(BF16)
    o_ref[...] = _dot(xn, w_ref[...]).astype(BF16)


def _mem_kv_proj(mem2, g, w, batch):
    nl = w.shape[0]
    m = mem2.shape[0] // batch
    n = w.shape[2]
    return pl.pallas_call(
        _mem_kv_body,
        grid=(nl, batch),
        in_specs=[pl.BlockSpec((m, D_MODEL), lambda l, b: (b, 0)),
                  pl.BlockSpec((None, 1, D_MODEL), lambda l, b: (l, 0, 0)),
                  pl.BlockSpec((None, D_MODEL, n), lambda l, b: (l, 0, 0))],
        out_specs=pl.BlockSpec((None, m, n), lambda l, b: (l, b, 0)),
        out_shape=jax.ShapeDtypeStruct((nl, mem2.shape[0], n), BF16),
        compiler_params=_cparams("arbitrary", "arbitrary"),
        name="mem_kv_proj",
    )(mem2, g, w)


BF16_ROWS = 2 * SUBLANE


def _mix_ffn_body(h_ref, hp_ref, a_ref, ap_ref, m_ref, mp_ref, wa_ref, wm_ref, g_ref, wup_ref, cw_ref, cb_ref,
                  wdn_ref, gf_ref, o_ref, act_ref, *, tm, tf, last):
    g = g_ref[...]
    nr = tm // SUBLANE
    interleave = lambda x: x.reshape(SUBLANE, nr, x.shape[-1]).swapaxes(0, 1).reshape(tm, x.shape[-1])
    deinterleave = lambda x: x.reshape(nr, SUBLANE, x.shape[-1]).swapaxes(0, 1).reshape(tm, x.shape[-1])
    h = interleave(h_ref[...] + _dot(a_ref[...], wa_ref[...]) + _dot(m_ref[...], wm_ref[...]))
    x = _rms(h, g).astype(BF16)
    keep = jnp.where(pl.program_id(1) > 0, 1.0, 0.0)
    h_prev = (hp_ref[...] + _dot(ap_ref[...], wa_ref[...]) + _dot(mp_ref[...], wm_ref[...]))[BF16_ROWS - SUBLANE:]
    x_prev = (_rms(h_prev, g) * keep).astype(BF16)
    first = lax.broadcasted_iota(jnp.int32, (SUBLANE, tf), 0) == 0
    nchunk = FFN_DIM // tf

    def up(j):
        cols = [pl.ds(off + j * tf, tf) for off in (0, FFN_DIM)]
        return tuple((_dot(x, wup_ref[:, c]), _dot(x_prev, wup_ref[:, c])) for c in cols)

    def conv(u, u_prev, off):
        w = cw_ref[:, off:off + tf]
        wrap = lambda r, k: jnp.where(first, u_prev[SUBLANE - k:SUBLANE - k + 1, :],
                                      pltpu.roll(u[r * SUBLANE:(r + 1) * SUBLANE, :], 1, 0))
        back1 = jnp.concatenate([wrap(nr - 1, 1), u[0:tm - SUBLANE, :]], axis=0)
        back2 = jnp.concatenate([wrap(nr - 2, 2), wrap(nr - 1, 1), u[0:tm - 2 * SUBLANE, :]], axis=0)
        return cb_ref[:, off:off + tf] + w[0:1, :] * back2 + w[1:2, :] * back1 + w[2:3, :] * u

    u_next = up(0)
    for j in range(nchunk):
        (ua, ua_prev), (ub, ub_prev) = u_next
        if j + 1 < nchunk:
            u_next = up(j + 1)
        a = conv(ua, ua_prev, j * tf)
        b = conv(ub, ub_prev, FFN_DIM + j * tf)
        act_ref[:, j * tf:(j + 1) * tf] = (a * _sigmoid(a) * b).astype(BF16)
    out = h + _dot(act_ref[...], wdn_ref[...])
    o_ref[...] = deinterleave(_rms(out, gf_ref[...]) if last else out)


def _mix_ffn(h, main, mo, wa, wm, g, wup, cw, cb, wdn, gf, last, batch):
    t = h.shape[0]
    tm = ROW_TILE
    nt = t // batch // tm
    hb = tm // BF16_ROWS
    cur = lambda n: pl.BlockSpec((tm, n), lambda b, i: (b * nt + i, 0))
    prev = lambda n: pl.BlockSpec((BF16_ROWS, n), lambda b, i: (jnp.maximum((b * nt + i) * hb - 1, 0), 0))
    return pl.pallas_call(
        functools.partial(_mix_ffn_body, tm=tm, tf=FFN_TILE, last=last),
        grid=(batch, nt),
        in_specs=[cur(D_MODEL), prev(D_MODEL), cur(main.shape[1]), prev(main.shape[1]),
                  cur(mo.shape[1]), prev(mo.shape[1]), _resident(wa), _resident(wm), _resident(g),
                  _resident(wup), _resident(cw), _resident(cb), _resident(wdn), _resident(gf)],
        out_specs=cur(D_MODEL),
        out_shape=jax.ShapeDtypeStruct((t, D_MODEL), F32),
        scratch_shapes=[pltpu.VMEM((tm, FFN_DIM), BF16)],
        compiler_params=_cparams("parallel", "parallel"),
        name="mix_ffn",
    )(h, h, main, main, mo, mo, wa, wm, g, wup, cw, cb, wdn, gf)


KV_NAT = NSA_GROUPS * NSA_HEAD_DIM


def _kv_proj_body(h_ref, g_ref, w_ref, ck_ref, cv_ref, ks_ref, kw_ref, vs_ref, vw_ref, *, tm):
    xn = _rms(h_ref[...], g_ref[...]).astype(BF16)
    ck_ref[...] = _dot(xn, w_ref[:, 0:KV_NAT]).astype(BF16)
    cv_ref[...] = _dot(xn, w_ref[:, KV_NAT:2 * KV_NAT]).astype(BF16)
    key = pl.program_id(1) * tm + lax.broadcasted_iota(jnp.int32, (tm, LANE), 0)
    lane = lax.broadcasted_iota(jnp.int32, (tm, LANE), 1)
    onehot = jnp.where(lax.shift_right_logical(key, int(math.log2(SEL_BLOCK))) == lane - NSA_HEAD_DIM, 1.0, 0.0)
    low = lane < NSA_HEAD_DIM
    for gi in range(NSA_GROUPS):
        slot = lambda n: _dot(xn, w_ref[:, 2 * KV_NAT + (n * NSA_GROUPS + gi) * LANE:
                                           2 * KV_NAT + (n * NSA_GROUPS + gi + 1) * LANE])
        ks_ref[gi] = jnp.where(low, slot(0), onehot).astype(BF16)
        kw_ref[gi] = jnp.where(low, slot(1), 0.0).astype(BF16)
        vt = slot(2).T.astype(BF16)
        ones = jnp.ones((V_ROWS - NSA_HEAD_DIM, tm), BF16)
        vs_ref[gi] = jnp.concatenate([vt[0:NSA_HEAD_DIM], ones], axis=0)
        vw_ref[gi] = jnp.concatenate([vt[NSA_HEAD_DIM:], ones], axis=0)


def _kv_proj(h, g, w, batch):
    t = h.shape[0]
    s = t // batch
    tm = ROW_TILE
    nt = s // tm
    gr = NSA_GROUPS
    return pl.pallas_call(
        functools.partial(_kv_proj_body, tm=tm),
        grid=(batch, nt),
        in_specs=[pl.BlockSpec((tm, D_MODEL), lambda b, i: (b * nt + i, 0)),
                  pl.BlockSpec(g.shape, lambda b, i: (0, 0)),
                  pl.BlockSpec(w.shape, lambda b, i: (0, 0))],
        out_specs=[pl.BlockSpec((tm, KV_NAT), lambda b, i: (b * nt + i, 0)),
                   pl.BlockSpec((tm, KV_NAT), lambda b, i: (b * nt + i, 0)),
                   pl.BlockSpec((None, gr, tm, LANE), lambda b, i: (b, 0, i, 0)),
                   pl.BlockSpec((None, gr, tm, LANE), lambda b, i: (b, 0, i, 0)),
                   pl.BlockSpec((None, gr, V_ROWS, tm), lambda b, i: (b, 0, 0, i)),
                   pl.BlockSpec((None, gr, V_ROWS, tm), lambda b, i: (b, 0, 0, i))],
        out_shape=[jax.ShapeDtypeStruct((t, KV_NAT), BF16), jax.ShapeDtypeStruct((t, KV_NAT), BF16),
                   jax.ShapeDtypeStruct((batch, gr, s, LANE), BF16),
                   jax.ShapeDtypeStruct((batch, gr, s, LANE), BF16),
                   jax.ShapeDtypeStruct((batch, gr, V_ROWS, s), BF16),
                   jax.ShapeDtypeStruct((batch, gr, V_ROWS, s), BF16)],
        compiler_params=_cparams("parallel", "parallel"),
        name="nsa_kv_proj",
    )(h, g, w)


def _compress_body(x_ref, wt_ref, wb_ref, pos_ref, w1_ref, b1_ref, w2_ref, b2_ref, on_ref, ot_ref, *, ncp):
    x = x_ref[...]
    top = _dot(x, wt_ref[...])
    bot = _dot(x, wb_ref[...])
    posb = _dot(pos_ref[...], w1_ref[...])[0:1, :] + b1_ref[...]
    for gi in range(NSA_GROUPS):
        sl = slice(gi * CMP_HIDDEN, (gi + 1) * CMP_HIDDEN)
        hid = top[:, sl] + pltpu.roll(bot[:, sl], ncp - 1, 0) + posb
        hid = (hid * _sigmoid(hid)).astype(BF16)
        out = _dot(hid, w2_ref[...]) + b2_ref[...]
        on_ref[gi] = out.astype(BF16)
        ot_ref[gi] = out.T.astype(BF16)


def _compress(x16, wt, wb, pos, w1, b1, w2, b2, batch):
    ncp = x16.shape[1] // batch
    gr = NSA_GROUPS
    per_j = lambda a: pl.BlockSpec((None,) + a.shape[1:], lambda j, b: (j,) + (0,) * (a.ndim - 1))
    return pl.pallas_call(
        functools.partial(_compress_body, ncp=ncp),
        grid=(2, batch),
        in_specs=[pl.BlockSpec((None, ncp, x16.shape[2]), lambda j, b: (j, b, 0)),
                  per_j(wt), per_j(wb), per_j(pos), per_j(w1), per_j(b1), per_j(w2), per_j(b2)],
        out_specs=[pl.BlockSpec((None, None, gr, ncp, LANE), lambda j, b: (j, b, 0, 0, 0)),
                   pl.BlockSpec((None, None, gr, LANE, ncp), lambda j, b: (j, b, 0, 0, 0))],
        out_shape=[jax.ShapeDtypeStruct((2, batch, gr, ncp, LANE), BF16),
                   jax.ShapeDtypeStruct((2, batch, gr, LANE, ncp), BF16)],
        compiler_params=_cparams("arbitrary", "arbitrary"),
        name="nsa_compress",
    )(x16, wt, wb, pos, w1, b1, w2, b2)


NSA_QW = NSA_HEADS * NSA_HEAD_DIM
G3 = NSA_REP * TQ


def _nsa_in_body(h_ref, g_ref, w_ref, kv_ref, qt_ref, gt_ref, mo_ref):
    xn = _rms(h_ref[...], g_ref[...]).astype(BF16)
    for j in range(NSA_QW // LANE):
        y = _dot(xn, w_ref[:, j * LANE:(j + 1) * LANE]) * (NSA_HEAD_DIM ** -0.5 * LOG2E)
        qt_ref[j * LANE:(j + 1) * LANE, :] = y.T.astype(BF16)
    gt_ref[...] = _sigmoid(_dot(xn, w_ref[:, NSA_QW:NSA_QW + LANE])).T
    mo_ref[...] = _mem_attention(_dot(xn, w_ref[:, NSA_QW + LANE:]).astype(BF16), kv_ref)


def _nsa_in_proj(h, g, w, mem_kv, batch):
    t = h.shape[0]
    tm = ROW_TILE
    row = lambda n: pl.BlockSpec((tm, n), lambda i: (i, 0))
    col = lambda n: pl.BlockSpec((n, tm), lambda i: (0, i))
    return pl.pallas_call(
        _nsa_in_body,
        grid=(t // tm,),
        in_specs=[row(D_MODEL), _resident(g), _resident(w), _mem_kv_spec(mem_kv, t, tm, batch)],
        out_specs=[col(NSA_QW), col(LANE), row(MEM_QW)],
        out_shape=[jax.ShapeDtypeStruct((NSA_QW, t), BF16), jax.ShapeDtypeStruct((LANE, t), F32),
                   jax.ShapeDtypeStruct((t, MEM_QW), BF16)],
        compiler_params=_cparams("parallel"),
        name="nsa_in_proj",
    )(h, g, w, mem_kv)


def _nsa_attn_body(qt_ref, gt_ref, kc_ref, vct_ref, ks_ref, kw_ref, vs_ref, vw_ref, ovt_ref, tz_ref, cb_ref,
                   o_ref, sc_ref, qa_ref, acc_ref, ot_ref, s0_ref, s1_ref, s2_ref, s3_ref, *, nsb, ncp, n_sel):
    n = pl.program_id(1)
    t0 = n * TQ
    dh = NSA_HEAD_DIM
    cstart = pl.multiple_of(ncp - n * (TQ // CMP_STRIDE), SUBLANE)
    sees_any = t0 + (lax.broadcasted_iota(jnp.int32, (1, G3), 1) & (TQ - 1)) >= CMP_BLOCK - 1
    jj = lax.broadcasted_iota(jnp.int32, (nsb, TQ), 0)
    cur = lax.shift_right_logical(t0 + lax.broadcasted_iota(jnp.int32, (nsb, TQ), 1), int(math.log2(SEL_BLOCK)))
    forced = (jj == 0) | (jj == cur) | (jj == cur - 1)
    zpad = jnp.zeros((dh, G3), BF16)
    nwt = WINDOW // TQ

    def scores(k_ref, gi, first, count, tz_index, qa):
        koff = pl.multiple_of(first * TQ, TQ)
        s = _dot(k_ref[gi, pl.ds(koff, count * TQ), :], qa)
        return [s[i * TQ:(i + 1) * TQ] + tz_ref[gi, tz_index(n - (first + i))] for i in range(count)]

    def col_max(parts):
        mx = parts[0]
        for x in parts[1:]:
            mx = jnp.maximum(mx, x)
        return jnp.max(mx, axis=0, keepdims=True)

    def probs(parts, m):
        return jnp.concatenate([jnp.exp2(x - m).astype(BF16) for x in parts], axis=0)

    def values(v_ref, gi, first, count, p):
        koff = pl.multiple_of(first * TQ, TQ)
        return _dot(v_ref[gi, :, pl.ds(koff, count * TQ)], p)

    sel_index = lambda d: jnp.where(d < 0, 2, jnp.minimum(d, 2))
    win_index = lambda d: jnp.where(d < 0, 4, jnp.where(d == nwt, 3, jnp.minimum(d, 2)))

    groups = range(NSA_GROUPS)
    gate = lambda h, c: gt_ref[pl.ds(h * 3 + c, 1), :]
    head_lanes = lambda r: slice(r * TQ, (r + 1) * TQ)

    q3s = [jnp.concatenate([qt_ref[(gi * NSA_REP + r) * dh:(gi * NSA_REP + r + 1) * dh, :] for r in range(NSA_REP)],
                           axis=1) for gi in groups]
    cs = [_dot(kc_ref[gi], jnp.concatenate([q3s[gi], zpad], axis=0)) + cb_ref[gi, pl.ds(cstart, ncp), :]
          for gi in groups]
    cps = [jnp.exp2(s - jnp.max(s, axis=0, keepdims=True)) for s in cs]
    cps = [p * jnp.where(sees_any, 1.0 / jnp.sum(p, axis=0, keepdims=True), 0.0) for p in cps]
    ocs = [_dot(vct_ref[gi], cps[gi].astype(BF16)) for gi in groups]
    scs = []
    for gi in groups:
        psum = cps[gi][:, 0:TQ]
        for r in range(1, NSA_REP):
            psum = psum + cps[gi][:, head_lanes(r)]
        p1 = psum.astype(BF16)
        p2 = (psum - p1.astype(F32)).astype(BF16)
        imp = _dot(ovt_ref[...], p1) + _dot(ovt_ref[...], p2)
        score = jnp.where(forced, 1e4, jnp.where(jj <= cur, imp[0:nsb], -1.0))
        scs.append(jnp.where(score < 0.0, -1, lax.bitcast_convert_type(score, jnp.int32)))
    for gi in groups:
        sc_ref[gi] = scs[gi]
        for r in range(NSA_REP):
            h = gi * NSA_REP + r
            ot_ref[h * dh:(h + 1) * dh, :] = gate(h, 0) * ocs[gi][0:dh, head_lanes(r)]

    scs1 = [k + 1 for k in scs]

    def rank_step(i4, cnts):
        cnts = list(cnts)
        for u in range(RANK_UNROLL):
            i = i4 * RANK_UNROLL + u
            lower = i < jj
            for gi in groups:
                rowk = sc_ref[gi, pl.ds(i, 1), :]
                before = rowk >= jnp.where(lower, scs[gi], scs1[gi])
                cnts[gi] = cnts[gi] + jnp.where(before, 1, 0)
        return tuple(cnts)
    rank_trips = jnp.minimum((2 * n + 2 + RANK_UNROLL - 1) // RANK_UNROLL, nsb // RANK_UNROLL)
    cnts = lax.fori_loop(0, rank_trips, rank_step, tuple(jnp.zeros((nsb, TQ), jnp.int32) for _ in groups))
    for gi in groups:
        selneg = jnp.where((cnts[gi] < n_sel) & (jj <= cur), 0.0, NEG).astype(BF16)
        if nsb < SEL_BLOCK:
            selneg = jnp.concatenate([selneg, jnp.zeros((SEL_BLOCK - nsb, TQ), BF16)], axis=0)
        qa_ref[gi] = jnp.concatenate([q3s[gi], jnp.concatenate([selneg] * NSA_REP, axis=1)], axis=0)

    acc_ref[...] = jnp.zeros_like(acc_ref)
    half = NSA_GROUPS // 2
    s_refs = ((s0_ref, s1_ref), (s2_ref, s3_ref))

    def score_half(it, hb):
        plist = [scores(ks_ref, hb * half + u, it * SEL_TILES, SEL_TILES, sel_index, qa_ref[hb * half + u])
                 for u in range(half)]
        for u, parts in enumerate(plist):
            for i, x in enumerate(parts):
                s_refs[hb][u][i * TQ:(i + 1) * TQ, :] = x
        return [col_max(parts) for parts in plist]

    def finish_half(it, hb, ms, bms):
        m2s = [jnp.maximum(ms[u], bms[u]) for u in range(half)]
        ps = [probs([s_refs[hb][u][i * TQ:(i + 1) * TQ, :] for i in range(SEL_TILES)], m2s[u]) for u in range(half)]
        vals = [values(vs_ref, hb * half + u, it * SEL_TILES, SEL_TILES, ps[u]) for u in range(half)]
        for u in range(half):
            gi = hb * half + u
            acc_ref[gi] = jnp.exp2(ms[u] - m2s[u]) * acc_ref[gi] + vals[u]
        return m2s

    def sel_trip(it, carry, score_next):
        ms0, ms1, bm0 = carry
        bm1 = score_half(it, 1)
        ms0 = finish_half(it, 0, ms0, bm0)
        if score_next:
            bm0 = score_half(it + 1, 0)
        ms1 = finish_half(it, 1, ms1, bm1)
        return ms0, ms1, bm0

    neg = [jnp.full((1, G3), NEG, F32) for _ in range(half)]
    carry = (neg, neg, score_half(0, 0))
    carry = lax.fori_loop(0, n // SEL_TILES, lambda it, c: sel_trip(it, c, True), carry)
    sel_trip(n // SEL_TILES, carry, False)

    wfirst = jnp.maximum(n - nwt, 0)
    wparts = [scores(kw_ref, gi, wfirst, nwt + 1, win_index, qa_ref[gi]) for gi in groups]
    wps = [probs(parts, col_max(parts)) for parts in wparts]
    ows = [values(vw_ref, gi, wfirst, nwt + 1, wps[gi]) for gi in groups]
    for gi in groups:
        o_s = acc_ref[gi, 0:dh, :] * (1.0 / acc_ref[gi, dh:dh + 1, :])
        o_w = ows[gi][0:dh, :] * (1.0 / ows[gi][dh:dh + 1, :])
        for r in range(NSA_REP):
            h = gi * NSA_REP + r
            ot_ref[h * dh:(h + 1) * dh, :] += gate(h, 1) * o_s[:, head_lanes(r)] + gate(h, 2) * o_w[:, head_lanes(r)]

    for j in range(NSA_QW // LANE):
        o_ref[:, j * LANE:(j + 1) * LANE] = ot_ref[j * LANE:(j + 1) * LANE, :].T.astype(BF16)


def _nsa_attn(qt, gt, kc, vct, ksel, kwin, vsel, vwin, ovt, tz, cb, batch):
    t = qt.shape[1]
    s = t // batch
    nt = s // TQ
    nsb = s // SEL_BLOCK
    ncp = kc.shape[-2]
    per_b = lambda a: pl.BlockSpec((None,) + a.shape[1:], lambda b, i: (b,) + (0,) * (a.ndim - 1))
    full = lambda a: pl.BlockSpec(a.shape, lambda b, i: (0,) * a.ndim)
    return pl.pallas_call(
        functools.partial(_nsa_attn_body, nsb=nsb, ncp=ncp, n_sel=min(SEL_TOPK, nsb)),
        grid=(batch, nt),
        in_specs=[pl.BlockSpec((NSA_QW, TQ), lambda b, i: (0, b * nt + i)),
                  pl.BlockSpec((LANE, TQ), lambda b, i: (0, b * nt + i)),
                  per_b(kc), per_b(vct), per_b(ksel), per_b(kwin), per_b(vsel), per_b(vwin),
                  full(ovt), full(tz), full(cb)],
        out_specs=pl.BlockSpec((TQ, NSA_QW), lambda b, i: (b * nt + i, 0)),
        out_shape=jax.ShapeDtypeStruct((t, NSA_QW), BF16),
        scratch_shapes=[pltpu.VMEM((NSA_GROUPS, nsb, TQ), jnp.int32), pltpu.VMEM((NSA_GROUPS, LANE, G3), BF16),
                        pltpu.VMEM((NSA_GROUPS, V_ROWS, G3), F32), pltpu.VMEM((NSA_QW, TQ), F32),
                        ] + [pltpu.VMEM((SEL_TILES * TQ, G3), F32)] * NSA_GROUPS,
        compiler_params=_cparams("parallel", "arbitrary"),
        name="nsa_attn",
    )(qt, gt, kc, vct, ksel, kwin, vsel, vwin, ovt, tz, cb)


def _rel_bucket_np(dist):
    dist = np.maximum(dist, 0)
    max_exact = REL_BUCKETS // 2
    ratio = np.log(np.maximum(dist, 1).astype(np.float32) / np.float32(max_exact)) / np.float32(
        math.log(REL_MAX_DIST / max_exact))
    large = max_exact + (ratio * np.float32(REL_BUCKETS - max_exact)).astype(np.int32)
    large = np.minimum(large, REL_BUCKETS - 1)
    return np.where(dist < max_exact, dist, large).astype(np.int32)


def _group_lanes(a):
    hh, r, c = a.shape
    return a.reshape(NSA_GROUPS, NSA_REP, r, c).transpose(0, 2, 1, 3).reshape(NSA_GROUPS, r, NSA_REP * c)


def _bias_tables(rel_bias, ncp):
    k = np.arange(TQ)[:, None]
    q = np.arange(TQ)[None, :]
    tbl = rel_bias.astype(F32)

    def lookup(idx):
        onehot = (jnp.asarray(idx.reshape(1, -1)) == jnp.arange(REL_BUCKETS)[:, None]).astype(F32)
        out = jnp.dot(tbl.T, onehot, precision=lax.Precision.HIGHEST)
        return out.reshape((NSA_HEADS,) + idx.shape)

    far = jnp.broadcast_to(tbl[REL_BUCKETS - 1][:, None, None], (NSA_HEADS, TQ, TQ))
    t0 = jnp.where(jnp.asarray(k <= q)[None], lookup(_rel_bucket_np(q - k)), NEG)
    t1 = lookup(_rel_bucket_np(TQ + q - k))
    t3 = jnp.where(jnp.asarray(k > q)[None], far, NEG)
    tz = jnp.stack([_group_lanes(x) for x in (t0, t1, far, t3, jnp.full_like(far, NEG))], axis=1)
    m = ncp - np.arange(2 * ncp)[:, None]
    d = CMP_STRIDE * m + q - (CMP_BLOCK - 1)
    idx = np.where((d >= 0) & (d < REL_MAX_DIST), _rel_bucket_np(d), REL_BUCKETS - 1)
    cb = _group_lanes(jnp.where(jnp.asarray(d >= 0)[None], lookup(idx), NEG))
    return tz * LOG2E, cb * LOG2E


def _overlap_table(s, ncp):
    nsb = s // SEL_BLOCK
    nc = (s - CMP_BLOCK) // CMP_STRIDE + 1
    cs = np.arange(ncp) * CMP_STRIDE
    ce = cs + CMP_BLOCK - 1
    ss = np.arange(SEL_BLOCK) * SEL_BLOCK
    ov = (cs[None, :] < ss[:, None] + SEL_BLOCK) & (ce[None, :] >= ss[:, None])
    ov &= (np.arange(ncp) < nc)[None, :] & (np.arange(SEL_BLOCK) < nsb)[:, None]
    return jnp.asarray(ov, dtype=BF16)


def kernel(x, mem, norm_mix, norm_mem, w_mem_kv, w_out, norm_ffn, w_up, conv_w, conv_b, w_down,
           gla_w_in, gla_w_gate_up, gla_b_gate, gla_out_norm, nsa_w_in, kv_norm, w_kv_shared,
           cmp_pos, cmp_w1, cmp_b1, cmp_w2, cmp_b2, rel_bias, final_norm):
    batch, seq = x.shape[0], x.shape[1]
    t = batch * seq
    h = x.reshape(t, D_MODEL)
    row = lambda v: v.reshape(1, -1).astype(F32)

    wk, wv = w_mem_kv[..., :MEM_W], w_mem_kv[..., MEM_W:]
    w_mkv = jnp.concatenate([_pad_heads(wk, MEM_HEADS, MEM_HEAD_DIM, MEM_DP),
                             _pad_heads(wv, MEM_HEADS, MEM_HEAD_DIM, MEM_DP)], axis=-1).astype(BF16)
    mem_kv_all = _mem_kv_proj(mem.reshape(-1, D_MODEL), norm_mem.reshape(DEPTH, 1, D_MODEL), w_mkv, batch)

    shared = None
    for i in range(DEPTH):
        w_o = w_out[i]
        w_o_mem = _pad_head_rows(w_o[MAIN_W:], MEM_HEADS, MEM_HEAD_DIM, MEM_DP).astype(BF16)
        if i < N_A_LAYERS:
            wi = gla_w_in[i]
            c0 = GLA_HEADS * GLA_DK
            c1 = 2 * c0
            c2 = c1 + GLA_HEADS * GLA_DV
            c3 = c2 + GLA_HEADS * GLA_DV
            c4 = c3 + GLA_RANK
            w_all = jnp.concatenate([
                _pad_heads(wi[:, :c0], GLA_HEADS, GLA_DK, GLA_DKP),
                _pad_heads(wi[:, c0:c1], GLA_HEADS, GLA_DK, GLA_DKP),
                _pad_heads(wi[:, c1:c2], GLA_HEADS, GLA_DV, GLA_DVP),
                _pad_heads(wi[:, c2:c3], GLA_HEADS, GLA_DV, GLA_DVP),
                _pad_heads(wi[:, c3:c4], 1, GLA_RANK, LANE),
                _pad_heads(wi[:, c4:], MEM_HEADS, MEM_HEAD_DIM, MEM_DP)], axis=1).astype(BF16)
            wg = jnp.pad(_pad_heads(gla_w_gate_up[i], GLA_HEADS, GLA_DK, GLA_DKP),
                         ((0, LANE - GLA_RANK), (0, 0))).astype(BF16)
            bg = row(_pad_heads(gla_b_gate[i], GLA_HEADS, GLA_DK, GLA_DKP))
            q, k, v, go, la, mo = _gla_in_proj(h, row(norm_mix[i]), w_all, wg, bg, mem_kv_all[i], batch)
            on = row(jnp.pad(gla_out_norm[i], (0, GLA_DVP - GLA_DV)))
            main = _gla_mixer(q, k, la, v, go, on, batch)
            w_o_main = _pad_head_rows(w_o[:MAIN_W], GLA_HEADS, GLA_DV, GLA_DVP).astype(BF16)
        else:
            if shared is None:
                ncp = seq // CMP_STRIDE
                gr, dh = NSA_GROUPS, NSA_HEAD_DIM
                wkv = w_kv_shared.reshape(D_MODEL, 6, gr, dh)
                pair = lambda a, b: jnp.concatenate([wkv[:, a], wkv[:, b]], axis=-1).reshape(D_MODEL, gr * 2 * dh)
                slot = lambda a: _pad_heads(wkv[:, a].reshape(D_MODEL, KV_NAT), gr, dh, LANE)
                w_kv = jnp.concatenate([wkv[:, 0].reshape(D_MODEL, KV_NAT), wkv[:, 1].reshape(D_MODEL, KV_NAT),
                                        slot(2), slot(4), pair(3, 5)], axis=1).astype(BF16)
                ck, cv, ksel, kwin, vsel, vwin = _kv_proj(h, row(kv_norm), w_kv, batch)
                x16 = jnp.stack([ck.reshape(batch * ncp, CMP_STRIDE * KV_NAT),
                                 cv.reshape(batch * ncp, CMP_STRIDE * KV_NAT)])
                w1 = cmp_w1.reshape(2, 2, CMP_STRIDE, dh, CMP_HIDDEN)
                eye = jnp.eye(gr, dtype=F32)
                w1x = jnp.einsum('jhldc,gk->jhlgdkc', w1, eye).reshape(2, 2, CMP_STRIDE * KV_NAT, gr * CMP_HIDDEN)
                w1x = w1x.astype(BF16)
                pos8 = jnp.broadcast_to(cmp_pos.reshape(2, 1, CMP_BLOCK * dh), (2, SUBLANE, CMP_BLOCK * dh)).astype(BF16)
                w2p = jnp.pad(cmp_w2, ((0, 0), (0, 0), (0, LANE - dh))).astype(BF16)
                b2p = jnp.pad(cmp_b2, ((0, 0), (0, LANE - dh))).reshape(2, 1, LANE).astype(F32)
                cnat, ctr = _compress(x16, w1x[:, 0], w1x[:, 1], pos8, cmp_w1.astype(BF16),
                                      cmp_b1.reshape(2, 1, CMP_HIDDEN).astype(F32), w2p, b2p, batch)
                tz, cb = _bias_tables(rel_bias, ncp)
                ov = _overlap_table(seq, ncp)
                shared = (cnat[0], ctr[1], ksel, kwin, vsel, vwin, ov, tz, cb)
            wi = nsa_w_in[i - N_A_LAYERS]
            c0 = NSA_HEADS * NSA_HEAD_DIM
            c1 = c0 + NSA_HEADS * 3
            w_all = jnp.concatenate([
                wi[:, :c0],
                _pad_heads(wi[:, c0:c1], 1, NSA_HEADS * 3, LANE),
                _pad_heads(wi[:, c1:], MEM_HEADS, MEM_HEAD_DIM, MEM_DP)], axis=1).astype(BF16)
            q, gates, mo = _nsa_in_proj(h, row(norm_mix[i]), w_all, mem_kv_all[i], batch)
            main = _nsa_attn(q, gates, *shared, batch)
            w_o_main = w_o[:MAIN_W].astype(BF16)
        h = _mix_ffn(h, main, mo, w_o_main, w_o_mem, row(norm_ffn[i]), w_up[i].astype(BF16), conv_w[i].astype(F32),
                     row(conv_b[i]), w_down[i].astype(BF16), row(final_norm), i == DEPTH - 1, batch)
    return h.reshape(batch, seq, D_MODEL)
```

```python
import functools
import math

import numpy as np
import jax
import jax.numpy as jnp
from jax import lax
from jax.experimental import pallas as pl
from jax.experimental.pallas import tpu as pltpu

F32 = jnp.float32
BF16 = jnp.bfloat16

D_MODEL = 1024
DEPTH = 4
N_A_LAYERS = DEPTH // 2
MEM_HEADS = 4
MEM_HEAD_DIM = 64
MEM_W = MEM_HEADS * MEM_HEAD_DIM
MAIN_W = D_MODEL - MEM_W
GLA_HEADS = 4
GLA_DV = MAIN_W // GLA_HEADS
GLA_DK = GLA_DV // 2
GLA_RANK = 16
GLA_GATE_NORM = 16.0
NSA_HEADS = 12
NSA_GROUPS = 4
NSA_HEAD_DIM = MAIN_W // NSA_HEADS
NSA_REP = NSA_HEADS // NSA_GROUPS
CMP_BLOCK = 32
CMP_STRIDE = 16
CMP_HIDDEN = 128
SEL_BLOCK = 64
SEL_TOPK = 16
WINDOW = 512
REL_BUCKETS = 32
REL_MAX_DIST = 128
FFN_DIM = 2816
CONV_WIDTH = 3
EPS = 1e-6

LANE = 128
SUBLANE = 8
VMEM_LIMIT = 56 * 1024 * 1024
GLA_DKP = LANE
GLA_DVP = 2 * LANE
MEM_DP = LANE
NEG = -1e30
TQ = 128
ROW_TILE = 512
GLA_CHUNK = 64
FFN_TILE = 256
FFN_ROWS = 1024
SEL_TILES = 4
V_ROWS = 64 + 16
RANK_UNROLL = 4
LOG2E = math.log2(math.e)

NT = (((1,), (1,)), ((), ()))
TN = (((0,), (0,)), ((), ()))


def _cparams(*sem):
    return pltpu.CompilerParams(dimension_semantics=sem, vmem_limit_bytes=VMEM_LIMIT)


def _rms(x, g):
    return x * lax.rsqrt(jnp.mean(x * x, axis=-1, keepdims=True) + EPS) * g


def _sigmoid(x):
    return 1.0 / (1.0 + jnp.exp(-x))


def _dot(a, b):
    return jnp.dot(a, b, preferred_element_type=F32)


def _resident(a):
    return pl.BlockSpec(a.shape, lambda *_: (0,) * a.ndim, pipeline_mode=pl.Buffered(1))


def _pad_heads(w, nh, d, dp):
    lead = w.shape[:-1]
    w = w.reshape(lead + (nh, d))
    w = jnp.pad(w, [(0, 0)] * len(lead) + [(0, 0), (0, dp - d)])
    return w.reshape(lead + (nh * dp,))


def _pad_head_rows(w, nh, d, dp):
    n = w.shape[-1]
    w = jnp.pad(w.reshape(nh, d, n), ((0, 0), (0, dp - d), (0, 0)))
    return w.reshape(nh * dp, n)


def _mem_attention(q, kv_ref):
    outs = []
    for hd in range(MEM_HEADS):
        sl = slice(hd * MEM_DP, (hd + 1) * MEM_DP)
        sv = slice(MEM_QW + hd * MEM_DP, MEM_QW + (hd + 1) * MEM_DP)
        s = lax.dot_general(q[:, sl], kv_ref[:, sl], NT, preferred_element_type=F32) * MEM_HEAD_DIM ** -0.5
        p = jnp.exp(s - jnp.max(s, axis=-1, keepdims=True))
        l = jnp.sum(p, axis=-1, keepdims=True)
        outs.append((_dot(p.astype(BF16), kv_ref[:, sv]) / l).astype(BF16))
    return jnp.concatenate(outs, axis=1)


GLA_QW = GLA_HEADS * GLA_DKP
GLA_VW = GLA_HEADS * GLA_DVP
MEM_QW = MEM_HEADS * MEM_DP
GLA_OFF_K = GLA_QW
GLA_OFF_V = 2 * GLA_QW
GLA_OFF_G = GLA_OFF_V + GLA_VW
GLA_OFF_LR = GLA_OFF_G + GLA_VW
GLA_OFF_MQ = GLA_OFF_LR + LANE
GLA_NP = GLA_OFF_MQ + MEM_QW


def _gla_in_body(h_ref, g_ref, w_ref, wg_ref, bg_ref, kv_ref, q_ref, k_ref, v_ref, go_ref, la_ref, mo_ref):
    xn = _rms(h_ref[...], g_ref[...]).astype(BF16)

    def proj(lo, n):
        return _dot(xn, w_ref[:, lo:lo + n])

    q_ref[...] = proj(0, GLA_QW).astype(BF16)
    k_ref[...] = proj(GLA_OFF_K, GLA_QW).astype(BF16)
    for j in range(GLA_VW // GLA_QW):
        v_ref[:, j * GLA_QW:(j + 1) * GLA_QW] = proj(GLA_OFF_V + j * GLA_QW, GLA_QW).astype(BF16)
        go_ref[:, j * GLA_QW:(j + 1) * GLA_QW] = proj(GLA_OFF_G + j * GLA_QW, GLA_QW).astype(BF16)
    lr = proj(GLA_OFF_LR, LANE).astype(BF16)
    z = _dot(lr, wg_ref[...]) + bg_ref[...]
    la_ref[...] = (jnp.minimum(z, 0.0) - jnp.log(1.0 + jnp.exp(-jnp.abs(z)))) * (1.0 / GLA_GATE_NORM)
    mo_ref[...] = _mem_attention(proj(GLA_OFF_MQ, MEM_QW).astype(BF16), kv_ref)


def _mem_kv_spec(mem_kv, t, tm, batch):
    nt = t // batch // tm
    return pl.BlockSpec((mem_kv.shape[0] // batch, mem_kv.shape[1]), lambda i: (i // nt, 0))


def _gla_in_proj(h, g, w, wg, bg, mem_kv, batch):
    t = h.shape[0]
    tm = ROW_TILE
    row = lambda n: pl.BlockSpec((tm, n), lambda i: (i, 0))
    return pl.pallas_call(
        _gla_in_body,
        grid=(t // tm,),
        in_specs=[row(D_MODEL), _resident(g), _resident(w), _resident(wg), _resident(bg),
                  _mem_kv_spec(mem_kv, t, tm, batch)],
        out_specs=[row(GLA_QW), row(GLA_QW), row(GLA_VW), row(GLA_VW), row(GLA_QW), row(MEM_QW)],
        out_shape=[jax.ShapeDtypeStruct((t, GLA_QW), BF16), jax.ShapeDtypeStruct((t, GLA_QW), BF16),
                   jax.ShapeDtypeStruct((t, GLA_VW), BF16), jax.ShapeDtypeStruct((t, GLA_VW), BF16),
                   jax.ShapeDtypeStruct((t, GLA_QW), F32), jax.ShapeDtypeStruct((t, MEM_QW), BF16)],
        compiler_params=_cparams("parallel"),
        name="gla_in_proj",
    )(h, g, w, wg, bg, mem_kv)


def _gla_mix_body(q_ref, k_ref, la_ref, v_ref, go_ref, on_ref, o_ref, st_ref, *, cn, batch):
    @pl.when(pl.program_id(0) == 0)
    def _():
        st_ref[...] = jnp.zeros_like(st_ref)

    row = lax.broadcasted_iota(jnp.int32, (cn, cn), 0)
    col = lax.broadcasted_iota(jnp.int32, (cn, cn), 1)
    causal = row >= col
    tril = jnp.where(causal, 1.0, 0.0).astype(BF16)
    scale = GLA_DK ** -0.5
    chains = [(bi, hd) for bi in range(batch) for hd in range(GLA_HEADS)]
    sk = lambda hd: slice(hd * GLA_DKP, (hd + 1) * GLA_DKP)
    sv = lambda hd: slice(hd * GLA_DVP, (hd + 1) * GLA_DVP)
    dg = lambda x, y, dims: lax.dot_general(x, y, dims, preferred_element_type=F32)

    def cum_decay(bi, hd):
        la = la_ref[bi, :, sk(hd)]
        la1 = la.astype(BF16)
        r1 = la - la1.astype(F32)
        la2 = r1.astype(BF16)
        la3 = (r1 - la2.astype(F32)).astype(BF16)
        return _dot(tril, la1) + _dot(tril, la2) + _dot(tril, la3)

    bs = [cum_decay(bi, hd) for bi, hd in chains]
    scaled = []
    for (bi, hd), b in zip(chains, bs):
        b_last = b[cn - 1:cn, :]
        q = q_ref[bi, :, sk(hd)].astype(F32)
        k = k_ref[bi, :, sk(hd)].astype(F32)
        scaled.append(((q * jnp.exp(b) * scale).astype(BF16), (k * jnp.exp(-b)).astype(BF16),
                       (k * jnp.exp(b_last - b)).astype(BF16), jnp.exp(b_last)))
    intra = [jnp.where(causal, dg(qs, ks, NT), 0.0).astype(BF16) for qs, ks, _, _ in scaled]
    outs = []
    for i, (bi, hd) in enumerate(chains):
        qs, _, ko, a_last = scaled[i]
        v = v_ref[bi, :, sv(hd)]
        st = st_ref[i]
        outs.append(_dot(intra[i], v) + dg(qs, st.astype(BF16), NT))
        st_ref[i] = st * a_last + dg(v, ko, TN)
    for (bi, hd), o in zip(chains, outs):
        ms = jnp.sum(o * o, axis=-1, keepdims=True) * (1.0 / GLA_DV)
        y = o * lax.rsqrt(ms + EPS) * on_ref[...]
        g = go_ref[bi, :, sv(hd)].astype(F32)
        o_ref[bi, :, sv(hd)] = (y * (g * _sigmoid(g))).astype(BF16)


def _gla_mixer(q, k, la, v, go, on, batch):
    t = q.shape[0]
    s = t // batch
    cn = GLA_CHUNK
    seq = lambda a: a.reshape(batch, s, a.shape[1])
    blk = lambda n: pl.BlockSpec((batch, cn, n), lambda c: (0, c, 0))
    out = pl.pallas_call(
        functools.partial(_gla_mix_body, cn=cn, batch=batch),
        grid=(s // cn,),
        in_specs=[blk(GLA_QW), blk(GLA_QW), blk(GLA_QW), blk(GLA_VW), blk(GLA_VW),
                  pl.BlockSpec(on.shape, lambda c: (0, 0))],
        out_specs=blk(GLA_VW),
        out_shape=jax.ShapeDtypeStruct((batch, s, GLA_VW), BF16),
        scratch_shapes=[pltpu.VMEM((batch * GLA_HEADS, GLA_DVP, GLA_DKP), F32)],
        compiler_params=_cparams("arbitrary"),
        name="gla_mixer",
    )(seq(q), seq(k), seq(la), seq(v), seq(go), on)
    return out.reshape(t, GLA_VW)


def _mem_kv_body(m_ref, g_ref, w_ref, o_ref):
    xn = _rms(m_ref[...], g_ref[...]).astype(BF16)
    o_ref[...] = _dot(xn, w_ref[...]).astype(BF16)


def _mem_kv_proj(mem2, g, w, batch):
    nl = w.shape[0]
    m = mem2.shape[0] // batch
    n = w.shape[2]
    return pl.pallas_call(
        _mem_kv_body,
        grid=(nl, batch),
        in_specs=[pl.BlockSpec((m, D_MODEL), lambda l, b: (b, 0)),
                  pl.BlockSpec((None, 1, D_MODEL), lambda l, b: (l, 0, 0)),
                  pl.BlockSpec((None, D_MODEL, n), lambda l, b: (l, 0, 0))],
        out_specs=pl.BlockSpec((None, m, n), lambda l, b: (l, b, 0)),
        out_shape=jax.ShapeDtypeStruct((nl, mem2.shape[0], n), BF16),
        compiler_params=_cparams("arbitrary", "arbitrary"),
        name="mem_kv_proj",
    )(mem2, g, w)


BF16_ROWS = 2 * SUBLANE


def _mix_ffn_body(h_ref, hp_ref, a_ref, ap_ref, m_ref, mp_ref, wa_ref, wm_ref, g_ref, wup_ref, cw_ref, cb_ref,
                  wdn_ref, gf_ref, o_ref, act_ref, *, tm, tf, last):
    g = g_ref[...]
    nr = tm // SUBLANE
    interleave = lambda x: x.reshape(SUBLANE, nr, x.shape[-1]).swapaxes(0, 1).reshape(tm, x.shape[-1])
    deinterleave = lambda x: x.reshape(nr, SUBLANE, x.shape[-1]).swapaxes(0, 1).reshape(tm, x.shape[-1])
    h = interleave(h_ref[...] + _dot(a_ref[...], wa_ref[...]) + _dot(m_ref[...], wm_ref[...]))
    x = _rms(h, g).astype(BF16)
    keep = jnp.where(pl.program_id(1) > 0, 1.0, 0.0)
    h_prev = (hp_ref[...] + _dot(ap_ref[...], wa_ref[...]) + _dot(mp_ref[...], wm_ref[...]))[BF16_ROWS - SUBLANE:]
    x_prev = (_rms(h_prev, g) * keep).astype(BF16)
    first = lax.broadcasted_iota(jnp.int32, (SUBLANE, tf), 0) == 0
    nchunk = FFN_DIM // tf

    def up(j):
        cols = [pl.ds(off + j * tf, tf) for off in (0, FFN_DIM)]
        return tuple((_dot(x, wup_ref[:, c]), _dot(x_prev, wup_ref[:, c])) for c in cols)

    def conv(u, u_prev, off):
        w = cw_ref[:, off:off + tf]
        wrap = lambda r, k: jnp.where(first, u_prev[SUBLANE - k:SUBLANE - k + 1, :],
                                      pltpu.roll(u[r * SUBLANE:(r + 1) * SUBLANE, :], 1, 0))
        back1 = jnp.concatenate([wrap(nr - 1, 1), u[0:tm - SUBLANE, :]], axis=0)
        back2 = jnp.concatenate([wrap(nr - 2, 2), wrap(nr - 1, 1), u[0:tm - 2 * SUBLANE, :]], axis=0)
        return cb_ref[:, off:off + tf] + w[0:1, :] * back2 + w[1:2, :] * back1 + w[2:3, :] * u

    u_next = up(0)
    for j in range(nchunk):
        (ua, ua_prev), (ub, ub_prev) = u_next
        if j + 1 < nchunk:
            u_next = up(j + 1)
        a = conv(ua, ua_prev, j * tf)
        b = conv(ub, ub_prev, FFN_DIM + j * tf)
        act_ref[:, j * tf:(j + 1) * tf] = (a * _sigmoid(a) * b).astype(BF16)
    out = h + _dot(act_ref[...], wdn_ref[...])
    o_ref[...] = deinterleave(_rms(out, gf_ref[...]) if last else out)


def _mix_ffn(h, main, mo, wa, wm, g, wup, cw, cb, wdn, gf, last, batch):
    t = h.shape[0]
    tm = FFN_ROWS
    nt = t // batch // tm
    hb = tm // BF16_ROWS
    cur = lambda n: pl.BlockSpec((tm, n), lambda b, i: (b * nt + i, 0))
    prev = lambda n: pl.BlockSpec((BF16_ROWS, n), lambda b, i: (jnp.maximum((b * nt + i) * hb - 1, 0), 0))
    return pl.pallas_call(
        functools.partial(_mix_ffn_body, tm=tm, tf=FFN_TILE, last=last),
        grid=(batch, nt),
        in_specs=[cur(D_MODEL), prev(D_MODEL), cur(main.shape[1]), prev(main.shape[1]),
                  cur(mo.shape[1]), prev(mo.shape[1]), _resident(wa), _resident(wm), _resident(g),
                  _resident(wup), _resident(cw), _resident(cb), _resident(wdn), _resident(gf)],
        out_specs=cur(D_MODEL),
        out_shape=jax.ShapeDtypeStruct((t, D_MODEL), F32),
        scratch_shapes=[pltpu.VMEM((tm, FFN_DIM), BF16)],
        compiler_params=_cparams("parallel", "parallel"),
        name="mix_ffn",
    )(h, h, main, main, mo, mo, wa, wm, g, wup, cw, cb, wdn, gf)


KV_NAT = NSA_GROUPS * NSA_HEAD_DIM


def _kv_proj_body(h_ref, g_ref, w_ref, ck_ref, cv_ref, ks_ref, kw_ref, vs_ref, vw_ref, *, tm):
    xn = _rms(h_ref[...], g_ref[...]).astype(BF16)
    ck_ref[...] = _dot(xn, w_ref[:, 0:KV_NAT]).astype(BF16)
    cv_ref[...] = _dot(xn, w_ref[:, KV_NAT:2 * KV_NAT]).astype(BF16)
    key = pl.program_id(1) * tm + lax.broadcasted_iota(jnp.int32, (tm, LANE), 0)
    lane = lax.broadcasted_iota(jnp.int32, (tm, LANE), 1)
    onehot = jnp.where(lax.shift_right_logical(key, int(math.log2(SEL_BLOCK))) == lane - NSA_HEAD_DIM, 1.0, 0.0)
    low = lane < NSA_HEAD_DIM
    for gi in range(NSA_GROUPS):
        slot = lambda n: _dot(xn, w_ref[:, 2 * KV_NAT + (n * NSA_GROUPS + gi) * LANE:
                                           2 * KV_NAT + (n * NSA_GROUPS + gi + 1) * LANE])
        ks_ref[gi] = jnp.where(low, slot(0), onehot).astype(BF16)
        kw_ref[gi] = jnp.where(low, slot(1), 0.0).astype(BF16)
        vt = slot(2).T.astype(BF16)
        ones = jnp.ones((V_ROWS - NSA_HEAD_DIM, tm), BF16)
        vs_ref[gi] = jnp.concatenate([vt[0:NSA_HEAD_DIM], ones], axis=0)
        vw_ref[gi] = jnp.concatenate([vt[NSA_HEAD_DIM:], ones], axis=0)


def _kv_proj(h, g, w, batch):
    t = h.shape[0]
    s = t // batch
    tm = ROW_TILE
    nt = s // tm
    gr = NSA_GROUPS
    return pl.pallas_call(
        functools.partial(_kv_proj_body, tm=tm),
        grid=(batch, nt),
        in_specs=[pl.BlockSpec((tm, D_MODEL), lambda b, i: (b * nt + i, 0)),
                  pl.BlockSpec(g.shape, lambda b, i: (0, 0)),
                  pl.BlockSpec(w.shape, lambda b, i: (0, 0))],
        out_specs=[pl.BlockSpec((tm, KV_NAT), lambda b, i: (b * nt + i, 0)),
                   pl.BlockSpec((tm, KV_NAT), lambda b, i: (b * nt + i, 0)),
                   pl.BlockSpec((None, gr, tm, LANE), lambda b, i: (b, 0, i, 0)),
                   pl.BlockSpec((None, gr, tm, LANE), lambda b, i: (b, 0, i, 0)),
                   pl.BlockSpec((None, gr, V_ROWS, tm), lambda b, i: (b, 0, 0, i)),
                   pl.BlockSpec((None, gr, V_ROWS, tm), lambda b, i: (b, 0, 0, i))],
        out_shape=[jax.ShapeDtypeStruct((t, KV_NAT), BF16), jax.ShapeDtypeStruct((t, KV_NAT), BF16),
                   jax.ShapeDtypeStruct((batch, gr, s, LANE), BF16),
                   jax.ShapeDtypeStruct((batch, gr, s, LANE), BF16),
                   jax.ShapeDtypeStruct((batch, gr, V_ROWS, s), BF16),
                   jax.ShapeDtypeStruct((batch, gr, V_ROWS, s), BF16)],
        compiler_params=_cparams("parallel", "parallel"),
        name="nsa_kv_proj",
    )(h, g, w)


def _compress_body(x_ref, wt_ref, wb_ref, pos_ref, w1_ref, b1_ref, w2_ref, b2_ref, on_ref, ot_ref, *, ncp):
    x = x_ref[...]
    top = _dot(x, wt_ref[...])
    bot = _dot(x, wb_ref[...])
    posb = _dot(pos_ref[...], w1_ref[...])[0:1, :] + b1_ref[...]
    for gi in range(NSA_GROUPS):
        sl = slice(gi * CMP_HIDDEN, (gi + 1) * CMP_HIDDEN)
        hid = top[:, sl] + pltpu.roll(bot[:, sl], ncp - 1, 0) + posb
        hid = (hid * _sigmoid(hid)).astype(BF16)
        out = _dot(hid, w2_ref[...]) + b2_ref[...]
        on_ref[gi] = out.astype(BF16)
        ot_ref[gi] = out.T.astype(BF16)


def _compress(x16, wt, wb, pos, w1, b1, w2, b2, batch):
    ncp = x16.shape[1] // batch
    gr = NSA_GROUPS
    per_j = lambda a: pl.BlockSpec((None,) + a.shape[1:], lambda j, b: (j,) + (0,) * (a.ndim - 1))
    return pl.pallas_call(
        functools.partial(_compress_body, ncp=ncp),
        grid=(2, batch),
        in_specs=[pl.BlockSpec((None, ncp, x16.shape[2]), lambda j, b: (j, b, 0)),
                  per_j(wt), per_j(wb), per_j(pos), per_j(w1), per_j(b1), per_j(w2), per_j(b2)],
        out_specs=[pl.BlockSpec((None, None, gr, ncp, LANE), lambda j, b: (j, b, 0, 0, 0)),
                   pl.BlockSpec((None, None, gr, LANE, ncp), lambda j, b: (j, b, 0, 0, 0))],
        out_shape=[jax.ShapeDtypeStruct((2, batch, gr, ncp, LANE), BF16),
                   jax.ShapeDtypeStruct((2, batch, gr, LANE, ncp), BF16)],
        compiler_params=_cparams("arbitrary", "arbitrary"),
        name="nsa_compress",
    )(x16, wt, wb, pos, w1, b1, w2, b2)


NSA_QW = NSA_HEADS * NSA_HEAD_DIM
G3 = NSA_REP * TQ


def _nsa_in_body(h_ref, g_ref, w_ref, kv_ref, qt_ref, gt_ref, mo_ref):
    xn = _rms(h_ref[...], g_ref[...]).astype(BF16)
    for j in range(NSA_QW // LANE):
        y = _dot(xn, w_ref[:, j * LANE:(j + 1) * LANE]) * (NSA_HEAD_DIM ** -0.5 * LOG2E)
        qt_ref[j * LANE:(j + 1) * LANE, :] = y.T.astype(BF16)
    gt_ref[...] = _sigmoid(_dot(xn, w_ref[:, NSA_QW:NSA_QW + LANE])).T
    mo_ref[...] = _mem_attention(_dot(xn, w_ref[:, NSA_QW + LANE:]).astype(BF16), kv_ref)


def _nsa_in_proj(h, g, w, mem_kv, batch):
    t = h.shape[0]
    tm = ROW_TILE
    row = lambda n: pl.BlockSpec((tm, n), lambda i: (i, 0))
    col = lambda n: pl.BlockSpec((n, tm), lambda i: (0, i))
    return pl.pallas_call(
        _nsa_in_body,
        grid=(t // tm,),
        in_specs=[row(D_MODEL), _resident(g), _resident(w), _mem_kv_spec(mem_kv, t, tm, batch)],
        out_specs=[col(NSA_QW), col(LANE), row(MEM_QW)],
        out_shape=[jax.ShapeDtypeStruct((NSA_QW, t), BF16), jax.ShapeDtypeStruct((LANE, t), F32),
                   jax.ShapeDtypeStruct((t, MEM_QW), BF16)],
        compiler_params=_cparams("parallel"),
        name="nsa_in_proj",
    )(h, g, w, mem_kv)


def _nsa_attn_body(qt_ref, gt_ref, kc_ref, vct_ref, ks_ref, kw_ref, vs_ref, vw_ref, ovt_ref, tz_ref, cb_ref,
                   o_ref, sc_ref, qa_ref, acc_ref, ot_ref, s0_ref, s1_ref, s2_ref, s3_ref, *, nsb, ncp, n_sel):
    n = pl.program_id(1)
    t0 = n * TQ
    dh = NSA_HEAD_DIM
    cstart = pl.multiple_of(ncp - n * (TQ // CMP_STRIDE), SUBLANE)
    sees_any = t0 + (lax.broadcasted_iota(jnp.int32, (1, G3), 1) & (TQ - 1)) >= CMP_BLOCK - 1
    jj = lax.broadcasted_iota(jnp.int32, (nsb, TQ), 0)
    cur = lax.shift_right_logical(t0 + lax.broadcasted_iota(jnp.int32, (nsb, TQ), 1), int(math.log2(SEL_BLOCK)))
    forced = (jj == 0) | (jj == cur) | (jj == cur - 1)
    zpad = jnp.zeros((dh, G3), BF16)
    nwt = WINDOW // TQ

    def scores(k_ref, gi, first, count, tz_index, qa):
        koff = pl.multiple_of(first * TQ, TQ)
        s = _dot(k_ref[gi, pl.ds(koff, count * TQ), :], qa)
        return [s[i * TQ:(i + 1) * TQ] + tz_ref[gi, tz_index(n - (first + i))] for i in range(count)]

    def col_max(parts):
        mx = parts[0]
        for x in parts[1:]:
            mx = jnp.maximum(mx, x)
        return jnp.max(mx, axis=0, keepdims=True)

    def probs(parts, m):
        return jnp.concatenate([jnp.exp2(x - m).astype(BF16) for x in parts], axis=0)

    def values(v_ref, gi, first, count, p):
        koff = pl.multiple_of(first * TQ, TQ)
        return _dot(v_ref[gi, :, pl.ds(koff, count * TQ)], p)

    sel_index = lambda d: jnp.where(d < 0, 2, jnp.minimum(d, 2))
    win_index = lambda d: jnp.where(d < 0, 4, jnp.where(d == nwt, 3, jnp.minimum(d, 2)))

    groups = range(NSA_GROUPS)
    gate = lambda h, c: gt_ref[pl.ds(h * 3 + c, 1), :]
    head_lanes = lambda r: slice(r * TQ, (r + 1) * TQ)

    q3s = [jnp.concatenate([qt_ref[(gi * NSA_REP + r) * dh:(gi * NSA_REP + r + 1) * dh, :] for r in range(NSA_REP)],
                           axis=1) for gi in groups]
    cs = [_dot(kc_ref[gi], jnp.concatenate([q3s[gi], zpad], axis=0)) + cb_ref[gi, pl.ds(cstart, ncp), :]
          for gi in groups]
    cps = [jnp.exp2(s - jnp.max(s, axis=0, keepdims=True)) for s in cs]
    cps = [p * jnp.where(sees_any, 1.0 / jnp.sum(p, axis=0, keepdims=True), 0.0) for p in cps]
    ocs = [_dot(vct_ref[gi], cps[gi].astype(BF16)) for gi in groups]
    scs = []
    for gi in groups:
        psum = cps[gi][:, 0:TQ]
        for r in range(1, NSA_REP):
            psum = psum + cps[gi][:, head_lanes(r)]
        p1 = psum.astype(BF16)
        p2 = (psum - p1.astype(F32)).astype(BF16)
        imp = _dot(ovt_ref[...], p1) + _dot(ovt_ref[...], p2)
        score = jnp.where(forced, 1e4, jnp.where(jj <= cur, imp[0:nsb], -1.0))
        scs.append(jnp.where(score < 0.0, -1, lax.bitcast_convert_type(score, jnp.int32)))
    for gi in groups:
        sc_ref[gi] = scs[gi]
        for r in range(NSA_REP):
            h = gi * NSA_REP + r
            ot_ref[h * dh:(h + 1) * dh, :] = gate(h, 0) * ocs[gi][0:dh, head_lanes(r)]

    scs1 = [k + 1 for k in scs]

    def rank_step(i4, cnts):
        cnts = list(cnts)
        for u in range(RANK_UNROLL):
            i = i4 * RANK_UNROLL + u
            lower = i < jj
            for gi in groups:
                rowk = sc_ref[gi, pl.ds(i, 1), :]
                before = rowk >= jnp.where(lower, scs[gi], scs1[gi])
                cnts[gi] = cnts[gi] + jnp.where(before, 1, 0)
        return tuple(cnts)
    rank_trips = jnp.minimum((2 * n + 2 + RANK_UNROLL - 1) // RANK_UNROLL, nsb // RANK_UNROLL)
    cnts = lax.fori_loop(0, rank_trips, rank_step, tuple(jnp.zeros((nsb, TQ), jnp.int32) for _ in groups))
    for gi in groups:
        selneg = jnp.where((cnts[gi] < n_sel) & (jj <= cur), 0.0, NEG).astype(BF16)
        if nsb < SEL_BLOCK:
            selneg = jnp.concatenate([selneg, jnp.zeros((SEL_BLOCK - nsb, TQ), BF16)], axis=0)
        qa_ref[gi] = jnp.concatenate([q3s[gi], jnp.concatenate([selneg] * NSA_REP, axis=1)], axis=0)

    acc_ref[...] = jnp.zeros_like(acc_ref)
    half = NSA_GROUPS // 2
    s_refs = ((s0_ref, s1_ref), (s2_ref, s3_ref))

    def score_half(it, hb):
        plist = [scores(ks_ref, hb * half + u, it * SEL_TILES, SEL_TILES, sel_index, qa_ref[hb * half + u])
                 for u in range(half)]
        for u, parts in enumerate(plist):
            for i, x in enumerate(parts):
                s_refs[hb][u][i * TQ:(i + 1) * TQ, :] = x
        return [col_max(parts) for parts in plist]

    def finish_half(it, hb, ms, bms):
        m2s = [jnp.maximum(ms[u], bms[u]) for u in range(half)]
        ps = [probs([s_refs[hb][u][i * TQ:(i + 1) * TQ, :] for i in range(SEL_TILES)], m2s[u]) for u in range(half)]
        vals = [values(vs_ref, hb * half + u, it * SEL_TILES, SEL_TILES, ps[u]) for u in range(half)]
        for u in range(half):
            gi = hb * half + u
            acc_ref[gi] = jnp.exp2(ms[u] - m2s[u]) * acc_ref[gi] + vals[u]
        return m2s

    def sel_trip(it, carry, score_next):
        ms0, ms1, bm0 = carry
        bm1 = score_half(it, 1)
        ms0 = finish_half(it, 0, ms0, bm0)
        if score_next:
            bm0 = score_half(it + 1, 0)
        ms1 = finish_half(it, 1, ms1, bm1)
        return ms0, ms1, bm0

    neg = [jnp.full((1, G3), NEG, F32) for _ in range(half)]
    carry = (neg, neg, score_half(0, 0))

    wfirst = jnp.maximum(n - nwt, 0)
    wparts = [scores(kw_ref, gi, wfirst, nwt + 1, win_index, qa_ref[gi]) for gi in groups]
    wps = [probs(parts, col_max(parts)) for parts in wparts]
    ows = [values(vw_ref, gi, wfirst, nwt + 1, wps[gi]) for gi in groups]
    for gi in groups:
        o_w = ows[gi][0:dh, :] * (1.0 / ows[gi][dh:dh + 1, :])
        for r in range(NSA_REP):
            h = gi * NSA_REP + r
            ot_ref[h * dh:(h + 1) * dh, :] += gate(h, 2) * o_w[:, head_lanes(r)]

    carry = lax.fori_loop(0, n // SEL_TILES, lambda it, c: sel_trip(it, c, True), carry)
    sel_trip(n // SEL_TILES, carry, False)

    for gi in groups:
        o_s = acc_ref[gi, 0:dh, :] * (1.0 / acc_ref[gi, dh:dh + 1, :])
        for r in range(NSA_REP):
            h = gi * NSA_REP + r
            ot_ref[h * dh:(h + 1) * dh, :] += gate(h, 1) * o_s[:, head_lanes(r)]

    for j in range(NSA_QW // LANE):
        o_ref[:, j * LANE:(j + 1) * LANE] = ot_ref[j * LANE:(j + 1) * LANE, :].T.astype(BF16)


def _nsa_attn(qt, gt, kc, vct, ksel, kwin, vsel, vwin, ovt, tz, cb, batch):
    t = qt.shape[1]
    s = t // batch
    nt = s // TQ
    nsb = s // SEL_BLOCK
    ncp = kc.shape[-2]
    per_b = lambda a: pl.BlockSpec((None,) + a.shape[1:], lambda b, i: (b,) + (0,) * (a.ndim - 1))
    full = lambda a: pl.BlockSpec(a.shape, lambda b, i: (0,) * a.ndim)
    return pl.pallas_call(
        functools.partial(_nsa_attn_body, nsb=nsb, ncp=ncp, n_sel=min(SEL_TOPK, nsb)),
        grid=(batch, nt),
        in_specs=[pl.BlockSpec((NSA_QW, TQ), lambda b, i: (0, b * nt + i)),
                  pl.BlockSpec((LANE, TQ), lambda b, i: (0, b * nt + i)),
                  per_b(kc), per_b(vct), per_b(ksel), per_b(kwin), per_b(vsel), per_b(vwin),
                  full(ovt), full(tz), full(cb)],
        out_specs=pl.BlockSpec((TQ, NSA_QW), lambda b, i: (b * nt + i, 0)),
        out_shape=jax.ShapeDtypeStruct((t, NSA_QW), BF16),
        scratch_shapes=[pltpu.VMEM((NSA_GROUPS, nsb, TQ), jnp.int32), pltpu.VMEM((NSA_GROUPS, LANE, G3), BF16),
                        pltpu.VMEM((NSA_GROUPS, V_ROWS, G3), F32), pltpu.VMEM((NSA_QW, TQ), F32),
                        ] + [pltpu.VMEM((SEL_TILES * TQ, G3), F32)] * NSA_GROUPS,
        compiler_params=_cparams("parallel", "arbitrary"),
        name="nsa_attn",
    )(qt, gt, kc, vct, ksel, kwin, vsel, vwin, ovt, tz, cb)


def _rel_bucket_np(dist):
    dist = np.maximum(dist, 0)
    max_exact = REL_BUCKETS // 2
    ratio = np.log(np.maximum(dist, 1).astype(np.float32) / np.float32(max_exact)) / np.float32(
        math.log(REL_MAX_DIST / max_exact))
    large = max_exact + (ratio * np.float32(REL_BUCKETS - max_exact)).astype(np.int32)
    large = np.minimum(large, REL_BUCKETS - 1)
    return np.where(dist < max_exact, dist, large).astype(np.int32)


def _group_lanes(a):
    hh, r, c = a.shape
    return a.reshape(NSA_GROUPS, NSA_REP, r, c).transpose(0, 2, 1, 3).reshape(NSA_GROUPS, r, NSA_REP * c)


def _bias_tables(rel_bias, ncp):
    k = np.arange(TQ)[:, None]
    q = np.arange(TQ)[None, :]
    tbl = rel_bias.astype(F32)

    def lookup(idx):
        onehot = (jnp.asarray(idx.reshape(1, -1)) == jnp.arange(REL_BUCKETS)[:, None]).astype(F32)
        out = jnp.dot(tbl.T, onehot, precision=lax.Precision.HIGHEST)
        return out.reshape((NSA_HEADS,) + idx.shape)

    far = jnp.broadcast_to(tbl[REL_BUCKETS - 1][:, None, None], (NSA_HEADS, TQ, TQ))
    t0 = jnp.where(jnp.asarray(k <= q)[None], lookup(_rel_bucket_np(q - k)), NEG)
    t1 = lookup(_rel_bucket_np(TQ + q - k))
    t3 = jnp.where(jnp.asarray(k > q)[None], far, NEG)
    tz = jnp.stack([_group_lanes(x) for x in (t0, t1, far, t3, jnp.full_like(far, NEG))], axis=1)
    m = ncp - np.arange(2 * ncp)[:, None]
    d = CMP_STRIDE * m + q - (CMP_BLOCK - 1)
    idx = np.where((d >= 0) & (d < REL_MAX_DIST), _rel_bucket_np(d), REL_BUCKETS - 1)
    cb = _group_lanes(jnp.where(jnp.asarray(d >= 0)[None], lookup(idx), NEG))
    return tz * LOG2E, cb * LOG2E


def _overlap_table(s, ncp):
    nsb = s // SEL_BLOCK
    nc = (s - CMP_BLOCK) // CMP_STRIDE + 1
    cs = np.arange(ncp) * CMP_STRIDE
    ce = cs + CMP_BLOCK - 1
    ss = np.arange(SEL_BLOCK) * SEL_BLOCK
    ov = (cs[None, :] < ss[:, None] + SEL_BLOCK) & (ce[None, :] >= ss[:, None])
    ov &= (np.arange(ncp) < nc)[None, :] & (np.arange(SEL_BLOCK) < nsb)[:, None]
    return jnp.asarray(ov, dtype=BF16)


def kernel(x, mem, norm_mix, norm_mem, w_mem_kv, w_out, norm_ffn, w_up, conv_w, conv_b, w_down,
           gla_w_in, gla_w_gate_up, gla_b_gate, gla_out_norm, nsa_w_in, kv_norm, w_kv_shared,
           cmp_pos, cmp_w1, cmp_b1, cmp_w2, cmp_b2, rel_bias, final_norm):
    batch, seq = x.shape[0], x.shape[1]
    t = batch * seq
    h = x.reshape(t, D_MODEL)
    row = lambda v: v.reshape(1, -1).astype(F32)

    wk, wv = w_mem_kv[..., :MEM_W], w_mem_kv[..., MEM_W:]
    w_mkv = jnp.concatenate([_pad_heads(wk, MEM_HEADS, MEM_HEAD_DIM, MEM_DP),
                             _pad_heads(wv, MEM_HEADS, MEM_HEAD_DIM, MEM_DP)], axis=-1).astype(BF16)
    mem_kv_all = _mem_kv_proj(mem.reshape(-1, D_MODEL), norm_mem.reshape(DEPTH, 1, D_MODEL), w_mkv, batch)

    shared = None
    for i in range(DEPTH):
        w_o = w_out[i]
        w_o_mem = _pad_head_rows(w_o[MAIN_W:], MEM_HEADS, MEM_HEAD_DIM, MEM_DP).astype(BF16)
        if i < N_A_LAYERS:
            wi = gla_w_in[i]
            c0 = GLA_HEADS * GLA_DK
            c1 = 2 * c0
            c2 = c1 + GLA_HEADS * GLA_DV
            c3 = c2 + GLA_HEADS * GLA_DV
            c4 = c3 + GLA_RANK
            w_all = jnp.concatenate([
                _pad_heads(wi[:, :c0], GLA_HEADS, GLA_DK, GLA_DKP),
                _pad_heads(wi[:, c0:c1], GLA_HEADS, GLA_DK, GLA_DKP),
                _pad_heads(wi[:, c1:c2], GLA_HEADS, GLA_DV, GLA_DVP),
                _pad_heads(wi[:, c2:c3], GLA_HEADS, GLA_DV, GLA_DVP),
                _pad_heads(wi[:, c3:c4], 1, GLA_RANK, LANE),
                _pad_heads(wi[:, c4:], MEM_HEADS, MEM_HEAD_DIM, MEM_DP)], axis=1).astype(BF16)
            wg = jnp.pad(_pad_heads(gla_w_gate_up[i], GLA_HEADS, GLA_DK, GLA_DKP),
                         ((0, LANE - GLA_RANK), (0, 0))).astype(BF16)
            bg = row(_pad_heads(gla_b_gate[i], GLA_HEADS, GLA_DK, GLA_DKP))
            q, k, v, go, la, mo = _gla_in_proj(h, row(norm_mix[i]), w_all, wg, bg, mem_kv_all[i], batch)
            on = row(jnp.pad(gla_out_norm[i], (0, GLA_DVP - GLA_DV)))
            main = _gla_mixer(q, k, la, v, go, on, batch)
            w_o_main = _pad_head_rows(w_o[:MAIN_W], GLA_HEADS, GLA_DV, GLA_DVP).astype(BF16)
        else:
            if shared is None:
                ncp = seq // CMP_STRIDE
                gr, dh = NSA_GROUPS, NSA_HEAD_DIM
                wkv = w_kv_shared.reshape(D_MODEL, 6, gr, dh)
                pair = lambda a, b: jnp.concatenate([wkv[:, a], wkv[:, b]], axis=-1).reshape(D_MODEL, gr * 2 * dh)
                slot = lambda a: _pad_heads(wkv[:, a].reshape(D_MODEL, KV_NAT), gr, dh, LANE)
                w_kv = jnp.concatenate([wkv[:, 0].reshape(D_MODEL, KV_NAT), wkv[:, 1].reshape(D_MODEL, KV_NAT),
                                        slot(2), slot(4), pair(3, 5)], axis=1).astype(BF16)
                ck, cv, ksel, kwin, vsel, vwin = _kv_proj(h, row(kv_norm), w_kv, batch)
                x16 = jnp.stack([ck.reshape(batch * ncp, CMP_STRIDE * KV_NAT),
                                 cv.reshape(batch * ncp, CMP_STRIDE * KV_NAT)])
                w1 = cmp_w1.reshape(2, 2, CMP_STRIDE, dh, CMP_HIDDEN)
                eye = jnp.eye(gr, dtype=F32)
                w1x = jnp.einsum('jhldc,gk->jhlgdkc', w1, eye).reshape(2, 2, CMP_STRIDE * KV_NAT, gr * CMP_HIDDEN)
                w1x = w1x.astype(BF16)
                pos8 = jnp.broadcast_to(cmp_pos.reshape(2, 1, CMP_BLOCK * dh), (2, SUBLANE, CMP_BLOCK * dh)).astype(BF16)
                w2p = jnp.pad(cmp_w2, ((0, 0), (0, 0), (0, LANE - dh))).astype(BF16)
                b2p = jnp.pad(cmp_b2, ((0, 0), (0, LANE - dh))).reshape(2, 1, LANE).astype(F32)
                cnat, ctr = _compress(x16, w1x[:, 0], w1x[:, 1], pos8, cmp_w1.astype(BF16),
                                      cmp_b1.reshape(2, 1, CMP_HIDDEN).astype(F32), w2p, b2p, batch)
                tz, cb = _bias_tables(rel_bias, ncp)
                ov = _overlap_table(seq, ncp)
                shared = (cnat[0], ctr[1], ksel, kwin, vsel, vwin, ov, tz, cb)
            wi = nsa_w_in[i - N_A_LAYERS]
            c0 = NSA_HEADS * NSA_HEAD_DIM
            c1 = c0 + NSA_HEADS * 3
            w_all = jnp.concatenate([
                wi[:, :c0],
                _pad_heads(wi[:, c0:c1], 1, NSA_HEADS * 3, LANE),
                _pad_heads(wi[:, c1:], MEM_HEADS, MEM_HEAD_DIM, MEM_DP)], axis=1).astype(BF16)
            q, gates, mo = _nsa_in_proj(h, row(norm_mix[i]), w_all, mem_kv_all[i], batch)
            main = _nsa_attn(q, gates, *shared, batch)
            w_o_main = w_o[:MAIN_W].astype(BF16)
        h = _mix_ffn(h, main, mo, w_o_main, w_o_mem, row(norm_ffn[i]), w_up[i].astype(BF16), conv_w[i].astype(F32),
                     row(conv_b[i]), w_down[i].astype(BF16), row(final_norm), i == DEPTH - 1, batch)
    return h.reshape(batch, seq, D_MODEL)
```

```python
import functools
import math

import numpy as np
import jax
import jax.numpy as jnp
from jax import lax
from jax.experimental import pallas as pl
from jax.experimental.pallas import tpu as pltpu

F32 = jnp.float32
BF16 = jnp.bfloat16

D_MODEL = 1024
DEPTH = 4
N_A_LAYERS = DEPTH // 2
MEM_HEADS = 4
MEM_HEAD_DIM = 64
MEM_W = MEM_HEADS * MEM_HEAD_DIM
MAIN_W = D_MODEL - MEM_W
GLA_HEADS = 4
GLA_DV = MAIN_W // GLA_HEADS
GLA_DK = GLA_DV // 2
GLA_RANK = 16
GLA_GATE_NORM = 16.0
NSA_HEADS = 12
NSA_GROUPS = 4
NSA_HEAD_DIM = MAIN_W // NSA_HEADS
NSA_REP = NSA_HEADS // NSA_GROUPS
CMP_BLOCK = 32
CMP_STRIDE = 16
CMP_HIDDEN = 128
SEL_BLOCK = 64
SEL_TOPK = 16
WINDOW = 512
REL_BUCKETS = 32
REL_MAX_DIST = 128
FFN_DIM = 2816
CONV_WIDTH = 3
EPS = 1e-6

LANE = 128
SUBLANE = 8
VMEM_LIMIT = 56 * 1024 * 1024
GLA_DKP = LANE
GLA_DVP = 2 * LANE
MEM_DP = LANE
NEG = -1e30
TQ = 128
ROW_TILE = 1024
GLA_CHUNK = 64
FFN_TILE = 256
FFN_ROWS = 1024
SEL_TILES = 4
NSA_SEQS = 2
V_ROWS = 64 + 16
RANK_UNROLL = 4
LOG2E = math.log2(math.e)

NT = (((1,), (1,)), ((), ()))
TN = (((0,), (0,)), ((), ()))


def _cparams(*sem):
    return pltpu.CompilerParams(dimension_semantics=sem, vmem_limit_bytes=VMEM_LIMIT)


def _rms(x, g):
    return x * lax.rsqrt(jnp.mean(x * x, axis=-1, keepdims=True) + EPS) * g


def _sigmoid(x):
    return 1.0 / (1.0 + jnp.exp(-x))


def _dot(a, b):
    return jnp.dot(a, b, preferred_element_type=F32)


def _resident(a):
    return pl.BlockSpec(a.shape, lambda *_: (0,) * a.ndim, pipeline_mode=pl.Buffered(1))


def _pad_heads(w, nh, d, dp):
    lead = w.shape[:-1]
    w = w.reshape(lead + (nh, d))
    w = jnp.pad(w, [(0, 0)] * len(lead) + [(0, 0), (0, dp - d)])
    return w.reshape(lead + (nh * dp,))


def _pad_head_rows(w, nh, d, dp):
    n = w.shape[-1]
    w = jnp.pad(w.reshape(nh, d, n), ((0, 0), (0, dp - d), (0, 0)))
    return w.reshape(nh * dp, n)


def _mem_attention(q, kv_ref):
    outs = []
    for hd in range(MEM_HEADS):
        sl = slice(hd * MEM_DP, (hd + 1) * MEM_DP)
        sv = slice(MEM_QW + hd * MEM_DP, MEM_QW + (hd + 1) * MEM_DP)
        s = lax.dot_general(q[:, sl], kv_ref[:, sl], NT, preferred_element_type=F32) * MEM_HEAD_DIM ** -0.5
        p = jnp.exp(s - jnp.max(s, axis=-1, keepdims=True))
        l = jnp.sum(p, axis=-1, keepdims=True)
        outs.append((_dot(p.astype(BF16), kv_ref[:, sv]) / l).astype(BF16))
    return jnp.concatenate(outs, axis=1)


GLA_QW = GLA_HEADS * GLA_DKP
GLA_VW = GLA_HEADS * GLA_DVP
MEM_QW = MEM_HEADS * MEM_DP
GLA_OFF_K = GLA_QW
GLA_OFF_V = 2 * GLA_QW
GLA_OFF_G = GLA_OFF_V + GLA_VW
GLA_OFF_LR = GLA_OFF_G + GLA_VW
GLA_OFF_MQ = GLA_OFF_LR + LANE
GLA_NP = GLA_OFF_MQ + MEM_QW


def _gla_in_body(h_ref, g_ref, w_ref, wg_ref, bg_ref, kv_ref, q_ref, k_ref, v_ref, go_ref, la_ref, mo_ref):
    xn = _rms(h_ref[...], g_ref[...]).astype(BF16)

    def proj(lo, n):
        return _dot(xn, w_ref[:, lo:lo + n])

    q_ref[...] = proj(0, GLA_QW).astype(BF16)
    k_ref[...] = proj(GLA_OFF_K, GLA_QW).astype(BF16)
    for j in range(GLA_VW // GLA_QW):
        v_ref[:, j * GLA_QW:(j + 1) * GLA_QW] = proj(GLA_OFF_V + j * GLA_QW, GLA_QW).astype(BF16)
        go_ref[:, j * GLA_QW:(j + 1) * GLA_QW] = proj(GLA_OFF_G + j * GLA_QW, GLA_QW).astype(BF16)
    lr = proj(GLA_OFF_LR, LANE).astype(BF16)
    z = _dot(lr, wg_ref[...]) + bg_ref[...]
    la_ref[...] = (jnp.minimum(z, 0.0) - jnp.log(1.0 + jnp.exp(-jnp.abs(z)))) * (1.0 / GLA_GATE_NORM)
    mo_ref[...] = _mem_attention(proj(GLA_OFF_MQ, MEM_QW).astype(BF16), kv_ref)


def _mem_kv_spec(mem_kv, t, tm, batch):
    nt = t // batch // tm
    return pl.BlockSpec((mem_kv.shape[0] // batch, mem_kv.shape[1]), lambda i: (i // nt, 0))


def _gla_in_proj(h, g, w, wg, bg, mem_kv, batch):
    t = h.shape[0]
    tm = ROW_TILE
    row = lambda n: pl.BlockSpec((tm, n), lambda i: (i, 0))
    return pl.pallas_call(
        _gla_in_body,
        grid=(t // tm,),
        in_specs=[row(D_MODEL), _resident(g), _resident(w), _resident(wg), _resident(bg),
                  _mem_kv_spec(mem_kv, t, tm, batch)],
        out_specs=[row(GLA_QW), row(GLA_QW), row(GLA_VW), row(GLA_VW), row(GLA_QW), row(MEM_QW)],
        out_shape=[jax.ShapeDtypeStruct((t, GLA_QW), BF16), jax.ShapeDtypeStruct((t, GLA_QW), BF16),
                   jax.ShapeDtypeStruct((t, GLA_VW), BF16), jax.ShapeDtypeStruct((t, GLA_VW), BF16),
                   jax.ShapeDtypeStruct((t, GLA_QW), F32), jax.ShapeDtypeStruct((t, MEM_QW), BF16)],
        compiler_params=_cparams("parallel"),
        name="gla_in_proj",
    )(h, g, w, wg, bg, mem_kv)


def _gla_mix_body(q_ref, k_ref, la_ref, v_ref, go_ref, on_ref, o_ref, st_ref, *, cn, batch):
    @pl.when(pl.program_id(0) == 0)
    def _():
        st_ref[...] = jnp.zeros_like(st_ref)

    row = lax.broadcasted_iota(jnp.int32, (cn, cn), 0)
    col = lax.broadcasted_iota(jnp.int32, (cn, cn), 1)
    causal = row >= col
    tril = jnp.where(causal, 1.0, 0.0).astype(BF16)
    scale = GLA_DK ** -0.5
    chains = [(bi, hd) for bi in range(batch) for hd in range(GLA_HEADS)]
    sk = lambda hd: slice(hd * GLA_DKP, (hd + 1) * GLA_DKP)
    sv = lambda hd: slice(hd * GLA_DVP, (hd + 1) * GLA_DVP)
    dg = lambda x, y, dims: lax.dot_general(x, y, dims, preferred_element_type=F32)

    def cum_decay(bi, hd):
        la = la_ref[bi, :, sk(hd)]
        la1 = la.astype(BF16)
        r1 = la - la1.astype(F32)
        la2 = r1.astype(BF16)
        la3 = (r1 - la2.astype(F32)).astype(BF16)
        return _dot(tril, la1) + _dot(tril, la2) + _dot(tril, la3)

    bs = [cum_decay(bi, hd) for bi, hd in chains]
    scaled = []
    for (bi, hd), b in zip(chains, bs):
        b_last = b[cn - 1:cn, :]
        q = q_ref[bi, :, sk(hd)].astype(F32)
        k = k_ref[bi, :, sk(hd)].astype(F32)
        scaled.append(((q * jnp.exp(b) * scale).astype(BF16), (k * jnp.exp(-b)).astype(BF16),
                       (k * jnp.exp(b_last - b)).astype(BF16), jnp.exp(b_last)))
    intra = [jnp.where(causal, dg(qs, ks, NT), 0.0).astype(BF16) for qs, ks, _, _ in scaled]
    outs = []
    for i, (bi, hd) in enumerate(chains):
        qs, _, ko, a_last = scaled[i]
        v = v_ref[bi, :, sv(hd)]
        st = st_ref[i]
        outs.append(_dot(intra[i], v) + dg(qs, st.astype(BF16), NT))
        st_ref[i] = st * a_last + dg(v, ko, TN)
    for (bi, hd), o in zip(chains, outs):
        ms = jnp.sum(o * o, axis=-1, keepdims=True) * (1.0 / GLA_DV)
        y = o * lax.rsqrt(ms + EPS) * on_ref[...]
        g = go_ref[bi, :, sv(hd)].astype(F32)
        o_ref[bi, :, sv(hd)] = (y * (g * _sigmoid(g))).astype(BF16)


def _gla_mixer(q, k, la, v, go, on, batch):
    t = q.shape[0]
    s = t // batch
    cn = GLA_CHUNK
    seq = lambda a: a.reshape(batch, s, a.shape[1])
    blk = lambda n: pl.BlockSpec((batch, cn, n), lambda c: (0, c, 0))
    out = pl.pallas_call(
        functools.partial(_gla_mix_body, cn=cn, batch=batch),
        grid=(s // cn,),
        in_specs=[blk(GLA_QW), blk(GLA_QW), blk(GLA_QW), blk(GLA_VW), blk(GLA_VW),
                  pl.BlockSpec(on.shape, lambda c: (0, 0))],
        out_specs=blk(GLA_VW),
        out_shape=jax.ShapeDtypeStruct((batch, s, GLA_VW), BF16),
        scratch_shapes=[pltpu.VMEM((batch * GLA_HEADS, GLA_DVP, GLA_DKP), F32)],
        compiler_params=_cparams("arbitrary"),
        name="gla_mixer",
    )(seq(q), seq(k), seq(la), seq(v), seq(go), on)
    return out.reshape(t, GLA_VW)


def _mem_kv_body(m_ref, g_ref, w_ref, o_ref):
    xn = _rms(m_ref[...], g_ref[...]).astype(BF16)
    o_ref[...] = _dot(xn, w_ref[...]).astype(BF16)


def _mem_kv_proj(mem2, g, w, batch):
    nl = w.shape[0]
    m = mem2.shape[0] // batch
    n = w.shape[2]
    return pl.pallas_call(
        _mem_kv_body,
        grid=(nl, batch),
        in_specs=[pl.BlockSpec((m, D_MODEL), lambda l, b: (b, 0)),
                  pl.BlockSpec((None, 1, D_MODEL), lambda l, b: (l, 0, 0)),
                  pl.BlockSpec((None, D_MODEL, n), lambda l, b: (l, 0, 0))],
        out_specs=pl.BlockSpec((None, m, n), lambda l, b: (l, b, 0)),
        out_shape=jax.ShapeDtypeStruct((nl, mem2.shape[0], n), BF16),
        compiler_params=_cparams("arbitrary", "arbitrary"),
        name="mem_kv_proj",
    )(mem2, g, w)


BF16_ROWS = 2 * SUBLANE


def _mix_ffn_body(h_ref, hp_ref, a_ref, ap_ref, m_ref, mp_ref, wa_ref, wm_ref, g_ref, wup_ref, cw_ref, cb_ref,
                  wdn_ref, gf_ref, o_ref, act_ref, *, tm, tf, last):
    g = g_ref[...]
    nr = tm // SUBLANE
    interleave = lambda x: x.reshape(SUBLANE, nr, x.shape[-1]).swapaxes(0, 1).reshape(tm, x.shape[-1])
    deinterleave = lambda x: x.reshape(nr, SUBLANE, x.shape[-1]).swapaxes(0, 1).reshape(tm, x.shape[-1])
    h = interleave(h_ref[...] + _dot(a_ref[...], wa_ref[...]) + _dot(m_ref[...], wm_ref[...]))
    x = _rms(h, g).astype(BF16)
    keep = jnp.where(pl.program_id(1) > 0, 1.0, 0.0)
    h_prev = (hp_ref[...] + _dot(ap_ref[...], wa_ref[...]) + _dot(mp_ref[...], wm_ref[...]))[BF16_ROWS - SUBLANE:]
    x_prev = (_rms(h_prev, g) * keep).astype(BF16)
    first = lax.broadcasted_iota(jnp.int32, (SUBLANE, tf), 0) == 0
    nchunk = FFN_DIM // tf

    def up(j):
        cols = [pl.ds(off + j * tf, tf) for off in (0, FFN_DIM)]
        return tuple((_dot(x, wup_ref[:, c]), _dot(x_prev, wup_ref[:, c])) for c in cols)

    def conv(u, u_prev, off):
        w = cw_ref[:, off:off + tf]
        wrap = lambda r, k: jnp.where(first, u_prev[SUBLANE - k:SUBLANE - k + 1, :],
                                      pltpu.roll(u[r * SUBLANE:(r + 1) * SUBLANE, :], 1, 0))
        back1 = jnp.concatenate([wrap(nr - 1, 1), u[0:tm - SUBLANE, :]], axis=0)
        back2 = jnp.concatenate([wrap(nr - 2, 2), wrap(nr - 1, 1), u[0:tm - 2 * SUBLANE, :]], axis=0)
        return cb_ref[:, off:off + tf] + w[0:1, :] * back2 + w[1:2, :] * back1 + w[2:3, :] * u

    u_next = up(0)
    for j in range(nchunk):
        (ua, ua_prev), (ub, ub_prev) = u_next
        if j + 1 < nchunk:
            u_next = up(j + 1)
        a = conv(ua, ua_prev, j * tf)
        b = conv(ub, ub_prev, FFN_DIM + j * tf)
        act_ref[:, j * tf:(j + 1) * tf] = (a * _sigmoid(a) * b).astype(BF16)
    out = h + _dot(act_ref[...], wdn_ref[...])
    o_ref[...] = deinterleave(_rms(out, gf_ref[...]) if last else out)


def _mix_ffn(h, main, mo, wa, wm, g, wup, cw, cb, wdn, gf, last, batch):
    t = h.shape[0]
    tm = FFN_ROWS
    nt = t // batch // tm
    hb = tm // BF16_ROWS
    cur = lambda n: pl.BlockSpec((tm, n), lambda b, i: (b * nt + i, 0))
    prev = lambda n: pl.BlockSpec((BF16_ROWS, n), lambda b, i: (jnp.maximum((b * nt + i) * hb - 1, 0), 0))
    return pl.pallas_call(
        functools.partial(_mix_ffn_body, tm=tm, tf=FFN_TILE, last=last),
        grid=(batch, nt),
        in_specs=[cur(D_MODEL), prev(D_MODEL), cur(main.shape[1]), prev(main.shape[1]),
                  cur(mo.shape[1]), prev(mo.shape[1]), _resident(wa), _resident(wm), _resident(g),
                  _resident(wup), _resident(cw), _resident(cb), _resident(wdn), _resident(gf)],
        out_specs=cur(D_MODEL),
        out_shape=jax.ShapeDtypeStruct((t, D_MODEL), F32),
        scratch_shapes=[pltpu.VMEM((tm, FFN_DIM), BF16)],
        compiler_params=_cparams("parallel", "parallel"),
        name="mix_ffn",
    )(h, h, main, main, mo, mo, wa, wm, g, wup, cw, cb, wdn, gf)


KV_NAT = NSA_GROUPS * NSA_HEAD_DIM


def _kv_proj_body(h_ref, g_ref, w_ref, ck_ref, cv_ref, ks_ref, kw_ref, vs_ref, vw_ref, *, tm):
    xn = _rms(h_ref[...], g_ref[...]).astype(BF16)
    ck_ref[...] = _dot(xn, w_ref[:, 0:KV_NAT]).astype(BF16)
    cv_ref[...] = _dot(xn, w_ref[:, KV_NAT:2 * KV_NAT]).astype(BF16)
    key = pl.program_id(1) * tm + lax.broadcasted_iota(jnp.int32, (tm, LANE), 0)
    lane = lax.broadcasted_iota(jnp.int32, (tm, LANE), 1)
    onehot = jnp.where(lax.shift_right_logical(key, int(math.log2(SEL_BLOCK))) == lane - NSA_HEAD_DIM, 1.0, 0.0)
    low = lane < NSA_HEAD_DIM
    for gi in range(NSA_GROUPS):
        slot = lambda n: _dot(xn, w_ref[:, 2 * KV_NAT + (n * NSA_GROUPS + gi) * LANE:
                                           2 * KV_NAT + (n * NSA_GROUPS + gi + 1) * LANE])
        ks_ref[gi] = jnp.where(low, slot(0), onehot).astype(BF16)
        kw_ref[gi] = jnp.where(low, slot(1), 0.0).astype(BF16)
        vt = slot(2).T.astype(BF16)
        ones = jnp.ones((V_ROWS - NSA_HEAD_DIM, tm), BF16)
        vs_ref[gi] = jnp.concatenate([vt[0:NSA_HEAD_DIM], ones], axis=0)
        vw_ref[gi] = jnp.concatenate([vt[NSA_HEAD_DIM:], ones], axis=0)


def _kv_proj(h, g, w, batch):
    t = h.shape[0]
    s = t // batch
    tm = ROW_TILE
    nt = s // tm
    gr = NSA_GROUPS
    return pl.pallas_call(
        functools.partial(_kv_proj_body, tm=tm),
        grid=(batch, nt),
        in_specs=[pl.BlockSpec((tm, D_MODEL), lambda b, i: (b * nt + i, 0)),
                  pl.BlockSpec(g.shape, lambda b, i: (0, 0)),
                  pl.BlockSpec(w.shape, lambda b, i: (0, 0))],
        out_specs=[pl.BlockSpec((tm, KV_NAT), lambda b, i: (b * nt + i, 0)),
                   pl.BlockSpec((tm, KV_NAT), lambda b, i: (b * nt + i, 0)),
                   pl.BlockSpec((None, gr, tm, LANE), lambda b, i: (b, 0, i, 0)),
                   pl.BlockSpec((None, gr, tm, LANE), lambda b, i: (b, 0, i, 0)),
                   pl.BlockSpec((None, gr, V_ROWS, tm), lambda b, i: (b, 0, 0, i)),
                   pl.BlockSpec((None, gr, V_ROWS, tm), lambda b, i: (b, 0, 0, i))],
        out_shape=[jax.ShapeDtypeStruct((t, KV_NAT), BF16), jax.ShapeDtypeStruct((t, KV_NAT), BF16),
                   jax.ShapeDtypeStruct((batch, gr, s, LANE), BF16),
                   jax.ShapeDtypeStruct((batch, gr, s, LANE), BF16),
                   jax.ShapeDtypeStruct((batch, gr, V_ROWS, s), BF16),
                   jax.ShapeDtypeStruct((batch, gr, V_ROWS, s), BF16)],
        compiler_params=_cparams("parallel", "parallel"),
        name="nsa_kv_proj",
    )(h, g, w)


def _compress_body(x_ref, wt_ref, wb_ref, pos_ref, w1_ref, b1_ref, w2_ref, b2_ref, on_ref, ot_ref, *, ncp):
    x = x_ref[...]
    top = _dot(x, wt_ref[...])
    bot = _dot(x, wb_ref[...])
    posb = _dot(pos_ref[...], w1_ref[...])[0:1, :] + b1_ref[...]
    for gi in range(NSA_GROUPS):
        sl = slice(gi * CMP_HIDDEN, (gi + 1) * CMP_HIDDEN)
        hid = top[:, sl] + pltpu.roll(bot[:, sl], ncp - 1, 0) + posb
        hid = (hid * _sigmoid(hid)).astype(BF16)
        out = _dot(hid, w2_ref[...]) + b2_ref[...]
        on_ref[gi] = out.astype(BF16)
        ot_ref[gi] = out.T.astype(BF16)


def _compress(x16, wt, wb, pos, w1, b1, w2, b2, batch):
    ncp = x16.shape[1] // batch
    gr = NSA_GROUPS
    per_j = lambda a: pl.BlockSpec((None,) + a.shape[1:], lambda j, b: (j,) + (0,) * (a.ndim - 1))
    return pl.pallas_call(
        functools.partial(_compress_body, ncp=ncp),
        grid=(2, batch),
        in_specs=[pl.BlockSpec((None, ncp, x16.shape[2]), lambda j, b: (j, b, 0)),
                  per_j(wt), per_j(wb), per_j(pos), per_j(w1), per_j(b1), per_j(w2), per_j(b2)],
        out_specs=[pl.BlockSpec((None, None, gr, ncp, LANE), lambda j, b: (j, b, 0, 0, 0)),
                   pl.BlockSpec((None, None, gr, LANE, ncp), lambda j, b: (j, b, 0, 0, 0))],
        out_shape=[jax.ShapeDtypeStruct((2, batch, gr, ncp, LANE), BF16),
                   jax.ShapeDtypeStruct((2, batch, gr, LANE, ncp), BF16)],
        compiler_params=_cparams("arbitrary", "arbitrary"),
        name="nsa_compress",
    )(x16, wt, wb, pos, w1, b1, w2, b2)


NSA_QW = NSA_HEADS * NSA_HEAD_DIM
G3 = NSA_REP * TQ


def _nsa_in_body(h_ref, g_ref, w_ref, kv_ref, qt_ref, gt_ref, mo_ref):
    xn = _rms(h_ref[...], g_ref[...]).astype(BF16)
    for j in range(NSA_QW // LANE):
        y = _dot(xn, w_ref[:, j * LANE:(j + 1) * LANE]) * (NSA_HEAD_DIM ** -0.5 * LOG2E)
        qt_ref[j * LANE:(j + 1) * LANE, :] = y.T.astype(BF16)
    gt_ref[...] = _sigmoid(_dot(xn, w_ref[:, NSA_QW:NSA_QW + LANE])).T
    mo_ref[...] = _mem_attention(_dot(xn, w_ref[:, NSA_QW + LANE:]).astype(BF16), kv_ref)


def _nsa_in_proj(h, g, w, mem_kv, batch):
    t = h.shape[0]
    tm = ROW_TILE
    row = lambda n: pl.BlockSpec((tm, n), lambda i: (i, 0))
    nt = t // batch // tm
    col = lambda n: pl.BlockSpec((None, n, tm), lambda i: (i // nt, 0, i % nt))
    return pl.pallas_call(
        _nsa_in_body,
        grid=(t // tm,),
        in_specs=[row(D_MODEL), _resident(g), _resident(w), _mem_kv_spec(mem_kv, t, tm, batch)],
        out_specs=[col(NSA_QW), col(LANE), row(MEM_QW)],
        out_shape=[jax.ShapeDtypeStruct((batch, NSA_QW, t // batch), BF16),
                   jax.ShapeDtypeStruct((batch, LANE, t // batch), F32),
                   jax.ShapeDtypeStruct((t, MEM_QW), BF16)],
        compiler_params=_cparams("parallel"),
        name="nsa_in_proj",
    )(h, g, w, mem_kv)


def _nsa_attn_body(qt_ref, gt_ref, kc_ref, vct_ref, ks_ref, kw_ref, vs_ref, vw_ref, ovt_ref, tz_ref, cb_ref,
                   o_ref, sc_ref, qa_ref, acc_ref, ot_ref, *s_slots, nsb, ncp, n_sel, nseq):
    n = pl.program_id(1)
    t0 = n * TQ
    dh = NSA_HEAD_DIM
    cstart = pl.multiple_of(ncp - n * (TQ // CMP_STRIDE), SUBLANE)
    sees_any = t0 + (lax.broadcasted_iota(jnp.int32, (1, G3), 1) & (TQ - 1)) >= CMP_BLOCK - 1
    jj = lax.broadcasted_iota(jnp.int32, (nsb, TQ), 0)
    cur = lax.shift_right_logical(t0 + lax.broadcasted_iota(jnp.int32, (nsb, TQ), 1), int(math.log2(SEL_BLOCK)))
    forced = (jj == 0) | (jj == cur) | (jj == cur - 1)
    zpad = jnp.zeros((dh, G3), BF16)
    nwt = WINDOW // TQ
    units = [(bi, gi) for bi in range(nseq) for gi in range(NSA_GROUPS)]

    def scores(k_ref, u, first, count, tz_index, qa):
        bi, gi = units[u]
        koff = pl.multiple_of(first * TQ, TQ)
        s = _dot(k_ref[bi, gi, pl.ds(koff, count * TQ), :], qa)
        return [s[i * TQ:(i + 1) * TQ] + tz_ref[gi, tz_index(n - (first + i))] for i in range(count)]

    def col_max(parts):
        mx = parts[0]
        for x in parts[1:]:
            mx = jnp.maximum(mx, x)
        return jnp.max(mx, axis=0, keepdims=True)

    def probs(parts, m):
        return jnp.concatenate([jnp.exp2(x - m).astype(BF16) for x in parts], axis=0)

    def values(v_ref, u, first, count, p):
        bi, gi = units[u]
        koff = pl.multiple_of(first * TQ, TQ)
        return _dot(v_ref[bi, gi, :, pl.ds(koff, count * TQ)], p)

    sel_index = lambda d: jnp.where(d < 0, 2, jnp.minimum(d, 2))
    win_index = lambda d: jnp.where(d < 0, 4, jnp.where(d == nwt, 3, jnp.minimum(d, 2)))

    groups = range(len(units))
    gate = lambda bi, h, c: gt_ref[bi, pl.ds(h * 3 + c, 1), :]
    head_lanes = lambda r: slice(r * TQ, (r + 1) * TQ)
    head_rows = lambda bi, h: (bi, slice(h * dh, (h + 1) * dh))

    q3s = [jnp.concatenate([qt_ref[head_rows(bi, gi * NSA_REP + r)] for r in range(NSA_REP)], axis=1)
           for bi, gi in units]
    cs = [_dot(kc_ref[bi, gi], jnp.concatenate([q3s[u], zpad], axis=0)) + cb_ref[gi, pl.ds(cstart, ncp), :]
          for u, (bi, gi) in enumerate(units)]
    cps = [jnp.exp2(s - jnp.max(s, axis=0, keepdims=True)) for s in cs]
    cps = [p * jnp.where(sees_any, 1.0 / jnp.sum(p, axis=0, keepdims=True), 0.0) for p in cps]
    ocs = [_dot(vct_ref[bi, gi], cps[u].astype(BF16)) for u, (bi, gi) in enumerate(units)]
    scs = []
    for gi in groups:
        psum = cps[gi][:, 0:TQ]
        for r in range(1, NSA_REP):
            psum = psum + cps[gi][:, head_lanes(r)]
        p1 = psum.astype(BF16)
        p2 = (psum - p1.astype(F32)).astype(BF16)
        imp = _dot(ovt_ref[...], p1) + _dot(ovt_ref[...], p2)
        score = jnp.where(forced, 1e4, jnp.where(jj <= cur, imp[0:nsb], -1.0))
        scs.append(jnp.where(score < 0.0, -1, lax.bitcast_convert_type(score, jnp.int32)))
    for u, (bi, gi) in enumerate(units):
        sc_ref[u] = scs[u]
        for r in range(NSA_REP):
            h = gi * NSA_REP + r
            ot_ref[head_rows(bi, h)] = gate(bi, h, 0) * ocs[u][0:dh, head_lanes(r)]

    scs1 = [k + 1 for k in scs]

    def rank_step(i4, cnts):
        cnts = list(cnts)
        for u in range(RANK_UNROLL):
            i = i4 * RANK_UNROLL + u
            lower = i < jj
            for gi in groups:
                rowk = sc_ref[gi, pl.ds(i, 1), :]
                before = rowk >= jnp.where(lower, scs[gi], scs1[gi])
                cnts[gi] = cnts[gi] + jnp.where(before, 1, 0)
        return tuple(cnts)
    rank_trips = jnp.minimum((2 * n + 2 + RANK_UNROLL - 1) // RANK_UNROLL, nsb // RANK_UNROLL)
    cnts = lax.fori_loop(0, rank_trips, rank_step, tuple(jnp.zeros((nsb, TQ), jnp.int32) for _ in groups))
    for gi in groups:
        selneg = jnp.where((cnts[gi] < n_sel) & (jj <= cur), 0.0, NEG).astype(BF16)
        if nsb < SEL_BLOCK:
            selneg = jnp.concatenate([selneg, jnp.zeros((SEL_BLOCK - nsb, TQ), BF16)], axis=0)
        qa_ref[gi] = jnp.concatenate([q3s[gi], jnp.concatenate([selneg] * NSA_REP, axis=1)], axis=0)

    acc_ref[...] = jnp.zeros_like(acc_ref)
    half = len(units) // 2
    s_refs = (s_slots[:half], s_slots[half:])

    def score_half(it, hb):
        plist = [scores(ks_ref, hb * half + u, it * SEL_TILES, SEL_TILES, sel_index, qa_ref[hb * half + u])
                 for u in range(half)]
        for u, parts in enumerate(plist):
            for i, x in enumerate(parts):
                s_refs[hb][u][i * TQ:(i + 1) * TQ, :] = x
        return [col_max(parts) for parts in plist]

    def finish_half(it, hb, ms, bms):
        m2s = [jnp.maximum(ms[u], bms[u]) for u in range(half)]
        ps = [probs([s_refs[hb][u][i * TQ:(i + 1) * TQ, :] for i in range(SEL_TILES)], m2s[u]) for u in range(half)]
        vals = [values(vs_ref, hb * half + u, it * SEL_TILES, SEL_TILES, ps[u]) for u in range(half)]
        for u in range(half):
            gi = hb * half + u
            acc_ref[gi] = jnp.exp2(ms[u] - m2s[u]) * acc_ref[gi] + vals[u]
        return m2s

    def sel_trip(it, carry, score_next):
        ms0, ms1, bm0 = carry
        bm1 = score_half(it, 1)
        ms0 = finish_half(it, 0, ms0, bm0)
        if score_next:
            bm0 = score_half(it + 1, 0)
        ms1 = finish_half(it, 1, ms1, bm1)
        return ms0, ms1, bm0

    neg = [jnp.full((1, G3), NEG, F32) for _ in range(half)]
    carry = (neg, neg, score_half(0, 0))

    wfirst = jnp.maximum(n - nwt, 0)
    wparts = [scores(kw_ref, gi, wfirst, nwt + 1, win_index, qa_ref[gi]) for gi in groups]
    wps = [probs(parts, col_max(parts)) for parts in wparts]
    ows = [values(vw_ref, gi, wfirst, nwt + 1, wps[gi]) for gi in groups]
    for u, (bi, gi) in enumerate(units):
        o_w = ows[u][0:dh, :] * (1.0 / ows[u][dh:dh + 1, :])
        for r in range(NSA_REP):
            h = gi * NSA_REP + r
            ot_ref[head_rows(bi, h)] += gate(bi, h, 2) * o_w[:, head_lanes(r)]

    carry = lax.fori_loop(0, n // SEL_TILES, lambda it, c: sel_trip(it, c, True), carry)
    sel_trip(n // SEL_TILES, carry, False)

    for u, (bi, gi) in enumerate(units):
        o_s = acc_ref[u, 0:dh, :] * (1.0 / acc_ref[u, dh:dh + 1, :])
        for r in range(NSA_REP):
            h = gi * NSA_REP + r
            ot_ref[head_rows(bi, h)] += gate(bi, h, 1) * o_s[:, head_lanes(r)]

    for bi in range(nseq):
        for j in range(NSA_QW // LANE):
            o_ref[bi, :, j * LANE:(j + 1) * LANE] = ot_ref[bi, j * LANE:(j + 1) * LANE, :].T.astype(BF16)


def _nsa_attn(qt, gt, kc, vct, ksel, kwin, vsel, vwin, ovt, tz, cb):
    batch, _, s = qt.shape
    nseq = NSA_SEQS if batch % NSA_SEQS == 0 else 1
    nt = s // TQ
    nsb = s // SEL_BLOCK
    ncp = kc.shape[-2]
    nu = nseq * NSA_GROUPS
    per_b = lambda a: pl.BlockSpec((nseq,) + a.shape[1:], lambda b, i: (b,) + (0,) * (a.ndim - 1),
                                   pipeline_mode=pl.Buffered(1))
    out = pl.pallas_call(
        functools.partial(_nsa_attn_body, nsb=nsb, ncp=ncp, n_sel=min(SEL_TOPK, nsb), nseq=nseq),
        grid=(batch // nseq, nt),
        in_specs=[pl.BlockSpec((nseq, NSA_QW, TQ), lambda b, i: (b, 0, i)),
                  pl.BlockSpec((nseq, LANE, TQ), lambda b, i: (b, 0, i)),
                  per_b(kc), per_b(vct), per_b(ksel), per_b(kwin), per_b(vsel), per_b(vwin),
                  _resident(ovt), _resident(tz), _resident(cb)],
        out_specs=pl.BlockSpec((nseq, TQ, NSA_QW), lambda b, i: (b, i, 0)),
        out_shape=jax.ShapeDtypeStruct((batch, s, NSA_QW), BF16),
        scratch_shapes=[pltpu.VMEM((nu, nsb, TQ), jnp.int32), pltpu.VMEM((nu, LANE, G3), BF16),
                        pltpu.VMEM((nu, V_ROWS, G3), F32), pltpu.VMEM((nseq, NSA_QW, TQ), F32),
                        ] + [pltpu.VMEM((SEL_TILES * TQ, G3), F32)] * nu,
        compiler_params=_cparams("parallel", "arbitrary"),
        name="nsa_attn",
    )(qt, gt, kc, vct, ksel, kwin, vsel, vwin, ovt, tz, cb)
    return out.reshape(batch * s, NSA_QW)


def _rel_bucket_np(dist):
    dist = np.maximum(dist, 0)
    max_exact = REL_BUCKETS // 2
    ratio = np.log(np.maximum(dist, 1).astype(np.float32) / np.float32(max_exact)) / np.float32(
        math.log(REL_MAX_DIST / max_exact))
    large = max_exact + (ratio * np.float32(REL_BUCKETS - max_exact)).astype(np.int32)
    large = np.minimum(large, REL_BUCKETS - 1)
    return np.where(dist < max_exact, dist, large).astype(np.int32)


def _group_lanes(a):
    hh, r, c = a.shape
    return a.reshape(NSA_GROUPS, NSA_REP, r, c).transpose(0, 2, 1, 3).reshape(NSA_GROUPS, r, NSA_REP * c)


def _bias_tables(rel_bias, ncp):
    k = np.arange(TQ)[:, None]
    q = np.arange(TQ)[None, :]
    tbl = rel_bias.astype(F32)

    def lookup(idx):
        onehot = (jnp.asarray(idx.reshape(1, -1)) == jnp.arange(REL_BUCKETS)[:, None]).astype(F32)
        out = jnp.dot(tbl.T, onehot, precision=lax.Precision.HIGHEST)
        return out.reshape((NSA_HEADS,) + idx.shape)

    far = jnp.broadcast_to(tbl[REL_BUCKETS - 1][:, None, None], (NSA_HEADS, TQ, TQ))
    t0 = jnp.where(jnp.asarray(k <= q)[None], lookup(_rel_bucket_np(q - k)), NEG)
    t1 = lookup(_rel_bucket_np(TQ + q - k))
    t3 = jnp.where(jnp.asarray(k > q)[None], far, NEG)
    tz = jnp.stack([_group_lanes(x) for x in (t0, t1, far, t3, jnp.full_like(far, NEG))], axis=1)
    m = ncp - np.arange(2 * ncp)[:, None]
    d = CMP_STRIDE * m + q - (CMP_BLOCK - 1)
    idx = np.where((d >= 0) & (d < REL_MAX_DIST), _rel_bucket_np(d), REL_BUCKETS - 1)
    cb = _group_lanes(jnp.where(jnp.asarray(d >= 0)[None], lookup(idx), NEG))
    return tz * LOG2E, cb * LOG2E


def _overlap_table(s, ncp):
    nsb = s // SEL_BLOCK
    nc = (s - CMP_BLOCK) // CMP_STRIDE + 1
    cs = np.arange(ncp) * CMP_STRIDE
    ce = cs + CMP_BLOCK - 1
    ss = np.arange(SEL_BLOCK) * SEL_BLOCK
    ov = (cs[None, :] < ss[:, None] + SEL_BLOCK) & (ce[None, :] >= ss[:, None])
    ov &= (np.arange(ncp) < nc)[None, :] & (np.arange(SEL_BLOCK) < nsb)[:, None]
    return jnp.asarray(ov, dtype=BF16)


def kernel(x, mem, norm_mix, norm_mem, w_mem_kv, w_out, norm_ffn, w_up, conv_w, conv_b, w_down,
           gla_w_in, gla_w_gate_up, gla_b_gate, gla_out_norm, nsa_w_in, kv_norm, w_kv_shared,
           cmp_pos, cmp_w1, cmp_b1, cmp_w2, cmp_b2, rel_bias, final_norm):
    batch, seq = x.shape[0], x.shape[1]
    t = batch * seq
    h = x.reshape(t, D_MODEL)
    row = lambda v: v.reshape(1, -1).astype(F32)

    wk, wv = w_mem_kv[..., :MEM_W], w_mem_kv[..., MEM_W:]
    w_mkv = jnp.concatenate([_pad_heads(wk, MEM_HEADS, MEM_HEAD_DIM, MEM_DP),
                             _pad_heads(wv, MEM_HEADS, MEM_HEAD_DIM, MEM_DP)], axis=-1).astype(BF16)
    mem_kv_all = _mem_kv_proj(mem.reshape(-1, D_MODEL), norm_mem.reshape(DEPTH, 1, D_MODEL), w_mkv, batch)

    shared = None
    for i in range(DEPTH):
        w_o = w_out[i]
        w_o_mem = _pad_head_rows(w_o[MAIN_W:], MEM_HEADS, MEM_HEAD_DIM, MEM_DP).astype(BF16)
        if i < N_A_LAYERS:
            wi = gla_w_in[i]
            c0 = GLA_HEADS * GLA_DK
            c1 = 2 * c0
            c2 = c1 + GLA_HEADS * GLA_DV
            c3 = c2 + GLA_HEADS * GLA_DV
            c4 = c3 + GLA_RANK
            w_all = jnp.concatenate([
                _pad_heads(wi[:, :c0], GLA_HEADS, GLA_DK, GLA_DKP),
                _pad_heads(wi[:, c0:c1], GLA_HEADS, GLA_DK, GLA_DKP),
                _pad_heads(wi[:, c1:c2], GLA_HEADS, GLA_DV, GLA_DVP),
                _pad_heads(wi[:, c2:c3], GLA_HEADS, GLA_DV, GLA_DVP),
                _pad_heads(wi[:, c3:c4], 1, GLA_RANK, LANE),
                _pad_heads(wi[:, c4:], MEM_HEADS, MEM_HEAD_DIM, MEM_DP)], axis=1).astype(BF16)
            wg = jnp.pad(_pad_heads(gla_w_gate_up[i], GLA_HEADS, GLA_DK, GLA_DKP),
                         ((0, LANE - GLA_RANK), (0, 0))).astype(BF16)
            bg = row(_pad_heads(gla_b_gate[i], GLA_HEADS, GLA_DK, GLA_DKP))
            q, k, v, go, la, mo = _gla_in_proj(h, row(norm_mix[i]), w_all, wg, bg, mem_kv_all[i], batch)
            on = row(jnp.pad(gla_out_norm[i], (0, GLA_DVP - GLA_DV)))
            main = _gla_mixer(q, k, la, v, go, on, batch)
            w_o_main = _pad_head_rows(w_o[:MAIN_W], GLA_HEADS, GLA_DV, GLA_DVP).astype(BF16)
        else:
            if shared is None:
                ncp = seq // CMP_STRIDE
                gr, dh = NSA_GROUPS, NSA_HEAD_DIM
                wkv = w_kv_shared.reshape(D_MODEL, 6, gr, dh)
                pair = lambda a, b: jnp.concatenate([wkv[:, a], wkv[:, b]], axis=-1).reshape(D_MODEL, gr * 2 * dh)
                slot = lambda a: _pad_heads(wkv[:, a].reshape(D_MODEL, KV_NAT), gr, dh, LANE)
                w_kv = jnp.concatenate([wkv[:, 0].reshape(D_MODEL, KV_NAT), wkv[:, 1].reshape(D_MODEL, KV_NAT),
                                        slot(2), slot(4), pair(3, 5)], axis=1).astype(BF16)
                ck, cv, ksel, kwin, vsel, vwin = _kv_proj(h, row(kv_norm), w_kv, batch)
                x16 = jnp.stack([ck.reshape(batch * ncp, CMP_STRIDE * KV_NAT),
                                 cv.reshape(batch * ncp, CMP_STRIDE * KV_NAT)])
                w1 = cmp_w1.reshape(2, 2, CMP_STRIDE, dh, CMP_HIDDEN)
                eye = jnp.eye(gr, dtype=F32)
                w1x = jnp.einsum('jhldc,gk->jhlgdkc', w1, eye).reshape(2, 2, CMP_STRIDE * KV_NAT, gr * CMP_HIDDEN)
                w1x = w1x.astype(BF16)
                pos8 = jnp.broadcast_to(cmp_pos.reshape(2, 1, CMP_BLOCK * dh), (2, SUBLANE, CMP_BLOCK * dh)).astype(BF16)
                w2p = jnp.pad(cmp_w2, ((0, 0), (0, 0), (0, LANE - dh))).astype(BF16)
                b2p = jnp.pad(cmp_b2, ((0, 0), (0, LANE - dh))).reshape(2, 1, LANE).astype(F32)
                cnat, ctr = _compress(x16, w1x[:, 0], w1x[:, 1], pos8, cmp_w1.astype(BF16),
                                      cmp_b1.reshape(2, 1, CMP_HIDDEN).astype(F32), w2p, b2p, batch)
                tz, cb = _bias_tables(rel_bias, ncp)
                ov = _overlap_table(seq, ncp)
                shared = (cnat[0], ctr[1], ksel, kwin, vsel, vwin, ov, tz, cb)
            wi = nsa_w_in[i - N_A_LAYERS]
            c0 = NSA_HEADS * NSA_HEAD_DIM
            c1 = c0 + NSA_HEADS * 3
            w_all = jnp.concatenate([
                wi[:, :c0],
                _pad_heads(wi[:, c0:c1], 1, NSA_HEADS * 3, LANE),
                _pad_heads(wi[:, c1:], MEM_HEADS, MEM_HEAD_DIM, MEM_DP)], axis=1).astype(BF16)
            q, gates, mo = _nsa_in_proj(h, row(norm_mix[i]), w_all, mem_kv_all[i], batch)
            main = _nsa_attn(q, gates, *shared)
            w_o_main = w_o[:MAIN_W].astype(BF16)
        h = _mix_ffn(h, main, mo, w_o_main, w_o_mem, row(norm_ffn[i]), w_up[i].astype(BF16), conv_w[i].astype(F32),
                     row(conv_b[i]), w_down[i].astype(BF16), row(final_norm), i == DEPTH - 1, batch)
    return h.reshape(batch, seq, D_MODEL)
```

```python
import functools
import math

import numpy as np
import jax
import jax.numpy as jnp
from jax import lax
from jax.experimental import pallas as pl
from jax.experimental.pallas import tpu as pltpu

F32 = jnp.float32
BF16 = jnp.bfloat16

D_MODEL = 1024
DEPTH = 4
N_A_LAYERS = DEPTH // 2
MEM_HEADS = 4
MEM_HEAD_DIM = 64
MEM_W = MEM_HEADS * MEM_HEAD_DIM
MAIN_W = D_MODEL - MEM_W
GLA_HEADS = 4
GLA_DV = MAIN_W // GLA_HEADS
GLA_DK = GLA_DV // 2
GLA_RANK = 16
GLA_GATE_NORM = 16.0
NSA_HEADS = 12
NSA_GROUPS = 4
NSA_HEAD_DIM = MAIN_W // NSA_HEADS
NSA_REP = NSA_HEADS // NSA_GROUPS
CMP_BLOCK = 32
CMP_STRIDE = 16
CMP_HIDDEN = 128
SEL_BLOCK = 64
SEL_TOPK = 16
WINDOW = 512
REL_BUCKETS = 32
REL_MAX_DIST = 128
FFN_DIM = 2816
CONV_WIDTH = 3
EPS = 1e-6

LANE = 128
SUBLANE = 8
VMEM_LIMIT = 56 * 1024 * 1024
GLA_DKP = LANE
GLA_DVP = 2 * LANE
MEM_DP = LANE
NEG = -1e30
TQ = 128
ROW_TILE = 1024
GLA_CHUNK = 64
FFN_TILE = 256
FFN_ROWS = 1024
SEL_TILES = 4
NSA_SEQS = 2
V_ROWS = NSA_HEAD_DIM + 2 * SUBLANE
RANK_UNROLL = 4
LOG2E = math.log2(math.e)

NT = (((1,), (1,)), ((), ()))
TN = (((0,), (0,)), ((), ()))


def _cparams(*sem):
    return pltpu.CompilerParams(dimension_semantics=sem, vmem_limit_bytes=VMEM_LIMIT)


def _rms(x, g):
    return x * lax.rsqrt(jnp.mean(x * x, axis=-1, keepdims=True) + EPS) * g


def _sigmoid(x):
    return 1.0 / (1.0 + jnp.exp(-x))


def _dot(a, b):
    return jnp.dot(a, b, preferred_element_type=F32)


def _resident(a):
    return pl.BlockSpec(a.shape, lambda *_: (0,) * a.ndim, pipeline_mode=pl.Buffered(1))


def _layer_of(a, layer):
    return pl.BlockSpec((None,) + a.shape[1:], lambda *_: (layer,) + (0,) * (a.ndim - 1),
                        pipeline_mode=pl.Buffered(1))


def _pad_heads(w, nh, d, dp):
    lead = w.shape[:-1]
    w = w.reshape(lead + (nh, d))
    w = jnp.pad(w, [(0, 0)] * len(lead) + [(0, 0), (0, dp - d)])
    return w.reshape(lead + (nh * dp,))


def _pad_head_rows(w, nh, d, dp):
    lead, n = w.shape[:-2], w.shape[-1]
    w = jnp.pad(w.reshape(lead + (nh, d, n)), [(0, 0)] * len(lead) + [(0, 0), (0, dp - d), (0, 0)])
    return w.reshape(lead + (nh * dp, n))


def _mem_attention(q, kv_ref):
    outs = []
    for hd in range(MEM_HEADS):
        sl = slice(hd * MEM_DP, (hd + 1) * MEM_DP)
        sv = slice(MEM_QW + hd * MEM_DP, MEM_QW + (hd + 1) * MEM_DP)
        s = lax.dot_general(q[:, sl], kv_ref[:, sl], NT, preferred_element_type=F32) * MEM_HEAD_DIM ** -0.5
        p = jnp.exp(s - jnp.max(s, axis=-1, keepdims=True))
        l = jnp.sum(p, axis=-1, keepdims=True)
        outs.append((_dot(p.astype(BF16), kv_ref[:, sv]) / l).astype(BF16))
    return jnp.concatenate(outs, axis=1)


GLA_QW = GLA_HEADS * GLA_DKP
GLA_VW = GLA_HEADS * GLA_DVP
MEM_QW = MEM_HEADS * MEM_DP
GLA_OFF_K = GLA_QW
GLA_OFF_V = 2 * GLA_QW
GLA_OFF_G = GLA_OFF_V + GLA_VW
GLA_OFF_LR = GLA_OFF_G + GLA_VW
GLA_OFF_MQ = GLA_OFF_LR + LANE


def _gla_in_body(h_ref, g_ref, w_ref, wg_ref, bg_ref, kv_ref, q_ref, k_ref, v_ref, go_ref, la_ref, mo_ref):
    xn = _rms(h_ref[...], g_ref[...]).astype(BF16)

    def proj(lo, n):
        return _dot(xn, w_ref[:, lo:lo + n])

    q_ref[...] = proj(0, GLA_QW).astype(BF16)
    k_ref[...] = proj(GLA_OFF_K, GLA_QW).astype(BF16)
    for j in range(GLA_VW // GLA_QW):
        v_ref[:, j * GLA_QW:(j + 1) * GLA_QW] = proj(GLA_OFF_V + j * GLA_QW, GLA_QW).astype(BF16)
        go_ref[:, j * GLA_QW:(j + 1) * GLA_QW] = proj(GLA_OFF_G + j * GLA_QW, GLA_QW).astype(BF16)
    lr = proj(GLA_OFF_LR, LANE).astype(BF16)
    z = _dot(lr, wg_ref[...]) + bg_ref[...]
    la_ref[...] = (jnp.minimum(z, 0.0) - jnp.log(1.0 + jnp.exp(-jnp.abs(z)))) * (1.0 / GLA_GATE_NORM)
    mo_ref[...] = _mem_attention(proj(GLA_OFF_MQ, MEM_QW).astype(BF16), kv_ref)


def _mem_kv_spec(mem_kv, layer, t, tm, batch):
    nt = t // batch // tm
    return pl.BlockSpec((None, mem_kv.shape[1] // batch, mem_kv.shape[2]), lambda i: (layer, i // nt, 0))


def _gla_in_proj(h, layer, glayer, g, w, wg, bg, mem_kv, batch):
    t = h.shape[0]
    tm = ROW_TILE
    row = lambda n: pl.BlockSpec((tm, n), lambda i: (i, 0))
    return pl.pallas_call(
        _gla_in_body,
        grid=(t // tm,),
        in_specs=[row(D_MODEL), _layer_of(g, layer), _layer_of(w, glayer), _layer_of(wg, glayer),
                  _layer_of(bg, glayer), _mem_kv_spec(mem_kv, layer, t, tm, batch)],
        out_specs=[row(GLA_QW), row(GLA_QW), row(GLA_VW), row(GLA_VW), row(GLA_QW), row(MEM_QW)],
        out_shape=[jax.ShapeDtypeStruct((t, GLA_QW), BF16), jax.ShapeDtypeStruct((t, GLA_QW), BF16),
                   jax.ShapeDtypeStruct((t, GLA_VW), BF16), jax.ShapeDtypeStruct((t, GLA_VW), BF16),
                   jax.ShapeDtypeStruct((t, GLA_QW), F32), jax.ShapeDtypeStruct((t, MEM_QW), BF16)],
        compiler_params=_cparams("parallel"),
        name="gla_in_proj",
    )(h, g, w, wg, bg, mem_kv)


def _gla_mix_body(q_ref, k_ref, la_ref, v_ref, go_ref, on_ref, o_ref, st_ref, *, cn, batch):
    @pl.when(pl.program_id(0) == 0)
    def _():
        st_ref[...] = jnp.zeros_like(st_ref)

    row = lax.broadcasted_iota(jnp.int32, (cn, cn), 0)
    col = lax.broadcasted_iota(jnp.int32, (cn, cn), 1)
    causal = row >= col
    tril = jnp.where(causal, 1.0, 0.0).astype(BF16)
    scale = GLA_DK ** -0.5
    chains = [(bi, hd) for bi in range(batch) for hd in range(GLA_HEADS)]
    sk = lambda hd: slice(hd * GLA_DKP, (hd + 1) * GLA_DKP)
    sv = lambda hd: slice(hd * GLA_DVP, (hd + 1) * GLA_DVP)
    dg = lambda x, y, dims: lax.dot_general(x, y, dims, preferred_element_type=F32)

    def cum_decay(bi, hd):
        la = la_ref[bi, :, sk(hd)]
        la1 = la.astype(BF16)
        r1 = la - la1.astype(F32)
        la2 = r1.astype(BF16)
        la3 = (r1 - la2.astype(F32)).astype(BF16)
        return _dot(tril, la1) + _dot(tril, la2) + _dot(tril, la3)

    bs = [cum_decay(bi, hd) for bi, hd in chains]
    scaled = []
    for (bi, hd), b in zip(chains, bs):
        b_last = b[cn - 1:cn, :]
        q = q_ref[bi, :, sk(hd)].astype(F32)
        k = k_ref[bi, :, sk(hd)].astype(F32)
        scaled.append(((q * jnp.exp(b) * scale).astype(BF16), (k * jnp.exp(-b)).astype(BF16),
                       (k * jnp.exp(b_last - b)).astype(BF16), jnp.exp(b_last)))
    intra = [jnp.where(causal, dg(qs, ks, NT), 0.0).astype(BF16) for qs, ks, _, _ in scaled]
    outs = []
    for i, (bi, hd) in enumerate(chains):
        qs, _, ko, a_last = scaled[i]
        v = v_ref[bi, :, sv(hd)]
        st = st_ref[i]
        outs.append(_dot(intra[i], v) + dg(qs, st.astype(BF16), NT))
        st_ref[i] = st * a_last + dg(v, ko, TN)
    for (bi, hd), o in zip(chains, outs):
        ms = jnp.sum(o * o, axis=-1, keepdims=True) * (1.0 / GLA_DV)
        y = o * lax.rsqrt(ms + EPS) * on_ref[...]
        g = go_ref[bi, :, sv(hd)].astype(F32)
        o_ref[bi, :, sv(hd)] = (y * (g * _sigmoid(g))).astype(BF16)


def _gla_mixer(q, k, la, v, go, on, glayer, batch):
    t = q.shape[0]
    s = t // batch
    cn = GLA_CHUNK
    seq = lambda a: a.reshape(batch, s, a.shape[1])
    blk = lambda n: pl.BlockSpec((batch, cn, n), lambda c: (0, c, 0))
    out = pl.pallas_call(
        functools.partial(_gla_mix_body, cn=cn, batch=batch),
        grid=(s // cn,),
        in_specs=[blk(GLA_QW), blk(GLA_QW), blk(GLA_QW), blk(GLA_VW), blk(GLA_VW),
                  _layer_of(on, glayer)],
        out_specs=blk(GLA_VW),
        out_shape=jax.ShapeDtypeStruct((batch, s, GLA_VW), BF16),
        scratch_shapes=[pltpu.VMEM((batch * GLA_HEADS, GLA_DVP, GLA_DKP), F32)],
        compiler_params=_cparams("arbitrary"),
        name="gla_mixer",
    )(seq(q), seq(k), seq(la), seq(v), seq(go), on)
    return out.reshape(t, GLA_VW)


def _mem_kv_body(m_ref, g_ref, w_ref, o_ref):
    xn = _rms(m_ref[...], g_ref[...]).astype(BF16)
    o_ref[...] = _dot(xn, w_ref[...]).astype(BF16)


def _mem_kv_proj(mem2, g, w, batch):
    nl = w.shape[0]
    m = mem2.shape[0] // batch
    n = w.shape[2]
    return pl.pallas_call(
        _mem_kv_body,
        grid=(nl, batch),
        in_specs=[pl.BlockSpec((m, D_MODEL), lambda l, b: (b, 0)),
                  pl.BlockSpec((None, 1, D_MODEL), lambda l, b: (l, 0, 0)),
                  pl.BlockSpec((None, D_MODEL, n), lambda l, b: (l, 0, 0))],
        out_specs=pl.BlockSpec((None, m, n), lambda l, b: (l, b, 0)),
        out_shape=jax.ShapeDtypeStruct((nl, mem2.shape[0], n), BF16),
        compiler_params=_cparams("arbitrary", "arbitrary"),
        name="mem_kv_proj",
    )(mem2, g, w)


BF16_ROWS = 2 * SUBLANE


def _mix_ffn_body(h_ref, hp_ref, a_ref, ap_ref, m_ref, mp_ref, wa_ref, wm_ref, g_ref, wup_ref, cw_ref, cb_ref,
                  wdn_ref, gf_ref, o_ref, act_ref, *, tm, tf, last):
    g = g_ref[...]
    nr = tm // SUBLANE
    interleave = lambda x: x.reshape(SUBLANE, nr, x.shape[-1]).swapaxes(0, 1).reshape(tm, x.shape[-1])
    deinterleave = lambda x: x.reshape(nr, SUBLANE, x.shape[-1]).swapaxes(0, 1).reshape(tm, x.shape[-1])
    h = interleave(h_ref[...] + _dot(a_ref[...], wa_ref[...]) + _dot(m_ref[...], wm_ref[...]))
    x = _rms(h, g).astype(BF16)
    keep = jnp.where(pl.program_id(1) > 0, 1.0, 0.0)
    h_prev = (hp_ref[...] + _dot(ap_ref[...], wa_ref[...]) + _dot(mp_ref[...], wm_ref[...]))[BF16_ROWS - SUBLANE:]
    x_prev = (_rms(h_prev, g) * keep).astype(BF16)
    first = lax.broadcasted_iota(jnp.int32, (SUBLANE, tf), 0) == 0
    nchunk = FFN_DIM // tf

    def up(j):
        cols = [pl.ds(off + j * tf, tf) for off in (0, FFN_DIM)]
        return tuple((_dot(x, wup_ref[:, c]), _dot(x_prev, wup_ref[:, c])) for c in cols)

    def conv(u, u_prev, off):
        w = cw_ref[:, off:off + tf]
        wrap = lambda r, k: jnp.where(first, u_prev[SUBLANE - k:SUBLANE - k + 1, :],
                                      pltpu.roll(u[r * SUBLANE:(r + 1) * SUBLANE, :], 1, 0))
        back1 = jnp.concatenate([wrap(nr - 1, 1), u[0:tm - SUBLANE, :]], axis=0)
        back2 = jnp.concatenate([wrap(nr - 2, 2), wrap(nr - 1, 1), u[0:tm - 2 * SUBLANE, :]], axis=0)
        return cb_ref[:, off:off + tf] + w[0:1, :] * back2 + w[1:2, :] * back1 + w[2:3, :] * u

    u_next = up(0)
    for j in range(nchunk):
        (ua, ua_prev), (ub, ub_prev) = u_next
        if j + 1 < nchunk:
            u_next = up(j + 1)
        a = conv(ua, ua_prev, j * tf)
        b = conv(ub, ub_prev, FFN_DIM + j * tf)
        act_ref[:, j * tf:(j + 1) * tf] = (a * _sigmoid(a) * b).astype(BF16)
    out = h + _dot(act_ref[...], wdn_ref[...])
    o_ref[...] = deinterleave(_rms(out, gf_ref[...]) if last else out)


def _mix_ffn(h, main, mo, wa, klayer, wm, layer, g, wup, cw, cb, wdn, gf, last, batch):
    t = h.shape[0]
    tm = FFN_ROWS
    nt = t // batch // tm
    hb = tm // BF16_ROWS
    cur = lambda n: pl.BlockSpec((tm, n), lambda b, i: (b * nt + i, 0))
    prev = lambda n: pl.BlockSpec((BF16_ROWS, n), lambda b, i: (jnp.maximum((b * nt + i) * hb - 1, 0), 0))
    return pl.pallas_call(
        functools.partial(_mix_ffn_body, tm=tm, tf=FFN_TILE, last=last),
        grid=(batch, nt),
        in_specs=[cur(D_MODEL), prev(D_MODEL), cur(main.shape[1]), prev(main.shape[1]),
                  cur(mo.shape[1]), prev(mo.shape[1]), _layer_of(wa, klayer), _layer_of(wm, layer), _layer_of(g, layer),
                  _layer_of(wup, layer), _layer_of(cw, layer), _layer_of(cb, layer), _layer_of(wdn, layer),
                  _resident(gf)],
        out_specs=cur(D_MODEL),
        out_shape=jax.ShapeDtypeStruct((t, D_MODEL), F32),
        scratch_shapes=[pltpu.VMEM((tm, FFN_DIM), BF16)],
        compiler_params=_cparams("parallel", "parallel"),
        name="mix_ffn",
    )(h, h, main, main, mo, mo, wa, wm, g, wup, cw, cb, wdn, gf)


KV_NAT = NSA_GROUPS * NSA_HEAD_DIM


def _kv_proj_body(h_ref, g_ref, w_ref, ck_ref, cv_ref, ks_ref, kw_ref, vs_ref, vw_ref, *, tm):
    xn = _rms(h_ref[...], g_ref[...]).astype(BF16)
    ck_ref[...] = _dot(xn, w_ref[:, 0:KV_NAT]).astype(BF16)
    cv_ref[...] = _dot(xn, w_ref[:, KV_NAT:2 * KV_NAT]).astype(BF16)
    key = pl.program_id(1) * tm + lax.broadcasted_iota(jnp.int32, (tm, LANE), 0)
    lane = lax.broadcasted_iota(jnp.int32, (tm, LANE), 1)
    onehot = jnp.where(lax.shift_right_logical(key, int(math.log2(SEL_BLOCK))) == lane - NSA_HEAD_DIM, 1.0, 0.0)
    low = lane < NSA_HEAD_DIM
    for gi in range(NSA_GROUPS):
        slot = lambda n: _dot(xn, w_ref[:, 2 * KV_NAT + (n * NSA_GROUPS + gi) * LANE:
                                           2 * KV_NAT + (n * NSA_GROUPS + gi + 1) * LANE])
        ks_ref[gi] = jnp.where(low, slot(0), onehot).astype(BF16)
        kw_ref[gi] = jnp.where(low, slot(1), 0.0).astype(BF16)
        vt = slot(2).T.astype(BF16)
        ones = jnp.ones((V_ROWS - NSA_HEAD_DIM, tm), BF16)
        vs_ref[gi] = jnp.concatenate([vt[0:NSA_HEAD_DIM], ones], axis=0)
        vw_ref[gi] = jnp.concatenate([vt[NSA_HEAD_DIM:], ones], axis=0)


def _kv_proj(h, g, w, batch):
    t = h.shape[0]
    s = t // batch
    tm = ROW_TILE
    nt = s // tm
    gr = NSA_GROUPS
    return pl.pallas_call(
        functools.partial(_kv_proj_body, tm=tm),
        grid=(batch, nt),
        in_specs=[pl.BlockSpec((tm, D_MODEL), lambda b, i: (b * nt + i, 0)),
                  pl.BlockSpec(g.shape, lambda b, i: (0, 0)),
                  pl.BlockSpec(w.shape, lambda b, i: (0, 0))],
        out_specs=[pl.BlockSpec((tm, KV_NAT), lambda b, i: (b * nt + i, 0)),
                   pl.BlockSpec((tm, KV_NAT), lambda b, i: (b * nt + i, 0)),
                   pl.BlockSpec((None, gr, tm, LANE), lambda b, i: (b, 0, i, 0)),
                   pl.BlockSpec((None, gr, tm, LANE), lambda b, i: (b, 0, i, 0)),
                   pl.BlockSpec((None, gr, V_ROWS, tm), lambda b, i: (b, 0, 0, i)),
                   pl.BlockSpec((None, gr, V_ROWS, tm), lambda b, i: (b, 0, 0, i))],
        out_shape=[jax.ShapeDtypeStruct((t, KV_NAT), BF16), jax.ShapeDtypeStruct((t, KV_NAT), BF16),
                   jax.ShapeDtypeStruct((batch, gr, s, LANE), BF16),
                   jax.ShapeDtypeStruct((batch, gr, s, LANE), BF16),
                   jax.ShapeDtypeStruct((batch, gr, V_ROWS, s), BF16),
                   jax.ShapeDtypeStruct((batch, gr, V_ROWS, s), BF16)],
        compiler_params=_cparams("parallel", "parallel"),
        name="nsa_kv_proj",
    )(h, g, w)


def _compress_body(x_ref, wt_ref, wb_ref, pos_ref, w1_ref, b1_ref, w2_ref, b2_ref, on_ref, ot_ref, *, ncp):
    x = x_ref[...]
    top = _dot(x, wt_ref[...])
    bot = _dot(x, wb_ref[...])
    posb = _dot(pos_ref[...], w1_ref[...])[0:1, :] + b1_ref[...]
    for gi in range(NSA_GROUPS):
        sl = slice(gi * CMP_HIDDEN, (gi + 1) * CMP_HIDDEN)
        hid = top[:, sl] + pltpu.roll(bot[:, sl], ncp - 1, 0) + posb
        hid = (hid * _sigmoid(hid)).astype(BF16)
        out = _dot(hid, w2_ref[...]) + b2_ref[...]
        on_ref[gi] = out.astype(BF16)
        ot_ref[gi] = out.T.astype(BF16)


def _compress(x16, wt, wb, pos, w1, b1, w2, b2, batch):
    ncp = x16.shape[1] // batch
    gr = NSA_GROUPS
    per_j = lambda a: pl.BlockSpec((None,) + a.shape[1:], lambda j, b: (j,) + (0,) * (a.ndim - 1))
    return pl.pallas_call(
        functools.partial(_compress_body, ncp=ncp),
        grid=(2, batch),
        in_specs=[pl.BlockSpec((None, ncp, x16.shape[2]), lambda j, b: (j, b, 0)),
                  per_j(wt), per_j(wb), per_j(pos), per_j(w1), per_j(b1), per_j(w2), per_j(b2)],
        out_specs=[pl.BlockSpec((None, None, gr, ncp, LANE), lambda j, b: (j, b, 0, 0, 0)),
                   pl.BlockSpec((None, None, gr, LANE, ncp), lambda j, b: (j, b, 0, 0, 0))],
        out_shape=[jax.ShapeDtypeStruct((2, batch, gr, ncp, LANE), BF16),
                   jax.ShapeDtypeStruct((2, batch, gr, LANE, ncp), BF16)],
        compiler_params=_cparams("arbitrary", "arbitrary"),
        name="nsa_compress",
    )(x16, wt, wb, pos, w1, b1, w2, b2)


NSA_QW = NSA_HEADS * NSA_HEAD_DIM
G3 = NSA_REP * TQ


def _nsa_in_body(h_ref, g_ref, w_ref, kv_ref, qt_ref, gt_ref, mo_ref):
    xn = _rms(h_ref[...], g_ref[...]).astype(BF16)
    for j in range(NSA_QW // LANE):
        y = _dot(xn, w_ref[:, j * LANE:(j + 1) * LANE]) * (NSA_HEAD_DIM ** -0.5 * LOG2E)
        qt_ref[j * LANE:(j + 1) * LANE, :] = y.T.astype(BF16)
    gt_ref[...] = _sigmoid(_dot(xn, w_ref[:, NSA_QW:NSA_QW + LANE])).T
    mo_ref[...] = _mem_attention(_dot(xn, w_ref[:, NSA_QW + LANE:]).astype(BF16), kv_ref)


def _nsa_in_proj(h, layer, nlayer, g, w, mem_kv, batch):
    t = h.shape[0]
    tm = ROW_TILE
    row = lambda n: pl.BlockSpec((tm, n), lambda i: (i, 0))
    nt = t // batch // tm
    col = lambda n: pl.BlockSpec((None, n, tm), lambda i: (i // nt, 0, i % nt))
    return pl.pallas_call(
        _nsa_in_body,
        grid=(t // tm,),
        in_specs=[row(D_MODEL), _layer_of(g, layer), _layer_of(w, nlayer), _mem_kv_spec(mem_kv, layer, t, tm, batch)],
        out_specs=[col(NSA_QW), col(LANE), row(MEM_QW)],
        out_shape=[jax.ShapeDtypeStruct((batch, NSA_QW, t // batch), BF16),
                   jax.ShapeDtypeStruct((batch, LANE, t // batch), F32),
                   jax.ShapeDtypeStruct((t, MEM_QW), BF16)],
        compiler_params=_cparams("parallel"),
        name="nsa_in_proj",
    )(h, g, w, mem_kv)


def _nsa_attn_body(qt_ref, gt_ref, kc_ref, vct_ref, ks_ref, kw_ref, vs_ref, vw_ref, ovt_ref, tz_ref, cb_ref,
                   o_ref, sc_ref, qa_ref, acc_ref, ot_ref, *s_slots, nsb, ncp, n_sel, nseq):
    n = pl.program_id(1)
    t0 = n * TQ
    dh = NSA_HEAD_DIM
    cstart = pl.multiple_of(ncp - n * (TQ // CMP_STRIDE), SUBLANE)
    sees_any = t0 + (lax.broadcasted_iota(jnp.int32, (1, G3), 1) & (TQ - 1)) >= CMP_BLOCK - 1
    jj = lax.broadcasted_iota(jnp.int32, (nsb, TQ), 0)
    cur = lax.shift_right_logical(t0 + lax.broadcasted_iota(jnp.int32, (nsb, TQ), 1), int(math.log2(SEL_BLOCK)))
    forced = (jj == 0) | (jj == cur) | (jj == cur - 1)
    zpad = jnp.zeros((dh, G3), BF16)
    nwt = WINDOW // TQ
    units = [(bi, gi) for bi in range(nseq) for gi in range(NSA_GROUPS)]

    def scores(k_ref, u, first, count, tz_index, qa):
        bi, gi = units[u]
        koff = pl.multiple_of(first * TQ, TQ)
        s = _dot(k_ref[bi, gi, pl.ds(koff, count * TQ), :], qa)
        return [s[i * TQ:(i + 1) * TQ] + tz_ref[gi, tz_index(n - (first + i))] for i in range(count)]

    def col_max(parts):
        mx = parts[0]
        for x in parts[1:]:
            mx = jnp.maximum(mx, x)
        return jnp.max(mx, axis=0, keepdims=True)

    def probs(parts, m):
        return jnp.concatenate([jnp.exp2(x - m).astype(BF16) for x in parts], axis=0)

    def values(v_ref, u, first, count, p):
        bi, gi = units[u]
        koff = pl.multiple_of(first * TQ, TQ)
        return _dot(v_ref[bi, gi, :, pl.ds(koff, count * TQ)], p)

    sel_index = lambda d: jnp.where(d < 0, 2, jnp.minimum(d, 2))
    win_index = lambda d: jnp.where(d < 0, 4, jnp.where(d == nwt, 3, jnp.minimum(d, 2)))

    groups = range(len(units))
    gate = lambda bi, h, c: gt_ref[bi, pl.ds(h * 3 + c, 1), :]
    head_lanes = lambda r: slice(r * TQ, (r + 1) * TQ)
    head_rows = lambda bi, h: (bi, slice(h * dh, (h + 1) * dh))

    q3s = [jnp.concatenate([qt_ref[head_rows(bi, gi * NSA_REP + r)] for r in range(NSA_REP)], axis=1)
           for bi, gi in units]
    cs = [_dot(kc_ref[bi, gi], jnp.concatenate([q3s[u], zpad], axis=0)) + cb_ref[gi, pl.ds(cstart, ncp), :]
          for u, (bi, gi) in enumerate(units)]
    cps = [jnp.exp2(s - jnp.max(s, axis=0, keepdims=True)) for s in cs]
    cps = [p * jnp.where(sees_any, 1.0 / jnp.sum(p, axis=0, keepdims=True), 0.0) for p in cps]
    ocs = [_dot(vct_ref[bi, gi], cps[u].astype(BF16)) for u, (bi, gi) in enumerate(units)]
    scs = []
    for gi in groups:
        psum = cps[gi][:, 0:TQ]
        for r in range(1, NSA_REP):
            psum = psum + cps[gi][:, head_lanes(r)]
        p1 = psum.astype(BF16)
        p2 = (psum - p1.astype(F32)).astype(BF16)
        imp = _dot(ovt_ref[...], p1) + _dot(ovt_ref[...], p2)
        score = jnp.where(forced, 1e4, jnp.where(jj <= cur, imp[0:nsb], -1.0))
        scs.append(jnp.where(score < 0.0, -1, lax.bitcast_convert_type(score, jnp.int32)))
    for u, (bi, gi) in enumerate(units):
        sc_ref[u] = scs[u]
        for r in range(NSA_REP):
            h = gi * NSA_REP + r
            ot_ref[head_rows(bi, h)] = gate(bi, h, 0) * ocs[u][0:dh, head_lanes(r)]

    scs1 = [k + 1 for k in scs]

    def rank_step(i4, cnts):
        cnts = list(cnts)
        for u in range(RANK_UNROLL):
            i = i4 * RANK_UNROLL + u
            lower = i < jj
            for gi in groups:
                rowk = sc_ref[gi, pl.ds(i, 1), :]
                before = rowk >= jnp.where(lower, scs[gi], scs1[gi])
                cnts[gi] = cnts[gi] + jnp.where(before, 1, 0)
        return tuple(cnts)
    rank_trips = jnp.minimum((2 * n + 2 + RANK_UNROLL - 1) // RANK_UNROLL, nsb // RANK_UNROLL)
    cnts = lax.fori_loop(0, rank_trips, rank_step, tuple(jnp.zeros((nsb, TQ), jnp.int32) for _ in groups))
    for gi in groups:
        selneg = jnp.where((cnts[gi] < n_sel) & (jj <= cur), 0.0, NEG).astype(BF16)
        if nsb < SEL_BLOCK:
            selneg = jnp.concatenate([selneg, jnp.zeros((SEL_BLOCK - nsb, TQ), BF16)], axis=0)
        qa_ref[gi] = jnp.concatenate([q3s[gi], jnp.concatenate([selneg] * NSA_REP, axis=1)], axis=0)

    acc_ref[...] = jnp.zeros_like(acc_ref)
    half = len(units) // 2
    s_refs = (s_slots[:half], s_slots[half:])

    def score_half(it, hb):
        plist = [scores(ks_ref, hb * half + u, it * SEL_TILES, SEL_TILES, sel_index, qa_ref[hb * half + u])
                 for u in range(half)]
        for u, parts in enumerate(plist):
            for i, x in enumerate(parts):
                s_refs[hb][u][i * TQ:(i + 1) * TQ, :] = x
        return [col_max(parts) for parts in plist]

    def finish_half(it, hb, ms, bms):
        m2s = [jnp.maximum(ms[u], bms[u]) for u in range(half)]
        ps = [probs([s_refs[hb][u][i * TQ:(i + 1) * TQ, :] for i in range(SEL_TILES)], m2s[u]) for u in range(half)]
        vals = [values(vs_ref, hb * half + u, it * SEL_TILES, SEL_TILES, ps[u]) for u in range(half)]
        for u in range(half):
            gi = hb * half + u
            acc_ref[gi] = jnp.exp2(ms[u] - m2s[u]) * acc_ref[gi] + vals[u]
        return m2s

    def sel_trip(it, carry, score_next):
        ms0, ms1, bm0 = carry
        bm1 = score_half(it, 1)
        ms0 = finish_half(it, 0, ms0, bm0)
        if score_next:
            bm0 = score_half(it + 1, 0)
        ms1 = finish_half(it, 1, ms1, bm1)
        return ms0, ms1, bm0

    neg = [jnp.full((1, G3), NEG, F32) for _ in range(half)]
    carry = (neg, neg, score_half(0, 0))

    wfirst = jnp.maximum(n - nwt, 0)
    wparts = [scores(kw_ref, gi, wfirst, nwt + 1, win_index, qa_ref[gi]) for gi in groups]
    wps = [probs(parts, col_max(parts)) for parts in wparts]
    ows = [values(vw_ref, gi, wfirst, nwt + 1, wps[gi]) for gi in groups]
    for u, (bi, gi) in enumerate(units):
        o_w = ows[u][0:dh, :] * (1.0 / ows[u][dh:dh + 1, :])
        for r in range(NSA_REP):
            h = gi * NSA_REP + r
            ot_ref[head_rows(bi, h)] += gate(bi, h, 2) * o_w[:, head_lanes(r)]

    carry = lax.fori_loop(0, n // SEL_TILES, lambda it, c: sel_trip(it, c, True), carry)
    sel_trip(n // SEL_TILES, carry, False)

    for u, (bi, gi) in enumerate(units):
        o_s = acc_ref[u, 0:dh, :] * (1.0 / acc_ref[u, dh:dh + 1, :])
        for r in range(NSA_REP):
            h = gi * NSA_REP + r
            ot_ref[head_rows(bi, h)] += gate(bi, h, 1) * o_s[:, head_lanes(r)]

    for bi in range(nseq):
        for j in range(NSA_QW // LANE):
            o_ref[bi, :, j * LANE:(j + 1) * LANE] = ot_ref[bi, j * LANE:(j + 1) * LANE, :].T.astype(BF16)


def _nsa_attn(qt, gt, kc, vct, ksel, kwin, vsel, vwin, ovt, tz, cb):
    batch, _, s = qt.shape
    nseq = NSA_SEQS if batch % NSA_SEQS == 0 else 1
    nt = s // TQ
    nsb = s // SEL_BLOCK
    ncp = kc.shape[-2]
    nu = nseq * NSA_GROUPS
    per_b = lambda a: pl.BlockSpec((nseq,) + a.shape[1:], lambda b, i: (b,) + (0,) * (a.ndim - 1),
                                   pipeline_mode=pl.Buffered(1))
    out = pl.pallas_call(
        functools.partial(_nsa_attn_body, nsb=nsb, ncp=ncp, n_sel=min(SEL_TOPK, nsb), nseq=nseq),
        grid=(batch // nseq, nt),
        in_specs=[pl.BlockSpec((nseq, NSA_QW, TQ), lambda b, i: (b, 0, i)),
                  pl.BlockSpec((nseq, LANE, TQ), lambda b, i: (b, 0, i)),
                  per_b(kc), per_b(vct), per_b(ksel), per_b(kwin), per_b(vsel), per_b(vwin),
                  _resident(ovt), _resident(tz), _resident(cb)],
        out_specs=pl.BlockSpec((nseq, TQ, NSA_QW), lambda b, i: (b, i, 0)),
        out_shape=jax.ShapeDtypeStruct((batch, s, NSA_QW), BF16),
        scratch_shapes=[pltpu.VMEM((nu, nsb, TQ), jnp.int32), pltpu.VMEM((nu, LANE, G3), BF16),
                        pltpu.VMEM((nu, V_ROWS, G3), F32), pltpu.VMEM((nseq, NSA_QW, TQ), F32),
                        ] + [pltpu.VMEM((SEL_TILES * TQ, G3), F32)] * nu,
        compiler_params=_cparams("parallel", "arbitrary"),
        name="nsa_attn",
    )(qt, gt, kc, vct, ksel, kwin, vsel, vwin, ovt, tz, cb)
    return out.reshape(batch * s, NSA_QW)


def _rel_bucket_np(dist):
    dist = np.maximum(dist, 0)
    max_exact = REL_BUCKETS // 2
    ratio = np.log(np.maximum(dist, 1).astype(np.float32) / np.float32(max_exact)) / np.float32(
        math.log(REL_MAX_DIST / max_exact))
    large = max_exact + (ratio * np.float32(REL_BUCKETS - max_exact)).astype(np.int32)
    large = np.minimum(large, REL_BUCKETS - 1)
    return np.where(dist < max_exact, dist, large).astype(np.int32)


def _group_lanes(a):
    hh, r, c = a.shape
    return a.reshape(NSA_GROUPS, NSA_REP, r, c).transpose(0, 2, 1, 3).reshape(NSA_GROUPS, r, NSA_REP * c)


def _bias_tables(rel_bias, ncp):
    k = np.arange(TQ)[:, None]
    q = np.arange(TQ)[None, :]
    tbl = rel_bias.astype(F32)

    def lookup(idx):
        onehot = (jnp.asarray(idx.reshape(1, -1)) == jnp.arange(REL_BUCKETS)[:, None]).astype(F32)
        out = jnp.dot(tbl.T, onehot, precision=lax.Precision.HIGHEST)
        return out.reshape((NSA_HEADS,) + idx.shape)

    far = jnp.broadcast_to(tbl[REL_BUCKETS - 1][:, None, None], (NSA_HEADS, TQ, TQ))
    t0 = jnp.where(jnp.asarray(k <= q)[None], lookup(_rel_bucket_np(q - k)), NEG)
    t1 = lookup(_rel_bucket_np(TQ + q - k))
    t3 = jnp.where(jnp.asarray(k > q)[None], far, NEG)
    tz = jnp.stack([_group_lanes(x) for x in (t0, t1, far, t3, jnp.full_like(far, NEG))], axis=1)
    m = ncp - np.arange(2 * ncp)[:, None]
    d = CMP_STRIDE * m + q - (CMP_BLOCK - 1)
    idx = np.where((d >= 0) & (d < REL_MAX_DIST), _rel_bucket_np(d), REL_BUCKETS - 1)
    cb = _group_lanes(jnp.where(jnp.asarray(d >= 0)[None], lookup(idx), NEG))
    return tz * LOG2E, cb * LOG2E


def _overlap_table(s, ncp):
    nsb = s // SEL_BLOCK
    nc = (s - CMP_BLOCK) // CMP_STRIDE + 1
    cs = np.arange(ncp) * CMP_STRIDE
    ce = cs + CMP_BLOCK - 1
    ss = np.arange(SEL_BLOCK) * SEL_BLOCK
    ov = (cs[None, :] < ss[:, None] + SEL_BLOCK) & (ce[None, :] >= ss[:, None])
    ov &= (np.arange(ncp) < nc)[None, :] & (np.arange(SEL_BLOCK) < nsb)[:, None]
    return jnp.asarray(ov, dtype=BF16)


def kernel(x, mem, norm_mix, norm_mem, w_mem_kv, w_out, norm_ffn, w_up, conv_w, conv_b, w_down,
           gla_w_in, gla_w_gate_up, gla_b_gate, gla_out_norm, nsa_w_in, kv_norm, w_kv_shared,
           cmp_pos, cmp_w1, cmp_b1, cmp_w2, cmp_b2, rel_bias, final_norm):
    batch, seq = x.shape[0], x.shape[1]
    t = batch * seq
    h = x.reshape(t, D_MODEL)
    row = lambda v: v.reshape(1, -1).astype(F32)

    wk, wv = w_mem_kv[..., :MEM_W], w_mem_kv[..., MEM_W:]
    w_mkv = jnp.concatenate([_pad_heads(wk, MEM_HEADS, MEM_HEAD_DIM, MEM_DP),
                             _pad_heads(wv, MEM_HEADS, MEM_HEAD_DIM, MEM_DP)], axis=-1).astype(BF16)
    mem_kv_all = _mem_kv_proj(mem.reshape(-1, D_MODEL), norm_mem.reshape(DEPTH, 1, D_MODEL), w_mkv, batch)

    ffn_params = (norm_ffn.reshape(DEPTH, 1, D_MODEL).astype(F32), w_up.astype(BF16), conv_w.astype(F32),
                  conv_b.reshape(DEPTH, 1, 2 * FFN_DIM).astype(F32), w_down.astype(BF16))
    norm_mix3 = norm_mix.reshape(DEPTH, 1, D_MODEL).astype(F32)
    w_o_mem = _pad_head_rows(w_out[:, MAIN_W:], MEM_HEADS, MEM_HEAD_DIM, MEM_DP).astype(BF16)
    mem_pad = lambda w: _pad_heads(w, MEM_HEADS, MEM_HEAD_DIM, MEM_DP)

    c0 = GLA_HEADS * GLA_DK
    c1 = 2 * c0
    c2 = c1 + GLA_HEADS * GLA_DV
    c3 = c2 + GLA_HEADS * GLA_DV
    c4 = c3 + GLA_RANK
    gla_w = jnp.concatenate([
        _pad_heads(gla_w_in[..., :c0], GLA_HEADS, GLA_DK, GLA_DKP),
        _pad_heads(gla_w_in[..., c0:c1], GLA_HEADS, GLA_DK, GLA_DKP),
        _pad_heads(gla_w_in[..., c1:c2], GLA_HEADS, GLA_DV, GLA_DVP),
        _pad_heads(gla_w_in[..., c2:c3], GLA_HEADS, GLA_DV, GLA_DVP),
        _pad_heads(gla_w_in[..., c3:c4], 1, GLA_RANK, LANE),
        mem_pad(gla_w_in[..., c4:])], axis=-1).astype(BF16)
    gla_wg = jnp.pad(_pad_heads(gla_w_gate_up, GLA_HEADS, GLA_DK, GLA_DKP),
                     ((0, 0), (0, LANE - GLA_RANK), (0, 0))).astype(BF16)
    gla_bg = _pad_heads(gla_b_gate, GLA_HEADS, GLA_DK, GLA_DKP).reshape(N_A_LAYERS, 1, GLA_QW).astype(F32)
    gla_on = jnp.pad(gla_out_norm, ((0, 0), (0, GLA_DVP - GLA_DV))).reshape(N_A_LAYERS, 1, GLA_DVP).astype(F32)
    gla_wo = _pad_head_rows(w_out[:N_A_LAYERS, :MAIN_W], GLA_HEADS, GLA_DV, GLA_DVP).astype(BF16)

    n0 = NSA_HEADS * NSA_HEAD_DIM
    n1 = n0 + NSA_HEADS * 3
    nsa_w = jnp.concatenate([nsa_w_in[..., :n0], _pad_heads(nsa_w_in[..., n0:n1], 1, NSA_HEADS * 3, LANE),
                             mem_pad(nsa_w_in[..., n1:])], axis=-1).astype(BF16)
    nsa_wo = w_out[N_A_LAYERS:, :MAIN_W].astype(BF16)

    shared = None
    for i in range(DEPTH):
        if i < N_A_LAYERS:
            q, k, v, go, la, mo = _gla_in_proj(h, i, i, norm_mix3, gla_w, gla_wg, gla_bg, mem_kv_all, batch)
            main = _gla_mixer(q, k, la, v, go, gla_on, i, batch)
            w_o_main, klayer = gla_wo, i
        else:
            if shared is None:
                ncp = seq // CMP_STRIDE
                gr, dh = NSA_GROUPS, NSA_HEAD_DIM
                wkv = w_kv_shared.reshape(D_MODEL, 6, gr, dh)
                pair = lambda a, b: jnp.concatenate([wkv[:, a], wkv[:, b]], axis=-1).reshape(D_MODEL, gr * 2 * dh)
                slot = lambda a: _pad_heads(wkv[:, a].reshape(D_MODEL, KV_NAT), gr, dh, LANE)
                w_kv = jnp.concatenate([wkv[:, 0].reshape(D_MODEL, KV_NAT), wkv[:, 1].reshape(D_MODEL, KV_NAT),
                                        slot(2), slot(4), pair(3, 5)], axis=1).astype(BF16)
                ck, cv, ksel, kwin, vsel, vwin = _kv_proj(h, row(kv_norm), w_kv, batch)
                x16 = jnp.stack([ck.reshape(batch * ncp, CMP_STRIDE * KV_NAT),
                                 cv.reshape(batch * ncp, CMP_STRIDE * KV_NAT)])
                w1 = cmp_w1.reshape(2, 2, CMP_STRIDE, dh, CMP_HIDDEN)
                eye = jnp.eye(gr, dtype=F32)
                w1x = jnp.einsum('jhldc,gk->jhlgdkc', w1, eye).reshape(2, 2, CMP_STRIDE * KV_NAT, gr * CMP_HIDDEN)
                w1x = w1x.astype(BF16)
                pos8 = jnp.broadcast_to(cmp_pos.reshape(2, 1, CMP_BLOCK * dh), (2, SUBLANE, CMP_BLOCK * dh)).astype(BF16)
                w2p = jnp.pad(cmp_w2, ((0, 0), (0, 0), (0, LANE - dh))).astype(BF16)
                b2p = jnp.pad(cmp_b2, ((0, 0), (0, LANE - dh))).reshape(2, 1, LANE).astype(F32)
                cnat, ctr = _compress(x16, w1x[:, 0], w1x[:, 1], pos8, cmp_w1.astype(BF16),
                                      cmp_b1.reshape(2, 1, CMP_HIDDEN).astype(F32), w2p, b2p, batch)
                tz, cb = _bias_tables(rel_bias, ncp)
                ov = _overlap_table(seq, ncp)
                shared = (cnat[0], ctr[1], ksel, kwin, vsel, vwin, ov, tz, cb)
            q, gates, mo = _nsa_in_proj(h, i, i - N_A_LAYERS, norm_mix3, nsa_w, mem_kv_all, batch)
            main = _nsa_attn(q, gates, *shared)
            w_o_main, klayer = nsa_wo, i - N_A_LAYERS
        h = _mix_ffn(h, main, mo, w_o_main, klayer, w_o_mem, i, *ffn_params, row(final_norm), i == DEPTH - 1, batch)
    return h.reshape(batch, seq, D_MODEL)
```

```python
import functools
import math

import numpy as np
import jax
import jax.numpy as jnp
from jax import lax
from jax.experimental import pallas as pl
from jax.experimental.pallas import tpu as pltpu

F32 = jnp.float32
BF16 = jnp.bfloat16

D_MODEL = 1024
DEPTH = 4
N_A_LAYERS = DEPTH // 2
MEM_HEADS = 4
MEM_HEAD_DIM = 64
MEM_W = MEM_HEADS * MEM_HEAD_DIM
MAIN_W = D_MODEL - MEM_W
GLA_HEADS = 4
GLA_DV = MAIN_W // GLA_HEADS
GLA_DK = GLA_DV // 2
GLA_RANK = 16
GLA_GATE_NORM = 16.0
NSA_HEADS = 12
NSA_GROUPS = 4
NSA_HEAD_DIM = MAIN_W // NSA_HEADS
NSA_REP = NSA_HEADS // NSA_GROUPS
CMP_BLOCK = 32
CMP_STRIDE = 16
CMP_HIDDEN = 128
SEL_BLOCK = 64
SEL_TOPK = 16
WINDOW = 512
REL_BUCKETS = 32
REL_MAX_DIST = 128
FFN_DIM = 2816
CONV_WIDTH = 3
EPS = 1e-6

LANE = 128
SUBLANE = 8
VMEM_LIMIT = 56 * 1024 * 1024
GLA_DKP = LANE
GLA_DVP = 2 * LANE
MEM_DP = LANE
NEG = -1e30
TQ = 128
ROW_TILE = 1024
GLA_CHUNK = 64
GLA_SAFE_DECAY = 80.0
FFN_TILE = 256
FFN_ROWS = 1024
SEL_TILES = 4
NSA_SEQS = 2
V_ROWS = NSA_HEAD_DIM + 2 * SUBLANE
RANK_UNROLL = 4
LOG2E = math.log2(math.e)

NT = (((1,), (1,)), ((), ()))
TN = (((0,), (0,)), ((), ()))


def _cparams(*sem):
    return pltpu.CompilerParams(dimension_semantics=sem, vmem_limit_bytes=VMEM_LIMIT)


def _rms(x, g):
    return x * lax.rsqrt(jnp.mean(x * x, axis=-1, keepdims=True) + EPS) * g


def _sigmoid(x):
    return 1.0 / (1.0 + jnp.exp(-x))


def _dot(a, b):
    return jnp.dot(a, b, preferred_element_type=F32)


def _resident(a):
    return pl.BlockSpec(a.shape, lambda *_: (0,) * a.ndim, pipeline_mode=pl.Buffered(1))


def _layer_of(a, layer):
    return pl.BlockSpec((None,) + a.shape[1:], lambda *_: (layer,) + (0,) * (a.ndim - 1),
                        pipeline_mode=pl.Buffered(1))


def _pad_heads(w, nh, d, dp):
    lead = w.shape[:-1]
    w = w.reshape(lead + (nh, d))
    w = jnp.pad(w, [(0, 0)] * len(lead) + [(0, 0), (0, dp - d)])
    return w.reshape(lead + (nh * dp,))


def _pad_head_rows(w, nh, d, dp):
    lead, n = w.shape[:-2], w.shape[-1]
    w = jnp.pad(w.reshape(lead + (nh, d, n)), [(0, 0)] * len(lead) + [(0, 0), (0, dp - d), (0, 0)])
    return w.reshape(lead + (nh * dp, n))


def _mem_attention(q, kv_ref):
    outs = []
    for hd in range(MEM_HEADS):
        sl = slice(hd * MEM_DP, (hd + 1) * MEM_DP)
        sv = slice(MEM_QW + hd * MEM_DP, MEM_QW + (hd + 1) * MEM_DP)
        s = lax.dot_general(q[:, sl], kv_ref[:, sl], NT, preferred_element_type=F32) * MEM_HEAD_DIM ** -0.5
        p = jnp.exp(s - jnp.max(s, axis=-1, keepdims=True))
        l = jnp.sum(p, axis=-1, keepdims=True)
        outs.append((_dot(p.astype(BF16), kv_ref[:, sv]) / l).astype(BF16))
    return jnp.concatenate(outs, axis=1)


GLA_QW = GLA_HEADS * GLA_DKP
GLA_VW = GLA_HEADS * GLA_DVP
MEM_QW = MEM_HEADS * MEM_DP
GLA_OFF_K = GLA_QW
GLA_OFF_V = 2 * GLA_QW
GLA_OFF_G = GLA_OFF_V + GLA_VW
GLA_OFF_LR = GLA_OFF_G + GLA_VW
GLA_OFF_MQ = GLA_OFF_LR + LANE


def _gla_in_body(h_ref, g_ref, w_ref, wg_ref, bg_ref, kv_ref, q_ref, k_ref, v_ref, go_ref, la_ref, mo_ref):
    xn = _rms(h_ref[...], g_ref[...]).astype(BF16)

    def proj(lo, n):
        return _dot(xn, w_ref[:, lo:lo + n])

    q_ref[...] = proj(0, GLA_QW).astype(BF16)
    k_ref[...] = proj(GLA_OFF_K, GLA_QW).astype(BF16)
    for j in range(GLA_VW // GLA_QW):
        v_ref[:, j * GLA_QW:(j + 1) * GLA_QW] = proj(GLA_OFF_V + j * GLA_QW, GLA_QW).astype(BF16)
        go_ref[:, j * GLA_QW:(j + 1) * GLA_QW] = proj(GLA_OFF_G + j * GLA_QW, GLA_QW).astype(BF16)
    lr = proj(GLA_OFF_LR, LANE).astype(BF16)
    z = _dot(lr, wg_ref[...]) + bg_ref[...]
    la_ref[...] = (jnp.minimum(z, 0.0) - jnp.log(1.0 + jnp.exp(-jnp.abs(z)))) * (1.0 / GLA_GATE_NORM)
    mo_ref[...] = _mem_attention(proj(GLA_OFF_MQ, MEM_QW).astype(BF16), kv_ref)


def _mem_kv_spec(mem_kv, layer, t, tm, batch):
    nt = t // batch // tm
    return pl.BlockSpec((None, mem_kv.shape[1] // batch, mem_kv.shape[2]), lambda i: (layer, i // nt, 0))


def _gla_in_proj(h, layer, glayer, g, w, wg, bg, mem_kv, batch):
    t = h.shape[0]
    tm = ROW_TILE
    row = lambda n: pl.BlockSpec((tm, n), lambda i: (i, 0))
    return pl.pallas_call(
        _gla_in_body,
        grid=(t // tm,),
        in_specs=[row(D_MODEL), _layer_of(g, layer), _layer_of(w, glayer), _layer_of(wg, glayer),
                  _layer_of(bg, glayer), _mem_kv_spec(mem_kv, layer, t, tm, batch)],
        out_specs=[row(GLA_QW), row(GLA_QW), row(GLA_VW), row(GLA_VW), row(GLA_QW), row(MEM_QW)],
        out_shape=[jax.ShapeDtypeStruct((t, GLA_QW), BF16), jax.ShapeDtypeStruct((t, GLA_QW), BF16),
                   jax.ShapeDtypeStruct((t, GLA_VW), BF16), jax.ShapeDtypeStruct((t, GLA_VW), BF16),
                   jax.ShapeDtypeStruct((t, GLA_QW), F32), jax.ShapeDtypeStruct((t, MEM_QW), BF16)],
        compiler_params=_cparams("parallel"),
        name="gla_in_proj",
    )(h, g, w, wg, bg, mem_kv)


def _gla_mix_body(q_ref, k_ref, la_ref, v_ref, go_ref, on_ref, o_ref, st_ref, inter_ref, b_scr, k_scr, v_scr, *,
                  cn, batch):
    @pl.when(pl.program_id(0) == 0)
    def _():
        st_ref[...] = jnp.zeros_like(st_ref)

    row = lax.broadcasted_iota(jnp.int32, (cn, cn), 0)
    col = lax.broadcasted_iota(jnp.int32, (cn, cn), 1)
    causal = row >= col
    tril = jnp.where(causal, 1.0, 0.0).astype(BF16)
    scale = GLA_DK ** -0.5
    chains = [(bi, hd) for bi in range(batch) for hd in range(GLA_HEADS)]
    sk = lambda hd: slice(hd * GLA_DKP, (hd + 1) * GLA_DKP)
    sv = lambda hd: slice(hd * GLA_DVP, (hd + 1) * GLA_DVP)
    dg = lambda x, y, dims: lax.dot_general(x, y, dims, preferred_element_type=F32)

    def cum_decay(bi, hd):
        la = la_ref[bi, :, sk(hd)]
        la1 = la.astype(BF16)
        r1 = la - la1.astype(F32)
        la2 = r1.astype(BF16)
        la3 = (r1 - la2.astype(F32)).astype(BF16)
        return _dot(tril, la1) + _dot(tril, la2) + _dot(tril, la3)

    def finish(i, o):
        bi, hd = chains[i]
        ms = jnp.sum(o * o, axis=-1, keepdims=True) * (1.0 / GLA_DV)
        y = o * lax.rsqrt(ms + EPS) * on_ref[...]
        g = go_ref[bi, :, sv(hd)].astype(F32)
        o_ref[bi, :, sv(hd)] = (y * (g * _sigmoid(g))).astype(BF16)

    bs = [cum_decay(bi, hd) for bi, hd in chains]
    scaled = []
    for (bi, hd), b in zip(chains, bs):
        b_last = b[cn - 1:cn, :]
        q = q_ref[bi, :, sk(hd)].astype(F32)
        k = k_ref[bi, :, sk(hd)].astype(F32)
        scaled.append(((q * jnp.exp(b) * scale).astype(BF16),
                       (k * jnp.exp(jnp.minimum(-b, GLA_SAFE_DECAY))).astype(BF16),
                       (k * jnp.exp(b_last - b)).astype(BF16), jnp.exp(b_last)))
    intra = [jnp.where(causal, dg(qs, ks, NT), 0.0).astype(BF16) for qs, ks, _, _ in scaled]
    outs = []
    for i, (bi, hd) in enumerate(chains):
        qs, _, ko, a_last = scaled[i]
        v = v_ref[bi, :, sv(hd)]
        st = st_ref[i]
        inter = dg(qs, st.astype(BF16), NT)
        inter_ref[i] = inter
        outs.append(_dot(intra[i], v) + inter)
        st_ref[i] = st * a_last + dg(v, ko, TN)
    for i, o in enumerate(outs):
        finish(i, o)

    b_min = bs[0][cn - 1:cn, :]
    for b in bs[1:]:
        b_min = jnp.minimum(b_min, b[cn - 1:cn, :])

    @pl.when(jnp.min(b_min) < -GLA_SAFE_DECAY)
    def _():
        rows = lax.broadcasted_iota(jnp.int32, (cn, 1), 0)
        for i, (bi, hd) in enumerate(chains):
            b = cum_decay(bi, hd)
            qf = q_ref[bi, :, sk(hd)].astype(F32) * scale
            b_scr[...] = b
            k_scr[...] = k_ref[bi, :, sk(hd)].astype(F32)
            v_scr[...] = v_ref[bi, :, sv(hd)].astype(F32)

            def add_key(j, acc):
                decay = jnp.exp(jnp.minimum(b - b_scr[pl.ds(j, 1), :], 0.0))
                a_col = jnp.sum(qf * k_scr[pl.ds(j, 1), :] * decay, axis=-1, keepdims=True)
                return acc + jnp.where(rows >= j, a_col, 0.0) * v_scr[pl.ds(j, 1), :]
            finish(i, lax.fori_loop(0, cn, add_key, inter_ref[i]))


def _gla_mixer(q, k, la, v, go, on, glayer, batch):
    t = q.shape[0]
    s = t // batch
    cn = GLA_CHUNK
    seq = lambda a: a.reshape(batch, s, a.shape[1])
    blk = lambda n: pl.BlockSpec((batch, cn, n), lambda c: (0, c, 0))
    out = pl.pallas_call(
        functools.partial(_gla_mix_body, cn=cn, batch=batch),
        grid=(s // cn,),
        in_specs=[blk(GLA_QW), blk(GLA_QW), blk(GLA_QW), blk(GLA_VW), blk(GLA_VW),
                  _layer_of(on, glayer)],
        out_specs=blk(GLA_VW),
        out_shape=jax.ShapeDtypeStruct((batch, s, GLA_VW), BF16),
        scratch_shapes=[pltpu.VMEM((batch * GLA_HEADS, GLA_DVP, GLA_DKP), F32),
                        pltpu.VMEM((batch * GLA_HEADS, cn, GLA_DVP), F32), pltpu.VMEM((cn, GLA_DKP), F32),
                        pltpu.VMEM((cn, GLA_DKP), F32), pltpu.VMEM((cn, GLA_DVP), F32)],
        compiler_params=_cparams("arbitrary"),
        name="gla_mixer",
    )(seq(q), seq(k), seq(la), seq(v), seq(go), on)
    return out.reshape(t, GLA_VW)


def _mem_kv_body(m_ref, g_ref, w_ref, o_ref):
    xn = _rms(m_ref[...], g_ref[...]).astype(BF16)
    o_ref[...] = _dot(xn, w_ref[...]).astype(BF16)


def _mem_kv_proj(mem2, g, w, batch):
    nl = w.shape[0]
    m = mem2.shape[0] // batch
    n = w.shape[2]
    return pl.pallas_call(
        _mem_kv_body,
        grid=(nl, batch),
        in_specs=[pl.BlockSpec((m, D_MODEL), lambda l, b: (b, 0)),
                  pl.BlockSpec((None, 1, D_MODEL), lambda l, b: (l, 0, 0)),
                  pl.BlockSpec((None, D_MODEL, n), lambda l, b: (l, 0, 0))],
        out_specs=pl.BlockSpec((None, m, n), lambda l, b: (l, b, 0)),
        out_shape=jax.ShapeDtypeStruct((nl, mem2.shape[0], n), BF16),
        compiler_params=_cparams("arbitrary", "arbitrary"),
        name="mem_kv_proj",
    )(mem2, g, w)


BF16_ROWS = 2 * SUBLANE


def _mix_ffn_body(h_ref, hp_ref, a_ref, ap_ref, m_ref, mp_ref, wa_ref, wm_ref, g_ref, wup_ref, cw_ref, cb_ref,
                  wdn_ref, gf_ref, o_ref, act_ref, *, tm, tf, last):
    g = g_ref[...]
    nr = tm // SUBLANE
    interleave = lambda x: x.reshape(SUBLANE, nr, x.shape[-1]).swapaxes(0, 1).reshape(tm, x.shape[-1])
    deinterleave = lambda x: x.reshape(nr, SUBLANE, x.shape[-1]).swapaxes(0, 1).reshape(tm, x.shape[-1])
    h = interleave(h_ref[...] + _dot(a_ref[...], wa_ref[...]) + _dot(m_ref[...], wm_ref[...]))
    x = _rms(h, g).astype(BF16)
    keep = jnp.where(pl.program_id(1) > 0, 1.0, 0.0)
    h_prev = (hp_ref[...] + _dot(ap_ref[...], wa_ref[...]) + _dot(mp_ref[...], wm_ref[...]))[BF16_ROWS - SUBLANE:]
    x_prev = (_rms(h_prev, g) * keep).astype(BF16)
    first = lax.broadcasted_iota(jnp.int32, (SUBLANE, tf), 0) == 0
    nchunk = FFN_DIM // tf

    def up(j):
        cols = [pl.ds(off + j * tf, tf) for off in (0, FFN_DIM)]
        return tuple((_dot(x, wup_ref[:, c]), _dot(x_prev, wup_ref[:, c])) for c in cols)

    def conv(u, u_prev, off):
        w = cw_ref[:, off:off + tf]
        wrap = lambda r, k: jnp.where(first, u_prev[SUBLANE - k:SUBLANE - k + 1, :],
                                      pltpu.roll(u[r * SUBLANE:(r + 1) * SUBLANE, :], 1, 0))
        back1 = jnp.concatenate([wrap(nr - 1, 1), u[0:tm - SUBLANE, :]], axis=0)
        back2 = jnp.concatenate([wrap(nr - 2, 2), wrap(nr - 1, 1), u[0:tm - 2 * SUBLANE, :]], axis=0)
        return cb_ref[:, off:off + tf] + w[0:1, :] * back2 + w[1:2, :] * back1 + w[2:3, :] * u

    u_next = up(0)
    for j in range(nchunk):
        (ua, ua_prev), (ub, ub_prev) = u_next
        if j + 1 < nchunk:
            u_next = up(j + 1)
        a = conv(ua, ua_prev, j * tf)
        b = conv(ub, ub_prev, FFN_DIM + j * tf)
        act_ref[:, j * tf:(j + 1) * tf] = (a * _sigmoid(a) * b).astype(BF16)
    out = h + _dot(act_ref[...], wdn_ref[...])
    o_ref[...] = deinterleave(_rms(out, gf_ref[...]) if last else out)


def _mix_ffn(h, main, mo, wa, klayer, wm, layer, g, wup, cw, cb, wdn, gf, last, batch):
    t = h.shape[0]
    tm = FFN_ROWS
    nt = t // batch // tm
    hb = tm // BF16_ROWS
    cur = lambda n: pl.BlockSpec((tm, n), lambda b, i: (b * nt + i, 0))
    prev = lambda n: pl.BlockSpec((BF16_ROWS, n), lambda b, i: (jnp.maximum((b * nt + i) * hb - 1, 0), 0))
    return pl.pallas_call(
        functools.partial(_mix_ffn_body, tm=tm, tf=FFN_TILE, last=last),
        grid=(batch, nt),
        in_specs=[cur(D_MODEL), prev(D_MODEL), cur(main.shape[1]), prev(main.shape[1]),
                  cur(mo.shape[1]), prev(mo.shape[1]), _layer_of(wa, klayer), _layer_of(wm, layer), _layer_of(g, layer),
                  _layer_of(wup, layer), _layer_of(cw, layer), _layer_of(cb, layer), _layer_of(wdn, layer),
                  _resident(gf)],
        out_specs=cur(D_MODEL),
        out_shape=jax.ShapeDtypeStruct((t, D_MODEL), F32),
        scratch_shapes=[pltpu.VMEM((tm, FFN_DIM), BF16)],
        compiler_params=_cparams("parallel", "parallel"),
        name="mix_ffn",
    )(h, h, main, main, mo, mo, wa, wm, g, wup, cw, cb, wdn, gf)


KV_NAT = NSA_GROUPS * NSA_HEAD_DIM


def _kv_proj_body(h_ref, g_ref, w_ref, ck_ref, cv_ref, ks_ref, kw_ref, vs_ref, vw_ref, *, tm):
    xn = _rms(h_ref[...], g_ref[...]).astype(BF16)
    ck_ref[...] = _dot(xn, w_ref[:, 0:KV_NAT]).astype(BF16)
    cv_ref[...] = _dot(xn, w_ref[:, KV_NAT:2 * KV_NAT]).astype(BF16)
    key = pl.program_id(1) * tm + lax.broadcasted_iota(jnp.int32, (tm, LANE), 0)
    lane = lax.broadcasted_iota(jnp.int32, (tm, LANE), 1)
    onehot = jnp.where(lax.shift_right_logical(key, int(math.log2(SEL_BLOCK))) == lane - NSA_HEAD_DIM, 1.0, 0.0)
    low = lane < NSA_HEAD_DIM
    for gi in range(NSA_GROUPS):
        slot = lambda n: _dot(xn, w_ref[:, 2 * KV_NAT + (n * NSA_GROUPS + gi) * LANE:
                                           2 * KV_NAT + (n * NSA_GROUPS + gi + 1) * LANE])
        ks_ref[gi] = jnp.where(low, slot(0), onehot).astype(BF16)
        kw_ref[gi] = jnp.where(low, slot(1), 0.0).astype(BF16)
        vt = slot(2).T.astype(BF16)
        ones = jnp.ones((V_ROWS - NSA_HEAD_DIM, tm), BF16)
        vs_ref[gi] = jnp.concatenate([vt[0:NSA_HEAD_DIM], ones], axis=0)
        vw_ref[gi] = jnp.concatenate([vt[NSA_HEAD_DIM:], ones], axis=0)


def _kv_proj(h, g, w, batch):
    t = h.shape[0]
    s = t // batch
    tm = ROW_TILE
    nt = s // tm
    gr = NSA_GROUPS
    return pl.pallas_call(
        functools.partial(_kv_proj_body, tm=tm),
        grid=(batch, nt),
        in_specs=[pl.BlockSpec((tm, D_MODEL), lambda b, i: (b * nt + i, 0)),
                  pl.BlockSpec(g.shape, lambda b, i: (0, 0)),
                  pl.BlockSpec(w.shape, lambda b, i: (0, 0))],
        out_specs=[pl.BlockSpec((tm, KV_NAT), lambda b, i: (b * nt + i, 0)),
                   pl.BlockSpec((tm, KV_NAT), lambda b, i: (b * nt + i, 0)),
                   pl.BlockSpec((None, gr, tm, LANE), lambda b, i: (b, 0, i, 0)),
                   pl.BlockSpec((None, gr, tm, LANE), lambda b, i: (b, 0, i, 0)),
                   pl.BlockSpec((None, gr, V_ROWS, tm), lambda b, i: (b, 0, 0, i)),
                   pl.BlockSpec((None, gr, V_ROWS, tm), lambda b, i: (b, 0, 0, i))],
        out_shape=[jax.ShapeDtypeStruct((t, KV_NAT), BF16), jax.ShapeDtypeStruct((t, KV_NAT), BF16),
                   jax.ShapeDtypeStruct((batch, gr, s, LANE), BF16),
                   jax.ShapeDtypeStruct((batch, gr, s, LANE), BF16),
                   jax.ShapeDtypeStruct((batch, gr, V_ROWS, s), BF16),
                   jax.ShapeDtypeStruct((batch, gr, V_ROWS, s), BF16)],
        compiler_params=_cparams("parallel", "parallel"),
        name="nsa_kv_proj",
    )(h, g, w)


def _compress_body(x_ref, wt_ref, wb_ref, pos_ref, w1_ref, b1_ref, w2_ref, b2_ref, on_ref, ot_ref, *, ncp):
    x = x_ref[...]
    top = _dot(x, wt_ref[...])
    bot = _dot(x, wb_ref[...])
    posb = _dot(pos_ref[...], w1_ref[...])[0:1, :] + b1_ref[...]
    for gi in range(NSA_GROUPS):
        sl = slice(gi * CMP_HIDDEN, (gi + 1) * CMP_HIDDEN)
        hid = top[:, sl] + pltpu.roll(bot[:, sl], ncp - 1, 0) + posb
        hid = (hid * _sigmoid(hid)).astype(BF16)
        out = _dot(hid, w2_ref[...]) + b2_ref[...]
        on_ref[gi] = out.astype(BF16)
        ot_ref[gi] = out.T.astype(BF16)


def _compress(x16, wt, wb, pos, w1, b1, w2, b2, batch):
    ncp = x16.shape[1] // batch
    gr = NSA_GROUPS
    per_j = lambda a: pl.BlockSpec((None,) + a.shape[1:], lambda j, b: (j,) + (0,) * (a.ndim - 1))
    return pl.pallas_call(
        functools.partial(_compress_body, ncp=ncp),
        grid=(2, batch),
        in_specs=[pl.BlockSpec((None, ncp, x16.shape[2]), lambda j, b: (j, b, 0)),
                  per_j(wt), per_j(wb), per_j(pos), per_j(w1), per_j(b1), per_j(w2), per_j(b2)],
        out_specs=[pl.BlockSpec((None, None, gr, ncp, LANE), lambda j, b: (j, b, 0, 0, 0)),
                   pl.BlockSpec((None, None, gr, LANE, ncp), lambda j, b: (j, b, 0, 0, 0))],
        out_shape=[jax.ShapeDtypeStruct((2, batch, gr, ncp, LANE), BF16),
                   jax.ShapeDtypeStruct((2, batch, gr, LANE, ncp), BF16)],
        compiler_params=_cparams("arbitrary", "arbitrary"),
        name="nsa_compress",
    )(x16, wt, wb, pos, w1, b1, w2, b2)


NSA_QW = NSA_HEADS * NSA_HEAD_DIM
G3 = NSA_REP * TQ


def _nsa_in_body(h_ref, g_ref, w_ref, kv_ref, qt_ref, gt_ref, mo_ref):
    xn = _rms(h_ref[...], g_ref[...]).astype(BF16)
    for j in range(NSA_QW // LANE):
        y = _dot(xn, w_ref[:, j * LANE:(j + 1) * LANE]) * (NSA_HEAD_DIM ** -0.5 * LOG2E)
        qt_ref[j * LANE:(j + 1) * LANE, :] = y.T.astype(BF16)
    gt_ref[...] = _sigmoid(_dot(xn, w_ref[:, NSA_QW:NSA_QW + LANE])).T
    mo_ref[...] = _mem_attention(_dot(xn, w_ref[:, NSA_QW + LANE:]).astype(BF16), kv_ref)


def _nsa_in_proj(h, layer, nlayer, g, w, mem_kv, batch):
    t = h.shape[0]
    tm = ROW_TILE
    row = lambda n: pl.BlockSpec((tm, n), lambda i: (i, 0))
    nt = t // batch // tm
    col = lambda n: pl.BlockSpec((None, n, tm), lambda i: (i // nt, 0, i % nt))
    return pl.pallas_call(
        _nsa_in_body,
        grid=(t // tm,),
        in_specs=[row(D_MODEL), _layer_of(g, layer), _layer_of(w, nlayer), _mem_kv_spec(mem_kv, layer, t, tm, batch)],
        out_specs=[col(NSA_QW), col(LANE), row(MEM_QW)],
        out_shape=[jax.ShapeDtypeStruct((batch, NSA_QW, t // batch), BF16),
                   jax.ShapeDtypeStruct((batch, LANE, t // batch), F32),
                   jax.ShapeDtypeStruct((t, MEM_QW), BF16)],
        compiler_params=_cparams("parallel"),
        name="nsa_in_proj",
    )(h, g, w, mem_kv)


def _nsa_attn_body(qt_ref, gt_ref, kc_ref, vct_ref, ks_ref, kw_ref, vs_ref, vw_ref, ovt_ref, tz_ref, cb_ref,
                   o_ref, sc_ref, qa_ref, acc_ref, ot_ref, *s_slots, nsb, ncp, n_sel, nseq):
    n = pl.program_id(1)
    t0 = n * TQ
    dh = NSA_HEAD_DIM
    cstart = pl.multiple_of(ncp - n * (TQ // CMP_STRIDE), SUBLANE)
    sees_any = t0 + (lax.broadcasted_iota(jnp.int32, (1, G3), 1) & (TQ - 1)) >= CMP_BLOCK - 1
    jj = lax.broadcasted_iota(jnp.int32, (nsb, TQ), 0)
    cur = lax.shift_right_logical(t0 + lax.broadcasted_iota(jnp.int32, (nsb, TQ), 1), int(math.log2(SEL_BLOCK)))
    forced = (jj == 0) | (jj == cur) | (jj == cur - 1)
    zpad = jnp.zeros((dh, G3), BF16)
    nwt = WINDOW // TQ
    units = [(bi, gi) for bi in range(nseq) for gi in range(NSA_GROUPS)]

    def scores(k_ref, u, first, count, tz_index, qa):
        bi, gi = units[u]
        koff = pl.multiple_of(first * TQ, TQ)
        s = _dot(k_ref[bi, gi, pl.ds(koff, count * TQ), :], qa)
        return [s[i * TQ:(i + 1) * TQ] + tz_ref[gi, tz_index(n - (first + i))] for i in range(count)]

    def col_max(parts):
        mx = parts[0]
        for x in parts[1:]:
            mx = jnp.maximum(mx, x)
        return jnp.max(mx, axis=0, keepdims=True)

    def probs(parts, m):
        return jnp.concatenate([jnp.exp2(x - m).astype(BF16) for x in parts], axis=0)

    def values(v_ref, u, first, count, p):
        bi, gi = units[u]
        koff = pl.multiple_of(first * TQ, TQ)
        return _dot(v_ref[bi, gi, :, pl.ds(koff, count * TQ)], p)

    sel_index = lambda d: jnp.where(d < 0, 2, jnp.minimum(d, 2))
    win_index = lambda d: jnp.where(d < 0, 4, jnp.where(d == nwt, 3, jnp.minimum(d, 2)))

    groups = range(len(units))
    gate = lambda bi, h, c: gt_ref[bi, pl.ds(h * 3 + c, 1), :]
    head_lanes = lambda r: slice(r * TQ, (r + 1) * TQ)
    head_rows = lambda bi, h: (bi, slice(h * dh, (h + 1) * dh))

    q3s = [jnp.concatenate([qt_ref[head_rows(bi, gi * NSA_REP + r)] for r in range(NSA_REP)], axis=1)
           for bi, gi in units]
    cs = [_dot(kc_ref[bi, gi], jnp.concatenate([q3s[u], zpad], axis=0)) + cb_ref[gi, pl.ds(cstart, ncp), :]
          for u, (bi, gi) in enumerate(units)]
    cps = [jnp.exp2(s - jnp.max(s, axis=0, keepdims=True)) for s in cs]
    cps = [p * jnp.where(sees_any, 1.0 / jnp.sum(p, axis=0, keepdims=True), 0.0) for p in cps]
    ocs = [_dot(vct_ref[bi, gi], cps[u].astype(BF16)) for u, (bi, gi) in enumerate(units)]
    scs = []
    for gi in groups:
        psum = cps[gi][:, 0:TQ]
        for r in range(1, NSA_REP):
            psum = psum + cps[gi][:, head_lanes(r)]
        p1 = psum.astype(BF16)
        p2 = (psum - p1.astype(F32)).astype(BF16)
        imp = _dot(ovt_ref[...], p1) + _dot(ovt_ref[...], p2)
        score = jnp.where(forced, 1e4, jnp.where(jj <= cur, imp[0:nsb], -1.0))
        scs.append(jnp.where(score < 0.0, -1, lax.bitcast_convert_type(score, jnp.int32)))
    for u, (bi, gi) in enumerate(units):
        sc_ref[u] = scs[u]
        for r in range(NSA_REP):
            h = gi * NSA_REP + r
            ot_ref[head_rows(bi, h)] = gate(bi, h, 0) * ocs[u][0:dh, head_lanes(r)]

    scs1 = [k + 1 for k in scs]

    def rank_step(i4, cnts):
        cnts = list(cnts)
        for u in range(RANK_UNROLL):
            i = i4 * RANK_UNROLL + u
            lower = i < jj
            for gi in groups:
                rowk = sc_ref[gi, pl.ds(i, 1), :]
                before = rowk >= jnp.where(lower, scs[gi], scs1[gi])
                cnts[gi] = cnts[gi] + jnp.where(before, 1, 0)
        return tuple(cnts)
    rank_trips = jnp.minimum((2 * n + 2 + RANK_UNROLL - 1) // RANK_UNROLL, nsb // RANK_UNROLL)
    cnts = lax.fori_loop(0, rank_trips, rank_step, tuple(jnp.zeros((nsb, TQ), jnp.int32) for _ in groups))
    for gi in groups:
        selneg = jnp.where((cnts[gi] < n_sel) & (jj <= cur), 0.0, NEG).astype(BF16)
        if nsb < SEL_BLOCK:
            selneg = jnp.concatenate([selneg, jnp.zeros((SEL_BLOCK - nsb, TQ), BF16)], axis=0)
        qa_ref[gi] = jnp.concatenate([q3s[gi], jnp.concatenate([selneg] * NSA_REP, axis=1)], axis=0)

    acc_ref[...] = jnp.zeros_like(acc_ref)
    half = len(units) // 2
    s_refs = (s_slots[:half], s_slots[half:])

    def score_half(it, hb):
        plist = [scores(ks_ref, hb * half + u, it * SEL_TILES, SEL_TILES, sel_index, qa_ref[hb * half + u])
                 for u in range(half)]
        for u, parts in enumerate(plist):
            for i, x in enumerate(parts):
                s_refs[hb][u][i * TQ:(i + 1) * TQ, :] = x
        return [col_max(parts) for parts in plist]

    def finish_half(it, hb, ms, bms):
        m2s = [jnp.maximum(ms[u], bms[u]) for u in range(half)]
        ps = [probs([s_refs[hb][u][i * TQ:(i + 1) * TQ, :] for i in range(SEL_TILES)], m2s[u]) for u in range(half)]
        vals = [values(vs_ref, hb * half + u, it * SEL_TILES, SEL_TILES, ps[u]) for u in range(half)]
        for u in range(half):
            gi = hb * half + u
            acc_ref[gi] = jnp.exp2(ms[u] - m2s[u]) * acc_ref[gi] + vals[u]
        return m2s

    def sel_trip(it, carry, score_next):
        ms0, ms1, bm0 = carry
        bm1 = score_half(it, 1)
        ms0 = finish_half(it, 0, ms0, bm0)
        if score_next:
            bm0 = score_half(it + 1, 0)
        ms1 = finish_half(it, 1, ms1, bm1)
        return ms0, ms1, bm0

    neg = [jnp.full((1, G3), NEG, F32) for _ in range(half)]
    carry = (neg, neg, score_half(0, 0))

    wfirst = jnp.maximum(n - nwt, 0)
    wparts = [scores(kw_ref, gi, wfirst, nwt + 1, win_index, qa_ref[gi]) for gi in groups]
    wps = [probs(parts, col_max(parts)) for parts in wparts]
    ows = [values(vw_ref, gi, wfirst, nwt + 1, wps[gi]) for gi in groups]
    for u, (bi, gi) in enumerate(units):
        o_w = ows[u][0:dh, :] * (1.0 / ows[u][dh:dh + 1, :])
        for r in range(NSA_REP):
            h = gi * NSA_REP + r
            ot_ref[head_rows(bi, h)] += gate(bi, h, 2) * o_w[:, head_lanes(r)]

    carry = lax.fori_loop(0, n // SEL_TILES, lambda it, c: sel_trip(it, c, True), carry)
    sel_trip(n // SEL_TILES, carry, False)

    for u, (bi, gi) in enumerate(units):
        o_s = acc_ref[u, 0:dh, :] * (1.0 / acc_ref[u, dh:dh + 1, :])
        for r in range(NSA_REP):
            h = gi * NSA_REP + r
            ot_ref[head_rows(bi, h)] += gate(bi, h, 1) * o_s[:, head_lanes(r)]

    for bi in range(nseq):
        for j in range(NSA_QW // LANE):
            o_ref[bi, :, j * LANE:(j + 1) * LANE] = ot_ref[bi, j * LANE:(j + 1) * LANE, :].T.astype(BF16)


def _nsa_attn(qt, gt, kc, vct, ksel, kwin, vsel, vwin, ovt, tz, cb):
    batch, _, s = qt.shape
    nseq = NSA_SEQS if batch % NSA_SEQS == 0 else 1
    nt = s // TQ
    nsb = s // SEL_BLOCK
    ncp = kc.shape[-2]
    nu = nseq * NSA_GROUPS
    per_b = lambda a: pl.BlockSpec((nseq,) + a.shape[1:], lambda b, i: (b,) + (0,) * (a.ndim - 1),
                                   pipeline_mode=pl.Buffered(1))
    out = pl.pallas_call(
        functools.partial(_nsa_attn_body, nsb=nsb, ncp=ncp, n_sel=min(SEL_TOPK, nsb), nseq=nseq),
        grid=(batch // nseq, nt),
        in_specs=[pl.BlockSpec((nseq, NSA_QW, TQ), lambda b, i: (b, 0, i)),
                  pl.BlockSpec((nseq, LANE, TQ), lambda b, i: (b, 0, i)),
                  per_b(kc), per_b(vct), per_b(ksel), per_b(kwin), per_b(vsel), per_b(vwin),
                  _resident(ovt), _resident(tz), _resident(cb)],
        out_specs=pl.BlockSpec((nseq, TQ, NSA_QW), lambda b, i: (b, i, 0)),
        out_shape=jax.ShapeDtypeStruct((batch, s, NSA_QW), BF16),
        scratch_shapes=[pltpu.VMEM((nu, nsb, TQ), jnp.int32), pltpu.VMEM((nu, LANE, G3), BF16),
                        pltpu.VMEM((nu, V_ROWS, G3), F32), pltpu.VMEM((nseq, NSA_QW, TQ), F32),
                        ] + [pltpu.VMEM((SEL_TILES * TQ, G3), F32)] * nu,
        compiler_params=_cparams("parallel", "arbitrary"),
        name="nsa_attn",
    )(qt, gt, kc, vct, ksel, kwin, vsel, vwin, ovt, tz, cb)
    return out.reshape(batch * s, NSA_QW)


def _rel_bucket_np(dist):
    dist = np.maximum(dist, 0)
    max_exact = REL_BUCKETS // 2
    ratio = np.log(np.maximum(dist, 1).astype(np.float32) / np.float32(max_exact)) / np.float32(
        math.log(REL_MAX_DIST / max_exact))
    large = max_exact + (ratio * np.float32(REL_BUCKETS - max_exact)).astype(np.int32)
    large = np.minimum(large, REL_BUCKETS - 1)
    return np.where(dist < max_exact, dist, large).astype(np.int32)


def _group_lanes(a):
    hh, r, c = a.shape
    return a.reshape(NSA_GROUPS, NSA_REP, r, c).transpose(0, 2, 1, 3).reshape(NSA_GROUPS, r, NSA_REP * c)


def _bias_tables(rel_bias, ncp):
    k = np.arange(TQ)[:, None]
    q = np.arange(TQ)[None, :]
    tbl = rel_bias.astype(F32)

    def lookup(idx):
        onehot = (jnp.asarray(idx.reshape(1, -1)) == jnp.arange(REL_BUCKETS)[:, None]).astype(F32)
        out = jnp.dot(tbl.T, onehot, precision=lax.Precision.HIGHEST)
        return out.reshape((NSA_HEADS,) + idx.shape)

    far = jnp.broadcast_to(tbl[REL_BUCKETS - 1][:, None, None], (NSA_HEADS, TQ, TQ))
    t0 = jnp.where(jnp.asarray(k <= q)[None], lookup(_rel_bucket_np(q - k)), NEG)
    t1 = lookup(_rel_bucket_np(TQ + q - k))
    t3 = jnp.where(jnp.asarray(k > q)[None], far, NEG)
    tz = jnp.stack([_group_lanes(x) for x in (t0, t1, far, t3, jnp.full_like(far, NEG))], axis=1)
    m = ncp - np.arange(2 * ncp)[:, None]
    d = CMP_STRIDE * m + q - (CMP_BLOCK - 1)
    idx = np.where((d >= 0) & (d < REL_MAX_DIST), _rel_bucket_np(d), REL_BUCKETS - 1)
    cb = _group_lanes(jnp.where(jnp.asarray(d >= 0)[None], lookup(idx), NEG))
    return tz * LOG2E, cb * LOG2E


def _overlap_table(s, ncp):
    nsb = s // SEL_BLOCK
    nc = (s - CMP_BLOCK) // CMP_STRIDE + 1
    cs = np.arange(ncp) * CMP_STRIDE
    ce = cs + CMP_BLOCK - 1
    ss = np.arange(SEL_BLOCK) * SEL_BLOCK
    ov = (cs[None, :] < ss[:, None] + SEL_BLOCK) & (ce[None, :] >= ss[:, None])
    ov &= (np.arange(ncp) < nc)[None, :] & (np.arange(SEL_BLOCK) < nsb)[:, None]
    return jnp.asarray(ov, dtype=BF16)


def kernel(x, mem, norm_mix, norm_mem, w_mem_kv, w_out, norm_ffn, w_up, conv_w, conv_b, w_down,
           gla_w_in, gla_w_gate_up, gla_b_gate, gla_out_norm, nsa_w_in, kv_norm, w_kv_shared,
           cmp_pos, cmp_w1, cmp_b1, cmp_w2, cmp_b2, rel_bias, final_norm):
    batch, seq = x.shape[0], x.shape[1]
    t = batch * seq
    h = x.reshape(t, D_MODEL)
    row = lambda v: v.reshape(1, -1).astype(F32)

    wk, wv = w_mem_kv[..., :MEM_W], w_mem_kv[..., MEM_W:]
    w_mkv = jnp.concatenate([_pad_heads(wk, MEM_HEADS, MEM_HEAD_DIM, MEM_DP),
                             _pad_heads(wv, MEM_HEADS, MEM_HEAD_DIM, MEM_DP)], axis=-1).astype(BF16)
    mem_kv_all = _mem_kv_proj(mem.reshape(-1, D_MODEL), norm_mem.reshape(DEPTH, 1, D_MODEL), w_mkv, batch)

    ffn_params = (norm_ffn.reshape(DEPTH, 1, D_MODEL).astype(F32), w_up.astype(BF16), conv_w.astype(F32),
                  conv_b.reshape(DEPTH, 1, 2 * FFN_DIM).astype(F32), w_down.astype(BF16))
    norm_mix3 = norm_mix.reshape(DEPTH, 1, D_MODEL).astype(F32)
    w_o_mem = _pad_head_rows(w_out[:, MAIN_W:], MEM_HEADS, MEM_HEAD_DIM, MEM_DP).astype(BF16)
    mem_pad = lambda w: _pad_heads(w, MEM_HEADS, MEM_HEAD_DIM, MEM_DP)

    c0 = GLA_HEADS * GLA_DK
    c1 = 2 * c0
    c2 = c1 + GLA_HEADS * GLA_DV
    c3 = c2 + GLA_HEADS * GLA_DV
    c4 = c3 + GLA_RANK
    gla_w = jnp.concatenate([
        _pad_heads(gla_w_in[..., :c0], GLA_HEADS, GLA_DK, GLA_DKP),
        _pad_heads(gla_w_in[..., c0:c1], GLA_HEADS, GLA_DK, GLA_DKP),
        _pad_heads(gla_w_in[..., c1:c2], GLA_HEADS, GLA_DV, GLA_DVP),
        _pad_heads(gla_w_in[..., c2:c3], GLA_HEADS, GLA_DV, GLA_DVP),
        _pad_heads(gla_w_in[..., c3:c4], 1, GLA_RANK, LANE),
        mem_pad(gla_w_in[..., c4:])], axis=-1).astype(BF16)
    gla_wg = jnp.pad(_pad_heads(gla_w_gate_up, GLA_HEADS, GLA_DK, GLA_DKP),
                     ((0, 0), (0, LANE - GLA_RANK), (0, 0))).astype(BF16)
    gla_bg = _pad_heads(gla_b_gate, GLA_HEADS, GLA_DK, GLA_DKP).reshape(N_A_LAYERS, 1, GLA_QW).astype(F32)
    gla_on = jnp.pad(gla_out_norm, ((0, 0), (0, GLA_DVP - GLA_DV))).reshape(N_A_LAYERS, 1, GLA_DVP).astype(F32)
    gla_wo = _pad_head_rows(w_out[:N_A_LAYERS, :MAIN_W], GLA_HEADS, GLA_DV, GLA_DVP).astype(BF16)

    n0 = NSA_HEADS * NSA_HEAD_DIM
    n1 = n0 + NSA_HEADS * 3
    nsa_w = jnp.concatenate([nsa_w_in[..., :n0], _pad_heads(nsa_w_in[..., n0:n1], 1, NSA_HEADS * 3, LANE),
                             mem_pad(nsa_w_in[..., n1:])], axis=-1).astype(BF16)
    nsa_wo = w_out[N_A_LAYERS:, :MAIN_W].astype(BF16)

    shared = None
    for i in range(DEPTH):
        if i < N_A_LAYERS:
            q, k, v, go, la, mo = _gla_in_proj(h, i, i, norm_mix3, gla_w, gla_wg, gla_bg, mem_kv_all, batch)
            main = _gla_mixer(q, k, la, v, go, gla_on, i, batch)
            w_o_main, klayer = gla_wo, i
        else:
            if shared is None:
                ncp = seq // CMP_STRIDE
                gr, dh = NSA_GROUPS, NSA_HEAD_DIM
                wkv = w_kv_shared.reshape(D_MODEL, 6, gr, dh)
                pair = lambda a, b: jnp.concatenate([wkv[:, a], wkv[:, b]], axis=-1).reshape(D_MODEL, gr * 2 * dh)
                slot = lambda a: _pad_heads(wkv[:, a].reshape(D_MODEL, KV_NAT), gr, dh, LANE)
                w_kv = jnp.concatenate([wkv[:, 0].reshape(D_MODEL, KV_NAT), wkv[:, 1].reshape(D_MODEL, KV_NAT),
                                        slot(2), slot(4), pair(3, 5)], axis=1).astype(BF16)
                ck, cv, ksel, kwin, vsel, vwin = _kv_proj(h, row(kv_norm), w_kv, batch)
                x16 = jnp.stack([ck.reshape(batch * ncp, CMP_STRIDE * KV_NAT),
                                 cv.reshape(batch * ncp, CMP_STRIDE * KV_NAT)])
                w1 = cmp_w1.reshape(2, 2, CMP_STRIDE, dh, CMP_HIDDEN)
                eye = jnp.eye(gr, dtype=F32)
                w1x = jnp.einsum('jhldc,gk->jhlgdkc', w1, eye).reshape(2, 2, CMP_STRIDE * KV_NAT, gr * CMP_HIDDEN)
                w1x = w1x.astype(BF16)
                pos8 = jnp.broadcast_to(cmp_pos.reshape(2, 1, CMP_BLOCK * dh), (2, SUBLANE, CMP_BLOCK * dh)).astype(BF16)
                w2p = jnp.pad(cmp_w2, ((0, 0), (0, 0), (0, LANE - dh))).astype(BF16)
                b2p = jnp.pad(cmp_b2, ((0, 0), (0, LANE - dh))).reshape(2, 1, LANE).astype(F32)
                cnat, ctr = _compress(x16, w1x[:, 0], w1x[:, 1], pos8, cmp_w1.astype(BF16),
                                      cmp_b1.reshape(2, 1, CMP_HIDDEN).astype(F32), w2p, b2p, batch)
                tz, cb = _bias_tables(rel_bias, ncp)
                ov = _overlap_table(seq, ncp)
                shared = (cnat[0], ctr[1], ksel, kwin, vsel, vwin, ov, tz, cb)
            q, gates, mo = _nsa_in_proj(h, i, i - N_A_LAYERS, norm_mix3, nsa_w, mem_kv_all, batch)
            main = _nsa_attn(q, gates, *shared)
            w_o_main, klayer = nsa_wo, i - N_A_LAYERS
        h = _mix_ffn(h, main, mo, w_o_main, klayer, w_o_mem, i, *ffn_params, row(final_norm), i == DEPTH - 1, batch)
    return h.reshape(batch, seq, D_MODEL)
```

```python
import functools
import math

import numpy as np
import jax
import jax.numpy as jnp
from jax import lax
from jax.experimental import pallas as pl
from jax.experimental.pallas import tpu as pltpu

F32 = jnp.float32
BF16 = jnp.bfloat16

D_MODEL = 1024
DEPTH = 4
N_A_LAYERS = DEPTH // 2
MEM_HEADS = 4
MEM_HEAD_DIM = 64
MEM_W = MEM_HEADS * MEM_HEAD_DIM
MAIN_W = D_MODEL - MEM_W
GLA_HEADS = 4
GLA_DV = MAIN_W // GLA_HEADS
GLA_DK = GLA_DV // 2
GLA_RANK = 16
GLA_GATE_NORM = 16.0
NSA_HEADS = 12
NSA_GROUPS = 4
NSA_HEAD_DIM = MAIN_W // NSA_HEADS
NSA_REP = NSA_HEADS // NSA_GROUPS
NSA_BRANCHES = 3
CMP_BLOCK = 32
CMP_STRIDE = 16
CMP_HIDDEN = 128
SEL_BLOCK = 64
SEL_TOPK = 16
WINDOW = 512
REL_BUCKETS = 32
REL_MAX_DIST = 128
FFN_DIM = 2816
CONV_WIDTH = 3
EPS = 1e-6

LANE = 128
SUBLANE = 8
VMEM_LIMIT = 56 * 1024 * 1024
GLA_DKP = LANE
GLA_DVP = 2 * LANE
MEM_DP = LANE
NEG = -1e30
TQ = 128
ROW_TILE = 1024
GLA_CHUNK = 64
GLA_SAFE_DECAY = 80.0
FFN_TILE = 256
FFN_ROWS = 1024
SEL_TILES = 4
NSA_SEQS = 2
V_ROWS = NSA_HEAD_DIM + 2 * SUBLANE
RANK_UNROLL = 4
LOG2E = math.log2(math.e)

NT = (((1,), (1,)), ((), ()))
TN = (((0,), (0,)), ((), ()))


def _cparams(*sem):
    return pltpu.CompilerParams(dimension_semantics=sem, vmem_limit_bytes=VMEM_LIMIT)


def _rms(x, g):
    return x * lax.rsqrt(jnp.mean(x * x, axis=-1, keepdims=True) + EPS) * g


def _sigmoid(x):
    return 1.0 / (1.0 + jnp.exp(-x))


def _dot(a, b):
    return jnp.dot(a, b, preferred_element_type=F32)


def _resident(a):
    return pl.BlockSpec(a.shape, lambda *_: (0,) * a.ndim, pipeline_mode=pl.Buffered(1))


def _layer_of(a, layer):
    return pl.BlockSpec((None,) + a.shape[1:], lambda *_: (layer,) + (0,) * (a.ndim - 1),
                        pipeline_mode=pl.Buffered(1))


def _pad_heads(w, nh, d, dp):
    lead = w.shape[:-1]
    w = w.reshape(lead + (nh, d))
    w = jnp.pad(w, [(0, 0)] * len(lead) + [(0, 0), (0, dp - d)])
    return w.reshape(lead + (nh * dp,))


def _pad_head_rows(w, nh, d, dp):
    lead, n = w.shape[:-2], w.shape[-1]
    w = jnp.pad(w.reshape(lead + (nh, d, n)), [(0, 0)] * len(lead) + [(0, 0), (0, dp - d), (0, 0)])
    return w.reshape(lead + (nh * dp, n))


def _mem_attention(q, kv_ref):
    outs = []
    for hd in range(MEM_HEADS):
        sl = slice(hd * MEM_DP, (hd + 1) * MEM_DP)
        sv = slice(MEM_QW + hd * MEM_DP, MEM_QW + (hd + 1) * MEM_DP)
        s = lax.dot_general(q[:, sl], kv_ref[:, sl], NT, preferred_element_type=F32) * MEM_HEAD_DIM ** -0.5
        p = jnp.exp(s - jnp.max(s, axis=-1, keepdims=True))
        l = jnp.sum(p, axis=-1, keepdims=True)
        outs.append((_dot(p.astype(BF16), kv_ref[:, sv]) / l).astype(BF16))
    return jnp.concatenate(outs, axis=1)


GLA_QW = GLA_HEADS * GLA_DKP
GLA_VW = GLA_HEADS * GLA_DVP
MEM_QW = MEM_HEADS * MEM_DP
GLA_OFF_K = GLA_QW
GLA_OFF_V = 2 * GLA_QW
GLA_OFF_G = GLA_OFF_V + GLA_VW
GLA_OFF_LR = GLA_OFF_G + GLA_VW
GLA_OFF_MQ = GLA_OFF_LR + LANE


def _gla_in_body(h_ref, g_ref, w_ref, wg_ref, bg_ref, kv_ref, q_ref, k_ref, v_ref, go_ref, la_ref, mo_ref):
    xn = _rms(h_ref[...], g_ref[...]).astype(BF16)

    def proj(lo, n):
        return _dot(xn, w_ref[:, lo:lo + n])

    q_ref[...] = proj(0, GLA_QW).astype(BF16)
    k_ref[...] = proj(GLA_OFF_K, GLA_QW).astype(BF16)
    for j in range(GLA_VW // GLA_QW):
        v_ref[:, j * GLA_QW:(j + 1) * GLA_QW] = proj(GLA_OFF_V + j * GLA_QW, GLA_QW).astype(BF16)
        go_ref[:, j * GLA_QW:(j + 1) * GLA_QW] = proj(GLA_OFF_G + j * GLA_QW, GLA_QW).astype(BF16)
    lr = proj(GLA_OFF_LR, LANE).astype(BF16)
    z = _dot(lr, wg_ref[...]) + bg_ref[...]
    la_ref[...] = (jnp.minimum(z, 0.0) - jnp.log(1.0 + jnp.exp(-jnp.abs(z)))) * (1.0 / GLA_GATE_NORM)
    mo_ref[...] = _mem_attention(proj(GLA_OFF_MQ, MEM_QW).astype(BF16), kv_ref)


def _mem_kv_spec(mem_kv, layer, t, tm, batch):
    nt = t // batch // tm
    return pl.BlockSpec((None, mem_kv.shape[1] // batch, mem_kv.shape[2]), lambda i: (layer, i // nt, 0))


def _gla_in_proj(h, layer, glayer, g, w, wg, bg, mem_kv, batch):
    t = h.shape[0]
    tm = ROW_TILE
    row = lambda n: pl.BlockSpec((tm, n), lambda i: (i, 0))
    return pl.pallas_call(
        _gla_in_body,
        grid=(t // tm,),
        in_specs=[row(D_MODEL), _layer_of(g, layer), _layer_of(w, glayer), _layer_of(wg, glayer),
                  _layer_of(bg, glayer), _mem_kv_spec(mem_kv, layer, t, tm, batch)],
        out_specs=[row(GLA_QW), row(GLA_QW), row(GLA_VW), row(GLA_VW), row(GLA_QW), row(MEM_QW)],
        out_shape=[jax.ShapeDtypeStruct((t, GLA_QW), BF16), jax.ShapeDtypeStruct((t, GLA_QW), BF16),
                   jax.ShapeDtypeStruct((t, GLA_VW), BF16), jax.ShapeDtypeStruct((t, GLA_VW), BF16),
                   jax.ShapeDtypeStruct((t, GLA_QW), F32), jax.ShapeDtypeStruct((t, MEM_QW), BF16)],
        compiler_params=_cparams("parallel"),
        name="gla_in_proj",
    )(h, g, w, wg, bg, mem_kv)


def _gla_mix_body(q_ref, k_ref, la_ref, v_ref, go_ref, on_ref, o_ref, st_ref, inter_ref, b_scr, k_scr, v_scr, *,
                  cn, batch):
    @pl.when(pl.program_id(0) == 0)
    def _():
        st_ref[...] = jnp.zeros_like(st_ref)

    row = lax.broadcasted_iota(jnp.int32, (cn, cn), 0)
    col = lax.broadcasted_iota(jnp.int32, (cn, cn), 1)
    causal = row >= col
    tril = jnp.where(causal, 1.0, 0.0).astype(BF16)
    scale = GLA_DK ** -0.5
    chains = [(bi, hd) for bi in range(batch) for hd in range(GLA_HEADS)]
    sk = lambda hd: slice(hd * GLA_DKP, (hd + 1) * GLA_DKP)
    sv = lambda hd: slice(hd * GLA_DVP, (hd + 1) * GLA_DVP)
    dg = lambda x, y, dims: lax.dot_general(x, y, dims, preferred_element_type=F32)

    def cum_decay(bi, hd):
        la = la_ref[bi, :, sk(hd)]
        la1 = la.astype(BF16)
        r1 = la - la1.astype(F32)
        la2 = r1.astype(BF16)
        la3 = (r1 - la2.astype(F32)).astype(BF16)
        return _dot(tril, la1) + _dot(tril, la2) + _dot(tril, la3)

    def finish(i, o):
        bi, hd = chains[i]
        ms = jnp.sum(o * o, axis=-1, keepdims=True) * (1.0 / GLA_DV)
        y = o * lax.rsqrt(ms + EPS) * on_ref[...]
        g = go_ref[bi, :, sv(hd)].astype(F32)
        o_ref[bi, :, sv(hd)] = (y * (g * _sigmoid(g))).astype(BF16)

    bs = [cum_decay(bi, hd) for bi, hd in chains]
    scaled = []
    for (bi, hd), b in zip(chains, bs):
        b_last = b[cn - 1:cn, :]
        q = q_ref[bi, :, sk(hd)].astype(F32)
        k = k_ref[bi, :, sk(hd)].astype(F32)
        scaled.append(((q * jnp.exp(b) * scale).astype(BF16),
                       (k * jnp.exp(jnp.minimum(-b, GLA_SAFE_DECAY))).astype(BF16),
                       (k * jnp.exp(b_last - b)).astype(BF16), jnp.exp(b_last)))
    intra = [jnp.where(causal, dg(qs, ks, NT), 0.0).astype(BF16) for qs, ks, _, _ in scaled]
    outs = []
    for i, (bi, hd) in enumerate(chains):
        qs, _, ko, a_last = scaled[i]
        v = v_ref[bi, :, sv(hd)]
        st = st_ref[i]
        inter = dg(qs, st.astype(BF16), NT)
        inter_ref[i] = inter
        outs.append(_dot(intra[i], v) + inter)
        st_ref[i] = st * a_last + dg(v, ko, TN)
    for i, o in enumerate(outs):
        finish(i, o)

    b_min = bs[0][cn - 1:cn, :]
    for b in bs[1:]:
        b_min = jnp.minimum(b_min, b[cn - 1:cn, :])

    @pl.when(jnp.min(b_min) < -GLA_SAFE_DECAY)
    def _():
        rows = lax.broadcasted_iota(jnp.int32, (cn, 1), 0)
        for i, (bi, hd) in enumerate(chains):
            b = cum_decay(bi, hd)
            qf = q_ref[bi, :, sk(hd)].astype(F32) * scale
            b_scr[...] = b
            k_scr[...] = k_ref[bi, :, sk(hd)].astype(F32)
            v_scr[...] = v_ref[bi, :, sv(hd)].astype(F32)

            def add_key(j, acc):
                decay = jnp.exp(jnp.minimum(b - b_scr[pl.ds(j, 1), :], 0.0))
                a_col = jnp.sum(qf * k_scr[pl.ds(j, 1), :] * decay, axis=-1, keepdims=True)
                return acc + jnp.where(rows >= j, a_col, 0.0) * v_scr[pl.ds(j, 1), :]
            finish(i, lax.fori_loop(0, cn, add_key, inter_ref[i]))


def _gla_mixer(q, k, la, v, go, on, glayer, batch):
    t = q.shape[0]
    s = t // batch
    cn = GLA_CHUNK
    seq = lambda a: a.reshape(batch, s, a.shape[1])
    blk = lambda n: pl.BlockSpec((batch, cn, n), lambda c: (0, c, 0))
    out = pl.pallas_call(
        functools.partial(_gla_mix_body, cn=cn, batch=batch),
        grid=(s // cn,),
        in_specs=[blk(GLA_QW), blk(GLA_QW), blk(GLA_QW), blk(GLA_VW), blk(GLA_VW),
                  _layer_of(on, glayer)],
        out_specs=blk(GLA_VW),
        out_shape=jax.ShapeDtypeStruct((batch, s, GLA_VW), BF16),
        scratch_shapes=[pltpu.VMEM((batch * GLA_HEADS, GLA_DVP, GLA_DKP), F32),
                        pltpu.VMEM((batch * GLA_HEADS, cn, GLA_DVP), F32), pltpu.VMEM((cn, GLA_DKP), F32),
                        pltpu.VMEM((cn, GLA_DKP), F32), pltpu.VMEM((cn, GLA_DVP), F32)],
        compiler_params=_cparams("arbitrary"),
        name="gla_mixer",
    )(seq(q), seq(k), seq(la), seq(v), seq(go), on)
    return out.reshape(t, GLA_VW)


def _mem_kv_body(m_ref, g_ref, w_ref, o_ref):
    xn = _rms(m_ref[...], g_ref[...]).astype(BF16)
    o_ref[...] = _dot(xn, w_ref[...]).astype(BF16)


def _mem_kv_proj(mem2, g, w, batch):
    nl = w.shape[0]
    m = mem2.shape[0] // batch
    n = w.shape[2]
    return pl.pallas_call(
        _mem_kv_body,
        grid=(nl, batch),
        in_specs=[pl.BlockSpec((m, D_MODEL), lambda l, b: (b, 0)),
                  pl.BlockSpec((None, 1, D_MODEL), lambda l, b: (l, 0, 0)),
                  pl.BlockSpec((None, D_MODEL, n), lambda l, b: (l, 0, 0))],
        out_specs=pl.BlockSpec((None, m, n), lambda l, b: (l, b, 0)),
        out_shape=jax.ShapeDtypeStruct((nl, mem2.shape[0], n), BF16),
        compiler_params=_cparams("arbitrary", "arbitrary"),
        name="mem_kv_proj",
    )(mem2, g, w)


BF16_ROWS = 2 * SUBLANE


def _mix_ffn_body(h_ref, hp_ref, a_ref, ap_ref, m_ref, mp_ref, wa_ref, wm_ref, g_ref, wup_ref, cw_ref, cb_ref,
                  wdn_ref, gf_ref, o_ref, act_ref, *, tm, tf, last):
    g = g_ref[...]
    nr = tm // SUBLANE
    interleave = lambda x: x.reshape(SUBLANE, nr, x.shape[-1]).swapaxes(0, 1).reshape(tm, x.shape[-1])
    deinterleave = lambda x: x.reshape(nr, SUBLANE, x.shape[-1]).swapaxes(0, 1).reshape(tm, x.shape[-1])
    h = interleave(h_ref[...] + _dot(a_ref[...], wa_ref[...]) + _dot(m_ref[...], wm_ref[...]))
    x = _rms(h, g).astype(BF16)
    keep = jnp.where(pl.program_id(1) > 0, 1.0, 0.0)
    h_prev = (hp_ref[...] + _dot(ap_ref[...], wa_ref[...]) + _dot(mp_ref[...], wm_ref[...]))[BF16_ROWS - SUBLANE:]
    x_prev = (_rms(h_prev, g) * keep).astype(BF16)
    first = lax.broadcasted_iota(jnp.int32, (SUBLANE, tf), 0) == 0
    nchunk = FFN_DIM // tf

    def up(j):
        cols = [pl.ds(off + j * tf, tf) for off in (0, FFN_DIM)]
        return tuple((_dot(x, wup_ref[:, c]), _dot(x_prev, wup_ref[:, c])) for c in cols)

    def conv(u, u_prev, off):
        w = cw_ref[:, off:off + tf]
        wrap = lambda r, k: jnp.where(first, u_prev[SUBLANE - k:SUBLANE - k + 1, :],
                                      pltpu.roll(u[r * SUBLANE:(r + 1) * SUBLANE, :], 1, 0))
        back1 = jnp.concatenate([wrap(nr - 1, 1), u[0:tm - SUBLANE, :]], axis=0)
        back2 = jnp.concatenate([wrap(nr - 2, 2), wrap(nr - 1, 1), u[0:tm - 2 * SUBLANE, :]], axis=0)
        return cb_ref[:, off:off + tf] + w[0:1, :] * back2 + w[1:2, :] * back1 + w[2:3, :] * u

    u_next = up(0)
    for j in range(nchunk):
        (ua, ua_prev), (ub, ub_prev) = u_next
        if j + 1 < nchunk:
            u_next = up(j + 1)
        a = conv(ua, ua_prev, j * tf)
        b = conv(ub, ub_prev, FFN_DIM + j * tf)
        act_ref[:, j * tf:(j + 1) * tf] = (a * _sigmoid(a) * b).astype(BF16)
    out = h + _dot(act_ref[...], wdn_ref[...])
    o_ref[...] = deinterleave(_rms(out, gf_ref[...]) if last else out)


def _mix_ffn(h, main, mo, wa, klayer, wm, layer, g, wup, cw, cb, wdn, gf, last, batch):
    t = h.shape[0]
    tm = FFN_ROWS
    nt = t // batch // tm
    hb = tm // BF16_ROWS
    cur = lambda n: pl.BlockSpec((tm, n), lambda b, i: (b * nt + i, 0))
    prev = lambda n: pl.BlockSpec((BF16_ROWS, n), lambda b, i: (jnp.maximum((b * nt + i) * hb - 1, 0), 0))
    return pl.pallas_call(
        functools.partial(_mix_ffn_body, tm=tm, tf=FFN_TILE, last=last),
        grid=(batch, nt),
        in_specs=[cur(D_MODEL), prev(D_MODEL), cur(main.shape[1]), prev(main.shape[1]),
                  cur(mo.shape[1]), prev(mo.shape[1]), _layer_of(wa, klayer), _layer_of(wm, layer), _layer_of(g, layer),
                  _layer_of(wup, layer), _layer_of(cw, layer), _layer_of(cb, layer), _layer_of(wdn, layer),
                  _resident(gf)],
        out_specs=cur(D_MODEL),
        out_shape=jax.ShapeDtypeStruct((t, D_MODEL), F32),
        scratch_shapes=[pltpu.VMEM((tm, FFN_DIM), BF16)],
        compiler_params=_cparams("parallel", "parallel"),
        name="mix_ffn",
    )(h, h, main, main, mo, mo, wa, wm, g, wup, cw, cb, wdn, gf)


KV_NAT = NSA_GROUPS * NSA_HEAD_DIM


def _kv_proj_body(h_ref, g_ref, w_ref, cx_ref, ks_ref, kw_ref, vs_ref, vw_ref, *, tm):
    xn = _rms(h_ref[...], g_ref[...]).astype(BF16)
    for j in range(2):
        y = _dot(xn, w_ref[:, j * KV_NAT:(j + 1) * KV_NAT])
        cx_ref[j] = y.reshape(tm // CMP_STRIDE, CMP_STRIDE, KV_NAT).swapaxes(0, 1).astype(BF16)
    key = pl.program_id(1) * tm + lax.broadcasted_iota(jnp.int32, (tm, LANE), 0)
    lane = lax.broadcasted_iota(jnp.int32, (tm, LANE), 1)
    onehot = jnp.where(lax.shift_right_logical(key, int(math.log2(SEL_BLOCK))) == lane - NSA_HEAD_DIM, 1.0, 0.0)
    low = lane < NSA_HEAD_DIM
    for gi in range(NSA_GROUPS):
        slot = lambda n: _dot(xn, w_ref[:, 2 * KV_NAT + (n * NSA_GROUPS + gi) * LANE:
                                           2 * KV_NAT + (n * NSA_GROUPS + gi + 1) * LANE])
        ks_ref[gi] = jnp.where(low, slot(0), onehot).astype(BF16)
        kw_ref[gi] = jnp.where(low, slot(1), 0.0).astype(BF16)
        vt = slot(2).T.astype(BF16)
        ones = jnp.ones((V_ROWS - NSA_HEAD_DIM, tm), BF16)
        vs_ref[gi] = jnp.concatenate([vt[0:NSA_HEAD_DIM], ones], axis=0)
        vw_ref[gi] = jnp.concatenate([vt[NSA_HEAD_DIM:], ones], axis=0)


def _kv_proj(h, g, w, batch):
    t = h.shape[0]
    s = t // batch
    tm = ROW_TILE
    nt = s // tm
    gr = NSA_GROUPS
    return pl.pallas_call(
        functools.partial(_kv_proj_body, tm=tm),
        grid=(batch, nt),
        in_specs=[pl.BlockSpec((tm, D_MODEL), lambda b, i: (b * nt + i, 0)),
                  pl.BlockSpec(g.shape, lambda b, i: (0, 0)),
                  pl.BlockSpec(w.shape, lambda b, i: (0, 0))],
        out_specs=[pl.BlockSpec((2, CMP_STRIDE, tm // CMP_STRIDE, KV_NAT), lambda b, i: (0, 0, b * nt + i, 0)),
                   pl.BlockSpec((None, gr, tm, LANE), lambda b, i: (b, 0, i, 0)),
                   pl.BlockSpec((None, gr, tm, LANE), lambda b, i: (b, 0, i, 0)),
                   pl.BlockSpec((None, gr, V_ROWS, tm), lambda b, i: (b, 0, 0, i)),
                   pl.BlockSpec((None, gr, V_ROWS, tm), lambda b, i: (b, 0, 0, i))],
        out_shape=[jax.ShapeDtypeStruct((2, CMP_STRIDE, t // CMP_STRIDE, KV_NAT), BF16),
                   jax.ShapeDtypeStruct((batch, gr, s, LANE), BF16),
                   jax.ShapeDtypeStruct((batch, gr, s, LANE), BF16),
                   jax.ShapeDtypeStruct((batch, gr, V_ROWS, s), BF16),
                   jax.ShapeDtypeStruct((batch, gr, V_ROWS, s), BF16)],
        compiler_params=_cparams("parallel", "parallel"),
        name="nsa_kv_proj",
    )(h, g, w)


def _compress_body(x_ref, wt_ref, wb_ref, pos_ref, w1_ref, b1_ref, w2_ref, b2_ref, on_ref, ot_ref, *, ncp):
    top = _dot(x_ref[0], wt_ref[0])
    bot = _dot(x_ref[0], wb_ref[0])
    for l in range(1, CMP_STRIDE):
        top = top + _dot(x_ref[l], wt_ref[l])
        bot = bot + _dot(x_ref[l], wb_ref[l])
    posb = _dot(pos_ref[...], w1_ref[...])[0:1, :] + b1_ref[...]
    for gi in range(NSA_GROUPS):
        sl = slice(gi * CMP_HIDDEN, (gi + 1) * CMP_HIDDEN)
        hid = top[:, sl] + pltpu.roll(bot[:, sl], ncp - 1, 0) + posb
        hid = (hid * _sigmoid(hid)).astype(BF16)
        out = _dot(hid, w2_ref[...]) + b2_ref[...]
        on_ref[gi] = out.astype(BF16)
        ot_ref[gi] = out.T.astype(BF16)


def _compress(x16, wt, wb, pos, w1, b1, w2, b2, batch):
    ncp = x16.shape[2] // batch
    gr = NSA_GROUPS
    per_j = lambda a: pl.BlockSpec((None,) + a.shape[1:], lambda j, b: (j,) + (0,) * (a.ndim - 1))
    return pl.pallas_call(
        functools.partial(_compress_body, ncp=ncp),
        grid=(2, batch),
        in_specs=[pl.BlockSpec((None, CMP_STRIDE, ncp, KV_NAT), lambda j, b: (j, 0, b, 0)),
                  per_j(wt), per_j(wb), per_j(pos), per_j(w1), per_j(b1), per_j(w2), per_j(b2)],
        out_specs=[pl.BlockSpec((None, None, gr, ncp, LANE), lambda j, b: (j, b, 0, 0, 0)),
                   pl.BlockSpec((None, None, gr, LANE, ncp), lambda j, b: (j, b, 0, 0, 0))],
        out_shape=[jax.ShapeDtypeStruct((2, batch, gr, ncp, LANE), BF16),
                   jax.ShapeDtypeStruct((2, batch, gr, LANE, ncp), BF16)],
        compiler_params=_cparams("arbitrary", "arbitrary"),
        name="nsa_compress",
    )(x16, wt, wb, pos, w1, b1, w2, b2)


NSA_QW = NSA_HEADS * NSA_HEAD_DIM
G3 = NSA_REP * TQ


def _nsa_in_body(h_ref, g_ref, w_ref, kv_ref, qt_ref, gt_ref, mo_ref):
    xn = _rms(h_ref[...], g_ref[...]).astype(BF16)
    for j in range(NSA_QW // LANE):
        y = _dot(xn, w_ref[:, j * LANE:(j + 1) * LANE]) * (NSA_HEAD_DIM ** -0.5 * LOG2E)
        qt_ref[j * LANE:(j + 1) * LANE, :] = y.T.astype(BF16)
    gt_ref[...] = _sigmoid(_dot(xn, w_ref[:, NSA_QW:NSA_QW + LANE])).T
    mo_ref[...] = _mem_attention(_dot(xn, w_ref[:, NSA_QW + LANE:]).astype(BF16), kv_ref)


def _nsa_in_proj(h, layer, nlayer, g, w, mem_kv, batch):
    t = h.shape[0]
    tm = ROW_TILE
    row = lambda n: pl.BlockSpec((tm, n), lambda i: (i, 0))
    nt = t // batch // tm
    col = lambda n: pl.BlockSpec((None, n, tm), lambda i: (i // nt, 0, i % nt))
    return pl.pallas_call(
        _nsa_in_body,
        grid=(t // tm,),
        in_specs=[row(D_MODEL), _layer_of(g, layer), _layer_of(w, nlayer), _mem_kv_spec(mem_kv, layer, t, tm, batch)],
        out_specs=[col(NSA_QW), col(LANE), row(MEM_QW)],
        out_shape=[jax.ShapeDtypeStruct((batch, NSA_QW, t // batch), BF16),
                   jax.ShapeDtypeStruct((batch, LANE, t // batch), F32),
                   jax.ShapeDtypeStruct((t, MEM_QW), BF16)],
        compiler_params=_cparams("parallel"),
        name="nsa_in_proj",
    )(h, g, w, mem_kv)


def _nsa_attn_body(qt_ref, gt_ref, kc_ref, vct_ref, ks_ref, kw_ref, vs_ref, vw_ref, ovt_ref, tz_ref, cb_ref,
                   o_ref, sc_ref, qa_ref, acc_ref, ot_ref, *s_slots, nsb, ncp, n_sel, nseq):
    n = pl.program_id(1)
    t0 = n * TQ
    dh = NSA_HEAD_DIM
    cstart = pl.multiple_of(ncp - n * (TQ // CMP_STRIDE), SUBLANE)
    sees_any = t0 + (lax.broadcasted_iota(jnp.int32, (1, G3), 1) & (TQ - 1)) >= CMP_BLOCK - 1
    jj = lax.broadcasted_iota(jnp.int32, (nsb, TQ), 0)
    cur = lax.shift_right_logical(t0 + lax.broadcasted_iota(jnp.int32, (nsb, TQ), 1), int(math.log2(SEL_BLOCK)))
    forced = (jj == 0) | (jj == cur) | (jj == cur - 1)
    zpad = jnp.zeros((dh, G3), BF16)
    nwt = WINDOW // TQ
    units = [(bi, gi) for bi in range(nseq) for gi in range(NSA_GROUPS)]

    def scores(k_ref, u, first, count, tz_index, qa):
        bi, gi = units[u]
        koff = pl.multiple_of(first * TQ, TQ)
        s = _dot(k_ref[bi, gi, pl.ds(koff, count * TQ), :], qa)
        return [s[i * TQ:(i + 1) * TQ] + tz_ref[gi, tz_index(n - (first + i))] for i in range(count)]

    def col_max(parts):
        mx = parts[0]
        for x in parts[1:]:
            mx = jnp.maximum(mx, x)
        return jnp.max(mx, axis=0, keepdims=True)

    def probs(parts, m):
        return jnp.concatenate([jnp.exp2(x - m).astype(BF16) for x in parts], axis=0)

    def values(v_ref, u, first, count, p):
        bi, gi = units[u]
        koff = pl.multiple_of(first * TQ, TQ)
        return _dot(v_ref[bi, gi, :, pl.ds(koff, count * TQ)], p)

    sel_index = lambda d: jnp.where(d < 0, 2, jnp.minimum(d, 2))
    win_index = lambda d: jnp.where(d < 0, 4, jnp.where(d == nwt, 3, jnp.minimum(d, 2)))

    groups = range(len(units))
    gate = lambda bi, h, c: gt_ref[bi, pl.ds(h * NSA_BRANCHES + c, 1), :]
    head_lanes = lambda r: slice(r * TQ, (r + 1) * TQ)
    head_rows = lambda bi, h: (bi, slice(h * dh, (h + 1) * dh))

    q3s = [jnp.concatenate([qt_ref[head_rows(bi, gi * NSA_REP + r)] for r in range(NSA_REP)], axis=1)
           for bi, gi in units]
    cs = [_dot(kc_ref[bi, gi], jnp.concatenate([q3s[u], zpad], axis=0)) + cb_ref[gi, pl.ds(cstart, ncp), :]
          for u, (bi, gi) in enumerate(units)]
    cps = [jnp.exp2(s - jnp.max(s, axis=0, keepdims=True)) for s in cs]
    cps = [p * jnp.where(sees_any, 1.0 / jnp.sum(p, axis=0, keepdims=True), 0.0) for p in cps]
    ocs = [_dot(vct_ref[bi, gi], cps[u].astype(BF16)) for u, (bi, gi) in enumerate(units)]
    scs = []
    for gi in groups:
        psum = cps[gi][:, 0:TQ]
        for r in range(1, NSA_REP):
            psum = psum + cps[gi][:, head_lanes(r)]
        p1 = psum.astype(BF16)
        p2 = (psum - p1.astype(F32)).astype(BF16)
        imp = _dot(ovt_ref[...], p1) + _dot(ovt_ref[...], p2)
        score = jnp.where(forced, 1e4, jnp.where(jj <= cur, imp[0:nsb], -1.0))
        scs.append(jnp.where(score < 0.0, -1, lax.bitcast_convert_type(score, jnp.int32)))
    for u, (bi, gi) in enumerate(units):
        sc_ref[u] = scs[u]
        for r in range(NSA_REP):
            h = gi * NSA_REP + r
            ot_ref[head_rows(bi, h)] = gate(bi, h, 0) * ocs[u][0:dh, head_lanes(r)]

    scs1 = [k + 1 for k in scs]

    def rank_step(i4, cnts):
        cnts = list(cnts)
        for u in range(RANK_UNROLL):
            i = i4 * RANK_UNROLL + u
            lower = i < jj
            for gi in groups:
                rowk = sc_ref[gi, pl.ds(i, 1), :]
                before = rowk >= jnp.where(lower, scs[gi], scs1[gi])
                cnts[gi] = cnts[gi] + jnp.where(before, 1, 0)
        return tuple(cnts)
    rank_trips = jnp.minimum((2 * n + 2 + RANK_UNROLL - 1) // RANK_UNROLL, nsb // RANK_UNROLL)
    cnts = lax.fori_loop(0, rank_trips, rank_step, tuple(jnp.zeros((nsb, TQ), jnp.int32) for _ in groups))
    for gi in groups:
        selneg = jnp.where((cnts[gi] < n_sel) & (jj <= cur), 0.0, NEG).astype(BF16)
        if nsb < SEL_BLOCK:
            selneg = jnp.concatenate([selneg, jnp.zeros((SEL_BLOCK - nsb, TQ), BF16)], axis=0)
        qa_ref[gi] = jnp.concatenate([q3s[gi], jnp.concatenate([selneg] * NSA_REP, axis=1)], axis=0)

    acc_ref[...] = jnp.zeros_like(acc_ref)
    half = len(units) // 2
    s_refs = (s_slots[:half], s_slots[half:])

    def score_half(it, hb):
        plist = [scores(ks_ref, hb * half + u, it * SEL_TILES, SEL_TILES, sel_index, qa_ref[hb * half + u])
                 for u in range(half)]
        for u, parts in enumerate(plist):
            for i, x in enumerate(parts):
                s_refs[hb][u][i * TQ:(i + 1) * TQ, :] = x
        return [col_max(parts) for parts in plist]

    def finish_half(it, hb, ms, bms):
        m2s = [jnp.maximum(ms[u], bms[u]) for u in range(half)]
        ps = [probs([s_refs[hb][u][i * TQ:(i + 1) * TQ, :] for i in range(SEL_TILES)], m2s[u]) for u in range(half)]
        vals = [values(vs_ref, hb * half + u, it * SEL_TILES, SEL_TILES, ps[u]) for u in range(half)]
        for u in range(half):
            gi = hb * half + u
            acc_ref[gi] = jnp.exp2(ms[u] - m2s[u]) * acc_ref[gi] + vals[u]
        return m2s

    def sel_trip(it, carry, score_next):
        ms0, ms1, bm0 = carry
        bm1 = score_half(it, 1)
        ms0 = finish_half(it, 0, ms0, bm0)
        if score_next:
            bm0 = score_half(it + 1, 0)
        ms1 = finish_half(it, 1, ms1, bm1)
        return ms0, ms1, bm0

    neg = [jnp.full((1, G3), NEG, F32) for _ in range(half)]
    carry = (neg, neg, score_half(0, 0))

    wfirst = jnp.maximum(n - nwt, 0)
    wparts = [scores(kw_ref, gi, wfirst, nwt + 1, win_index, qa_ref[gi]) for gi in groups]
    wps = [probs(parts, col_max(parts)) for parts in wparts]
    ows = [values(vw_ref, gi, wfirst, nwt + 1, wps[gi]) for gi in groups]
    for u, (bi, gi) in enumerate(units):
        o_w = ows[u][0:dh, :] * (1.0 / ows[u][dh:dh + 1, :])
        for r in range(NSA_REP):
            h = gi * NSA_REP + r
            ot_ref[head_rows(bi, h)] += gate(bi, h, 2) * o_w[:, head_lanes(r)]

    carry = lax.fori_loop(0, n // SEL_TILES, lambda it, c: sel_trip(it, c, True), carry)
    sel_trip(n // SEL_TILES, carry, False)

    for u, (bi, gi) in enumerate(units):
        o_s = acc_ref[u, 0:dh, :] * (1.0 / acc_ref[u, dh:dh + 1, :])
        for r in range(NSA_REP):
            h = gi * NSA_REP + r
            ot_ref[head_rows(bi, h)] += gate(bi, h, 1) * o_s[:, head_lanes(r)]

    for bi in range(nseq):
        for j in range(NSA_QW // LANE):
            o_ref[bi, :, j * LANE:(j + 1) * LANE] = ot_ref[bi, j * LANE:(j + 1) * LANE, :].T.astype(BF16)


def _nsa_attn(qt, gt, kc, vct, ksel, kwin, vsel, vwin, ovt, tz, cb):
    batch, _, s = qt.shape
    nseq = NSA_SEQS if batch % NSA_SEQS == 0 else 1
    nt = s // TQ
    nsb = s // SEL_BLOCK
    ncp = kc.shape[-2]
    nu = nseq * NSA_GROUPS
    per_b = lambda a: pl.BlockSpec((nseq,) + a.shape[1:], lambda b, i: (b,) + (0,) * (a.ndim - 1),
                                   pipeline_mode=pl.Buffered(1))
    out = pl.pallas_call(
        functools.partial(_nsa_attn_body, nsb=nsb, ncp=ncp, n_sel=min(SEL_TOPK, nsb), nseq=nseq),
        grid=(batch // nseq, nt),
        in_specs=[pl.BlockSpec((nseq, NSA_QW, TQ), lambda b, i: (b, 0, i)),
                  pl.BlockSpec((nseq, LANE, TQ), lambda b, i: (b, 0, i)),
                  per_b(kc), per_b(vct), per_b(ksel), per_b(kwin), per_b(vsel), per_b(vwin),
                  _resident(ovt), _resident(tz), _resident(cb)],
        out_specs=pl.BlockSpec((nseq, TQ, NSA_QW), lambda b, i: (b, i, 0)),
        out_shape=jax.ShapeDtypeStruct((batch, s, NSA_QW), BF16),
        scratch_shapes=[pltpu.VMEM((nu, nsb, TQ), jnp.int32), pltpu.VMEM((nu, LANE, G3), BF16),
                        pltpu.VMEM((nu, V_ROWS, G3), F32), pltpu.VMEM((nseq, NSA_QW, TQ), F32),
                        ] + [pltpu.VMEM((SEL_TILES * TQ, G3), F32)] * nu,
        compiler_params=_cparams("parallel", "arbitrary"),
        name="nsa_attn",
    )(qt, gt, kc, vct, ksel, kwin, vsel, vwin, ovt, tz, cb)
    return out.reshape(batch * s, NSA_QW)


def _rel_bucket_np(dist):
    dist = np.maximum(dist, 0)
    max_exact = REL_BUCKETS // 2
    ratio = np.log(np.maximum(dist, 1).astype(np.float32) / np.float32(max_exact)) / np.float32(
        math.log(REL_MAX_DIST / max_exact))
    large = max_exact + (ratio * np.float32(REL_BUCKETS - max_exact)).astype(np.int32)
    large = np.minimum(large, REL_BUCKETS - 1)
    return np.where(dist < max_exact, dist, large).astype(np.int32)


def _group_lanes(a):
    hh, r, c = a.shape
    return a.reshape(NSA_GROUPS, NSA_REP, r, c).transpose(0, 2, 1, 3).reshape(NSA_GROUPS, r, NSA_REP * c)


def _bias_tables(rel_bias, ncp):
    k = np.arange(TQ)[:, None]
    q = np.arange(TQ)[None, :]
    tbl = rel_bias.astype(F32)

    def lookup(idx):
        onehot = (jnp.asarray(idx.reshape(1, -1)) == jnp.arange(REL_BUCKETS)[:, None]).astype(F32)
        out = jnp.dot(tbl.T, onehot, precision=lax.Precision.HIGHEST)
        return out.reshape((NSA_HEADS,) + idx.shape)

    far = jnp.broadcast_to(tbl[REL_BUCKETS - 1][:, None, None], (NSA_HEADS, TQ, TQ))
    t0 = jnp.where(jnp.asarray(k <= q)[None], lookup(_rel_bucket_np(q - k)), NEG)
    t1 = lookup(_rel_bucket_np(TQ + q - k))
    t3 = jnp.where(jnp.asarray(k > q)[None], far, NEG)
    tz = jnp.stack([_group_lanes(x) for x in (t0, t1, far, t3, jnp.full_like(far, NEG))], axis=1)
    m = ncp - np.arange(2 * ncp)[:, None]
    d = CMP_STRIDE * m + q - (CMP_BLOCK - 1)
    idx = np.where((d >= 0) & (d < REL_MAX_DIST), _rel_bucket_np(d), REL_BUCKETS - 1)
    cb = _group_lanes(jnp.where(jnp.asarray(d >= 0)[None], lookup(idx), NEG))
    return tz * LOG2E, cb * LOG2E


def _overlap_table(s, ncp):
    nsb = s // SEL_BLOCK
    nc = (s - CMP_BLOCK) // CMP_STRIDE + 1
    cs = np.arange(ncp) * CMP_STRIDE
    ce = cs + CMP_BLOCK - 1
    ss = np.arange(SEL_BLOCK) * SEL_BLOCK
    ov = (cs[None, :] < ss[:, None] + SEL_BLOCK) & (ce[None, :] >= ss[:, None])
    ov &= (np.arange(ncp) < nc)[None, :] & (np.arange(SEL_BLOCK) < nsb)[:, None]
    return jnp.asarray(ov, dtype=BF16)


def kernel(x, mem, norm_mix, norm_mem, w_mem_kv, w_out, norm_ffn, w_up, conv_w, conv_b, w_down,
           gla_w_in, gla_w_gate_up, gla_b_gate, gla_out_norm, nsa_w_in, kv_norm, w_kv_shared,
           cmp_pos, cmp_w1, cmp_b1, cmp_w2, cmp_b2, rel_bias, final_norm):
    batch, seq = x.shape[0], x.shape[1]
    t = batch * seq
    h = x.reshape(t, D_MODEL)
    row = lambda v: v.reshape(1, -1).astype(F32)

    wk, wv = w_mem_kv[..., :MEM_W], w_mem_kv[..., MEM_W:]
    w_mkv = jnp.concatenate([_pad_heads(wk, MEM_HEADS, MEM_HEAD_DIM, MEM_DP),
                             _pad_heads(wv, MEM_HEADS, MEM_HEAD_DIM, MEM_DP)], axis=-1).astype(BF16)
    mem_kv_all = _mem_kv_proj(mem.reshape(-1, D_MODEL), norm_mem.reshape(DEPTH, 1, D_MODEL), w_mkv, batch)

    ffn_params = (norm_ffn.reshape(DEPTH, 1, D_MODEL).astype(F32), w_up.astype(BF16), conv_w.astype(F32),
                  conv_b.reshape(DEPTH, 1, 2 * FFN_DIM).astype(F32), w_down.astype(BF16))
    norm_mix3 = norm_mix.reshape(DEPTH, 1, D_MODEL).astype(F32)
    w_o_mem = _pad_head_rows(w_out[:, MAIN_W:], MEM_HEADS, MEM_HEAD_DIM, MEM_DP).astype(BF16)
    mem_pad = lambda w: _pad_heads(w, MEM_HEADS, MEM_HEAD_DIM, MEM_DP)

    c0 = GLA_HEADS * GLA_DK
    c1 = 2 * c0
    c2 = c1 + GLA_HEADS * GLA_DV
    c3 = c2 + GLA_HEADS * GLA_DV
    c4 = c3 + GLA_RANK
    gla_w = jnp.concatenate([
        _pad_heads(gla_w_in[..., :c0], GLA_HEADS, GLA_DK, GLA_DKP),
        _pad_heads(gla_w_in[..., c0:c1], GLA_HEADS, GLA_DK, GLA_DKP),
        _pad_heads(gla_w_in[..., c1:c2], GLA_HEADS, GLA_DV, GLA_DVP),
        _pad_heads(gla_w_in[..., c2:c3], GLA_HEADS, GLA_DV, GLA_DVP),
        _pad_heads(gla_w_in[..., c3:c4], 1, GLA_RANK, LANE),
        mem_pad(gla_w_in[..., c4:])], axis=-1).astype(BF16)
    gla_wg = jnp.pad(_pad_heads(gla_w_gate_up, GLA_HEADS, GLA_DK, GLA_DKP),
                     ((0, 0), (0, LANE - GLA_RANK), (0, 0))).astype(BF16)
    gla_bg = _pad_heads(gla_b_gate, GLA_HEADS, GLA_DK, GLA_DKP).reshape(N_A_LAYERS, 1, GLA_QW).astype(F32)
    gla_on = jnp.pad(gla_out_norm, ((0, 0), (0, GLA_DVP - GLA_DV))).reshape(N_A_LAYERS, 1, GLA_DVP).astype(F32)
    gla_wo = _pad_head_rows(w_out[:N_A_LAYERS, :MAIN_W], GLA_HEADS, GLA_DV, GLA_DVP).astype(BF16)

    n0 = NSA_HEADS * NSA_HEAD_DIM
    n1 = n0 + NSA_HEADS * NSA_BRANCHES
    nsa_w = jnp.concatenate([nsa_w_in[..., :n0], _pad_heads(nsa_w_in[..., n0:n1], 1, NSA_HEADS * NSA_BRANCHES, LANE),
                             mem_pad(nsa_w_in[..., n1:])], axis=-1).astype(BF16)
    nsa_wo = w_out[N_A_LAYERS:, :MAIN_W].astype(BF16)

    shared = None
    for i in range(DEPTH):
        if i < N_A_LAYERS:
            q, k, v, go, la, mo = _gla_in_proj(h, i, i, norm_mix3, gla_w, gla_wg, gla_bg, mem_kv_all, batch)
            main = _gla_mixer(q, k, la, v, go, gla_on, i, batch)
            w_o_main, klayer = gla_wo, i
        else:
            if shared is None:
                ncp = seq // CMP_STRIDE
                gr, dh = NSA_GROUPS, NSA_HEAD_DIM
                wkv = w_kv_shared.reshape(D_MODEL, 6, gr, dh)
                pair = lambda a, b: jnp.concatenate([wkv[:, a], wkv[:, b]], axis=-1).reshape(D_MODEL, gr * 2 * dh)
                slot = lambda a: _pad_heads(wkv[:, a].reshape(D_MODEL, KV_NAT), gr, dh, LANE)
                w_kv = jnp.concatenate([wkv[:, 0].reshape(D_MODEL, KV_NAT), wkv[:, 1].reshape(D_MODEL, KV_NAT),
                                        slot(2), slot(4), pair(3, 5)], axis=1).astype(BF16)
                x16, ksel, kwin, vsel, vwin = _kv_proj(h, row(kv_norm), w_kv, batch)
                w1 = cmp_w1.reshape(2, 2, CMP_STRIDE, dh, CMP_HIDDEN)
                eye = jnp.eye(gr, dtype=F32)
                w1x = jnp.einsum('jhldc,gk->jhlgdkc', w1, eye).reshape(2, 2, CMP_STRIDE, KV_NAT, gr * CMP_HIDDEN)
                w1x = w1x.astype(BF16)
                pos8 = jnp.broadcast_to(cmp_pos.reshape(2, 1, CMP_BLOCK * dh), (2, SUBLANE, CMP_BLOCK * dh)).astype(BF16)
                w2p = jnp.pad(cmp_w2, ((0, 0), (0, 0), (0, LANE - dh))).astype(BF16)
                b2p = jnp.pad(cmp_b2, ((0, 0), (0, LANE - dh))).reshape(2, 1, LANE).astype(F32)
                cnat, ctr = _compress(x16, w1x[:, 0], w1x[:, 1], pos8, cmp_w1.astype(BF16),
                                      cmp_b1.reshape(2, 1, CMP_HIDDEN).astype(F32), w2p, b2p, batch)
                tz, cb = _bias_tables(rel_bias, ncp)
                ov = _overlap_table(seq, ncp)
                shared = (cnat[0], ctr[1], ksel, kwin, vsel, vwin, ov, tz, cb)
            q, gates, mo = _nsa_in_proj(h, i, i - N_A_LAYERS, norm_mix3, nsa_w, mem_kv_all, batch)
            main = _nsa_attn(q, gates, *shared)
            w_o_main, klayer = nsa_wo, i - N_A_LAYERS
        h = _mix_ffn(h, main, mo, w_o_main, klayer, w_o_mem, i, *ffn_params, row(final_norm), i == DEPTH - 1, batch)
    return h.reshape(batch, seq, D_MODEL)
```

```python
import functools
import math

import numpy as np
import jax
import jax.numpy as jnp
from jax import lax
from jax.experimental import pallas as pl
from jax.experimental.pallas import tpu as pltpu

F32 = jnp.float32
BF16 = jnp.bfloat16

D_MODEL = 1024
DEPTH = 4
N_A_LAYERS = DEPTH // 2
MEM_HEADS = 4
MEM_HEAD_DIM = 64
MEM_W = MEM_HEADS * MEM_HEAD_DIM
MAIN_W = D_MODEL - MEM_W
GLA_HEADS = 4
GLA_DV = MAIN_W // GLA_HEADS
GLA_DK = GLA_DV // 2
GLA_RANK = 16
GLA_GATE_NORM = 16.0
NSA_HEADS = 12
NSA_GROUPS = 4
NSA_HEAD_DIM = MAIN_W // NSA_HEADS
NSA_REP = NSA_HEADS // NSA_GROUPS
NSA_BRANCHES = 3
CMP_BLOCK = 32
CMP_STRIDE = 16
CMP_HIDDEN = 128
SEL_BLOCK = 64
SEL_TOPK = 16
WINDOW = 512
REL_BUCKETS = 32
REL_MAX_DIST = 128
FFN_DIM = 2816
CONV_WIDTH = 3
EPS = 1e-6

LANE = 128
SUBLANE = 8
VMEM_LIMIT = 56 * 1024 * 1024
GLA_DKP = LANE
GLA_DVP = 2 * LANE
MEM_DP = LANE
NEG = -1e30
TQ = 128
ROW_TILE = 1024
GLA_CHUNK = 64
GLA_SAFE_DECAY = 80.0
FFN_TILE = 256
FFN_ROWS = 1024
SEL_TILES = 4
NSA_SEQS = 2
V_ROWS = NSA_HEAD_DIM + 2 * SUBLANE
RANK_UNROLL = 4
LOG2E = math.log2(math.e)

NT = (((1,), (1,)), ((), ()))
TN = (((0,), (0,)), ((), ()))


def _cparams(*sem):
    return pltpu.CompilerParams(dimension_semantics=sem, vmem_limit_bytes=VMEM_LIMIT)


def _rms(x, g):
    return x * lax.rsqrt(jnp.mean(x * x, axis=-1, keepdims=True) + EPS) * g


def _sigmoid(x):
    return 1.0 / (1.0 + jnp.exp(-x))


def _dot(a, b):
    return jnp.dot(a, b, preferred_element_type=F32)


def _resident(a):
    return pl.BlockSpec(a.shape, lambda *_: (0,) * a.ndim, pipeline_mode=pl.Buffered(1))


def _layer_of(a, layer):
    return pl.BlockSpec((None,) + a.shape[1:], lambda *_: (layer,) + (0,) * (a.ndim - 1),
                        pipeline_mode=pl.Buffered(1))


def _pad_heads(w, nh, d, dp):
    lead = w.shape[:-1]
    w = w.reshape(lead + (nh, d))
    w = jnp.pad(w, [(0, 0)] * len(lead) + [(0, 0), (0, dp - d)])
    return w.reshape(lead + (nh * dp,))


def _pad_head_rows(w, nh, d, dp):
    lead, n = w.shape[:-2], w.shape[-1]
    w = jnp.pad(w.reshape(lead + (nh, d, n)), [(0, 0)] * len(lead) + [(0, 0), (0, dp - d), (0, 0)])
    return w.reshape(lead + (nh * dp, n))


def _mem_attention(q, kv_ref):
    outs = []
    for hd in range(MEM_HEADS):
        sl = slice(hd * MEM_DP, (hd + 1) * MEM_DP)
        sv = slice(MEM_QW + hd * MEM_DP, MEM_QW + (hd + 1) * MEM_DP)
        s = lax.dot_general(q[:, sl], kv_ref[:, sl], NT, preferred_element_type=F32) * MEM_HEAD_DIM ** -0.5
        p = jnp.exp(s - jnp.max(s, axis=-1, keepdims=True))
        l = jnp.sum(p, axis=-1, keepdims=True)
        outs.append((_dot(p.astype(BF16), kv_ref[:, sv]) / l).astype(BF16))
    return jnp.concatenate(outs, axis=1)


GLA_QW = GLA_HEADS * GLA_DKP
GLA_VW = GLA_HEADS * GLA_DVP
MEM_QW = MEM_HEADS * MEM_DP
GLA_OFF_K = GLA_QW
GLA_OFF_V = 2 * GLA_QW
GLA_OFF_G = GLA_OFF_V + GLA_VW
GLA_OFF_LR = GLA_OFF_G + GLA_VW
GLA_OFF_MQ = GLA_OFF_LR + LANE


def _gla_in_body(h_ref, g_ref, w_ref, wg_ref, bg_ref, kv_ref, q_ref, k_ref, v_ref, go_ref, la_ref, mo_ref):
    xn = _rms(h_ref[...], g_ref[...]).astype(BF16)

    def proj(lo, n):
        return _dot(xn, w_ref[:, lo:lo + n])

    q_ref[...] = proj(0, GLA_QW).astype(BF16)
    k_ref[...] = proj(GLA_OFF_K, GLA_QW).astype(BF16)
    for j in range(GLA_VW // GLA_QW):
        v_ref[:, j * GLA_QW:(j + 1) * GLA_QW] = proj(GLA_OFF_V + j * GLA_QW, GLA_QW).astype(BF16)
        go_ref[:, j * GLA_QW:(j + 1) * GLA_QW] = proj(GLA_OFF_G + j * GLA_QW, GLA_QW).astype(BF16)
    lr = proj(GLA_OFF_LR, LANE).astype(BF16)
    z = _dot(lr, wg_ref[...]) + bg_ref[...]
    la_ref[...] = (jnp.minimum(z, 0.0) - jnp.log(1.0 + jnp.exp(-jnp.abs(z)))) * (1.0 / GLA_GATE_NORM)
    mo_ref[...] = _mem_attention(proj(GLA_OFF_MQ, MEM_QW).astype(BF16), kv_ref)


def _mem_kv_spec(mem_kv, layer, t, tm, batch):
    nt = t // batch // tm
    return pl.BlockSpec((None, mem_kv.shape[1] // batch, mem_kv.shape[2]), lambda i: (layer, i // nt, 0))


def _gla_in_proj(h, layer, glayer, g, w, wg, bg, mem_kv, batch):
    t = h.shape[0]
    tm = ROW_TILE
    row = lambda n: pl.BlockSpec((tm, n), lambda i: (i, 0))
    return pl.pallas_call(
        _gla_in_body,
        grid=(t // tm,),
        in_specs=[row(D_MODEL), _layer_of(g, layer), _layer_of(w, glayer), _layer_of(wg, glayer),
                  _layer_of(bg, glayer), _mem_kv_spec(mem_kv, layer, t, tm, batch)],
        out_specs=[row(GLA_QW), row(GLA_QW), row(GLA_VW), row(GLA_VW), row(GLA_QW), row(MEM_QW)],
        out_shape=[jax.ShapeDtypeStruct((t, GLA_QW), BF16), jax.ShapeDtypeStruct((t, GLA_QW), BF16),
                   jax.ShapeDtypeStruct((t, GLA_VW), BF16), jax.ShapeDtypeStruct((t, GLA_VW), BF16),
                   jax.ShapeDtypeStruct((t, GLA_QW), F32), jax.ShapeDtypeStruct((t, MEM_QW), BF16)],
        compiler_params=_cparams("parallel"),
        name="gla_in_proj",
    )(h, g, w, wg, bg, mem_kv)


def _gla_mix_body(q_ref, k_ref, la_ref, v_ref, go_ref, on_ref, o_ref, st_ref, inter_ref, b_scr, k_scr, v_scr, *,
                  cn, batch):
    @pl.when(pl.program_id(0) == 0)
    def _():
        st_ref[...] = jnp.zeros_like(st_ref)

    row = lax.broadcasted_iota(jnp.int32, (cn, cn), 0)
    col = lax.broadcasted_iota(jnp.int32, (cn, cn), 1)
    causal = row >= col
    tril = jnp.where(causal, 1.0, 0.0).astype(BF16)
    scale = GLA_DK ** -0.5
    chains = [(bi, hd) for bi in range(batch) for hd in range(GLA_HEADS)]
    sk = lambda hd: slice(hd * GLA_DKP, (hd + 1) * GLA_DKP)
    sv = lambda hd: slice(hd * GLA_DVP, (hd + 1) * GLA_DVP)
    dg = lambda x, y, dims: lax.dot_general(x, y, dims, preferred_element_type=F32)

    def cum_decay(bi, hd):
        la = la_ref[bi, :, sk(hd)]
        la1 = la.astype(BF16)
        r1 = la - la1.astype(F32)
        la2 = r1.astype(BF16)
        la3 = (r1 - la2.astype(F32)).astype(BF16)
        return _dot(tril, la1) + _dot(tril, la2) + _dot(tril, la3)

    def finish(i, o):
        bi, hd = chains[i]
        ms = jnp.sum(o * o, axis=-1, keepdims=True) * (1.0 / GLA_DV)
        y = o * lax.rsqrt(ms + EPS) * on_ref[...]
        g = go_ref[bi, :, sv(hd)].astype(F32)
        o_ref[bi, :, sv(hd)] = (y * (g * _sigmoid(g))).astype(BF16)

    bs = [cum_decay(bi, hd) for bi, hd in chains]
    scaled = []
    for (bi, hd), b in zip(chains, bs):
        b_last = b[cn - 1:cn, :]
        q = q_ref[bi, :, sk(hd)].astype(F32)
        k = k_ref[bi, :, sk(hd)].astype(F32)
        scaled.append(((q * jnp.exp(b) * scale).astype(BF16),
                       (k * jnp.exp(jnp.minimum(-b, GLA_SAFE_DECAY))).astype(BF16),
                       (k * jnp.exp(b_last - b)).astype(BF16), jnp.exp(b_last)))
    intra = [jnp.where(causal, dg(qs, ks, NT), 0.0).astype(BF16) for qs, ks, _, _ in scaled]
    outs = []
    for i, (bi, hd) in enumerate(chains):
        qs, _, ko, a_last = scaled[i]
        v = v_ref[bi, :, sv(hd)]
        st = st_ref[i]
        inter = dg(qs, st.astype(BF16), NT)
        inter_ref[i] = inter
        outs.append(_dot(intra[i], v) + inter)
        st_ref[i] = st * a_last + dg(v, ko, TN)
    for i, o in enumerate(outs):
        finish(i, o)

    b_min = bs[0][cn - 1:cn, :]
    for b in bs[1:]:
        b_min = jnp.minimum(b_min, b[cn - 1:cn, :])

    @pl.when(jnp.min(b_min) < -GLA_SAFE_DECAY)
    def _():
        rows = lax.broadcasted_iota(jnp.int32, (cn, 1), 0)
        for i, (bi, hd) in enumerate(chains):
            b = cum_decay(bi, hd)
            qf = q_ref[bi, :, sk(hd)].astype(F32) * scale
            b_scr[...] = b
            k_scr[...] = k_ref[bi, :, sk(hd)].astype(F32)
            v_scr[...] = v_ref[bi, :, sv(hd)].astype(F32)

            def add_key(j, acc):
                decay = jnp.exp(jnp.minimum(b - b_scr[pl.ds(j, 1), :], 0.0))
                a_col = jnp.sum(qf * k_scr[pl.ds(j, 1), :] * decay, axis=-1, keepdims=True)
                return acc + jnp.where(rows >= j, a_col, 0.0) * v_scr[pl.ds(j, 1), :]
            finish(i, lax.fori_loop(0, cn, add_key, inter_ref[i]))


def _gla_mixer(q, k, la, v, go, on, glayer, batch):
    t = q.shape[0]
    s = t // batch
    cn = GLA_CHUNK
    seq = lambda a: a.reshape(batch, s, a.shape[1])
    blk = lambda n: pl.BlockSpec((batch, cn, n), lambda c: (0, c, 0))
    out = pl.pallas_call(
        functools.partial(_gla_mix_body, cn=cn, batch=batch),
        grid=(s // cn,),
        in_specs=[blk(GLA_QW), blk(GLA_QW), blk(GLA_QW), blk(GLA_VW), blk(GLA_VW),
                  _layer_of(on, glayer)],
        out_specs=blk(GLA_VW),
        out_shape=jax.ShapeDtypeStruct((batch, s, GLA_VW), BF16),
        scratch_shapes=[pltpu.VMEM((batch * GLA_HEADS, GLA_DVP, GLA_DKP), F32),
                        pltpu.VMEM((batch * GLA_HEADS, cn, GLA_DVP), F32), pltpu.VMEM((cn, GLA_DKP), F32),
                        pltpu.VMEM((cn, GLA_DKP), F32), pltpu.VMEM((cn, GLA_DVP), F32)],
        compiler_params=_cparams("arbitrary"),
        name="gla_mixer",
    )(seq(q), seq(k), seq(la), seq(v), seq(go), on)
    return out.reshape(t, GLA_VW)


def _mem_kv_body(m_ref, g_ref, w_ref, o_ref):
    xn = _rms(m_ref[...], g_ref[...]).astype(BF16)
    o_ref[...] = _dot(xn, w_ref[...]).astype(BF16)


def _mem_kv_proj(mem2, g, w, batch):
    nl = w.shape[0]
    m = mem2.shape[0] // batch
    n = w.shape[2]
    return pl.pallas_call(
        _mem_kv_body,
        grid=(nl, batch),
        in_specs=[pl.BlockSpec((m, D_MODEL), lambda l, b: (b, 0)),
                  pl.BlockSpec((None, 1, D_MODEL), lambda l, b: (l, 0, 0)),
                  pl.BlockSpec((None, D_MODEL, n), lambda l, b: (l, 0, 0))],
        out_specs=pl.BlockSpec((None, m, n), lambda l, b: (l, b, 0)),
        out_shape=jax.ShapeDtypeStruct((nl, mem2.shape[0], n), BF16),
        compiler_params=_cparams("arbitrary", "arbitrary"),
        name="mem_kv_proj",
    )(mem2, g, w)


BF16_ROWS = 2 * SUBLANE


def _mix_ffn_body(h_ref, hp_ref, a_ref, ap_ref, m_ref, mp_ref, wa_ref, wm_ref, g_ref, wup_ref, cw_ref, cb_ref,
                  wdn_ref, gf_ref, o_ref, act_ref, *, tm, tf, last):
    g = g_ref[...]
    nr = tm // SUBLANE
    interleave = lambda x: x.reshape(SUBLANE, nr, x.shape[-1]).swapaxes(0, 1).reshape(tm, x.shape[-1])
    deinterleave = lambda x: x.reshape(nr, SUBLANE, x.shape[-1]).swapaxes(0, 1).reshape(tm, x.shape[-1])
    h = interleave(h_ref[...] + _dot(a_ref[...], wa_ref[...]) + _dot(m_ref[...], wm_ref[...]))
    x = _rms(h, g).astype(BF16)
    keep = jnp.where(pl.program_id(1) > 0, 1.0, 0.0)
    h_prev = (hp_ref[...] + _dot(ap_ref[...], wa_ref[...]) + _dot(mp_ref[...], wm_ref[...]))[BF16_ROWS - SUBLANE:]
    x_prev = (_rms(h_prev, g) * keep).astype(BF16)
    first = lax.broadcasted_iota(jnp.int32, (SUBLANE, tf), 0) == 0
    nchunk = FFN_DIM // tf

    def up(j):
        cols = [pl.ds(off + j * tf, tf) for off in (0, FFN_DIM)]
        return tuple((_dot(x, wup_ref[:, c]), _dot(x_prev, wup_ref[:, c])) for c in cols)

    def conv(u, u_prev, off):
        w = cw_ref[:, off:off + tf]
        wrap = lambda r, k: jnp.where(first, u_prev[SUBLANE - k:SUBLANE - k + 1, :],
                                      pltpu.roll(u[r * SUBLANE:(r + 1) * SUBLANE, :], 1, 0))
        back1 = jnp.concatenate([wrap(nr - 1, 1), u[0:tm - SUBLANE, :]], axis=0)
        back2 = jnp.concatenate([wrap(nr - 2, 2), wrap(nr - 1, 1), u[0:tm - 2 * SUBLANE, :]], axis=0)
        return cb_ref[:, off:off + tf] + w[0:1, :] * back2 + w[1:2, :] * back1 + w[2:3, :] * u

    u_next = up(0)
    for j in range(nchunk):
        (ua, ua_prev), (ub, ub_prev) = u_next
        if j + 1 < nchunk:
            u_next = up(j + 1)
        a = conv(ua, ua_prev, j * tf)
        b = conv(ub, ub_prev, FFN_DIM + j * tf)
        act_ref[:, j * tf:(j + 1) * tf] = (a * _sigmoid(a) * b).astype(BF16)
    out = h + _dot(act_ref[...], wdn_ref[...])
    o_ref[...] = deinterleave(_rms(out, gf_ref[...]) if last else out)


def _mix_ffn(h, main, mo, wa, klayer, wm, layer, g, wup, cw, cb, wdn, gf, last, batch):
    t = h.shape[0]
    tm = FFN_ROWS
    nt = t // batch // tm
    hb = tm // BF16_ROWS
    cur = lambda n: pl.BlockSpec((tm, n), lambda b, i: (b * nt + i, 0))
    prev = lambda n: pl.BlockSpec((BF16_ROWS, n), lambda b, i: (jnp.maximum((b * nt + i) * hb - 1, 0), 0))
    return pl.pallas_call(
        functools.partial(_mix_ffn_body, tm=tm, tf=FFN_TILE, last=last),
        grid=(batch, nt),
        in_specs=[cur(D_MODEL), prev(D_MODEL), cur(main.shape[1]), prev(main.shape[1]),
                  cur(mo.shape[1]), prev(mo.shape[1]), _layer_of(wa, klayer), _layer_of(wm, layer), _layer_of(g, layer),
                  _layer_of(wup, layer), _layer_of(cw, layer), _layer_of(cb, layer), _layer_of(wdn, layer),
                  _resident(gf)],
        out_specs=cur(D_MODEL),
        out_shape=jax.ShapeDtypeStruct((t, D_MODEL), F32),
        scratch_shapes=[pltpu.VMEM((tm, FFN_DIM), BF16)],
        compiler_params=_cparams("parallel", "parallel"),
        name="mix_ffn",
    )(h, h, main, main, mo, mo, wa, wm, g, wup, cw, cb, wdn, gf)


KV_NAT = NSA_GROUPS * NSA_HEAD_DIM


def _kv_proj_body(h_ref, g_ref, w_ref, cx_ref, ks_ref, kw_ref, vs_ref, vw_ref, *, tm):
    xn = _rms(h_ref[...], g_ref[...]).astype(BF16)
    for j in range(2):
        y = _dot(xn, w_ref[:, j * KV_NAT:(j + 1) * KV_NAT])
        cx_ref[j] = y.reshape(tm // CMP_STRIDE, CMP_STRIDE, KV_NAT).swapaxes(0, 1).astype(BF16)
    key = pl.program_id(1) * tm + lax.broadcasted_iota(jnp.int32, (tm, LANE), 0)
    lane = lax.broadcasted_iota(jnp.int32, (tm, LANE), 1)
    onehot = jnp.where(lax.shift_right_logical(key, int(math.log2(SEL_BLOCK))) == lane - NSA_HEAD_DIM, 1.0, 0.0)
    low = lane < NSA_HEAD_DIM
    for gi in range(NSA_GROUPS):
        slot = lambda n: _dot(xn, w_ref[:, 2 * KV_NAT + (n * NSA_GROUPS + gi) * LANE:
                                           2 * KV_NAT + (n * NSA_GROUPS + gi + 1) * LANE])
        ks_ref[gi] = jnp.where(low, slot(0), onehot).astype(BF16)
        kw_ref[gi] = jnp.where(low, slot(1), 0.0).astype(BF16)
        vt = slot(2).T.astype(BF16)
        ones = jnp.ones((V_ROWS - NSA_HEAD_DIM, tm), BF16)
        vs_ref[gi] = jnp.concatenate([vt[0:NSA_HEAD_DIM], ones], axis=0)
        vw_ref[gi] = jnp.concatenate([vt[NSA_HEAD_DIM:], ones], axis=0)


def _kv_proj(h, g, w, batch):
    t = h.shape[0]
    s = t // batch
    tm = ROW_TILE
    nt = s // tm
    gr = NSA_GROUPS
    return pl.pallas_call(
        functools.partial(_kv_proj_body, tm=tm),
        grid=(batch, nt),
        in_specs=[pl.BlockSpec((tm, D_MODEL), lambda b, i: (b * nt + i, 0)),
                  pl.BlockSpec(g.shape, lambda b, i: (0, 0)),
                  pl.BlockSpec(w.shape, lambda b, i: (0, 0))],
        out_specs=[pl.BlockSpec((2, CMP_STRIDE, tm // CMP_STRIDE, KV_NAT), lambda b, i: (0, 0, b * nt + i, 0)),
                   pl.BlockSpec((None, gr, tm, LANE), lambda b, i: (b, 0, i, 0)),
                   pl.BlockSpec((None, gr, tm, LANE), lambda b, i: (b, 0, i, 0)),
                   pl.BlockSpec((None, gr, V_ROWS, tm), lambda b, i: (b, 0, 0, i)),
                   pl.BlockSpec((None, gr, V_ROWS, tm), lambda b, i: (b, 0, 0, i))],
        out_shape=[jax.ShapeDtypeStruct((2, CMP_STRIDE, t // CMP_STRIDE, KV_NAT), BF16),
                   jax.ShapeDtypeStruct((batch, gr, s, LANE), BF16),
                   jax.ShapeDtypeStruct((batch, gr, s, LANE), BF16),
                   jax.ShapeDtypeStruct((batch, gr, V_ROWS, s), BF16),
                   jax.ShapeDtypeStruct((batch, gr, V_ROWS, s), BF16)],
        compiler_params=_cparams("parallel", "parallel"),
        name="nsa_kv_proj",
    )(h, g, w)


def _compress_body(x_ref, wt_ref, wb_ref, pos_ref, w1_ref, b1_ref, w2_ref, b2_ref, on_ref, ot_ref, *, ncp):
    top = _dot(x_ref[0], wt_ref[0])
    bot = _dot(x_ref[0], wb_ref[0])
    for l in range(1, CMP_STRIDE):
        top = top + _dot(x_ref[l], wt_ref[l])
        bot = bot + _dot(x_ref[l], wb_ref[l])
    posb = _dot(pos_ref[...], w1_ref[...])[0:1, :] + b1_ref[...]
    for gi in range(NSA_GROUPS):
        sl = slice(gi * CMP_HIDDEN, (gi + 1) * CMP_HIDDEN)
        hid = top[:, sl] + pltpu.roll(bot[:, sl], ncp - 1, 0) + posb
        hid = (hid * _sigmoid(hid)).astype(BF16)
        out = _dot(hid, w2_ref[...]) + b2_ref[...]
        on_ref[gi] = out.astype(BF16)
        ot_ref[gi] = out.T.astype(BF16)


def _compress(x16, wt, wb, pos, w1, b1, w2, b2, batch):
    ncp = x16.shape[2] // batch
    gr = NSA_GROUPS
    per_j = lambda a: pl.BlockSpec((None,) + a.shape[1:], lambda j, b: (j,) + (0,) * (a.ndim - 1))
    return pl.pallas_call(
        functools.partial(_compress_body, ncp=ncp),
        grid=(2, batch),
        in_specs=[pl.BlockSpec((None, CMP_STRIDE, ncp, KV_NAT), lambda j, b: (j, 0, b, 0)),
                  per_j(wt), per_j(wb), per_j(pos), per_j(w1), per_j(b1), per_j(w2), per_j(b2)],
        out_specs=[pl.BlockSpec((None, None, gr, ncp, LANE), lambda j, b: (j, b, 0, 0, 0)),
                   pl.BlockSpec((None, None, gr, LANE, ncp), lambda j, b: (j, b, 0, 0, 0))],
        out_shape=[jax.ShapeDtypeStruct((2, batch, gr, ncp, LANE), BF16),
                   jax.ShapeDtypeStruct((2, batch, gr, LANE, ncp), BF16)],
        compiler_params=_cparams("arbitrary", "arbitrary"),
        name="nsa_compress",
    )(x16, wt, wb, pos, w1, b1, w2, b2)


NSA_QW = NSA_HEADS * NSA_HEAD_DIM
G3 = NSA_REP * TQ


def _nsa_in_body(h_ref, g_ref, w_ref, kv_ref, qt_ref, gt_ref, mo_ref):
    xn = _rms(h_ref[...], g_ref[...]).astype(BF16)
    for j in range(NSA_QW // LANE):
        y = _dot(xn, w_ref[:, j * LANE:(j + 1) * LANE]) * (NSA_HEAD_DIM ** -0.5 * LOG2E)
        qt_ref[j * LANE:(j + 1) * LANE, :] = y.T.astype(BF16)
    gt_ref[...] = _sigmoid(_dot(xn, w_ref[:, NSA_QW:NSA_QW + LANE])).T
    mo_ref[...] = _mem_attention(_dot(xn, w_ref[:, NSA_QW + LANE:]).astype(BF16), kv_ref)


def _nsa_in_proj(h, layer, nlayer, g, w, mem_kv, batch):
    t = h.shape[0]
    tm = ROW_TILE
    row = lambda n: pl.BlockSpec((tm, n), lambda i: (i, 0))
    nt = t // batch // tm
    col = lambda n: pl.BlockSpec((None, n, tm), lambda i: (i // nt, 0, i % nt))
    return pl.pallas_call(
        _nsa_in_body,
        grid=(t // tm,),
        in_specs=[row(D_MODEL), _layer_of(g, layer), _layer_of(w, nlayer), _mem_kv_spec(mem_kv, layer, t, tm, batch)],
        out_specs=[col(NSA_QW), col(LANE), row(MEM_QW)],
        out_shape=[jax.ShapeDtypeStruct((batch, NSA_QW, t // batch), BF16),
                   jax.ShapeDtypeStruct((batch, LANE, t // batch), F32),
                   jax.ShapeDtypeStruct((t, MEM_QW), BF16)],
        compiler_params=_cparams("parallel"),
        name="nsa_in_proj",
    )(h, g, w, mem_kv)


def _nsa_attn_body(qt_ref, gt_ref, kc_ref, vct_ref, ks_ref, kw_ref, vs_ref, vw_ref, ovt_ref, tz_ref, cb_ref,
                   o_ref, sc_ref, qa_ref, acc_ref, ot_ref, *s_slots, nsb, ncp, n_sel, nseq):
    n = pl.program_id(1)
    t0 = n * TQ
    dh = NSA_HEAD_DIM
    cstart = pl.multiple_of(ncp - n * (TQ // CMP_STRIDE), SUBLANE)
    sees_any = t0 + (lax.broadcasted_iota(jnp.int32, (1, G3), 1) & (TQ - 1)) >= CMP_BLOCK - 1
    jj = lax.broadcasted_iota(jnp.int32, (nsb, TQ), 0)
    cur = lax.shift_right_logical(t0 + lax.broadcasted_iota(jnp.int32, (nsb, TQ), 1), int(math.log2(SEL_BLOCK)))
    forced = (jj == 0) | (jj == cur) | (jj == cur - 1)
    zpad = jnp.zeros((dh, G3), BF16)
    nwt = WINDOW // TQ
    units = [(bi, gi) for bi in range(nseq) for gi in range(NSA_GROUPS)]

    def scores(k_ref, u, first, count, tz_index, qa):
        bi, gi = units[u]
        koff = pl.multiple_of(first * TQ, TQ)
        s = _dot(k_ref[bi, gi, pl.ds(koff, count * TQ), :], qa)
        return [s[i * TQ:(i + 1) * TQ] + tz_ref[gi, tz_index(n - (first + i))] for i in range(count)]

    def col_max(parts):
        mx = parts[0]
        for x in parts[1:]:
            mx = jnp.maximum(mx, x)
        return jnp.max(mx, axis=0, keepdims=True)

    def probs(parts, m):
        return jnp.concatenate([jnp.exp2(x - m).astype(BF16) for x in parts], axis=0)

    def values(v_ref, u, first, count, p):
        bi, gi = units[u]
        koff = pl.multiple_of(first * TQ, TQ)
        return _dot(v_ref[bi, gi, :, pl.ds(koff, count * TQ)], p)

    sel_index = lambda d: jnp.where(d < 0, 2, jnp.minimum(d, 2))
    win_index = lambda d: jnp.where(d < 0, 4, jnp.where(d == nwt, 3, jnp.minimum(d, 2)))

    groups = range(len(units))
    gate = lambda bi, h, c: gt_ref[bi, pl.ds(h * NSA_BRANCHES + c, 1), :]
    head_lanes = lambda r: slice(r * TQ, (r + 1) * TQ)
    head_rows = lambda bi, h: (bi, slice(h * dh, (h + 1) * dh))

    q3s = [jnp.concatenate([qt_ref[head_rows(bi, gi * NSA_REP + r)] for r in range(NSA_REP)], axis=1)
           for bi, gi in units]
    cs = [_dot(kc_ref[bi, gi], jnp.concatenate([q3s[u], zpad], axis=0)) + cb_ref[gi, pl.ds(cstart, ncp), :]
          for u, (bi, gi) in enumerate(units)]
    cps = [jnp.exp2(s - jnp.max(s, axis=0, keepdims=True)) for s in cs]
    cps = [p * jnp.where(sees_any, 1.0 / jnp.sum(p, axis=0, keepdims=True), 0.0) for p in cps]
    ocs = [_dot(vct_ref[bi, gi], cps[u].astype(BF16)) for u, (bi, gi) in enumerate(units)]
    scs = []
    for gi in groups:
        psum = cps[gi][:, 0:TQ]
        for r in range(1, NSA_REP):
            psum = psum + cps[gi][:, head_lanes(r)]
        p1 = psum.astype(BF16)
        p2 = (psum - p1.astype(F32)).astype(BF16)
        imp = _dot(ovt_ref[...], p1) + _dot(ovt_ref[...], p2)
        score = jnp.where(forced, 1e4, jnp.where(jj <= cur, imp[0:nsb], -1.0))
        scs.append(jnp.where(score < 0.0, -1, lax.bitcast_convert_type(score, jnp.int32)))
    for u, (bi, gi) in enumerate(units):
        sc_ref[u] = scs[u]
        for r in range(NSA_REP):
            h = gi * NSA_REP + r
            ot_ref[head_rows(bi, h)] = gate(bi, h, 0) * ocs[u][0:dh, head_lanes(r)]

    scs1 = [k + 1 for k in scs]

    def rank_step(i4, cnts):
        cnts = list(cnts)
        for u in range(RANK_UNROLL):
            i = i4 * RANK_UNROLL + u
            lower = i < jj
            for gi in groups:
                rowk = sc_ref[gi, pl.ds(i, 1), :]
                before = rowk >= jnp.where(lower, scs[gi], scs1[gi])
                cnts[gi] = cnts[gi] + jnp.where(before, 1, 0)
        return tuple(cnts)
    rank_trips = jnp.minimum((2 * n + 2 + RANK_UNROLL - 1) // RANK_UNROLL, nsb // RANK_UNROLL)
    cnts = lax.fori_loop(0, rank_trips, rank_step, tuple(jnp.zeros((nsb, TQ), jnp.int32) for _ in groups))
    for gi in groups:
        selneg = jnp.where((cnts[gi] < n_sel) & (jj <= cur), 0.0, NEG).astype(BF16)
        if nsb < SEL_BLOCK:
            selneg = jnp.concatenate([selneg, jnp.zeros((SEL_BLOCK - nsb, TQ), BF16)], axis=0)
        qa_ref[gi] = jnp.concatenate([q3s[gi], jnp.concatenate([selneg] * NSA_REP, axis=1)], axis=0)

    acc_ref[...] = jnp.zeros_like(acc_ref)
    half = len(units) // 2
    s_refs = (s_slots[:half], s_slots[half:])

    def score_half(it, hb):
        plist = [scores(ks_ref, hb * half + u, it * SEL_TILES, SEL_TILES, sel_index, qa_ref[hb * half + u])
                 for u in range(half)]
        for u, parts in enumerate(plist):
            for i, x in enumerate(parts):
                s_refs[hb][u][i * TQ:(i + 1) * TQ, :] = x
        return [col_max(parts) for parts in plist]

    def finish_half(it, hb, ms, bms):
        m2s = [jnp.maximum(ms[u], bms[u]) for u in range(half)]
        ps = [probs([s_refs[hb][u][i * TQ:(i + 1) * TQ, :] for i in range(SEL_TILES)], m2s[u]) for u in range(half)]
        vals = [values(vs_ref, hb * half + u, it * SEL_TILES, SEL_TILES, ps[u]) for u in range(half)]
        for u in range(half):
            gi = hb * half + u
            acc_ref[gi] = jnp.exp2(ms[u] - m2s[u]) * acc_ref[gi] + vals[u]
        return m2s

    def sel_trip(it, carry, score_next):
        ms0, ms1, bm0 = carry
        bm1 = score_half(it, 1)
        ms0 = finish_half(it, 0, ms0, bm0)
        if score_next:
            bm0 = score_half(it + 1, 0)
        ms1 = finish_half(it, 1, ms1, bm1)
        return ms0, ms1, bm0

    neg = [jnp.full((1, G3), NEG, F32) for _ in range(half)]
    carry = (neg, neg, score_half(0, 0))

    wfirst = jnp.maximum(n - nwt, 0)
    wparts = [scores(kw_ref, gi, wfirst, nwt + 1, win_index, qa_ref[gi]) for gi in groups]
    wps = [probs(parts, col_max(parts)) for parts in wparts]
    ows = [values(vw_ref, gi, wfirst, nwt + 1, wps[gi]) for gi in groups]
    for u, (bi, gi) in enumerate(units):
        o_w = ows[u][0:dh, :] * (1.0 / ows[u][dh:dh + 1, :])
        for r in range(NSA_REP):
            h = gi * NSA_REP + r
            ot_ref[head_rows(bi, h)] += gate(bi, h, 2) * o_w[:, head_lanes(r)]

    carry = lax.fori_loop(0, n // SEL_TILES, lambda it, c: sel_trip(it, c, True), carry)
    sel_trip(n // SEL_TILES, carry, False)

    for u, (bi, gi) in enumerate(units):
        o_s = acc_ref[u, 0:dh, :] * (1.0 / acc_ref[u, dh:dh + 1, :])
        for r in range(NSA_REP):
            h = gi * NSA_REP + r
            ot_ref[head_rows(bi, h)] += gate(bi, h, 1) * o_s[:, head_lanes(r)]

    for bi in range(nseq):
        for j in range(NSA_QW // LANE):
            o_ref[bi, :, j * LANE:(j + 1) * LANE] = ot_ref[bi, j * LANE:(j + 1) * LANE, :].T.astype(BF16)


def _nsa_attn(qt, gt, kc, vct, ksel, kwin, vsel, vwin, ovt, tz, cb):
    batch, _, s = qt.shape
    nseq = NSA_SEQS if batch % NSA_SEQS == 0 else 1
    nt = s // TQ
    nsb = s // SEL_BLOCK
    ncp = kc.shape[-2]
    nu = nseq * NSA_GROUPS
    per_b = lambda a: pl.BlockSpec((nseq,) + a.shape[1:], lambda b, i: (b,) + (0,) * (a.ndim - 1),
                                   pipeline_mode=pl.Buffered(1))
    out = pl.pallas_call(
        functools.partial(_nsa_attn_body, nsb=nsb, ncp=ncp, n_sel=min(SEL_TOPK, nsb), nseq=nseq),
        grid=(batch // nseq, nt),
        in_specs=[pl.BlockSpec((nseq, NSA_QW, TQ), lambda b, i: (b, 0, i)),
                  pl.BlockSpec((nseq, LANE, TQ), lambda b, i: (b, 0, i)),
                  per_b(kc), per_b(vct), per_b(ksel), per_b(kwin), per_b(vsel), per_b(vwin),
                  _resident(ovt), _resident(tz), _resident(cb)],
        out_specs=pl.BlockSpec((nseq, TQ, NSA_QW), lambda b, i: (b, i, 0)),
        out_shape=jax.ShapeDtypeStruct((batch, s, NSA_QW), BF16),
        scratch_shapes=[pltpu.VMEM((nu, nsb, TQ), jnp.int32), pltpu.VMEM((nu, LANE, G3), BF16),
                        pltpu.VMEM((nu, V_ROWS, G3), F32), pltpu.VMEM((nseq, NSA_QW, TQ), F32),
                        ] + [pltpu.VMEM((SEL_TILES * TQ, G3), F32)] * nu,
        compiler_params=_cparams("parallel", "arbitrary"),
        name="nsa_attn",
    )(qt, gt, kc, vct, ksel, kwin, vsel, vwin, ovt, tz, cb)
    return out.reshape(batch * s, NSA_QW)


def _rel_bucket_np(dist):
    dist = np.maximum(dist, 0)
    max_exact = REL_BUCKETS // 2
    ratio = np.log(np.maximum(dist, 1).astype(np.float32) / np.float32(max_exact)) / np.float32(
        math.log(REL_MAX_DIST / max_exact))
    large = max_exact + (ratio * np.float32(REL_BUCKETS - max_exact)).astype(np.int32)
    large = np.minimum(large, REL_BUCKETS - 1)
    return np.where(dist < max_exact, dist, large).astype(np.int32)


def _group_lanes(a):
    hh, r, c = a.shape
    return a.reshape(NSA_GROUPS, NSA_REP, r, c).transpose(0, 2, 1, 3).reshape(NSA_GROUPS, r, NSA_REP * c)


def _bias_tables(rel_bias, ncp):
    k = np.arange(TQ)[:, None]
    q = np.arange(TQ)[None, :]
    tbl = rel_bias.astype(F32)

    def lookup(idx):
        onehot = (jnp.asarray(idx.reshape(1, -1)) == jnp.arange(REL_BUCKETS)[:, None]).astype(F32)
        out = jnp.dot(tbl.T, onehot, precision=lax.Precision.HIGHEST)
        return out.reshape((NSA_HEADS,) + idx.shape)

    far = jnp.broadcast_to(tbl[REL_BUCKETS - 1][:, None, None], (NSA_HEADS, TQ, TQ))
    t0 = jnp.where(jnp.asarray(k <= q)[None], lookup(_rel_bucket_np(q - k)), NEG)
    t1 = lookup(_rel_bucket_np(TQ + q - k))
    t3 = jnp.where(jnp.asarray(k > q)[None], far, NEG)
    tz = jnp.stack([_group_lanes(x) for x in (t0, t1, far, t3, jnp.full_like(far, NEG))], axis=1)
    m = ncp - np.arange(2 * ncp)[:, None]
    d = CMP_STRIDE * m + q - (CMP_BLOCK - 1)
    idx = np.where((d >= 0) & (d < REL_MAX_DIST), _rel_bucket_np(d), REL_BUCKETS - 1)
    cb = _group_lanes(jnp.where(jnp.asarray(d >= 0)[None], lookup(idx), NEG))
    return tz * LOG2E, cb * LOG2E


def _overlap_table(s, ncp):
    nsb = s // SEL_BLOCK
    nc = (s - CMP_BLOCK) // CMP_STRIDE + 1
    cs = np.arange(ncp) * CMP_STRIDE
    ce = cs + CMP_BLOCK - 1
    ss = np.arange(SEL_BLOCK) * SEL_BLOCK
    ov = (cs[None, :] < ss[:, None] + SEL_BLOCK) & (ce[None, :] >= ss[:, None])
    ov &= (np.arange(ncp) < nc)[None, :] & (np.arange(SEL_BLOCK) < nsb)[:, None]
    return jnp.asarray(ov, dtype=BF16)


def kernel(x, mem, norm_mix, norm_mem, w_mem_kv, w_out, norm_ffn, w_up, conv_w, conv_b, w_down,
           gla_w_in, gla_w_gate_up, gla_b_gate, gla_out_norm, nsa_w_in, kv_norm, w_kv_shared,
           cmp_pos, cmp_w1, cmp_b1, cmp_w2, cmp_b2, rel_bias, final_norm):
    batch, seq = x.shape[0], x.shape[1]
    t = batch * seq
    h = x.reshape(t, D_MODEL)
    row = lambda v: v.reshape(1, -1).astype(F32)

    wk, wv = w_mem_kv[..., :MEM_W], w_mem_kv[..., MEM_W:]
    w_mkv = jnp.concatenate([_pad_heads(wk, MEM_HEADS, MEM_HEAD_DIM, MEM_DP),
                             _pad_heads(wv, MEM_HEADS, MEM_HEAD_DIM, MEM_DP)], axis=-1).astype(BF16)
    mem_kv_all = _mem_kv_proj(mem.reshape(-1, D_MODEL), norm_mem.reshape(DEPTH, 1, D_MODEL), w_mkv, batch)

    ffn_params = (norm_ffn.reshape(DEPTH, 1, D_MODEL).astype(F32), w_up.astype(BF16), conv_w.astype(F32),
                  conv_b.reshape(DEPTH, 1, 2 * FFN_DIM).astype(F32), w_down.astype(BF16))
    norm_mix3 = norm_mix.reshape(DEPTH, 1, D_MODEL).astype(F32)
    w_o_mem = _pad_head_rows(w_out[:, MAIN_W:], MEM_HEADS, MEM_HEAD_DIM, MEM_DP).astype(BF16)
    mem_pad = lambda w: _pad_heads(w, MEM_HEADS, MEM_HEAD_DIM, MEM_DP)

    c0 = GLA_HEADS * GLA_DK
    c1 = 2 * c0
    c2 = c1 + GLA_HEADS * GLA_DV
    c3 = c2 + GLA_HEADS * GLA_DV
    c4 = c3 + GLA_RANK
    gla_w = jnp.concatenate([
        _pad_heads(gla_w_in[..., :c0], GLA_HEADS, GLA_DK, GLA_DKP),
        _pad_heads(gla_w_in[..., c0:c1], GLA_HEADS, GLA_DK, GLA_DKP),
        _pad_heads(gla_w_in[..., c1:c2], GLA_HEADS, GLA_DV, GLA_DVP),
        _pad_heads(gla_w_in[..., c2:c3], GLA_HEADS, GLA_DV, GLA_DVP),
        _pad_heads(gla_w_in[..., c3:c4], 1, GLA_RANK, LANE),
        mem_pad(gla_w_in[..., c4:])], axis=-1).astype(BF16)
    gla_wg = jnp.pad(_pad_heads(gla_w_gate_up, GLA_HEADS, GLA_DK, GLA_DKP),
                     ((0, 0), (0, LANE - GLA_RANK), (0, 0))).astype(BF16)
    gla_bg = _pad_heads(gla_b_gate, GLA_HEADS, GLA_DK, GLA_DKP).reshape(N_A_LAYERS, 1, GLA_QW).astype(F32)
    gla_on = jnp.pad(gla_out_norm, ((0, 0), (0, GLA_DVP - GLA_DV))).reshape(N_A_LAYERS, 1, GLA_DVP).astype(F32)
    gla_wo = _pad_head_rows(w_out[:N_A_LAYERS, :MAIN_W], GLA_HEADS, GLA_DV, GLA_DVP).astype(BF16)

    n0 = NSA_HEADS * NSA_HEAD_DIM
    n1 = n0 + NSA_HEADS * NSA_BRANCHES
    nsa_w = jnp.concatenate([nsa_w_in[..., :n0], _pad_heads(nsa_w_in[..., n0:n1], 1, NSA_HEADS * NSA_BRANCHES, LANE),
                             mem_pad(nsa_w_in[..., n1:])], axis=-1).astype(BF16)
    nsa_wo = w_out[N_A_LAYERS:, :MAIN_W].astype(BF16)

    shared = None
    for i in range(DEPTH):
        if i < N_A_LAYERS:
            q, k, v, go, la, mo = _gla_in_proj(h, i, i, norm_mix3, gla_w, gla_wg, gla_bg, mem_kv_all, batch)
            main = _gla_mixer(q, k, la, v, go, gla_on, i, batch)
            w_o_main, klayer = gla_wo, i
        else:
            if shared is None:
                ncp = seq // CMP_STRIDE
                gr, dh = NSA_GROUPS, NSA_HEAD_DIM
                wkv = w_kv_shared.reshape(D_MODEL, 6, gr, dh)
                pair = lambda a, b: jnp.concatenate([wkv[:, a], wkv[:, b]], axis=-1).reshape(D_MODEL, gr * 2 * dh)
                slot = lambda a: _pad_heads(wkv[:, a].reshape(D_MODEL, KV_NAT), gr, dh, LANE)
                w_kv = jnp.concatenate([wkv[:, 0].reshape(D_MODEL, KV_NAT), wkv[:, 1].reshape(D_MODEL, KV_NAT),
                                        slot(2), slot(4), pair(3, 5)], axis=1).astype(BF16)
                x16, ksel, kwin, vsel, vwin = _kv_proj(h, row(kv_norm), w_kv, batch)
                w1 = cmp_w1.astype(BF16).reshape(2, 2, CMP_STRIDE, 1, dh, 1, CMP_HIDDEN)
                eye = jnp.eye(gr, dtype=BF16).reshape(1, 1, 1, gr, 1, gr, 1)
                w1x = (w1 * eye).reshape(2, 2, CMP_STRIDE, KV_NAT, gr * CMP_HIDDEN)
                pos8 = jnp.broadcast_to(cmp_pos.reshape(2, 1, CMP_BLOCK * dh), (2, SUBLANE, CMP_BLOCK * dh)).astype(BF16)
                w2p = jnp.pad(cmp_w2, ((0, 0), (0, 0), (0, LANE - dh))).astype(BF16)
                b2p = jnp.pad(cmp_b2, ((0, 0), (0, LANE - dh))).reshape(2, 1, LANE).astype(F32)
                cnat, ctr = _compress(x16, w1x[:, 0], w1x[:, 1], pos8, cmp_w1.astype(BF16),
                                      cmp_b1.reshape(2, 1, CMP_HIDDEN).astype(F32), w2p, b2p, batch)
                tz, cb = _bias_tables(rel_bias, ncp)
                ov = _overlap_table(seq, ncp)
                shared = (cnat[0], ctr[1], ksel, kwin, vsel, vwin, ov, tz, cb)
            q, gates, mo = _nsa_in_proj(h, i, i - N_A_LAYERS, norm_mix3, nsa_w, mem_kv_all, batch)
            main = _nsa_attn(q, gates, *shared)
            w_o_main, klayer = nsa_wo, i - N_A_LAYERS
        h = _mix_ffn(h, main, mo, w_o_main, klayer, w_o_mem, i, *ffn_params, row(final_norm), i == DEPTH - 1, batch)
    return h.reshape(batch, seq, D_MODEL)
```

```python
import functools
import math

import numpy as np
import jax
import jax.numpy as jnp
from jax import lax
from jax.experimental import pallas as pl
from jax.experimental.pallas import tpu as pltpu

F32 = jnp.float32
BF16 = jnp.bfloat16

D_MODEL = 1024
DEPTH = 4
N_A_LAYERS = DEPTH // 2
MEM_HEADS = 4
MEM_HEAD_DIM = 64
MEM_W = MEM_HEADS * MEM_HEAD_DIM
MAIN_W = D_MODEL - MEM_W
GLA_HEADS = 4
GLA_DV = MAIN_W // GLA_HEADS
GLA_DK = GLA_DV // 2
GLA_RANK = 16
GLA_GATE_NORM = 16.0
NSA_HEADS = 12
NSA_GROUPS = 4
NSA_HEAD_DIM = MAIN_W // NSA_HEADS
NSA_REP = NSA_HEADS // NSA_GROUPS
NSA_BRANCHES = 3
CMP_BLOCK = 32
CMP_STRIDE = 16
CMP_HIDDEN = 128
SEL_BLOCK = 64
SEL_TOPK = 16
WINDOW = 512
REL_BUCKETS = 32
REL_MAX_DIST = 128
FFN_DIM = 2816
CONV_WIDTH = 3
EPS = 1e-6

LANE = 128
MXU_N = 256
SUBLANE = 8
VMEM_LIMIT = 56 * 1024 * 1024
GLA_DKP = LANE
GLA_DVP = 2 * LANE
MEM_DP = LANE
NEG = -1e30
TQ = 128
ROW_TILE = 1024
GLA_CHUNK = 64
GLA_SAFE_DECAY = 80.0
FFN_TILE = 256
FFN_ROWS = 1024
SEL_TILES = 4
NSA_SEQS = 2
V_ROWS = NSA_HEAD_DIM + 2 * SUBLANE
RANK_UNROLL = 4
LOG2E = math.log2(math.e)

NT = (((1,), (1,)), ((), ()))
TN = (((0,), (0,)), ((), ()))


def _cparams(*sem):
    return pltpu.CompilerParams(dimension_semantics=sem, vmem_limit_bytes=VMEM_LIMIT)


def _rms(x, g):
    return x * lax.rsqrt(jnp.mean(x * x, axis=-1, keepdims=True) + EPS) * g


def _sigmoid(x):
    return 1.0 / (1.0 + jnp.exp(-x))


def _dot(a, b):
    return jnp.dot(a, b, preferred_element_type=F32)


def _resident(a):
    return pl.BlockSpec(a.shape, lambda *_: (0,) * a.ndim, pipeline_mode=pl.Buffered(1))


def _layer_of(a, layer):
    return pl.BlockSpec((None,) + a.shape[1:], lambda *_: (layer,) + (0,) * (a.ndim - 1),
                        pipeline_mode=pl.Buffered(1))


def _pad_heads(w, nh, d, dp):
    lead = w.shape[:-1]
    w = w.reshape(lead + (nh, d))
    w = jnp.pad(w, [(0, 0)] * len(lead) + [(0, 0), (0, dp - d)])
    return w.reshape(lead + (nh * dp,))


def _pad_head_rows(w, nh, d, dp):
    lead, n = w.shape[:-2], w.shape[-1]
    w = jnp.pad(w.reshape(lead + (nh, d, n)), [(0, 0)] * len(lead) + [(0, 0), (0, dp - d), (0, 0)])
    return w.reshape(lead + (nh * dp, n))


def _mem_attention(q, kv_ref):
    outs = []
    for hd in range(MEM_HEADS):
        sl = slice(hd * MEM_DP, (hd + 1) * MEM_DP)
        sv = slice(MEM_QW + hd * MEM_DP, MEM_QW + (hd + 1) * MEM_DP)
        s = lax.dot_general(q[:, sl], kv_ref[:, sl], NT, preferred_element_type=F32) * MEM_HEAD_DIM ** -0.5
        p = jnp.exp(s - jnp.max(s, axis=-1, keepdims=True))
        l = jnp.sum(p, axis=-1, keepdims=True)
        outs.append((_dot(p.astype(BF16), kv_ref[:, sv]) / l).astype(BF16))
    return jnp.concatenate(outs, axis=1)


GLA_QW = GLA_HEADS * GLA_DKP
GLA_VW = GLA_HEADS * GLA_DVP
MEM_QW = MEM_HEADS * MEM_DP
GLA_OFF_K = GLA_QW
GLA_OFF_V = 2 * GLA_QW
GLA_OFF_G = GLA_OFF_V + GLA_VW
GLA_OFF_LR = GLA_OFF_G + GLA_VW
GLA_OFF_MQ = GLA_OFF_LR + LANE


def _gla_in_body(h_ref, g_ref, w_ref, wg_ref, bg_ref, kv_ref, q_ref, k_ref, v_ref, go_ref, la_ref, mo_ref):
    xn = _rms(h_ref[...], g_ref[...]).astype(BF16)

    def proj(lo, n):
        return _dot(xn, w_ref[:, lo:lo + n])

    q_ref[...] = proj(0, GLA_QW).astype(BF16)
    k_ref[...] = proj(GLA_OFF_K, GLA_QW).astype(BF16)
    for j in range(GLA_VW // GLA_QW):
        v_ref[:, j * GLA_QW:(j + 1) * GLA_QW] = proj(GLA_OFF_V + j * GLA_QW, GLA_QW).astype(BF16)
        go_ref[:, j * GLA_QW:(j + 1) * GLA_QW] = proj(GLA_OFF_G + j * GLA_QW, GLA_QW).astype(BF16)
    lr = proj(GLA_OFF_LR, LANE).astype(BF16)
    z = _dot(lr, wg_ref[...]) + bg_ref[...]
    la_ref[...] = (jnp.minimum(z, 0.0) - jnp.log(1.0 + jnp.exp(-jnp.abs(z)))) * (1.0 / GLA_GATE_NORM)
    mo_ref[...] = _mem_attention(proj(GLA_OFF_MQ, MEM_QW).astype(BF16), kv_ref)


def _mem_kv_spec(mem_kv, layer, t, tm, batch):
    nt = t // batch // tm
    return pl.BlockSpec((None, mem_kv.shape[1] // batch, mem_kv.shape[2]), lambda i: (layer, i // nt, 0))


def _gla_in_proj(h, layer, glayer, g, w, wg, bg, mem_kv, batch):
    t = h.shape[0]
    tm = ROW_TILE
    row = lambda n: pl.BlockSpec((tm, n), lambda i: (i, 0))
    return pl.pallas_call(
        _gla_in_body,
        grid=(t // tm,),
        in_specs=[row(D_MODEL), _layer_of(g, layer), _layer_of(w, glayer), _layer_of(wg, glayer),
                  _layer_of(bg, glayer), _mem_kv_spec(mem_kv, layer, t, tm, batch)],
        out_specs=[row(GLA_QW), row(GLA_QW), row(GLA_VW), row(GLA_VW), row(GLA_QW), row(MEM_QW)],
        out_shape=[jax.ShapeDtypeStruct((t, GLA_QW), BF16), jax.ShapeDtypeStruct((t, GLA_QW), BF16),
                   jax.ShapeDtypeStruct((t, GLA_VW), BF16), jax.ShapeDtypeStruct((t, GLA_VW), BF16),
                   jax.ShapeDtypeStruct((t, GLA_QW), F32), jax.ShapeDtypeStruct((t, MEM_QW), BF16)],
        compiler_params=_cparams("parallel"),
        name="gla_in_proj",
    )(h, g, w, wg, bg, mem_kv)


def _gla_mix_body(q_ref, k_ref, la_ref, v_ref, go_ref, on_ref, o_ref, st_ref, inter_ref, b_scr, k_scr, v_scr, *,
                  cn, batch):
    @pl.when(pl.program_id(0) == 0)
    def _():
        st_ref[...] = jnp.zeros_like(st_ref)

    row = lax.broadcasted_iota(jnp.int32, (cn, cn), 0)
    col = lax.broadcasted_iota(jnp.int32, (cn, cn), 1)
    causal = row >= col
    tril = jnp.where(causal, 1.0, 0.0).astype(BF16)
    scale = GLA_DK ** -0.5
    chains = [(bi, hd) for bi in range(batch) for hd in range(GLA_HEADS)]
    sk = lambda hd: slice(hd * GLA_DKP, (hd + 1) * GLA_DKP)
    sv = lambda hd: slice(hd * GLA_DVP, (hd + 1) * GLA_DVP)
    dg = lambda x, y, dims: lax.dot_general(x, y, dims, preferred_element_type=F32)

    def cum_decay(bi, hd):
        la = la_ref[bi, :, sk(hd)]
        la1 = la.astype(BF16)
        r1 = la - la1.astype(F32)
        la2 = r1.astype(BF16)
        la3 = (r1 - la2.astype(F32)).astype(BF16)
        return _dot(tril, la1) + _dot(tril, la2) + _dot(tril, la3)

    def finish(i, o):
        bi, hd = chains[i]
        ms = jnp.sum(o * o, axis=-1, keepdims=True) * (1.0 / GLA_DV)
        y = o * lax.rsqrt(ms + EPS) * on_ref[...]
        g = go_ref[bi, :, sv(hd)].astype(F32)
        o_ref[bi, :, sv(hd)] = (y * (g * _sigmoid(g))).astype(BF16)

    bs = [cum_decay(bi, hd) for bi, hd in chains]
    scaled = []
    for (bi, hd), b in zip(chains, bs):
        b_last = b[cn - 1:cn, :]
        q = q_ref[bi, :, sk(hd)].astype(F32)
        k = k_ref[bi, :, sk(hd)].astype(F32)
        scaled.append(((q * jnp.exp(b) * scale).astype(BF16),
                       (k * jnp.exp(jnp.minimum(-b, GLA_SAFE_DECAY))).astype(BF16),
                       (k * jnp.exp(b_last - b)).astype(BF16), jnp.exp(b_last)))
    intra = [jnp.where(causal, dg(qs, ks, NT), 0.0).astype(BF16) for qs, ks, _, _ in scaled]
    outs = []
    for i, (bi, hd) in enumerate(chains):
        qs, _, ko, a_last = scaled[i]
        v = v_ref[bi, :, sv(hd)]
        st = st_ref[i]
        inter = dg(qs, st.astype(BF16), NT)
        inter_ref[i] = inter
        outs.append(_dot(intra[i], v) + inter)
        st_ref[i] = st * a_last + dg(v, ko, TN)
    for i, o in enumerate(outs):
        finish(i, o)

    b_min = bs[0][cn - 1:cn, :]
    for b in bs[1:]:
        b_min = jnp.minimum(b_min, b[cn - 1:cn, :])

    @pl.when(jnp.min(b_min) < -GLA_SAFE_DECAY)
    def _():
        rows = lax.broadcasted_iota(jnp.int32, (cn, 1), 0)
        for i, (bi, hd) in enumerate(chains):
            b = cum_decay(bi, hd)
            qf = q_ref[bi, :, sk(hd)].astype(F32) * scale
            b_scr[...] = b
            k_scr[...] = k_ref[bi, :, sk(hd)].astype(F32)
            v_scr[...] = v_ref[bi, :, sv(hd)].astype(F32)

            def add_key(j, acc):
                decay = jnp.exp(jnp.minimum(b - b_scr[pl.ds(j, 1), :], 0.0))
                a_col = jnp.sum(qf * k_scr[pl.ds(j, 1), :] * decay, axis=-1, keepdims=True)
                return acc + jnp.where(rows >= j, a_col, 0.0) * v_scr[pl.ds(j, 1), :]
            finish(i, lax.fori_loop(0, cn, add_key, inter_ref[i]))


def _gla_mixer(q, k, la, v, go, on, glayer, batch):
    t = q.shape[0]
    s = t // batch
    cn = GLA_CHUNK
    seq = lambda a: a.reshape(batch, s, a.shape[1])
    blk = lambda n: pl.BlockSpec((batch, cn, n), lambda c: (0, c, 0))
    out = pl.pallas_call(
        functools.partial(_gla_mix_body, cn=cn, batch=batch),
        grid=(s // cn,),
        in_specs=[blk(GLA_QW), blk(GLA_QW), blk(GLA_QW), blk(GLA_VW), blk(GLA_VW),
                  _layer_of(on, glayer)],
        out_specs=blk(GLA_VW),
        out_shape=jax.ShapeDtypeStruct((batch, s, GLA_VW), BF16),
        scratch_shapes=[pltpu.VMEM((batch * GLA_HEADS, GLA_DVP, GLA_DKP), F32),
                        pltpu.VMEM((batch * GLA_HEADS, cn, GLA_DVP), F32), pltpu.VMEM((cn, GLA_DKP), F32),
                        pltpu.VMEM((cn, GLA_DKP), F32), pltpu.VMEM((cn, GLA_DVP), F32)],
        compiler_params=_cparams("arbitrary"),
        name="gla_mixer",
    )(seq(q), seq(k), seq(la), seq(v), seq(go), on)
    return out.reshape(t, GLA_VW)


def _mem_kv_body(m_ref, g_ref, w_ref, o_ref):
    xn = _rms(m_ref[...], g_ref[...]).astype(BF16)
    o_ref[...] = _dot(xn, w_ref[...]).astype(BF16)


def _mem_kv_proj(mem2, g, w, batch):
    nl = w.shape[0]
    m = mem2.shape[0] // batch
    n = w.shape[2]
    return pl.pallas_call(
        _mem_kv_body,
        grid=(nl, batch),
        in_specs=[pl.BlockSpec((m, D_MODEL), lambda l, b: (b, 0)),
                  pl.BlockSpec((None, 1, D_MODEL), lambda l, b: (l, 0, 0)),
                  pl.BlockSpec((None, D_MODEL, n), lambda l, b: (l, 0, 0))],
        out_specs=pl.BlockSpec((None, m, n), lambda l, b: (l, b, 0)),
        out_shape=jax.ShapeDtypeStruct((nl, mem2.shape[0], n), BF16),
        compiler_params=_cparams("arbitrary", "arbitrary"),
        name="mem_kv_proj",
    )(mem2, g, w)


BF16_ROWS = 2 * SUBLANE


def _mix_ffn_body(h_ref, hp_ref, a_ref, ap_ref, m_ref, mp_ref, wa_ref, wm_ref, g_ref, wup_ref, cw_ref, cb_ref,
                  wdn_ref, gf_ref, o_ref, act_ref, *, tm, tf, last):
    g = g_ref[...]
    nr = tm // SUBLANE
    interleave = lambda x: x.reshape(SUBLANE, nr, x.shape[-1]).swapaxes(0, 1).reshape(tm, x.shape[-1])
    deinterleave = lambda x: x.reshape(nr, SUBLANE, x.shape[-1]).swapaxes(0, 1).reshape(tm, x.shape[-1])
    h = interleave(h_ref[...] + _dot(a_ref[...], wa_ref[...]) + _dot(m_ref[...], wm_ref[...]))
    x = _rms(h, g).astype(BF16)
    keep = jnp.where(pl.program_id(1) > 0, 1.0, 0.0)
    h_prev = (hp_ref[...] + _dot(ap_ref[...], wa_ref[...]) + _dot(mp_ref[...], wm_ref[...]))[BF16_ROWS - SUBLANE:]
    x_prev = (_rms(h_prev, g) * keep).astype(BF16)
    first = lax.broadcasted_iota(jnp.int32, (SUBLANE, tf), 0) == 0
    nchunk = FFN_DIM // tf

    def up(j):
        cols = [pl.ds(off + j * tf, tf) for off in (0, FFN_DIM)]
        return tuple((_dot(x, wup_ref[:, c]), _dot(x_prev, wup_ref[:, c])) for c in cols)

    def conv(u, u_prev, off):
        w = cw_ref[:, off:off + tf]
        wrap = lambda r, k: jnp.where(first, u_prev[SUBLANE - k:SUBLANE - k + 1, :],
                                      pltpu.roll(u[r * SUBLANE:(r + 1) * SUBLANE, :], 1, 0))
        back1 = jnp.concatenate([wrap(nr - 1, 1), u[0:tm - SUBLANE, :]], axis=0)
        back2 = jnp.concatenate([wrap(nr - 2, 2), wrap(nr - 1, 1), u[0:tm - 2 * SUBLANE, :]], axis=0)
        return cb_ref[:, off:off + tf] + w[0:1, :] * back2 + w[1:2, :] * back1 + w[2:3, :] * u

    u_next = up(0)
    for j in range(nchunk):
        (ua, ua_prev), (ub, ub_prev) = u_next
        if j + 1 < nchunk:
            u_next = up(j + 1)
        a = conv(ua, ua_prev, j * tf)
        b = conv(ub, ub_prev, FFN_DIM + j * tf)
        act_ref[:, j * tf:(j + 1) * tf] = (a * _sigmoid(a) * b).astype(BF16)
    out = h + _dot(act_ref[...], wdn_ref[...])
    o_ref[...] = deinterleave(_rms(out, gf_ref[...]) if last else out)


def _mix_ffn(h, main, mo, wa, klayer, wm, layer, g, wup, cw, cb, wdn, gf, last, batch):
    t = h.shape[0]
    tm = FFN_ROWS
    nt = t // batch // tm
    hb = tm // BF16_ROWS
    cur = lambda n: pl.BlockSpec((tm, n), lambda b, i: (b * nt + i, 0))
    prev = lambda n: pl.BlockSpec((BF16_ROWS, n), lambda b, i: (jnp.maximum((b * nt + i) * hb - 1, 0), 0))
    return pl.pallas_call(
        functools.partial(_mix_ffn_body, tm=tm, tf=FFN_TILE, last=last),
        grid=(batch, nt),
        in_specs=[cur(D_MODEL), prev(D_MODEL), cur(main.shape[1]), prev(main.shape[1]),
                  cur(mo.shape[1]), prev(mo.shape[1]), _layer_of(wa, klayer), _layer_of(wm, layer), _layer_of(g, layer),
                  _layer_of(wup, layer), _layer_of(cw, layer), _layer_of(cb, layer), _layer_of(wdn, layer),
                  _resident(gf)],
        out_specs=cur(D_MODEL),
        out_shape=jax.ShapeDtypeStruct((t, D_MODEL), F32),
        scratch_shapes=[pltpu.VMEM((tm, FFN_DIM), BF16)],
        compiler_params=_cparams("parallel", "parallel"),
        name="mix_ffn",
    )(h, h, main, main, mo, mo, wa, wm, g, wup, cw, cb, wdn, gf)


KV_NAT = NSA_GROUPS * NSA_HEAD_DIM


def _kv_proj_body(h_ref, g_ref, w_ref, cx_ref, ks_ref, kw_ref, vs_ref, vw_ref, *, tm):
    xn = _rms(h_ref[...], g_ref[...]).astype(BF16)
    for j in range(2):
        y = _dot(xn, w_ref[:, j * KV_NAT:(j + 1) * KV_NAT])
        cx_ref[j] = y.reshape(tm // CMP_STRIDE, CMP_STRIDE, KV_NAT).swapaxes(0, 1).astype(BF16)
    key = pl.program_id(1) * tm + lax.broadcasted_iota(jnp.int32, (tm, LANE), 0)
    lane = lax.broadcasted_iota(jnp.int32, (tm, LANE), 1)
    onehot = jnp.where(lax.shift_right_logical(key, int(math.log2(SEL_BLOCK))) == lane - NSA_HEAD_DIM, 1.0, 0.0)
    low = lane < NSA_HEAD_DIM
    ones = jnp.ones((V_ROWS - NSA_HEAD_DIM, tm), BF16)
    slots = lambda n: _dot(xn, w_ref[:, 2 * KV_NAT + n * MXU_N:2 * KV_NAT + (n + 1) * MXU_N])
    for gi in range(NSA_GROUPS):
        kk = slots(gi)
        ks_ref[gi] = jnp.where(low, kk[:, 0:LANE], onehot).astype(BF16)
        kw_ref[gi] = jnp.where(low, kk[:, LANE:], 0.0).astype(BF16)
    for gp in range(NSA_GROUPS // 2):
        vv = slots(NSA_GROUPS + gp)
        for i in range(2):
            vt = vv[:, i * LANE:(i + 1) * LANE].T.astype(BF16)
            vs_ref[2 * gp + i] = jnp.concatenate([vt[0:NSA_HEAD_DIM], ones], axis=0)
            vw_ref[2 * gp + i] = jnp.concatenate([vt[NSA_HEAD_DIM:], ones], axis=0)


def _kv_proj(h, g, w, batch):
    t = h.shape[0]
    s = t // batch
    tm = ROW_TILE
    nt = s // tm
    gr = NSA_GROUPS
    return pl.pallas_call(
        functools.partial(_kv_proj_body, tm=tm),
        grid=(batch, nt),
        in_specs=[pl.BlockSpec((tm, D_MODEL), lambda b, i: (b * nt + i, 0)),
                  pl.BlockSpec(g.shape, lambda b, i: (0, 0)),
                  pl.BlockSpec(w.shape, lambda b, i: (0, 0))],
        out_specs=[pl.BlockSpec((2, CMP_STRIDE, tm // CMP_STRIDE, KV_NAT), lambda b, i: (0, 0, b * nt + i, 0)),
                   pl.BlockSpec((None, gr, tm, LANE), lambda b, i: (b, 0, i, 0)),
                   pl.BlockSpec((None, gr, tm, LANE), lambda b, i: (b, 0, i, 0)),
                   pl.BlockSpec((None, gr, V_ROWS, tm), lambda b, i: (b, 0, 0, i)),
                   pl.BlockSpec((None, gr, V_ROWS, tm), lambda b, i: (b, 0, 0, i))],
        out_shape=[jax.ShapeDtypeStruct((2, CMP_STRIDE, t // CMP_STRIDE, KV_NAT), BF16),
                   jax.ShapeDtypeStruct((batch, gr, s, LANE), BF16),
                   jax.ShapeDtypeStruct((batch, gr, s, LANE), BF16),
                   jax.ShapeDtypeStruct((batch, gr, V_ROWS, s), BF16),
                   jax.ShapeDtypeStruct((batch, gr, V_ROWS, s), BF16)],
        compiler_params=_cparams("parallel", "parallel"),
        name="nsa_kv_proj",
    )(h, g, w)


def _compress_body(x_ref, wt_ref, wb_ref, pos_ref, w1_ref, b1_ref, w2_ref, b2_ref, on_ref, ot_ref, *, ncp):
    top = _dot(x_ref[0], wt_ref[0])
    bot = _dot(x_ref[0], wb_ref[0])
    for l in range(1, CMP_STRIDE):
        top = top + _dot(x_ref[l], wt_ref[l])
        bot = bot + _dot(x_ref[l], wb_ref[l])
    posb = _dot(pos_ref[...], w1_ref[...])[0:1, :] + b1_ref[...]
    for gi in range(NSA_GROUPS):
        sl = slice(gi * CMP_HIDDEN, (gi + 1) * CMP_HIDDEN)
        hid = top[:, sl] + pltpu.roll(bot[:, sl], ncp - 1, 0) + posb
        hid = (hid * _sigmoid(hid)).astype(BF16)
        out = _dot(hid, w2_ref[...]) + b2_ref[...]
        on_ref[gi] = out.astype(BF16)
        ot_ref[gi] = out.T.astype(BF16)


def _compress(x16, wt, wb, pos, w1, b1, w2, b2, batch):
    ncp = x16.shape[2] // batch
    gr = NSA_GROUPS
    per_j = lambda a: pl.BlockSpec((None,) + a.shape[1:], lambda j, b: (j,) + (0,) * (a.ndim - 1))
    return pl.pallas_call(
        functools.partial(_compress_body, ncp=ncp),
        grid=(2, batch),
        in_specs=[pl.BlockSpec((None, CMP_STRIDE, ncp, KV_NAT), lambda j, b: (j, 0, b, 0)),
                  per_j(wt), per_j(wb), per_j(pos), per_j(w1), per_j(b1), per_j(w2), per_j(b2)],
        out_specs=[pl.BlockSpec((None, None, gr, ncp, LANE), lambda j, b: (j, b, 0, 0, 0)),
                   pl.BlockSpec((None, None, gr, LANE, ncp), lambda j, b: (j, b, 0, 0, 0))],
        out_shape=[jax.ShapeDtypeStruct((2, batch, gr, ncp, LANE), BF16),
                   jax.ShapeDtypeStruct((2, batch, gr, LANE, ncp), BF16)],
        compiler_params=_cparams("arbitrary", "arbitrary"),
        name="nsa_compress",
    )(x16, wt, wb, pos, w1, b1, w2, b2)


NSA_QW = NSA_HEADS * NSA_HEAD_DIM
G3 = NSA_REP * TQ


def _nsa_in_body(h_ref, g_ref, w_ref, kv_ref, qt_ref, gt_ref, mo_ref):
    xn = _rms(h_ref[...], g_ref[...]).astype(BF16)
    for j in range(NSA_QW // MXU_N):
        y = _dot(xn, w_ref[:, j * MXU_N:(j + 1) * MXU_N]) * (NSA_HEAD_DIM ** -0.5 * LOG2E)
        qt_ref[j * MXU_N:(j + 1) * MXU_N, :] = y.T.astype(BF16)
    gt_ref[...] = _sigmoid(_dot(xn, w_ref[:, NSA_QW:NSA_QW + LANE])).T
    mo_ref[...] = _mem_attention(_dot(xn, w_ref[:, NSA_QW + LANE:]).astype(BF16), kv_ref)


def _nsa_in_proj(h, layer, nlayer, g, w, mem_kv, batch):
    t = h.shape[0]
    tm = ROW_TILE
    row = lambda n: pl.BlockSpec((tm, n), lambda i: (i, 0))
    nt = t // batch // tm
    col = lambda n: pl.BlockSpec((None, n, tm), lambda i: (i // nt, 0, i % nt))
    return pl.pallas_call(
        _nsa_in_body,
        grid=(t // tm,),
        in_specs=[row(D_MODEL), _layer_of(g, layer), _layer_of(w, nlayer), _mem_kv_spec(mem_kv, layer, t, tm, batch)],
        out_specs=[col(NSA_QW), col(LANE), row(MEM_QW)],
        out_shape=[jax.ShapeDtypeStruct((batch, NSA_QW, t // batch), BF16),
                   jax.ShapeDtypeStruct((batch, LANE, t // batch), F32),
                   jax.ShapeDtypeStruct((t, MEM_QW), BF16)],
        compiler_params=_cparams("parallel"),
        name="nsa_in_proj",
    )(h, g, w, mem_kv)


def _nsa_attn_body(qt_ref, gt_ref, kc_ref, vct_ref, ks_ref, kw_ref, vs_ref, vw_ref, ovt_ref, tz_ref, cb_ref,
                   o_ref, sc_ref, qa_ref, acc_ref, ot_ref, *s_slots, nsb, ncp, n_sel, nseq):
    n = pl.program_id(1)
    t0 = n * TQ
    dh = NSA_HEAD_DIM
    cstart = pl.multiple_of(ncp - n * (TQ // CMP_STRIDE), SUBLANE)
    sees_any = t0 + (lax.broadcasted_iota(jnp.int32, (1, G3), 1) & (TQ - 1)) >= CMP_BLOCK - 1
    jj = lax.broadcasted_iota(jnp.int32, (nsb, TQ), 0)
    cur = lax.shift_right_logical(t0 + lax.broadcasted_iota(jnp.int32, (nsb, TQ), 1), int(math.log2(SEL_BLOCK)))
    forced = (jj == 0) | (jj == cur) | (jj == cur - 1)
    zpad = jnp.zeros((dh, G3), BF16)
    nwt = WINDOW // TQ
    units = [(bi, gi) for bi in range(nseq) for gi in range(NSA_GROUPS)]

    def scores(k_ref, u, first, count, tz_index, qa):
        bi, gi = units[u]
        koff = pl.multiple_of(first * TQ, TQ)
        s = _dot(k_ref[bi, gi, pl.ds(koff, count * TQ), :], qa)
        return [s[i * TQ:(i + 1) * TQ] + tz_ref[gi, tz_index(n - (first + i))] for i in range(count)]

    def col_max(parts):
        mx = parts[0]
        for x in parts[1:]:
            mx = jnp.maximum(mx, x)
        return jnp.max(mx, axis=0, keepdims=True)

    def probs(parts, m):
        return jnp.concatenate([jnp.exp2(x - m).astype(BF16) for x in parts], axis=0)

    def values(v_ref, u, first, count, p):
        bi, gi = units[u]
        koff = pl.multiple_of(first * TQ, TQ)
        return _dot(v_ref[bi, gi, :, pl.ds(koff, count * TQ)], p)

    sel_index = lambda d: jnp.where(d < 0, 2, jnp.minimum(d, 2))
    win_index = lambda d: jnp.where(d < 0, 4, jnp.where(d == nwt, 3, jnp.minimum(d, 2)))

    groups = range(len(units))
    gate = lambda bi, h, c: gt_ref[bi, pl.ds(h * NSA_BRANCHES + c, 1), :]
    head_lanes = lambda r: slice(r * TQ, (r + 1) * TQ)
    head_rows = lambda bi, h: (bi, slice(h * dh, (h + 1) * dh))

    q3s = [jnp.concatenate([qt_ref[head_rows(bi, gi * NSA_REP + r)] for r in range(NSA_REP)], axis=1)
           for bi, gi in units]
    cs = [_dot(kc_ref[bi, gi], jnp.concatenate([q3s[u], zpad], axis=0)) + cb_ref[gi, pl.ds(cstart, ncp), :]
          for u, (bi, gi) in enumerate(units)]
    cps = [jnp.exp2(s - jnp.max(s, axis=0, keepdims=True)) for s in cs]
    cps = [p * jnp.where(sees_any, 1.0 / jnp.sum(p, axis=0, keepdims=True), 0.0) for p in cps]
    ocs = [_dot(vct_ref[bi, gi], cps[u].astype(BF16)) for u, (bi, gi) in enumerate(units)]
    scs = []
    for gi in groups:
        psum = cps[gi][:, 0:TQ]
        for r in range(1, NSA_REP):
            psum = psum + cps[gi][:, head_lanes(r)]
        p1 = psum.astype(BF16)
        p2 = (psum - p1.astype(F32)).astype(BF16)
        imp = _dot(ovt_ref[...], p1) + _dot(ovt_ref[...], p2)
        score = jnp.where(forced, 1e4, jnp.where(jj <= cur, imp[0:nsb], -1.0))
        scs.append(jnp.where(score < 0.0, -1, lax.bitcast_convert_type(score, jnp.int32)))
    for u, (bi, gi) in enumerate(units):
        sc_ref[u] = scs[u]
        for r in range(NSA_REP):
            h = gi * NSA_REP + r
            ot_ref[head_rows(bi, h)] = gate(bi, h, 0) * ocs[u][0:dh, head_lanes(r)]

    scs1 = [k + 1 for k in scs]

    def rank_step(i4, cnts):
        cnts = list(cnts)
        for u in range(RANK_UNROLL):
            i = i4 * RANK_UNROLL + u
            lower = i < jj
            for gi in groups:
                rowk = sc_ref[gi, pl.ds(i, 1), :]
                before = rowk >= jnp.where(lower, scs[gi], scs1[gi])
                cnts[gi] = cnts[gi] + jnp.where(before, 1, 0)
        return tuple(cnts)
    rank_trips = jnp.minimum((2 * n + 2 + RANK_UNROLL - 1) // RANK_UNROLL, nsb // RANK_UNROLL)
    cnts = lax.fori_loop(0, rank_trips, rank_step, tuple(jnp.zeros((nsb, TQ), jnp.int32) for _ in groups))
    for gi in groups:
        selneg = jnp.where((cnts[gi] < n_sel) & (jj <= cur), 0.0, NEG).astype(BF16)
        if nsb < SEL_BLOCK:
            selneg = jnp.concatenate([selneg, jnp.zeros((SEL_BLOCK - nsb, TQ), BF16)], axis=0)
        qa_ref[gi] = jnp.concatenate([q3s[gi], jnp.concatenate([selneg] * NSA_REP, axis=1)], axis=0)

    acc_ref[...] = jnp.zeros_like(acc_ref)
    half = len(units) // 2
    s_refs = (s_slots[:half], s_slots[half:])

    def score_half(it, hb):
        plist = [scores(ks_ref, hb * half + u, it * SEL_TILES, SEL_TILES, sel_index, qa_ref[hb * half + u])
                 for u in range(half)]
        for u, parts in enumerate(plist):
            for i, x in enumerate(parts):
                s_refs[hb][u][i * TQ:(i + 1) * TQ, :] = x
        return [col_max(parts) for parts in plist]

    def finish_half(it, hb, ms, bms):
        m2s = [jnp.maximum(ms[u], bms[u]) for u in range(half)]
        ps = [probs([s_refs[hb][u][i * TQ:(i + 1) * TQ, :] for i in range(SEL_TILES)], m2s[u]) for u in range(half)]
        vals = [values(vs_ref, hb * half + u, it * SEL_TILES, SEL_TILES, ps[u]) for u in range(half)]
        for u in range(half):
            gi = hb * half + u
            acc_ref[gi] = jnp.exp2(ms[u] - m2s[u]) * acc_ref[gi] + vals[u]
        return m2s

    def sel_trip(it, carry, score_next):
        ms0, ms1, bm0 = carry
        bm1 = score_half(it, 1)
        ms0 = finish_half(it, 0, ms0, bm0)
        if score_next:
            bm0 = score_half(it + 1, 0)
        ms1 = finish_half(it, 1, ms1, bm1)
        return ms0, ms1, bm0

    neg = [jnp.full((1, G3), NEG, F32) for _ in range(half)]
    carry = (neg, neg, score_half(0, 0))

    wfirst = jnp.maximum(n - nwt, 0)
    wparts = [scores(kw_ref, gi, wfirst, nwt + 1, win_index, qa_ref[gi]) for gi in groups]
    wps = [probs(parts, col_max(parts)) for parts in wparts]
    ows = [values(vw_ref, gi, wfirst, nwt + 1, wps[gi]) for gi in groups]
    for u, (bi, gi) in enumerate(units):
        o_w = ows[u][0:dh, :] * (1.0 / ows[u][dh:dh + 1, :])
        for r in range(NSA_REP):
            h = gi * NSA_REP + r
            ot_ref[head_rows(bi, h)] += gate(bi, h, 2) * o_w[:, head_lanes(r)]

    carry = lax.fori_loop(0, n // SEL_TILES, lambda it, c: sel_trip(it, c, True), carry)
    sel_trip(n // SEL_TILES, carry, False)

    for u, (bi, gi) in enumerate(units):
        o_s = acc_ref[u, 0:dh, :] * (1.0 / acc_ref[u, dh:dh + 1, :])
        for r in range(NSA_REP):
            h = gi * NSA_REP + r
            ot_ref[head_rows(bi, h)] += gate(bi, h, 1) * o_s[:, head_lanes(r)]

    for bi in range(nseq):
        for j in range(NSA_QW // LANE):
            o_ref[bi, :, j * LANE:(j + 1) * LANE] = ot_ref[bi, j * LANE:(j + 1) * LANE, :].T.astype(BF16)


def _nsa_attn(qt, gt, kc, vct, ksel, kwin, vsel, vwin, ovt, tz, cb):
    batch, _, s = qt.shape
    nseq = NSA_SEQS if batch % NSA_SEQS == 0 else 1
    nt = s // TQ
    nsb = s // SEL_BLOCK
    ncp = kc.shape[-2]
    nu = nseq * NSA_GROUPS
    per_b = lambda a: pl.BlockSpec((nseq,) + a.shape[1:], lambda b, i: (b,) + (0,) * (a.ndim - 1),
                                   pipeline_mode=pl.Buffered(1))
    out = pl.pallas_call(
        functools.partial(_nsa_attn_body, nsb=nsb, ncp=ncp, n_sel=min(SEL_TOPK, nsb), nseq=nseq),
        grid=(batch // nseq, nt),
        in_specs=[pl.BlockSpec((nseq, NSA_QW, TQ), lambda b, i: (b, 0, i)),
                  pl.BlockSpec((nseq, LANE, TQ), lambda b, i: (b, 0, i)),
                  per_b(kc), per_b(vct), per_b(ksel), per_b(kwin), per_b(vsel), per_b(vwin),
                  _resident(ovt), _resident(tz), _resident(cb)],
        out_specs=pl.BlockSpec((nseq, TQ, NSA_QW), lambda b, i: (b, i, 0)),
        out_shape=jax.ShapeDtypeStruct((batch, s, NSA_QW), BF16),
        scratch_shapes=[pltpu.VMEM((nu, nsb, TQ), jnp.int32), pltpu.VMEM((nu, LANE, G3), BF16),
                        pltpu.VMEM((nu, V_ROWS, G3), F32), pltpu.VMEM((nseq, NSA_QW, TQ), F32),
                        ] + [pltpu.VMEM((SEL_TILES * TQ, G3), F32)] * nu,
        compiler_params=_cparams("parallel", "arbitrary"),
        name="nsa_attn",
    )(qt, gt, kc, vct, ksel, kwin, vsel, vwin, ovt, tz, cb)
    return out.reshape(batch * s, NSA_QW)


def _rel_bucket_np(dist):
    dist = np.maximum(dist, 0)
    max_exact = REL_BUCKETS // 2
    ratio = np.log(np.maximum(dist, 1).astype(np.float32) / np.float32(max_exact)) / np.float32(
        math.log(REL_MAX_DIST / max_exact))
    large = max_exact + (ratio * np.float32(REL_BUCKETS - max_exact)).astype(np.int32)
    large = np.minimum(large, REL_BUCKETS - 1)
    return np.where(dist < max_exact, dist, large).astype(np.int32)


def _group_lanes(a):
    hh, r, c = a.shape
    return a.reshape(NSA_GROUPS, NSA_REP, r, c).transpose(0, 2, 1, 3).reshape(NSA_GROUPS, r, NSA_REP * c)


def _bias_tables(rel_bias, ncp):
    k = np.arange(TQ)[:, None]
    q = np.arange(TQ)[None, :]
    tbl = rel_bias.astype(F32)

    def lookup(idx):
        onehot = (jnp.asarray(idx.reshape(1, -1)) == jnp.arange(REL_BUCKETS)[:, None]).astype(F32)
        out = jnp.dot(tbl.T, onehot, precision=lax.Precision.HIGHEST)
        return out.reshape((NSA_HEADS,) + idx.shape)

    far = jnp.broadcast_to(tbl[REL_BUCKETS - 1][:, None, None], (NSA_HEADS, TQ, TQ))
    t0 = jnp.where(jnp.asarray(k <= q)[None], lookup(_rel_bucket_np(q - k)), NEG)
    t1 = lookup(_rel_bucket_np(TQ + q - k))
    t3 = jnp.where(jnp.asarray(k > q)[None], far, NEG)
    tz = jnp.stack([_group_lanes(x) for x in (t0, t1, far, t3, jnp.full_like(far, NEG))], axis=1)
    m = ncp - np.arange(2 * ncp)[:, None]
    d = CMP_STRIDE * m + q - (CMP_BLOCK - 1)
    idx = np.where((d >= 0) & (d < REL_MAX_DIST), _rel_bucket_np(d), REL_BUCKETS - 1)
    cb = _group_lanes(jnp.where(jnp.asarray(d >= 0)[None], lookup(idx), NEG))
    return tz * LOG2E, cb * LOG2E


def _overlap_table(s, ncp):
    nsb = s // SEL_BLOCK
    nc = (s - CMP_BLOCK) // CMP_STRIDE + 1
    cs = np.arange(ncp) * CMP_STRIDE
    ce = cs + CMP_BLOCK - 1
    ss = np.arange(SEL_BLOCK) * SEL_BLOCK
    ov = (cs[None, :] < ss[:, None] + SEL_BLOCK) & (ce[None, :] >= ss[:, None])
    ov &= (np.arange(ncp) < nc)[None, :] & (np.arange(SEL_BLOCK) < nsb)[:, None]
    return jnp.asarray(ov, dtype=BF16)


def kernel(x, mem, norm_mix, norm_mem, w_mem_kv, w_out, norm_ffn, w_up, conv_w, conv_b, w_down,
           gla_w_in, gla_w_gate_up, gla_b_gate, gla_out_norm, nsa_w_in, kv_norm, w_kv_shared,
           cmp_pos, cmp_w1, cmp_b1, cmp_w2, cmp_b2, rel_bias, final_norm):
    batch, seq = x.shape[0], x.shape[1]
    t = batch * seq
    h = x.reshape(t, D_MODEL)
    row = lambda v: v.reshape(1, -1).astype(F32)

    wk, wv = w_mem_kv[..., :MEM_W], w_mem_kv[..., MEM_W:]
    w_mkv = jnp.concatenate([_pad_heads(wk, MEM_HEADS, MEM_HEAD_DIM, MEM_DP),
                             _pad_heads(wv, MEM_HEADS, MEM_HEAD_DIM, MEM_DP)], axis=-1).astype(BF16)
    mem_kv_all = _mem_kv_proj(mem.reshape(-1, D_MODEL), norm_mem.reshape(DEPTH, 1, D_MODEL), w_mkv, batch)

    ffn_params = (norm_ffn.reshape(DEPTH, 1, D_MODEL).astype(F32), w_up.astype(BF16), conv_w.astype(F32),
                  conv_b.reshape(DEPTH, 1, 2 * FFN_DIM).astype(F32), w_down.astype(BF16))
    norm_mix3 = norm_mix.reshape(DEPTH, 1, D_MODEL).astype(F32)
    w_o_mem = _pad_head_rows(w_out[:, MAIN_W:], MEM_HEADS, MEM_HEAD_DIM, MEM_DP).astype(BF16)
    mem_pad = lambda w: _pad_heads(w, MEM_HEADS, MEM_HEAD_DIM, MEM_DP)

    c0 = GLA_HEADS * GLA_DK
    c1 = 2 * c0
    c2 = c1 + GLA_HEADS * GLA_DV
    c3 = c2 + GLA_HEADS * GLA_DV
    c4 = c3 + GLA_RANK
    gla_w = jnp.concatenate([
        _pad_heads(gla_w_in[..., :c0], GLA_HEADS, GLA_DK, GLA_DKP),
        _pad_heads(gla_w_in[..., c0:c1], GLA_HEADS, GLA_DK, GLA_DKP),
        _pad_heads(gla_w_in[..., c1:c2], GLA_HEADS, GLA_DV, GLA_DVP),
        _pad_heads(gla_w_in[..., c2:c3], GLA_HEADS, GLA_DV, GLA_DVP),
        _pad_heads(gla_w_in[..., c3:c4], 1, GLA_RANK, LANE),
        mem_pad(gla_w_in[..., c4:])], axis=-1).astype(BF16)
    gla_wg = jnp.pad(_pad_heads(gla_w_gate_up, GLA_HEADS, GLA_DK, GLA_DKP),
                     ((0, 0), (0, LANE - GLA_RANK), (0, 0))).astype(BF16)
    gla_bg = _pad_heads(gla_b_gate, GLA_HEADS, GLA_DK, GLA_DKP).reshape(N_A_LAYERS, 1, GLA_QW).astype(F32)
    gla_on = jnp.pad(gla_out_norm, ((0, 0), (0, GLA_DVP - GLA_DV))).reshape(N_A_LAYERS, 1, GLA_DVP).astype(F32)
    gla_wo = _pad_head_rows(w_out[:N_A_LAYERS, :MAIN_W], GLA_HEADS, GLA_DV, GLA_DVP).astype(BF16)

    n0 = NSA_HEADS * NSA_HEAD_DIM
    n1 = n0 + NSA_HEADS * NSA_BRANCHES
    nsa_w = jnp.concatenate([nsa_w_in[..., :n0], _pad_heads(nsa_w_in[..., n0:n1], 1, NSA_HEADS * NSA_BRANCHES, LANE),
                             mem_pad(nsa_w_in[..., n1:])], axis=-1).astype(BF16)
    nsa_wo = w_out[N_A_LAYERS:, :MAIN_W].astype(BF16)

    shared = None
    for i in range(DEPTH):
        if i < N_A_LAYERS:
            q, k, v, go, la, mo = _gla_in_proj(h, i, i, norm_mix3, gla_w, gla_wg, gla_bg, mem_kv_all, batch)
            main = _gla_mixer(q, k, la, v, go, gla_on, i, batch)
            w_o_main, klayer = gla_wo, i
        else:
            if shared is None:
                ncp = seq // CMP_STRIDE
                gr, dh = NSA_GROUPS, NSA_HEAD_DIM
                wkv = w_kv_shared.reshape(D_MODEL, 6, gr, dh)
                pair = lambda a, b: jnp.concatenate([wkv[:, a], wkv[:, b]], axis=-1).reshape(D_MODEL, gr * 2 * dh)
                slot = lambda a: _pad_heads(wkv[:, a].reshape(D_MODEL, KV_NAT), gr, dh, LANE)
                ksw = jnp.stack([slot(2).reshape(D_MODEL, gr, LANE), slot(4).reshape(D_MODEL, gr, LANE)], axis=2)
                w_kv = jnp.concatenate([wkv[:, 0].reshape(D_MODEL, KV_NAT), wkv[:, 1].reshape(D_MODEL, KV_NAT),
                                        ksw.reshape(D_MODEL, 2 * gr * LANE), pair(3, 5)], axis=1).astype(BF16)
                x16, ksel, kwin, vsel, vwin = _kv_proj(h, row(kv_norm), w_kv, batch)
                w1 = cmp_w1.reshape(2, 2, CMP_STRIDE, dh, CMP_HIDDEN)
                eye = jnp.eye(gr, dtype=F32)
                w1x = jnp.einsum('jhldc,gk->jhlgdkc', w1, eye).reshape(2, 2, CMP_STRIDE, KV_NAT, gr * CMP_HIDDEN)
                w1x = w1x.astype(BF16)
                pos8 = jnp.broadcast_to(cmp_pos.reshape(2, 1, CMP_BLOCK * dh), (2, SUBLANE, CMP_BLOCK * dh)).astype(BF16)
                w2p = jnp.pad(cmp_w2, ((0, 0), (0, 0), (0, LANE - dh))).astype(BF16)
                b2p = jnp.pad(cmp_b2, ((0, 0), (0, LANE - dh))).reshape(2, 1, LANE).astype(F32)
                cnat, ctr = _compress(x16, w1x[:, 0], w1x[:, 1], pos8, cmp_w1.astype(BF16),
                                      cmp_b1.reshape(2, 1, CMP_HIDDEN).astype(F32), w2p, b2p, batch)
                tz, cb = _bias_tables(rel_bias, ncp)
                ov = _overlap_table(seq, ncp)
                shared = (cnat[0], ctr[1], ksel, kwin, vsel, vwin, ov, tz, cb)
            q, gates, mo = _nsa_in_proj(h, i, i - N_A_LAYERS, norm_mix3, nsa_w, mem_kv_all, batch)
            main = _nsa_attn(q, gates, *shared)
            w_o_main, klayer = nsa_wo, i - N_A_LAYERS
        h = _mix_ffn(h, main, mo, w_o_main, klayer, w_o_mem, i, *ffn_params, row(final_norm), i == DEPTH - 1, batch)
    return h.reshape(batch, seq, D_MODEL)
```

```python
import functools
import math

import numpy as np
import jax
import jax.numpy as jnp
from jax import lax
from jax.experimental import pallas as pl
from jax.experimental.pallas import tpu as pltpu

F32 = jnp.float32
BF16 = jnp.bfloat16

D_MODEL = 1024
DEPTH = 4
N_A_LAYERS = DEPTH // 2
MEM_HEADS = 4
MEM_HEAD_DIM = 64
MEM_W = MEM_HEADS * MEM_HEAD_DIM
MAIN_W = D_MODEL - MEM_W
GLA_HEADS = 4
GLA_DV = MAIN_W // GLA_HEADS
GLA_DK = GLA_DV // 2
GLA_RANK = 16
GLA_GATE_NORM = 16.0
NSA_HEADS = 12
NSA_GROUPS = 4
NSA_HEAD_DIM = MAIN_W // NSA_HEADS
NSA_REP = NSA_HEADS // NSA_GROUPS
NSA_BRANCHES = 3
CMP_BLOCK = 32
CMP_STRIDE = 16
CMP_HIDDEN = 128
SEL_BLOCK = 64
SEL_TOPK = 16
WINDOW = 512
REL_BUCKETS = 32
REL_MAX_DIST = 128
FFN_DIM = 2816
CONV_WIDTH = 3
EPS = 1e-6

LANE = 128
MXU_N = 256
SUBLANE = 8
VMEM_LIMIT = 56 * 1024 * 1024
GLA_DKP = LANE
GLA_DVP = 2 * LANE
MEM_DP = LANE
NEG = -1e30
TQ = 128
ROW_TILE = 1024
GLA_CHUNK = 64
GLA_SAFE_DECAY = 80.0
FFN_TILE = 256
FFN_ROWS = 1024
SEL_TILES = 4
NSA_SEQS = 2
V_ROWS = NSA_HEAD_DIM + 2 * SUBLANE
RANK_UNROLL = 4
LOG2E = math.log2(math.e)

NT = (((1,), (1,)), ((), ()))
TN = (((0,), (0,)), ((), ()))


def _cparams(*sem):
    return pltpu.CompilerParams(dimension_semantics=sem, vmem_limit_bytes=VMEM_LIMIT)


def _rms(x, g):
    return x * lax.rsqrt(jnp.mean(x * x, axis=-1, keepdims=True) + EPS) * g


def _sigmoid(x):
    return 1.0 / (1.0 + jnp.exp(-x))


def _dot(a, b):
    return jnp.dot(a, b, preferred_element_type=F32)


def _resident(a):
    return pl.BlockSpec(a.shape, lambda *_: (0,) * a.ndim, pipeline_mode=pl.Buffered(1))


def _layer_of(a, layer):
    return pl.BlockSpec((None,) + a.shape[1:], lambda *_: (layer,) + (0,) * (a.ndim - 1),
                        pipeline_mode=pl.Buffered(1))


def _pad_heads(w, nh, d, dp):
    lead = w.shape[:-1]
    w = w.reshape(lead + (nh, d))
    w = jnp.pad(w, [(0, 0)] * len(lead) + [(0, 0), (0, dp - d)])
    return w.reshape(lead + (nh * dp,))


def _pad_head_rows(w, nh, d, dp):
    lead, n = w.shape[:-2], w.shape[-1]
    w = jnp.pad(w.reshape(lead + (nh, d, n)), [(0, 0)] * len(lead) + [(0, 0), (0, dp - d), (0, 0)])
    return w.reshape(lead + (nh * dp, n))


def _mem_attention(q, kv_ref):
    outs = []
    for hd in range(MEM_HEADS):
        sl = slice(hd * MEM_DP, (hd + 1) * MEM_DP)
        sv = slice(MEM_QW + hd * MEM_DP, MEM_QW + (hd + 1) * MEM_DP)
        s = lax.dot_general(q[:, sl], kv_ref[:, sl], NT, preferred_element_type=F32) * MEM_HEAD_DIM ** -0.5
        p = jnp.exp(s - jnp.max(s, axis=-1, keepdims=True))
        l = jnp.sum(p, axis=-1, keepdims=True)
        outs.append((_dot(p.astype(BF16), kv_ref[:, sv]) / l).astype(BF16))
    return jnp.concatenate(outs, axis=1)


GLA_QW = GLA_HEADS * GLA_DKP
GLA_VW = GLA_HEADS * GLA_DVP
MEM_QW = MEM_HEADS * MEM_DP
GLA_QK = GLA_HEADS * GLA_DK
GLA_VD = GLA_HEADS * GLA_DV
GLA_OFF_V = 2 * GLA_QK
GLA_OFF_G = GLA_OFF_V + GLA_VD
GLA_OFF_MQ = GLA_OFF_G + GLA_VD
GLA_OFF_LR = GLA_OFF_MQ + MEM_W


def _gla_in_body(h_ref, g_ref, w_ref, wg_ref, bg_ref, kv_ref, q_ref, k_ref, v_ref, go_ref, la_ref, mo_ref):
    xn = _rms(h_ref[...], g_ref[...]).astype(BF16)

    tm = xn.shape[0]

    def proj(lo, n):
        return _dot(xn, w_ref[:, lo:lo + n])

    def spread(y, off, nh, d, dp):
        pad = jnp.zeros((tm, dp - d), F32)
        return jnp.concatenate([x for hd in range(nh) for x in (y[:, off + hd * d:off + (hd + 1) * d], pad)], axis=1)

    qk = proj(0, 2 * GLA_QK)
    q_ref[...] = spread(qk, 0, GLA_HEADS, GLA_DK, GLA_DKP).astype(BF16)
    k_ref[...] = spread(qk, GLA_QK, GLA_HEADS, GLA_DK, GLA_DKP).astype(BF16)
    v_ref[...] = spread(proj(GLA_OFF_V, GLA_VD), 0, GLA_HEADS, GLA_DV, GLA_DVP).astype(BF16)
    go_ref[...] = spread(proj(GLA_OFF_G, GLA_VD), 0, GLA_HEADS, GLA_DV, GLA_DVP).astype(BF16)
    tail = proj(GLA_OFF_MQ, MEM_W + LANE)
    z = _dot(tail[:, MEM_W:].astype(BF16), wg_ref[...]) + bg_ref[...]
    la_ref[...] = (jnp.minimum(z, 0.0) - jnp.log(1.0 + jnp.exp(-jnp.abs(z)))) * (1.0 / GLA_GATE_NORM)
    mo_ref[...] = _mem_attention(spread(tail, 0, MEM_HEADS, MEM_HEAD_DIM, MEM_DP).astype(BF16), kv_ref)


def _mem_kv_spec(mem_kv, layer, t, tm, batch):
    nt = t // batch // tm
    return pl.BlockSpec((None, mem_kv.shape[1] // batch, mem_kv.shape[2]), lambda i: (layer, i // nt, 0))


def _gla_in_proj(h, layer, glayer, g, w, wg, bg, mem_kv, batch):
    t = h.shape[0]
    tm = ROW_TILE
    row = lambda n: pl.BlockSpec((tm, n), lambda i: (i, 0))
    return pl.pallas_call(
        _gla_in_body,
        grid=(t // tm,),
        in_specs=[row(D_MODEL), _layer_of(g, layer), _layer_of(w, glayer), _layer_of(wg, glayer),
                  _layer_of(bg, glayer), _mem_kv_spec(mem_kv, layer, t, tm, batch)],
        out_specs=[row(GLA_QW), row(GLA_QW), row(GLA_VW), row(GLA_VW), row(GLA_QW), row(MEM_QW)],
        out_shape=[jax.ShapeDtypeStruct((t, GLA_QW), BF16), jax.ShapeDtypeStruct((t, GLA_QW), BF16),
                   jax.ShapeDtypeStruct((t, GLA_VW), BF16), jax.ShapeDtypeStruct((t, GLA_VW), BF16),
                   jax.ShapeDtypeStruct((t, GLA_QW), F32), jax.ShapeDtypeStruct((t, MEM_QW), BF16)],
        compiler_params=_cparams("parallel"),
        name="gla_in_proj",
    )(h, g, w, wg, bg, mem_kv)


def _gla_mix_body(q_ref, k_ref, la_ref, v_ref, go_ref, on_ref, o_ref, st_ref, inter_ref, b_scr, k_scr, v_scr, *,
                  cn, batch):
    @pl.when(pl.program_id(0) == 0)
    def _():
        st_ref[...] = jnp.zeros_like(st_ref)

    row = lax.broadcasted_iota(jnp.int32, (cn, cn), 0)
    col = lax.broadcasted_iota(jnp.int32, (cn, cn), 1)
    causal = row >= col
    tril = jnp.where(causal, 1.0, 0.0).astype(BF16)
    scale = GLA_DK ** -0.5
    chains = [(bi, hd) for bi in range(batch) for hd in range(GLA_HEADS)]
    sk = lambda hd: slice(hd * GLA_DKP, (hd + 1) * GLA_DKP)
    sv = lambda hd: slice(hd * GLA_DVP, (hd + 1) * GLA_DVP)
    dg = lambda x, y, dims: lax.dot_general(x, y, dims, preferred_element_type=F32)

    def cum_decay(bi, hd):
        la = la_ref[bi, :, sk(hd)]
        la1 = la.astype(BF16)
        r1 = la - la1.astype(F32)
        la2 = r1.astype(BF16)
        la3 = (r1 - la2.astype(F32)).astype(BF16)
        return _dot(tril, la1) + _dot(tril, la2) + _dot(tril, la3)

    def finish(i, o):
        bi, hd = chains[i]
        ms = jnp.sum(o * o, axis=-1, keepdims=True) * (1.0 / GLA_DV)
        y = o * lax.rsqrt(ms + EPS) * on_ref[...]
        g = go_ref[bi, :, sv(hd)].astype(F32)
        o_ref[bi, :, sv(hd)] = (y * (g * _sigmoid(g))).astype(BF16)

    bs = [cum_decay(bi, hd) for bi, hd in chains]
    scaled = []
    for (bi, hd), b in zip(chains, bs):
        b_last = b[cn - 1:cn, :]
        q = q_ref[bi, :, sk(hd)].astype(F32)
        k = k_ref[bi, :, sk(hd)].astype(F32)
        scaled.append(((q * jnp.exp(b) * scale).astype(BF16),
                       (k * jnp.exp(jnp.minimum(-b, GLA_SAFE_DECAY))).astype(BF16),
                       (k * jnp.exp(b_last - b)).astype(BF16), jnp.exp(b_last)))
    intra = [jnp.where(causal, dg(qs, ks, NT), 0.0).astype(BF16) for qs, ks, _, _ in scaled]
    outs = []
    for i, (bi, hd) in enumerate(chains):
        qs, _, ko, a_last = scaled[i]
        v = v_ref[bi, :, sv(hd)]
        st = st_ref[i]
        inter = dg(qs, st.astype(BF16), NT)
        inter_ref[i] = inter
        outs.append(_dot(intra[i], v) + inter)
        st_ref[i] = st * a_last + dg(v, ko, TN)
    for i, o in enumerate(outs):
        finish(i, o)

    b_min = bs[0][cn - 1:cn, :]
    for b in bs[1:]:
        b_min = jnp.minimum(b_min, b[cn - 1:cn, :])

    @pl.when(jnp.min(b_min) < -GLA_SAFE_DECAY)
    def _():
        rows = lax.broadcasted_iota(jnp.int32, (cn, 1), 0)
        for i, (bi, hd) in enumerate(chains):
            b = cum_decay(bi, hd)
            qf = q_ref[bi, :, sk(hd)].astype(F32) * scale
            b_scr[...] = b
            k_scr[...] = k_ref[bi, :, sk(hd)].astype(F32)
            v_scr[...] = v_ref[bi, :, sv(hd)].astype(F32)

            def add_key(j, acc):
                decay = jnp.exp(jnp.minimum(b - b_scr[pl.ds(j, 1), :], 0.0))
                a_col = jnp.sum(qf * k_scr[pl.ds(j, 1), :] * decay, axis=-1, keepdims=True)
                return acc + jnp.where(rows >= j, a_col, 0.0) * v_scr[pl.ds(j, 1), :]
            finish(i, lax.fori_loop(0, cn, add_key, inter_ref[i]))


def _gla_mixer(q, k, la, v, go, on, glayer, batch):
    t = q.shape[0]
    s = t // batch
    cn = GLA_CHUNK
    seq = lambda a: a.reshape(batch, s, a.shape[1])
    blk = lambda n: pl.BlockSpec((batch, cn, n), lambda c: (0, c, 0))
    out = pl.pallas_call(
        functools.partial(_gla_mix_body, cn=cn, batch=batch),
        grid=(s // cn,),
        in_specs=[blk(GLA_QW), blk(GLA_QW), blk(GLA_QW), blk(GLA_VW), blk(GLA_VW),
                  _layer_of(on, glayer)],
        out_specs=blk(GLA_VW),
        out_shape=jax.ShapeDtypeStruct((batch, s, GLA_VW), BF16),
        scratch_shapes=[pltpu.VMEM((batch * GLA_HEADS, GLA_DVP, GLA_DKP), F32),
                        pltpu.VMEM((batch * GLA_HEADS, cn, GLA_DVP), F32), pltpu.VMEM((cn, GLA_DKP), F32),
                        pltpu.VMEM((cn, GLA_DKP), F32), pltpu.VMEM((cn, GLA_DVP), F32)],
        compiler_params=_cparams("arbitrary"),
        name="gla_mixer",
    )(seq(q), seq(k), seq(la), seq(v), seq(go), on)
    return out.reshape(t, GLA_VW)


def _mem_kv_body(m_ref, g_ref, w_ref, o_ref):
    xn = _rms(m_ref[...], g_ref[...]).astype(BF16)
    o_ref[...] = _dot(xn, w_ref[...]).astype(BF16)


def _mem_kv_proj(mem2, g, w, batch):
    nl = w.shape[0]
    m = mem2.shape[0] // batch
    n = w.shape[2]
    return pl.pallas_call(
        _mem_kv_body,
        grid=(nl, batch),
        in_specs=[pl.BlockSpec((m, D_MODEL), lambda l, b: (b, 0)),
                  pl.BlockSpec((None, 1, D_MODEL), lambda l, b: (l, 0, 0)),
                  pl.BlockSpec((None, D_MODEL, n), lambda l, b: (l, 0, 0))],
        out_specs=pl.BlockSpec((None, m, n), lambda l, b: (l, b, 0)),
        out_shape=jax.ShapeDtypeStruct((nl, mem2.shape[0], n), BF16),
        compiler_params=_cparams("arbitrary", "arbitrary"),
        name="mem_kv_proj",
    )(mem2, g, w)


BF16_ROWS = 2 * SUBLANE


def _mix_ffn_body(h_ref, hp_ref, a_ref, ap_ref, m_ref, mp_ref, wa_ref, wm_ref, g_ref, wup_ref, cw_ref, cb_ref,
                  wdn_ref, gf_ref, o_ref, act_ref, *, tm, tf, last):
    g = g_ref[...]
    nr = tm // SUBLANE
    interleave = lambda x: x.reshape(SUBLANE, nr, x.shape[-1]).swapaxes(0, 1).reshape(tm, x.shape[-1])
    deinterleave = lambda x: x.reshape(nr, SUBLANE, x.shape[-1]).swapaxes(0, 1).reshape(tm, x.shape[-1])
    h = interleave(h_ref[...] + _dot(a_ref[...], wa_ref[...]) + _dot(m_ref[...], wm_ref[...]))
    x = _rms(h, g).astype(BF16)
    keep = jnp.where(pl.program_id(1) > 0, 1.0, 0.0)
    h_prev = (hp_ref[...] + _dot(ap_ref[...], wa_ref[...]) + _dot(mp_ref[...], wm_ref[...]))[BF16_ROWS - SUBLANE:]
    x_prev = (_rms(h_prev, g) * keep).astype(BF16)
    first = lax.broadcasted_iota(jnp.int32, (SUBLANE, tf), 0) == 0
    nchunk = FFN_DIM // tf

    def up(j):
        cols = [pl.ds(off + j * tf, tf) for off in (0, FFN_DIM)]
        return tuple((_dot(x, wup_ref[:, c]), _dot(x_prev, wup_ref[:, c])) for c in cols)

    def conv(u, u_prev, off):
        w = cw_ref[:, off:off + tf]
        wrap = lambda r, k: jnp.where(first, u_prev[SUBLANE - k:SUBLANE - k + 1, :],
                                      pltpu.roll(u[r * SUBLANE:(r + 1) * SUBLANE, :], 1, 0))
        back1 = jnp.concatenate([wrap(nr - 1, 1), u[0:tm - SUBLANE, :]], axis=0)
        back2 = jnp.concatenate([wrap(nr - 2, 2), wrap(nr - 1, 1), u[0:tm - 2 * SUBLANE, :]], axis=0)
        return cb_ref[:, off:off + tf] + w[0:1, :] * back2 + w[1:2, :] * back1 + w[2:3, :] * u

    u_next = up(0)
    for j in range(nchunk):
        (ua, ua_prev), (ub, ub_prev) = u_next
        if j + 1 < nchunk:
            u_next = up(j + 1)
        a = conv(ua, ua_prev, j * tf)
        b = conv(ub, ub_prev, FFN_DIM + j * tf)
        act_ref[:, j * tf:(j + 1) * tf] = (a * _sigmoid(a) * b).astype(BF16)
    out = h + _dot(act_ref[...], wdn_ref[...])
    o_ref[...] = deinterleave(_rms(out, gf_ref[...]) if last else out)


def _mix_ffn(h, main, mo, wa, klayer, wm, layer, g, wup, cw, cb, wdn, gf, last, batch):
    t = h.shape[0]
    tm = FFN_ROWS
    nt = t // batch // tm
    hb = tm // BF16_ROWS
    cur = lambda n: pl.BlockSpec((tm, n), lambda b, i: (b * nt + i, 0))
    prev = lambda n: pl.BlockSpec((BF16_ROWS, n), lambda b, i: (jnp.maximum((b * nt + i) * hb - 1, 0), 0))
    return pl.pallas_call(
        functools.partial(_mix_ffn_body, tm=tm, tf=FFN_TILE, last=last),
        grid=(batch, nt),
        in_specs=[cur(D_MODEL), prev(D_MODEL), cur(main.shape[1]), prev(main.shape[1]),
                  cur(mo.shape[1]), prev(mo.shape[1]), _layer_of(wa, klayer), _layer_of(wm, layer), _layer_of(g, layer),
                  _layer_of(wup, layer), _layer_of(cw, layer), _layer_of(cb, layer), _layer_of(wdn, layer),
                  _resident(gf)],
        out_specs=cur(D_MODEL),
        out_shape=jax.ShapeDtypeStruct((t, D_MODEL), F32),
        scratch_shapes=[pltpu.VMEM((tm, FFN_DIM), BF16)],
        compiler_params=_cparams("parallel", "parallel"),
        name="mix_ffn",
    )(h, h, main, main, mo, mo, wa, wm, g, wup, cw, cb, wdn, gf)


KV_NAT = NSA_GROUPS * NSA_HEAD_DIM


def _kv_proj_body(h_ref, g_ref, w_ref, cx_ref, ks_ref, kw_ref, vs_ref, vw_ref, *, tm):
    xn = _rms(h_ref[...], g_ref[...]).astype(BF16)
    for j in range(2):
        y = _dot(xn, w_ref[:, j * KV_NAT:(j + 1) * KV_NAT])
        cx_ref[j] = y.reshape(tm // CMP_STRIDE, CMP_STRIDE, KV_NAT).swapaxes(0, 1).astype(BF16)
    key = pl.program_id(1) * tm + lax.broadcasted_iota(jnp.int32, (tm, LANE), 0)
    lane = lax.broadcasted_iota(jnp.int32, (tm, LANE), 1)
    onehot = jnp.where(lax.shift_right_logical(key, int(math.log2(SEL_BLOCK))) == lane - NSA_HEAD_DIM, 1.0, 0.0)
    low = lane < NSA_HEAD_DIM
    ones = jnp.ones((V_ROWS - NSA_HEAD_DIM, tm), BF16)
    slots = lambda n: _dot(xn, w_ref[:, 2 * KV_NAT + n * MXU_N:2 * KV_NAT + (n + 1) * MXU_N])
    for gi in range(NSA_GROUPS):
        kk = slots(gi)
        ks_ref[gi] = jnp.where(low, kk[:, 0:LANE], onehot).astype(BF16)
        kw_ref[gi] = jnp.where(low, kk[:, LANE:], 0.0).astype(BF16)
    for gp in range(NSA_GROUPS // 2):
        vv = slots(NSA_GROUPS + gp)
        for i in range(2):
            vt = vv[:, i * LANE:(i + 1) * LANE].T.astype(BF16)
            vs_ref[2 * gp + i] = jnp.concatenate([vt[0:NSA_HEAD_DIM], ones], axis=0)
            vw_ref[2 * gp + i] = jnp.concatenate([vt[NSA_HEAD_DIM:], ones], axis=0)


def _kv_proj(h, g, w, batch):
    t = h.shape[0]
    s = t // batch
    tm = ROW_TILE
    nt = s // tm
    gr = NSA_GROUPS
    return pl.pallas_call(
        functools.partial(_kv_proj_body, tm=tm),
        grid=(batch, nt),
        in_specs=[pl.BlockSpec((tm, D_MODEL), lambda b, i: (b * nt + i, 0)),
                  pl.BlockSpec(g.shape, lambda b, i: (0, 0)),
                  pl.BlockSpec(w.shape, lambda b, i: (0, 0))],
        out_specs=[pl.BlockSpec((2, CMP_STRIDE, tm // CMP_STRIDE, KV_NAT), lambda b, i: (0, 0, b * nt + i, 0)),
                   pl.BlockSpec((None, gr, tm, LANE), lambda b, i: (b, 0, i, 0)),
                   pl.BlockSpec((None, gr, tm, LANE), lambda b, i: (b, 0, i, 0)),
                   pl.BlockSpec((None, gr, V_ROWS, tm), lambda b, i: (b, 0, 0, i)),
                   pl.BlockSpec((None, gr, V_ROWS, tm), lambda b, i: (b, 0, 0, i))],
        out_shape=[jax.ShapeDtypeStruct((2, CMP_STRIDE, t // CMP_STRIDE, KV_NAT), BF16),
                   jax.ShapeDtypeStruct((batch, gr, s, LANE), BF16),
                   jax.ShapeDtypeStruct((batch, gr, s, LANE), BF16),
                   jax.ShapeDtypeStruct((batch, gr, V_ROWS, s), BF16),
                   jax.ShapeDtypeStruct((batch, gr, V_ROWS, s), BF16)],
        compiler_params=_cparams("parallel", "parallel"),
        name="nsa_kv_proj",
    )(h, g, w)


def _compress_body(x_ref, wt_ref, wb_ref, pos_ref, w1_ref, b1_ref, w2_ref, b2_ref, on_ref, ot_ref, *, ncp):
    top = _dot(x_ref[0], wt_ref[0])
    bot = _dot(x_ref[0], wb_ref[0])
    for l in range(1, CMP_STRIDE):
        top = top + _dot(x_ref[l], wt_ref[l])
        bot = bot + _dot(x_ref[l], wb_ref[l])
    posb = _dot(pos_ref[...], w1_ref[...])[0:1, :] + b1_ref[...]
    for gi in range(NSA_GROUPS):
        sl = slice(gi * CMP_HIDDEN, (gi + 1) * CMP_HIDDEN)
        hid = top[:, sl] + pltpu.roll(bot[:, sl], ncp - 1, 0) + posb
        hid = (hid * _sigmoid(hid)).astype(BF16)
        out = _dot(hid, w2_ref[...]) + b2_ref[...]
        on_ref[gi] = out.astype(BF16)
        ot_ref[gi] = out.T.astype(BF16)


def _compress(x16, wt, wb, pos, w1, b1, w2, b2, batch):
    ncp = x16.shape[2] // batch
    gr = NSA_GROUPS
    per_j = lambda a: pl.BlockSpec((None,) + a.shape[1:], lambda j, b: (j,) + (0,) * (a.ndim - 1))
    return pl.pallas_call(
        functools.partial(_compress_body, ncp=ncp),
        grid=(2, batch),
        in_specs=[pl.BlockSpec((None, CMP_STRIDE, ncp, KV_NAT), lambda j, b: (j, 0, b, 0)),
                  per_j(wt), per_j(wb), per_j(pos), per_j(w1), per_j(b1), per_j(w2), per_j(b2)],
        out_specs=[pl.BlockSpec((None, None, gr, ncp, LANE), lambda j, b: (j, b, 0, 0, 0)),
                   pl.BlockSpec((None, None, gr, LANE, ncp), lambda j, b: (j, b, 0, 0, 0))],
        out_shape=[jax.ShapeDtypeStruct((2, batch, gr, ncp, LANE), BF16),
                   jax.ShapeDtypeStruct((2, batch, gr, LANE, ncp), BF16)],
        compiler_params=_cparams("arbitrary", "arbitrary"),
        name="nsa_compress",
    )(x16, wt, wb, pos, w1, b1, w2, b2)


NSA_QW = NSA_HEADS * NSA_HEAD_DIM
G3 = NSA_REP * TQ


def _nsa_in_body(h_ref, g_ref, w_ref, kv_ref, qt_ref, gt_ref, mo_ref):
    xn = _rms(h_ref[...], g_ref[...]).astype(BF16)
    for j in range(NSA_QW // MXU_N):
        y = _dot(xn, w_ref[:, j * MXU_N:(j + 1) * MXU_N]) * (NSA_HEAD_DIM ** -0.5 * LOG2E)
        qt_ref[j * MXU_N:(j + 1) * MXU_N, :] = y.T.astype(BF16)
    gt_ref[...] = _sigmoid(_dot(xn, w_ref[:, NSA_QW:NSA_QW + LANE])).T
    mo_ref[...] = _mem_attention(_dot(xn, w_ref[:, NSA_QW + LANE:]).astype(BF16), kv_ref)


def _nsa_in_proj(h, layer, nlayer, g, w, mem_kv, batch):
    t = h.shape[0]
    tm = ROW_TILE
    row = lambda n: pl.BlockSpec((tm, n), lambda i: (i, 0))
    nt = t // batch // tm
    col = lambda n: pl.BlockSpec((None, n, tm), lambda i: (i // nt, 0, i % nt))
    return pl.pallas_call(
        _nsa_in_body,
        grid=(t // tm,),
        in_specs=[row(D_MODEL), _layer_of(g, layer), _layer_of(w, nlayer), _mem_kv_spec(mem_kv, layer, t, tm, batch)],
        out_specs=[col(NSA_QW), col(LANE), row(MEM_QW)],
        out_shape=[jax.ShapeDtypeStruct((batch, NSA_QW, t // batch), BF16),
                   jax.ShapeDtypeStruct((batch, LANE, t // batch), F32),
                   jax.ShapeDtypeStruct((t, MEM_QW), BF16)],
        compiler_params=_cparams("parallel"),
        name="nsa_in_proj",
    )(h, g, w, mem_kv)


def _nsa_attn_body(qt_ref, gt_ref, kc_ref, vct_ref, ks_ref, kw_ref, vs_ref, vw_ref, ovt_ref, tz_ref, cb_ref,
                   o_ref, sc_ref, qa_ref, acc_ref, ot_ref, *s_slots, nsb, ncp, n_sel, nseq):
    n = pl.program_id(1)
    t0 = n * TQ
    dh = NSA_HEAD_DIM
    cstart = pl.multiple_of(ncp - n * (TQ // CMP_STRIDE), SUBLANE)
    sees_any = t0 + (lax.broadcasted_iota(jnp.int32, (1, G3), 1) & (TQ - 1)) >= CMP_BLOCK - 1
    jj = lax.broadcasted_iota(jnp.int32, (nsb, TQ), 0)
    cur = lax.shift_right_logical(t0 + lax.broadcasted_iota(jnp.int32, (nsb, TQ), 1), int(math.log2(SEL_BLOCK)))
    forced = (jj == 0) | (jj == cur) | (jj == cur - 1)
    zpad = jnp.zeros((dh, G3), BF16)
    nwt = WINDOW // TQ
    units = [(bi, gi) for bi in range(nseq) for gi in range(NSA_GROUPS)]

    def scores(k_ref, u, first, count, tz_index, qa):
        bi, gi = units[u]
        koff = pl.multiple_of(first * TQ, TQ)
        s = _dot(k_ref[bi, gi, pl.ds(koff, count * TQ), :], qa)
        return [s[i * TQ:(i + 1) * TQ] + tz_ref[gi, tz_index(n - (first + i))] for i in range(count)]

    def col_max(parts):
        mx = parts[0]
        for x in parts[1:]:
            mx = jnp.maximum(mx, x)
        return jnp.max(mx, axis=0, keepdims=True)

    def probs(parts, m):
        return jnp.concatenate([jnp.exp2(x - m).astype(BF16) for x in parts], axis=0)

    def values(v_ref, u, first, count, p):
        bi, gi = units[u]
        koff = pl.multiple_of(first * TQ, TQ)
        return _dot(v_ref[bi, gi, :, pl.ds(koff, count * TQ)], p)

    sel_index = lambda d: jnp.where(d < 0, 2, jnp.minimum(d, 2))
    win_index = lambda d: jnp.where(d < 0, 4, jnp.where(d == nwt, 3, jnp.minimum(d, 2)))

    groups = range(len(units))
    gate = lambda bi, h, c: gt_ref[bi, pl.ds(h * NSA_BRANCHES + c, 1), :]
    head_lanes = lambda r: slice(r * TQ, (r + 1) * TQ)
    head_rows = lambda bi, h: (bi, slice(h * dh, (h + 1) * dh))

    q3s = [jnp.concatenate([qt_ref[head_rows(bi, gi * NSA_REP + r)] for r in range(NSA_REP)], axis=1)
           for bi, gi in units]
    cs = [_dot(kc_ref[bi, gi], jnp.concatenate([q3s[u], zpad], axis=0)) + cb_ref[gi, pl.ds(cstart, ncp), :]
          for u, (bi, gi) in enumerate(units)]
    cps = [jnp.exp2(s - jnp.max(s, axis=0, keepdims=True)) for s in cs]
    cps = [p * jnp.where(sees_any, 1.0 / jnp.sum(p, axis=0, keepdims=True), 0.0) for p in cps]
    ocs = [_dot(vct_ref[bi, gi], cps[u].astype(BF16)) for u, (bi, gi) in enumerate(units)]
    scs = []
    for gi in groups:
        psum = cps[gi][:, 0:TQ]
        for r in range(1, NSA_REP):
            psum = psum + cps[gi][:, head_lanes(r)]
        p1 = psum.astype(BF16)
        p2 = (psum - p1.astype(F32)).astype(BF16)
        imp = _dot(ovt_ref[...], p1) + _dot(ovt_ref[...], p2)
        score = jnp.where(forced, 1e4, jnp.where(jj <= cur, imp[0:nsb], -1.0))
        scs.append(jnp.where(score < 0.0, -1, lax.bitcast_convert_type(score, jnp.int32)))
    for u, (bi, gi) in enumerate(units):
        sc_ref[u] = scs[u]
        for r in range(NSA_REP):
            h = gi * NSA_REP + r
            ot_ref[head_rows(bi, h)] = gate(bi, h, 0) * ocs[u][0:dh, head_lanes(r)]

    scs1 = [k + 1 for k in scs]

    def rank_step(i4, cnts):
        cnts = list(cnts)
        for u in range(RANK_UNROLL):
            i = i4 * RANK_UNROLL + u
            lower = i < jj
            for gi in groups:
                rowk = sc_ref[gi, pl.ds(i, 1), :]
                before = rowk >= jnp.where(lower, scs[gi], scs1[gi])
                cnts[gi] = cnts[gi] + jnp.where(before, 1, 0)
        return tuple(cnts)
    rank_trips = jnp.minimum((2 * n + 2 + RANK_UNROLL - 1) // RANK_UNROLL, nsb // RANK_UNROLL)
    cnts = lax.fori_loop(0, rank_trips, rank_step, tuple(jnp.zeros((nsb, TQ), jnp.int32) for _ in groups))
    for gi in groups:
        selneg = jnp.where((cnts[gi] < n_sel) & (jj <= cur), 0.0, NEG).astype(BF16)
        if nsb < SEL_BLOCK:
            selneg = jnp.concatenate([selneg, jnp.zeros((SEL_BLOCK - nsb, TQ), BF16)], axis=0)
        qa_ref[gi] = jnp.concatenate([q3s[gi], jnp.concatenate([selneg] * NSA_REP, axis=1)], axis=0)

    acc_ref[...] = jnp.zeros_like(acc_ref)
    half = len(units) // 2
    s_refs = (s_slots[:half], s_slots[half:])

    def score_half(it, hb):
        plist = [scores(ks_ref, hb * half + u, it * SEL_TILES, SEL_TILES, sel_index, qa_ref[hb * half + u])
                 for u in range(half)]
        for u, parts in enumerate(plist):
            for i, x in enumerate(parts):
                s_refs[hb][u][i * TQ:(i + 1) * TQ, :] = x
        return [col_max(parts) for parts in plist]

    def finish_half(it, hb, ms, bms):
        m2s = [jnp.maximum(ms[u], bms[u]) for u in range(half)]
        ps = [probs([s_refs[hb][u][i * TQ:(i + 1) * TQ, :] for i in range(SEL_TILES)], m2s[u]) for u in range(half)]
        vals = [values(vs_ref, hb * half + u, it * SEL_TILES, SEL_TILES, ps[u]) for u in range(half)]
        for u in range(half):
            gi = hb * half + u
            acc_ref[gi] = jnp.exp2(ms[u] - m2s[u]) * acc_ref[gi] + vals[u]
        return m2s

    def sel_trip(it, carry, score_next):
        ms0, ms1, bm0 = carry
        bm1 = score_half(it, 1)
        ms0 = finish_half(it, 0, ms0, bm0)
        if score_next:
            bm0 = score_half(it + 1, 0)
        ms1 = finish_half(it, 1, ms1, bm1)
        return ms0, ms1, bm0

    neg = [jnp.full((1, G3), NEG, F32) for _ in range(half)]
    carry = (neg, neg, score_half(0, 0))

    wfirst = jnp.maximum(n - nwt, 0)
    wparts = [scores(kw_ref, gi, wfirst, nwt + 1, win_index, qa_ref[gi]) for gi in groups]
    wps = [probs(parts, col_max(parts)) for parts in wparts]
    ows = [values(vw_ref, gi, wfirst, nwt + 1, wps[gi]) for gi in groups]
    for u, (bi, gi) in enumerate(units):
        o_w = ows[u][0:dh, :] * (1.0 / ows[u][dh:dh + 1, :])
        for r in range(NSA_REP):
            h = gi * NSA_REP + r
            ot_ref[head_rows(bi, h)] += gate(bi, h, 2) * o_w[:, head_lanes(r)]

    carry = lax.fori_loop(0, n // SEL_TILES, lambda it, c: sel_trip(it, c, True), carry)
    sel_trip(n // SEL_TILES, carry, False)

    for u, (bi, gi) in enumerate(units):
        o_s = acc_ref[u, 0:dh, :] * (1.0 / acc_ref[u, dh:dh + 1, :])
        for r in range(NSA_REP):
            h = gi * NSA_REP + r
            ot_ref[head_rows(bi, h)] += gate(bi, h, 1) * o_s[:, head_lanes(r)]

    for bi in range(nseq):
        for j in range(NSA_QW // LANE):
            o_ref[bi, :, j * LANE:(j + 1) * LANE] = ot_ref[bi, j * LANE:(j + 1) * LANE, :].T.astype(BF16)


def _nsa_attn(qt, gt, kc, vct, ksel, kwin, vsel, vwin, ovt, tz, cb):
    batch, _, s = qt.shape
    nseq = NSA_SEQS if batch % NSA_SEQS == 0 else 1
    nt = s // TQ
    nsb = s // SEL_BLOCK
    ncp = kc.shape[-2]
    nu = nseq * NSA_GROUPS
    per_b = lambda a: pl.BlockSpec((nseq,) + a.shape[1:], lambda b, i: (b,) + (0,) * (a.ndim - 1),
                                   pipeline_mode=pl.Buffered(1))
    out = pl.pallas_call(
        functools.partial(_nsa_attn_body, nsb=nsb, ncp=ncp, n_sel=min(SEL_TOPK, nsb), nseq=nseq),
        grid=(batch // nseq, nt),
        in_specs=[pl.BlockSpec((nseq, NSA_QW, TQ), lambda b, i: (b, 0, i)),
                  pl.BlockSpec((nseq, LANE, TQ), lambda b, i: (b, 0, i)),
                  per_b(kc), per_b(vct), per_b(ksel), per_b(kwin), per_b(vsel), per_b(vwin),
                  _resident(ovt), _resident(tz), _resident(cb)],
        out_specs=pl.BlockSpec((nseq, TQ, NSA_QW), lambda b, i: (b, i, 0)),
        out_shape=jax.ShapeDtypeStruct((batch, s, NSA_QW), BF16),
        scratch_shapes=[pltpu.VMEM((nu, nsb, TQ), jnp.int32), pltpu.VMEM((nu, LANE, G3), BF16),
                        pltpu.VMEM((nu, V_ROWS, G3), F32), pltpu.VMEM((nseq, NSA_QW, TQ), F32),
                        ] + [pltpu.VMEM((SEL_TILES * TQ, G3), F32)] * nu,
        compiler_params=_cparams("parallel", "arbitrary"),
        name="nsa_attn",
    )(qt, gt, kc, vct, ksel, kwin, vsel, vwin, ovt, tz, cb)
    return out.reshape(batch * s, NSA_QW)


def _rel_bucket_np(dist):
    dist = np.maximum(dist, 0)
    max_exact = REL_BUCKETS // 2
    ratio = np.log(np.maximum(dist, 1).astype(np.float32) / np.float32(max_exact)) / np.float32(
        math.log(REL_MAX_DIST / max_exact))
    large = max_exact + (ratio * np.float32(REL_BUCKETS - max_exact)).astype(np.int32)
    large = np.minimum(large, REL_BUCKETS - 1)
    return np.where(dist < max_exact, dist, large).astype(np.int32)


def _group_lanes(a):
    hh, r, c = a.shape
    return a.reshape(NSA_GROUPS, NSA_REP, r, c).transpose(0, 2, 1, 3).reshape(NSA_GROUPS, r, NSA_REP * c)


def _bias_tables(rel_bias, ncp):
    k = np.arange(TQ)[:, None]
    q = np.arange(TQ)[None, :]
    tbl = rel_bias.astype(F32)

    def lookup(idx):
        onehot = (jnp.asarray(idx.reshape(1, -1)) == jnp.arange(REL_BUCKETS)[:, None]).astype(F32)
        out = jnp.dot(tbl.T, onehot, precision=lax.Precision.HIGHEST)
        return out.reshape((NSA_HEADS,) + idx.shape)

    far = jnp.broadcast_to(tbl[REL_BUCKETS - 1][:, None, None], (NSA_HEADS, TQ, TQ))
    t0 = jnp.where(jnp.asarray(k <= q)[None], lookup(_rel_bucket_np(q - k)), NEG)
    t1 = lookup(_rel_bucket_np(TQ + q - k))
    t3 = jnp.where(jnp.asarray(k > q)[None], far, NEG)
    tz = jnp.stack([_group_lanes(x) for x in (t0, t1, far, t3, jnp.full_like(far, NEG))], axis=1)
    m = ncp - np.arange(2 * ncp)[:, None]
    d = CMP_STRIDE * m + q - (CMP_BLOCK - 1)
    idx = np.where((d >= 0) & (d < REL_MAX_DIST), _rel_bucket_np(d), REL_BUCKETS - 1)
    cb = _group_lanes(jnp.where(jnp.asarray(d >= 0)[None], lookup(idx), NEG))
    return tz * LOG2E, cb * LOG2E


def _overlap_table(s, ncp):
    nsb = s // SEL_BLOCK
    nc = (s - CMP_BLOCK) // CMP_STRIDE + 1
    cs = np.arange(ncp) * CMP_STRIDE
    ce = cs + CMP_BLOCK - 1
    ss = np.arange(SEL_BLOCK) * SEL_BLOCK
    ov = (cs[None, :] < ss[:, None] + SEL_BLOCK) & (ce[None, :] >= ss[:, None])
    ov &= (np.arange(ncp) < nc)[None, :] & (np.arange(SEL_BLOCK) < nsb)[:, None]
    return jnp.asarray(ov, dtype=BF16)


def kernel(x, mem, norm_mix, norm_mem, w_mem_kv, w_out, norm_ffn, w_up, conv_w, conv_b, w_down,
           gla_w_in, gla_w_gate_up, gla_b_gate, gla_out_norm, nsa_w_in, kv_norm, w_kv_shared,
           cmp_pos, cmp_w1, cmp_b1, cmp_w2, cmp_b2, rel_bias, final_norm):
    batch, seq = x.shape[0], x.shape[1]
    t = batch * seq
    h = x.reshape(t, D_MODEL)
    row = lambda v: v.reshape(1, -1).astype(F32)

    wk, wv = w_mem_kv[..., :MEM_W], w_mem_kv[..., MEM_W:]
    w_mkv = jnp.concatenate([_pad_heads(wk, MEM_HEADS, MEM_HEAD_DIM, MEM_DP),
                             _pad_heads(wv, MEM_HEADS, MEM_HEAD_DIM, MEM_DP)], axis=-1).astype(BF16)
    mem_kv_all = _mem_kv_proj(mem.reshape(-1, D_MODEL), norm_mem.reshape(DEPTH, 1, D_MODEL), w_mkv, batch)

    ffn_params = (norm_ffn.reshape(DEPTH, 1, D_MODEL).astype(F32), w_up.astype(BF16), conv_w.astype(F32),
                  conv_b.reshape(DEPTH, 1, 2 * FFN_DIM).astype(F32), w_down.astype(BF16))
    norm_mix3 = norm_mix.reshape(DEPTH, 1, D_MODEL).astype(F32)
    w_o_mem = _pad_head_rows(w_out[:, MAIN_W:], MEM_HEADS, MEM_HEAD_DIM, MEM_DP).astype(BF16)
    mem_pad = lambda w: _pad_heads(w, MEM_HEADS, MEM_HEAD_DIM, MEM_DP)

    c0 = GLA_HEADS * GLA_DK
    c1 = 2 * c0
    c2 = c1 + GLA_HEADS * GLA_DV
    c3 = c2 + GLA_HEADS * GLA_DV
    c4 = c3 + GLA_RANK
    gla_w = jnp.concatenate([gla_w_in[..., :c3], gla_w_in[..., c4:],
                             _pad_heads(gla_w_in[..., c3:c4], 1, GLA_RANK, LANE)], axis=-1).astype(BF16)
    gla_wg = jnp.pad(_pad_heads(gla_w_gate_up, GLA_HEADS, GLA_DK, GLA_DKP),
                     ((0, 0), (0, LANE - GLA_RANK), (0, 0))).astype(BF16)
    gla_bg = _pad_heads(gla_b_gate, GLA_HEADS, GLA_DK, GLA_DKP).reshape(N_A_LAYERS, 1, GLA_QW).astype(F32)
    gla_on = jnp.pad(gla_out_norm, ((0, 0), (0, GLA_DVP - GLA_DV))).reshape(N_A_LAYERS, 1, GLA_DVP).astype(F32)
    gla_wo = _pad_head_rows(w_out[:N_A_LAYERS, :MAIN_W], GLA_HEADS, GLA_DV, GLA_DVP).astype(BF16)

    n0 = NSA_HEADS * NSA_HEAD_DIM
    n1 = n0 + NSA_HEADS * NSA_BRANCHES
    nsa_w = jnp.concatenate([nsa_w_in[..., :n0], _pad_heads(nsa_w_in[..., n0:n1], 1, NSA_HEADS * NSA_BRANCHES, LANE),
                             mem_pad(nsa_w_in[..., n1:])], axis=-1).astype(BF16)
    nsa_wo = w_out[N_A_LAYERS:, :MAIN_W].astype(BF16)

    shared = None
    for i in range(DEPTH):
        if i < N_A_LAYERS:
            q, k, v, go, la, mo = _gla_in_proj(h, i, i, norm_mix3, gla_w, gla_wg, gla_bg, mem_kv_all, batch)
            main = _gla_mixer(q, k, la, v, go, gla_on, i, batch)
            w_o_main, klayer = gla_wo, i
        else:
            if shared is None:
                ncp = seq // CMP_STRIDE
                gr, dh = NSA_GROUPS, NSA_HEAD_DIM
                wkv = w_kv_shared.reshape(D_MODEL, 6, gr, dh)
                pair = lambda a, b: jnp.concatenate([wkv[:, a], wkv[:, b]], axis=-1).reshape(D_MODEL, gr * 2 * dh)
                slot = lambda a: _pad_heads(wkv[:, a].reshape(D_MODEL, KV_NAT), gr, dh, LANE)
                ksw = jnp.stack([slot(2).reshape(D_MODEL, gr, LANE), slot(4).reshape(D_MODEL, gr, LANE)], axis=2)
                w_kv = jnp.concatenate([wkv[:, 0].reshape(D_MODEL, KV_NAT), wkv[:, 1].reshape(D_MODEL, KV_NAT),
                                        ksw.reshape(D_MODEL, 2 * gr * LANE), pair(3, 5)], axis=1).astype(BF16)
                x16, ksel, kwin, vsel, vwin = _kv_proj(h, row(kv_norm), w_kv, batch)
                w1 = cmp_w1.reshape(2, 2, CMP_STRIDE, dh, CMP_HIDDEN)
                eye = jnp.eye(gr, dtype=F32)
                w1x = jnp.einsum('jhldc,gk->jhlgdkc', w1, eye).reshape(2, 2, CMP_STRIDE, KV_NAT, gr * CMP_HIDDEN)
                w1x = w1x.astype(BF16)
                pos8 = jnp.broadcast_to(cmp_pos.reshape(2, 1, CMP_BLOCK * dh), (2, SUBLANE, CMP_BLOCK * dh)).astype(BF16)
                w2p = jnp.pad(cmp_w2, ((0, 0), (0, 0), (0, LANE - dh))).astype(BF16)
                b2p = jnp.pad(cmp_b2, ((0, 0), (0, LANE - dh))).reshape(2, 1, LANE).astype(F32)
                cnat, ctr = _compress(x16, w1x[:, 0], w1x[:, 1], pos8, cmp_w1.astype(BF16),
                                      cmp_b1.reshape(2, 1, CMP_HIDDEN).astype(F32), w2p, b2p, batch)
                tz, cb = _bias_tables(rel_bias, ncp)
                ov = _overlap_table(seq, ncp)
                shared = (cnat[0], ctr[1], ksel, kwin, vsel, vwin, ov, tz, cb)
            q, gates, mo = _nsa_in_proj(h, i, i - N_A_LAYERS, norm_mix3, nsa_w, mem_kv_all, batch)
            main = _nsa_attn(q, gates, *shared)
            w_o_main, klayer = nsa_wo, i - N_A_LAYERS
        h = _mix_ffn(h, main, mo, w_o_main, klayer, w_o_mem, i, *ffn_params, row(final_norm), i == DEPTH - 1, batch)
    return h.reshape(batch, seq, D_MODEL)
```

```python
import functools
import math

import numpy as np
import jax
import jax.numpy as jnp
from jax import lax
from jax.experimental import pallas as pl
from jax.experimental.pallas import tpu as pltpu

F32 = jnp.float32
BF16 = jnp.bfloat16

D_MODEL = 1024
DEPTH = 4
N_A_LAYERS = DEPTH // 2
MEM_HEADS = 4
MEM_HEAD_DIM = 64
MEM_W = MEM_HEADS * MEM_HEAD_DIM
MAIN_W = D_MODEL - MEM_W
GLA_HEADS = 4
GLA_DV = MAIN_W // GLA_HEADS
GLA_DK = GLA_DV // 2
GLA_RANK = 16
GLA_GATE_NORM = 16.0
NSA_HEADS = 12
NSA_GROUPS = 4
NSA_HEAD_DIM = MAIN_W // NSA_HEADS
NSA_REP = NSA_HEADS // NSA_GROUPS
NSA_BRANCHES = 3
CMP_BLOCK = 32
CMP_STRIDE = 16
CMP_HIDDEN = 128
SEL_BLOCK = 64
SEL_TOPK = 16
WINDOW = 512
REL_BUCKETS = 32
REL_MAX_DIST = 128
FFN_DIM = 2816
CONV_WIDTH = 3
EPS = 1e-6

LANE = 128
MXU_N = 256
SUBLANE = 8
VMEM_LIMIT = 56 * 1024 * 1024
GLA_DKP = LANE
GLA_DVP = 2 * LANE
MEM_DP = LANE
NEG = -1e30
TQ = 128
ROW_TILE = 1024
GLA_CHUNK = 64
GLA_SAFE_DECAY = 80.0
FFN_TILE = 256
FFN_ROWS = 1024
SEL_TILES = 4
NSA_SEQS = 2
V_ROWS = NSA_HEAD_DIM + 2 * SUBLANE
RANK_UNROLL = 4
LOG2E = math.log2(math.e)

NT = (((1,), (1,)), ((), ()))
TN = (((0,), (0,)), ((), ()))


def _cparams(*sem):
    return pltpu.CompilerParams(dimension_semantics=sem, vmem_limit_bytes=VMEM_LIMIT)


def _rms(x, g):
    return x * lax.rsqrt(jnp.mean(x * x, axis=-1, keepdims=True) + EPS) * g


def _sigmoid(x):
    return 1.0 / (1.0 + jnp.exp(-x))


def _dot(a, b):
    return jnp.dot(a, b, preferred_element_type=F32)


def _resident(a):
    return pl.BlockSpec(a.shape, lambda *_: (0,) * a.ndim, pipeline_mode=pl.Buffered(1))


def _layer_of(a, layer):
    return pl.BlockSpec((None,) + a.shape[1:], lambda *_: (layer,) + (0,) * (a.ndim - 1),
                        pipeline_mode=pl.Buffered(1))


def _pad_heads(w, nh, d, dp):
    lead = w.shape[:-1]
    w = w.reshape(lead + (nh, d))
    w = jnp.pad(w, [(0, 0)] * len(lead) + [(0, 0), (0, dp - d)])
    return w.reshape(lead + (nh * dp,))


def _pad_head_rows(w, nh, d, dp):
    lead, n = w.shape[:-2], w.shape[-1]
    w = jnp.pad(w.reshape(lead + (nh, d, n)), [(0, 0)] * len(lead) + [(0, 0), (0, dp - d), (0, 0)])
    return w.reshape(lead + (nh * dp, n))


def _spread_heads(y, off, nh, d, dp):
    pad = jnp.zeros((y.shape[0], dp - d), F32)
    return jnp.concatenate([x for hd in range(nh) for x in (y[:, off + hd * d:off + (hd + 1) * d], pad)], axis=1)


def _mem_attention(q, kv_ref):
    outs = []
    for hd in range(MEM_HEADS):
        sl = slice(hd * MEM_DP, (hd + 1) * MEM_DP)
        sv = slice(MEM_QW + hd * MEM_DP, MEM_QW + (hd + 1) * MEM_DP)
        s = lax.dot_general(q[:, sl], kv_ref[:, sl], NT, preferred_element_type=F32) * MEM_HEAD_DIM ** -0.5
        p = jnp.exp(s - jnp.max(s, axis=-1, keepdims=True))
        l = jnp.sum(p, axis=-1, keepdims=True)
        outs.append((_dot(p.astype(BF16), kv_ref[:, sv]) / l).astype(BF16))
    return jnp.concatenate(outs, axis=1)


GLA_QW = GLA_HEADS * GLA_DKP
GLA_VW = GLA_HEADS * GLA_DVP
MEM_QW = MEM_HEADS * MEM_DP
GLA_QK = GLA_HEADS * GLA_DK
GLA_VD = GLA_HEADS * GLA_DV
GLA_OFF_V = 2 * GLA_QK
GLA_OFF_G = GLA_OFF_V + GLA_VD
GLA_OFF_MQ = GLA_OFF_G + GLA_VD
GLA_OFF_LR = GLA_OFF_MQ + MEM_W


def _gla_in_body(h_ref, g_ref, w_ref, wg_ref, bg_ref, kv_ref, q_ref, k_ref, v_ref, go_ref, la_ref, mo_ref):
    xn = _rms(h_ref[...], g_ref[...]).astype(BF16)

    spread = _spread_heads

    def proj(lo, n):
        return _dot(xn, w_ref[:, lo:lo + n])

    qk = proj(0, 2 * GLA_QK)
    q_ref[...] = spread(qk, 0, GLA_HEADS, GLA_DK, GLA_DKP).astype(BF16)
    k_ref[...] = spread(qk, GLA_QK, GLA_HEADS, GLA_DK, GLA_DKP).astype(BF16)
    v_ref[...] = spread(proj(GLA_OFF_V, GLA_VD), 0, GLA_HEADS, GLA_DV, GLA_DVP).astype(BF16)
    go_ref[...] = spread(proj(GLA_OFF_G, GLA_VD), 0, GLA_HEADS, GLA_DV, GLA_DVP).astype(BF16)
    tail = proj(GLA_OFF_MQ, MEM_W + LANE)
    z = _dot(tail[:, MEM_W:].astype(BF16), wg_ref[...]) + bg_ref[...]
    la_ref[...] = (jnp.minimum(z, 0.0) - jnp.log(1.0 + jnp.exp(-jnp.abs(z)))) * (1.0 / GLA_GATE_NORM)
    mo_ref[...] = _mem_attention(spread(tail, 0, MEM_HEADS, MEM_HEAD_DIM, MEM_DP).astype(BF16), kv_ref)


def _mem_kv_spec(mem_kv, layer, t, tm, batch):
    nt = t // batch // tm
    return pl.BlockSpec((None, mem_kv.shape[1] // batch, mem_kv.shape[2]), lambda i: (layer, i // nt, 0))


def _gla_in_proj(h, layer, glayer, g, w, wg, bg, mem_kv, batch):
    t = h.shape[0]
    tm = ROW_TILE
    row = lambda n: pl.BlockSpec((tm, n), lambda i: (i, 0))
    return pl.pallas_call(
        _gla_in_body,
        grid=(t // tm,),
        in_specs=[row(D_MODEL), _layer_of(g, layer), _layer_of(w, glayer), _layer_of(wg, glayer),
                  _layer_of(bg, glayer), _mem_kv_spec(mem_kv, layer, t, tm, batch)],
        out_specs=[row(GLA_QW), row(GLA_QW), row(GLA_VW), row(GLA_VW), row(GLA_QW), row(MEM_QW)],
        out_shape=[jax.ShapeDtypeStruct((t, GLA_QW), BF16), jax.ShapeDtypeStruct((t, GLA_QW), BF16),
                   jax.ShapeDtypeStruct((t, GLA_VW), BF16), jax.ShapeDtypeStruct((t, GLA_VW), BF16),
                   jax.ShapeDtypeStruct((t, GLA_QW), F32), jax.ShapeDtypeStruct((t, MEM_QW), BF16)],
        compiler_params=_cparams("parallel"),
        name="gla_in_proj",
    )(h, g, w, wg, bg, mem_kv)


def _gla_mix_body(q_ref, k_ref, la_ref, v_ref, go_ref, on_ref, o_ref, st_ref, inter_ref, b_scr, k_scr, v_scr, *,
                  cn, batch):
    @pl.when(pl.program_id(0) == 0)
    def _():
        st_ref[...] = jnp.zeros_like(st_ref)

    row = lax.broadcasted_iota(jnp.int32, (cn, cn), 0)
    col = lax.broadcasted_iota(jnp.int32, (cn, cn), 1)
    causal = row >= col
    tril = jnp.where(causal, 1.0, 0.0).astype(BF16)
    scale = GLA_DK ** -0.5
    chains = [(bi, hd) for bi in range(batch) for hd in range(GLA_HEADS)]
    sk = lambda hd: slice(hd * GLA_DKP, (hd + 1) * GLA_DKP)
    sv = lambda hd: slice(hd * GLA_DVP, (hd + 1) * GLA_DVP)
    dg = lambda x, y, dims: lax.dot_general(x, y, dims, preferred_element_type=F32)

    def cum_decay(bi, hd):
        la = la_ref[bi, :, sk(hd)]
        la1 = la.astype(BF16)
        r1 = la - la1.astype(F32)
        la2 = r1.astype(BF16)
        la3 = (r1 - la2.astype(F32)).astype(BF16)
        return _dot(tril, la1) + _dot(tril, la2) + _dot(tril, la3)

    def finish(i, o):
        bi, hd = chains[i]
        ms = jnp.sum(o * o, axis=-1, keepdims=True) * (1.0 / GLA_DV)
        y = o * lax.rsqrt(ms + EPS) * on_ref[...]
        g = go_ref[bi, :, sv(hd)].astype(F32)
        o_ref[bi, :, sv(hd)] = (y * (g * _sigmoid(g))).astype(BF16)

    bs = [cum_decay(bi, hd) for bi, hd in chains]
    scaled = []
    for (bi, hd), b in zip(chains, bs):
        b_last = b[cn - 1:cn, :]
        q = q_ref[bi, :, sk(hd)].astype(F32)
        k = k_ref[bi, :, sk(hd)].astype(F32)
        scaled.append(((q * jnp.exp(b) * scale).astype(BF16),
                       (k * jnp.exp(jnp.minimum(-b, GLA_SAFE_DECAY))).astype(BF16),
                       (k * jnp.exp(b_last - b)).astype(BF16), jnp.exp(b_last)))
    intra = [jnp.where(causal, dg(qs, ks, NT), 0.0).astype(BF16) for qs, ks, _, _ in scaled]
    outs = []
    for i, (bi, hd) in enumerate(chains):
        qs, _, ko, a_last = scaled[i]
        v = v_ref[bi, :, sv(hd)]
        st = st_ref[i]
        inter = dg(qs, st.astype(BF16), NT)
        inter_ref[i] = inter
        outs.append(_dot(intra[i], v) + inter)
        st_ref[i] = st * a_last + dg(v, ko, TN)
    for i, o in enumerate(outs):
        finish(i, o)

    b_min = bs[0][cn - 1:cn, :]
    for b in bs[1:]:
        b_min = jnp.minimum(b_min, b[cn - 1:cn, :])

    @pl.when(jnp.min(b_min) < -GLA_SAFE_DECAY)
    def _():
        rows = lax.broadcasted_iota(jnp.int32, (cn, 1), 0)
        for i, (bi, hd) in enumerate(chains):
            b = cum_decay(bi, hd)
            qf = q_ref[bi, :, sk(hd)].astype(F32) * scale
            b_scr[...] = b
            k_scr[...] = k_ref[bi, :, sk(hd)].astype(F32)
            v_scr[...] = v_ref[bi, :, sv(hd)].astype(F32)

            def add_key(j, acc):
                decay = jnp.exp(jnp.minimum(b - b_scr[pl.ds(j, 1), :], 0.0))
                a_col = jnp.sum(qf * k_scr[pl.ds(j, 1), :] * decay, axis=-1, keepdims=True)
                return acc + jnp.where(rows >= j, a_col, 0.0) * v_scr[pl.ds(j, 1), :]
            finish(i, lax.fori_loop(0, cn, add_key, inter_ref[i]))


def _gla_mixer(q, k, la, v, go, on, glayer, batch):
    t = q.shape[0]
    s = t // batch
    cn = GLA_CHUNK
    seq = lambda a: a.reshape(batch, s, a.shape[1])
    blk = lambda n: pl.BlockSpec((batch, cn, n), lambda c: (0, c, 0))
    out = pl.pallas_call(
        functools.partial(_gla_mix_body, cn=cn, batch=batch),
        grid=(s // cn,),
        in_specs=[blk(GLA_QW), blk(GLA_QW), blk(GLA_QW), blk(GLA_VW), blk(GLA_VW),
                  _layer_of(on, glayer)],
        out_specs=blk(GLA_VW),
        out_shape=jax.ShapeDtypeStruct((batch, s, GLA_VW), BF16),
        scratch_shapes=[pltpu.VMEM((batch * GLA_HEADS, GLA_DVP, GLA_DKP), F32),
                        pltpu.VMEM((batch * GLA_HEADS, cn, GLA_DVP), F32), pltpu.VMEM((cn, GLA_DKP), F32),
                        pltpu.VMEM((cn, GLA_DKP), F32), pltpu.VMEM((cn, GLA_DVP), F32)],
        compiler_params=_cparams("arbitrary"),
        name="gla_mixer",
    )(seq(q), seq(k), seq(la), seq(v), seq(go), on)
    return out.reshape(t, GLA_VW)


def _mem_kv_body(m_ref, g_ref, w_ref, o_ref):
    xn = _rms(m_ref[...], g_ref[...]).astype(BF16)
    o_ref[...] = _dot(xn, w_ref[...]).astype(BF16)


def _mem_kv_proj(mem2, g, w, batch):
    nl = w.shape[0]
    m = mem2.shape[0] // batch
    n = w.shape[2]
    return pl.pallas_call(
        _mem_kv_body,
        grid=(nl, batch),
        in_specs=[pl.BlockSpec((m, D_MODEL), lambda l, b: (b, 0)),
                  pl.BlockSpec((None, 1, D_MODEL), lambda l, b: (l, 0, 0)),
                  pl.BlockSpec((None, D_MODEL, n), lambda l, b: (l, 0, 0))],
        out_specs=pl.BlockSpec((None, m, n), lambda l, b: (l, b, 0)),
        out_shape=jax.ShapeDtypeStruct((nl, mem2.shape[0], n), BF16),
        compiler_params=_cparams("arbitrary", "arbitrary"),
        name="mem_kv_proj",
    )(mem2, g, w)


BF16_ROWS = 2 * SUBLANE


def _mix_ffn_body(h_ref, hp_ref, a_ref, ap_ref, m_ref, mp_ref, wa_ref, wm_ref, g_ref, wup_ref, cw_ref, cb_ref,
                  wdn_ref, gf_ref, o_ref, act_ref, *, tm, tf, last):
    g = g_ref[...]
    nr = tm // SUBLANE
    interleave = lambda x: x.reshape(SUBLANE, nr, x.shape[-1]).swapaxes(0, 1).reshape(tm, x.shape[-1])
    deinterleave = lambda x: x.reshape(nr, SUBLANE, x.shape[-1]).swapaxes(0, 1).reshape(tm, x.shape[-1])
    h = interleave(h_ref[...] + _dot(a_ref[...], wa_ref[...]) + _dot(m_ref[...], wm_ref[...]))
    x = _rms(h, g).astype(BF16)
    keep = jnp.where(pl.program_id(1) > 0, 1.0, 0.0)
    h_prev = (hp_ref[...] + _dot(ap_ref[...], wa_ref[...]) + _dot(mp_ref[...], wm_ref[...]))[BF16_ROWS - SUBLANE:]
    x_prev = (_rms(h_prev, g) * keep).astype(BF16)
    first = lax.broadcasted_iota(jnp.int32, (SUBLANE, tf), 0) == 0
    nchunk = FFN_DIM // tf

    def up(j):
        cols = [pl.ds(off + j * tf, tf) for off in (0, FFN_DIM)]
        return tuple((_dot(x, wup_ref[:, c]), _dot(x_prev, wup_ref[:, c])) for c in cols)

    def conv(u, u_prev, off):
        w = cw_ref[:, off:off + tf]
        wrap = lambda r, k: jnp.where(first, u_prev[SUBLANE - k:SUBLANE - k + 1, :],
                                      pltpu.roll(u[r * SUBLANE:(r + 1) * SUBLANE, :], 1, 0))
        back1 = jnp.concatenate([wrap(nr - 1, 1), u[0:tm - SUBLANE, :]], axis=0)
        back2 = jnp.concatenate([wrap(nr - 2, 2), wrap(nr - 1, 1), u[0:tm - 2 * SUBLANE, :]], axis=0)
        return cb_ref[:, off:off + tf] + w[0:1, :] * back2 + w[1:2, :] * back1 + w[2:3, :] * u

    u_next = up(0)
    for j in range(nchunk):
        (ua, ua_prev), (ub, ub_prev) = u_next
        if j + 1 < nchunk:
            u_next = up(j + 1)
        a = conv(ua, ua_prev, j * tf)
        b = conv(ub, ub_prev, FFN_DIM + j * tf)
        act_ref[:, j * tf:(j + 1) * tf] = (a * _sigmoid(a) * b).astype(BF16)
    out = h + _dot(act_ref[...], wdn_ref[...])
    o_ref[...] = deinterleave(_rms(out, gf_ref[...]) if last else out)


def _mix_ffn(h, main, mo, wa, klayer, wm, layer, g, wup, cw, cb, wdn, gf, last, batch):
    t = h.shape[0]
    tm = FFN_ROWS
    nt = t // batch // tm
    hb = tm // BF16_ROWS
    cur = lambda n: pl.BlockSpec((tm, n), lambda b, i: (b * nt + i, 0))
    prev = lambda n: pl.BlockSpec((BF16_ROWS, n), lambda b, i: (jnp.maximum((b * nt + i) * hb - 1, 0), 0))
    return pl.pallas_call(
        functools.partial(_mix_ffn_body, tm=tm, tf=FFN_TILE, last=last),
        grid=(batch, nt),
        in_specs=[cur(D_MODEL), prev(D_MODEL), cur(main.shape[1]), prev(main.shape[1]),
                  cur(mo.shape[1]), prev(mo.shape[1]), _layer_of(wa, klayer), _layer_of(wm, layer), _layer_of(g, layer),
                  _layer_of(wup, layer), _layer_of(cw, layer), _layer_of(cb, layer), _layer_of(wdn, layer),
                  _resident(gf)],
        out_specs=cur(D_MODEL),
        out_shape=jax.ShapeDtypeStruct((t, D_MODEL), F32),
        scratch_shapes=[pltpu.VMEM((tm, FFN_DIM), BF16)],
        compiler_params=_cparams("parallel", "parallel"),
        name="mix_ffn",
    )(h, h, main, main, mo, mo, wa, wm, g, wup, cw, cb, wdn, gf)


KV_NAT = NSA_GROUPS * NSA_HEAD_DIM


def _kv_proj_body(h_ref, g_ref, w_ref, cx_ref, ks_ref, kw_ref, vs_ref, vw_ref, *, tm):
    xn = _rms(h_ref[...], g_ref[...]).astype(BF16)
    for j in range(2):
        y = _dot(xn, w_ref[:, j * KV_NAT:(j + 1) * KV_NAT])
        cx_ref[j] = y.reshape(tm // CMP_STRIDE, CMP_STRIDE, KV_NAT).swapaxes(0, 1).astype(BF16)
    key = pl.program_id(1) * tm + lax.broadcasted_iota(jnp.int32, (tm, LANE), 0)
    lane = lax.broadcasted_iota(jnp.int32, (tm, LANE), 1)
    onehot = jnp.where(lax.shift_right_logical(key, int(math.log2(SEL_BLOCK))) == lane - NSA_HEAD_DIM, 1.0, 0.0)
    low = lane < NSA_HEAD_DIM
    ones = jnp.ones((V_ROWS - NSA_HEAD_DIM, tm), BF16)
    slots = lambda n: _dot(xn, w_ref[:, 2 * KV_NAT + n * MXU_N:2 * KV_NAT + (n + 1) * MXU_N])
    for gp in range(NSA_GROUPS // 2):
        kk = slots(gp)
        vv = slots(NSA_GROUPS // 2 + gp)
        for i in range(2):
            kslot = kk[:, i * LANE:(i + 1) * LANE]
            ks_ref[2 * gp + i] = jnp.where(low, kslot, onehot).astype(BF16)
            kw_ref[2 * gp + i] = jnp.where(low, pltpu.roll(kslot, NSA_HEAD_DIM, 1), 0.0).astype(BF16)
            vt = vv[:, i * LANE:(i + 1) * LANE].T.astype(BF16)
            vs_ref[2 * gp + i] = jnp.concatenate([vt[0:NSA_HEAD_DIM], ones], axis=0)
            vw_ref[2 * gp + i] = jnp.concatenate([vt[NSA_HEAD_DIM:], ones], axis=0)


def _kv_proj(h, g, w, batch):
    t = h.shape[0]
    s = t // batch
    tm = ROW_TILE
    nt = s // tm
    gr = NSA_GROUPS
    return pl.pallas_call(
        functools.partial(_kv_proj_body, tm=tm),
        grid=(batch, nt),
        in_specs=[pl.BlockSpec((tm, D_MODEL), lambda b, i: (b * nt + i, 0)),
                  pl.BlockSpec(g.shape, lambda b, i: (0, 0)),
                  pl.BlockSpec(w.shape, lambda b, i: (0, 0))],
        out_specs=[pl.BlockSpec((2, CMP_STRIDE, tm // CMP_STRIDE, KV_NAT), lambda b, i: (0, 0, b * nt + i, 0)),
                   pl.BlockSpec((None, gr, tm, LANE), lambda b, i: (b, 0, i, 0)),
                   pl.BlockSpec((None, gr, tm, LANE), lambda b, i: (b, 0, i, 0)),
                   pl.BlockSpec((None, gr, V_ROWS, tm), lambda b, i: (b, 0, 0, i)),
                   pl.BlockSpec((None, gr, V_ROWS, tm), lambda b, i: (b, 0, 0, i))],
        out_shape=[jax.ShapeDtypeStruct((2, CMP_STRIDE, t // CMP_STRIDE, KV_NAT), BF16),
                   jax.ShapeDtypeStruct((batch, gr, s, LANE), BF16),
                   jax.ShapeDtypeStruct((batch, gr, s, LANE), BF16),
                   jax.ShapeDtypeStruct((batch, gr, V_ROWS, s), BF16),
                   jax.ShapeDtypeStruct((batch, gr, V_ROWS, s), BF16)],
        compiler_params=_cparams("parallel", "parallel"),
        name="nsa_kv_proj",
    )(h, g, w)


def _compress_body(x_ref, wt_ref, wb_ref, pos_ref, w1_ref, b1_ref, w2_ref, b2_ref, on_ref, ot_ref, *, ncp):
    top = _dot(x_ref[0], wt_ref[0])
    bot = _dot(x_ref[0], wb_ref[0])
    for l in range(1, CMP_STRIDE):
        top = top + _dot(x_ref[l], wt_ref[l])
        bot = bot + _dot(x_ref[l], wb_ref[l])
    posb = _dot(pos_ref[...], w1_ref[...])[0:1, :] + b1_ref[...]
    for gi in range(NSA_GROUPS):
        sl = slice(gi * CMP_HIDDEN, (gi + 1) * CMP_HIDDEN)
        hid = top[:, sl] + pltpu.roll(bot[:, sl], ncp - 1, 0) + posb
        hid = (hid * _sigmoid(hid)).astype(BF16)
        out = _dot(hid, w2_ref[...]) + b2_ref[...]
        on_ref[gi] = out.astype(BF16)
        ot_ref[gi] = out.T.astype(BF16)


def _compress(x16, wt, wb, pos, w1, b1, w2, b2, batch):
    ncp = x16.shape[2] // batch
    gr = NSA_GROUPS
    per_j = lambda a: pl.BlockSpec((None,) + a.shape[1:], lambda j, b: (j,) + (0,) * (a.ndim - 1))
    return pl.pallas_call(
        functools.partial(_compress_body, ncp=ncp),
        grid=(2, batch),
        in_specs=[pl.BlockSpec((None, CMP_STRIDE, ncp, KV_NAT), lambda j, b: (j, 0, b, 0)),
                  per_j(wt), per_j(wb), per_j(pos), per_j(w1), per_j(b1), per_j(w2), per_j(b2)],
        out_specs=[pl.BlockSpec((None, None, gr, ncp, LANE), lambda j, b: (j, b, 0, 0, 0)),
                   pl.BlockSpec((None, None, gr, LANE, ncp), lambda j, b: (j, b, 0, 0, 0))],
        out_shape=[jax.ShapeDtypeStruct((2, batch, gr, ncp, LANE), BF16),
                   jax.ShapeDtypeStruct((2, batch, gr, LANE, ncp), BF16)],
        compiler_params=_cparams("arbitrary", "arbitrary"),
        name="nsa_compress",
    )(x16, wt, wb, pos, w1, b1, w2, b2)


NSA_QW = NSA_HEADS * NSA_HEAD_DIM
G3 = NSA_REP * TQ


def _nsa_in_body(h_ref, g_ref, w_ref, kv_ref, qt_ref, gt_ref, mo_ref):
    xn = _rms(h_ref[...], g_ref[...]).astype(BF16)
    for j in range(NSA_QW // MXU_N):
        y = _dot(xn, w_ref[:, j * MXU_N:(j + 1) * MXU_N]) * (NSA_HEAD_DIM ** -0.5 * LOG2E)
        qt_ref[j * MXU_N:(j + 1) * MXU_N, :] = y.T.astype(BF16)
    tail = _dot(xn, w_ref[:, NSA_QW:])
    gt_ref[...] = _sigmoid(tail[:, 0:LANE]).T
    mo_ref[...] = _mem_attention(_spread_heads(tail, LANE, MEM_HEADS, MEM_HEAD_DIM, MEM_DP).astype(BF16), kv_ref)


def _nsa_in_proj(h, layer, nlayer, g, w, mem_kv, batch):
    t = h.shape[0]
    tm = ROW_TILE
    row = lambda n: pl.BlockSpec((tm, n), lambda i: (i, 0))
    nt = t // batch // tm
    col = lambda n: pl.BlockSpec((None, n, tm), lambda i: (i // nt, 0, i % nt))
    return pl.pallas_call(
        _nsa_in_body,
        grid=(t // tm,),
        in_specs=[row(D_MODEL), _layer_of(g, layer), _layer_of(w, nlayer), _mem_kv_spec(mem_kv, layer, t, tm, batch)],
        out_specs=[col(NSA_QW), col(LANE), row(MEM_QW)],
        out_shape=[jax.ShapeDtypeStruct((batch, NSA_QW, t // batch), BF16),
                   jax.ShapeDtypeStruct((batch, LANE, t // batch), F32),
                   jax.ShapeDtypeStruct((t, MEM_QW), BF16)],
        compiler_params=_cparams("parallel"),
        name="nsa_in_proj",
    )(h, g, w, mem_kv)


def _nsa_attn_body(qt_ref, gt_ref, kc_ref, vct_ref, ks_ref, kw_ref, vs_ref, vw_ref, ovt_ref, tz_ref, cb_ref,
                   o_ref, sc_ref, qa_ref, acc_ref, ot_ref, *s_slots, nsb, ncp, n_sel, nseq):
    n = pl.program_id(1)
    t0 = n * TQ
    dh = NSA_HEAD_DIM
    cstart = pl.multiple_of(ncp - n * (TQ // CMP_STRIDE), SUBLANE)
    sees_any = t0 + (lax.broadcasted_iota(jnp.int32, (1, G3), 1) & (TQ - 1)) >= CMP_BLOCK - 1
    jj = lax.broadcasted_iota(jnp.int32, (nsb, TQ), 0)
    cur = lax.shift_right_logical(t0 + lax.broadcasted_iota(jnp.int32, (nsb, TQ), 1), int(math.log2(SEL_BLOCK)))
    forced = (jj == 0) | (jj == cur) | (jj == cur - 1)
    zpad = jnp.zeros((dh, G3), BF16)
    nwt = WINDOW // TQ
    units = [(bi, gi) for bi in range(nseq) for gi in range(NSA_GROUPS)]

    def scores(k_ref, u, first, count, tz_index, qa):
        bi, gi = units[u]
        koff = pl.multiple_of(first * TQ, TQ)
        s = _dot(k_ref[bi, gi, pl.ds(koff, count * TQ), :], qa)
        return [s[i * TQ:(i + 1) * TQ] + tz_ref[gi, tz_index(n - (first + i))] for i in range(count)]

    def col_max(parts):
        mx = parts[0]
        for x in parts[1:]:
            mx = jnp.maximum(mx, x)
        return jnp.max(mx, axis=0, keepdims=True)

    def probs(parts, m):
        return jnp.concatenate([jnp.exp2(x - m).astype(BF16) for x in parts], axis=0)

    def values(v_ref, u, first, count, p):
        bi, gi = units[u]
        koff = pl.multiple_of(first * TQ, TQ)
        return _dot(v_ref[bi, gi, :, pl.ds(koff, count * TQ)], p)

    sel_index = lambda d: jnp.where(d < 0, 2, jnp.minimum(d, 2))
    win_index = lambda d: jnp.where(d < 0, 4, jnp.where(d == nwt, 3, jnp.minimum(d, 2)))

    groups = range(len(units))
    gate = lambda bi, h, c: gt_ref[bi, pl.ds(h * NSA_BRANCHES + c, 1), :]
    head_lanes = lambda r: slice(r * TQ, (r + 1) * TQ)
    head_rows = lambda bi, h: (bi, slice(h * dh, (h + 1) * dh))

    q3s = [jnp.concatenate([qt_ref[head_rows(bi, gi * NSA_REP + r)] for r in range(NSA_REP)], axis=1)
           for bi, gi in units]
    cs = [_dot(kc_ref[bi, gi], jnp.concatenate([q3s[u], zpad], axis=0)) + cb_ref[gi, pl.ds(cstart, ncp), :]
          for u, (bi, gi) in enumerate(units)]
    cps = [jnp.exp2(s - jnp.max(s, axis=0, keepdims=True)) for s in cs]
    cps = [p * jnp.where(sees_any, 1.0 / jnp.sum(p, axis=0, keepdims=True), 0.0) for p in cps]
    ocs = [_dot(vct_ref[bi, gi], cps[u].astype(BF16)) for u, (bi, gi) in enumerate(units)]
    scs = []
    for gi in groups:
        psum = cps[gi][:, 0:TQ]
        for r in range(1, NSA_REP):
            psum = psum + cps[gi][:, head_lanes(r)]
        p1 = psum.astype(BF16)
        p2 = (psum - p1.astype(F32)).astype(BF16)
        imp = _dot(ovt_ref[...], p1) + _dot(ovt_ref[...], p2)
        score = jnp.where(forced, 1e4, jnp.where(jj <= cur, imp[0:nsb], -1.0))
        scs.append(jnp.where(score < 0.0, -1, lax.bitcast_convert_type(score, jnp.int32)))
    for u, (bi, gi) in enumerate(units):
        sc_ref[u] = scs[u]
        for r in range(NSA_REP):
            h = gi * NSA_REP + r
            ot_ref[head_rows(bi, h)] = gate(bi, h, 0) * ocs[u][0:dh, head_lanes(r)]

    scs1 = [k + 1 for k in scs]

    def rank_step(i4, cnts):
        cnts = list(cnts)
        for u in range(RANK_UNROLL):
            i = i4 * RANK_UNROLL + u
            lower = i < jj
            for gi in groups:
                rowk = sc_ref[gi, pl.ds(i, 1), :]
                before = rowk >= jnp.where(lower, scs[gi], scs1[gi])
                cnts[gi] = cnts[gi] + jnp.where(before, 1, 0)
        return tuple(cnts)
    rank_trips = jnp.minimum((2 * n + 2 + RANK_UNROLL - 1) // RANK_UNROLL, nsb // RANK_UNROLL)
    cnts = lax.fori_loop(0, rank_trips, rank_step, tuple(jnp.zeros((nsb, TQ), jnp.int32) for _ in groups))
    for gi in groups:
        selneg = jnp.where((cnts[gi] < n_sel) & (jj <= cur), 0.0, NEG).astype(BF16)
        if nsb < SEL_BLOCK:
            selneg = jnp.concatenate([selneg, jnp.zeros((SEL_BLOCK - nsb, TQ), BF16)], axis=0)
        qa_ref[gi] = jnp.concatenate([q3s[gi], jnp.concatenate([selneg] * NSA_REP, axis=1)], axis=0)

    acc_ref[...] = jnp.zeros_like(acc_ref)
    half = len(units) // 2
    s_refs = (s_slots[:half], s_slots[half:])

    def score_half(it, hb):
        plist = [scores(ks_ref, hb * half + u, it * SEL_TILES, SEL_TILES, sel_index, qa_ref[hb * half + u])
                 for u in range(half)]
        for u, parts in enumerate(plist):
            for i, x in enumerate(parts):
                s_refs[hb][u][i * TQ:(i + 1) * TQ, :] = x
        return [col_max(parts) for parts in plist]

    def finish_half(it, hb, ms, bms):
        m2s = [jnp.maximum(ms[u], bms[u]) for u in range(half)]
        ps = [probs([s_refs[hb][u][i * TQ:(i + 1) * TQ, :] for i in range(SEL_TILES)], m2s[u]) for u in range(half)]
        vals = [values(vs_ref, hb * half + u, it * SEL_TILES, SEL_TILES, ps[u]) for u in range(half)]
        for u in range(half):
            gi = hb * half + u
            acc_ref[gi] = jnp.exp2(ms[u] - m2s[u]) * acc_ref[gi] + vals[u]
        return m2s

    def sel_trip(it, carry, score_next):
        ms0, ms1, bm0 = carry
        bm1 = score_half(it, 1)
        ms0 = finish_half(it, 0, ms0, bm0)
        if score_next:
            bm0 = score_half(it + 1, 0)
        ms1 = finish_half(it, 1, ms1, bm1)
        return ms0, ms1, bm0

    neg = [jnp.full((1, G3), NEG, F32) for _ in range(half)]
    carry = (neg, neg, score_half(0, 0))

    wfirst = jnp.maximum(n - nwt, 0)
    wparts = [scores(kw_ref, gi, wfirst, nwt + 1, win_index, qa_ref[gi]) for gi in groups]
    wps = [probs(parts, col_max(parts)) for parts in wparts]
    ows = [values(vw_ref, gi, wfirst, nwt + 1, wps[gi]) for gi in groups]
    for u, (bi, gi) in enumerate(units):
        o_w = ows[u][0:dh, :] * (1.0 / ows[u][dh:dh + 1, :])
        for r in range(NSA_REP):
            h = gi * NSA_REP + r
            ot_ref[head_rows(bi, h)] += gate(bi, h, 2) * o_w[:, head_lanes(r)]

    carry = lax.fori_loop(0, n // SEL_TILES, lambda it, c: sel_trip(it, c, True), carry)
    sel_trip(n // SEL_TILES, carry, False)

    for u, (bi, gi) in enumerate(units):
        o_s = acc_ref[u, 0:dh, :] * (1.0 / acc_ref[u, dh:dh + 1, :])
        for r in range(NSA_REP):
            h = gi * NSA_REP + r
            ot_ref[head_rows(bi, h)] += gate(bi, h, 1) * o_s[:, head_lanes(r)]

    for bi in range(nseq):
        for j in range(NSA_QW // LANE):
            o_ref[bi, :, j * LANE:(j + 1) * LANE] = ot_ref[bi, j * LANE:(j + 1) * LANE, :].T.astype(BF16)


def _nsa_attn(qt, gt, kc, vct, ksel, kwin, vsel, vwin, ovt, tz, cb):
    batch, _, s = qt.shape
    nseq = NSA_SEQS if batch % NSA_SEQS == 0 else 1
    nt = s // TQ
    nsb = s // SEL_BLOCK
    ncp = kc.shape[-2]
    nu = nseq * NSA_GROUPS
    per_b = lambda a: pl.BlockSpec((nseq,) + a.shape[1:], lambda b, i: (b,) + (0,) * (a.ndim - 1),
                                   pipeline_mode=pl.Buffered(1))
    out = pl.pallas_call(
        functools.partial(_nsa_attn_body, nsb=nsb, ncp=ncp, n_sel=min(SEL_TOPK, nsb), nseq=nseq),
        grid=(batch // nseq, nt),
        in_specs=[pl.BlockSpec((nseq, NSA_QW, TQ), lambda b, i: (b, 0, i)),
                  pl.BlockSpec((nseq, LANE, TQ), lambda b, i: (b, 0, i)),
                  per_b(kc), per_b(vct), per_b(ksel), per_b(kwin), per_b(vsel), per_b(vwin),
                  _resident(ovt), _resident(tz), _resident(cb)],
        out_specs=pl.BlockSpec((nseq, TQ, NSA_QW), lambda b, i: (b, i, 0)),
        out_shape=jax.ShapeDtypeStruct((batch, s, NSA_QW), BF16),
        scratch_shapes=[pltpu.VMEM((nu, nsb, TQ), jnp.int32), pltpu.VMEM((nu, LANE, G3), BF16),
                        pltpu.VMEM((nu, V_ROWS, G3), F32), pltpu.VMEM((nseq, NSA_QW, TQ), F32),
                        ] + [pltpu.VMEM((SEL_TILES * TQ, G3), F32)] * nu,
        compiler_params=_cparams("parallel", "arbitrary"),
        name="nsa_attn",
    )(qt, gt, kc, vct, ksel, kwin, vsel, vwin, ovt, tz, cb)
    return out.reshape(batch * s, NSA_QW)


def _rel_bucket_np(dist):
    dist = np.maximum(dist, 0)
    max_exact = REL_BUCKETS // 2
    ratio = np.log(np.maximum(dist, 1).astype(np.float32) / np.float32(max_exact)) / np.float32(
        math.log(REL_MAX_DIST / max_exact))
    large = max_exact + (ratio * np.float32(REL_BUCKETS - max_exact)).astype(np.int32)
    large = np.minimum(large, REL_BUCKETS - 1)
    return np.where(dist < max_exact, dist, large).astype(np.int32)


def _group_lanes(a):
    hh, r, c = a.shape
    return a.reshape(NSA_GROUPS, NSA_REP, r, c).transpose(0, 2, 1, 3).reshape(NSA_GROUPS, r, NSA_REP * c)


def _bias_tables(rel_bias, ncp):
    k = np.arange(TQ)[:, None]
    q = np.arange(TQ)[None, :]
    tbl = rel_bias.astype(F32)

    def lookup(idx):
        onehot = (jnp.asarray(idx.reshape(1, -1)) == jnp.arange(REL_BUCKETS)[:, None]).astype(F32)
        out = jnp.dot(tbl.T, onehot, precision=lax.Precision.HIGHEST)
        return out.reshape((NSA_HEADS,) + idx.shape)

    far = jnp.broadcast_to(tbl[REL_BUCKETS - 1][:, None, None], (NSA_HEADS, TQ, TQ))
    t0 = jnp.where(jnp.asarray(k <= q)[None], lookup(_rel_bucket_np(q - k)), NEG)
    t1 = lookup(_rel_bucket_np(TQ + q - k))
    t3 = jnp.where(jnp.asarray(k > q)[None], far, NEG)
    tz = jnp.stack([_group_lanes(x) for x in (t0, t1, far, t3, jnp.full_like(far, NEG))], axis=1)
    m = ncp - np.arange(2 * ncp)[:, None]
    d = CMP_STRIDE * m + q - (CMP_BLOCK - 1)
    idx = np.where((d >= 0) & (d < REL_MAX_DIST), _rel_bucket_np(d), REL_BUCKETS - 1)
    cb = _group_lanes(jnp.where(jnp.asarray(d >= 0)[None], lookup(idx), NEG))
    return tz * LOG2E, cb * LOG2E


def _overlap_table(s, ncp):
    nsb = s // SEL_BLOCK
    nc = (s - CMP_BLOCK) // CMP_STRIDE + 1
    cs = np.arange(ncp) * CMP_STRIDE
    ce = cs + CMP_BLOCK - 1
    ss = np.arange(SEL_BLOCK) * SEL_BLOCK
    ov = (cs[None, :] < ss[:, None] + SEL_BLOCK) & (ce[None, :] >= ss[:, None])
    ov &= (np.arange(ncp) < nc)[None, :] & (np.arange(SEL_BLOCK) < nsb)[:, None]
    return jnp.asarray(ov, dtype=BF16)


def kernel(x, mem, norm_mix, norm_mem, w_mem_kv, w_out, norm_ffn, w_up, conv_w, conv_b, w_down,
           gla_w_in, gla_w_gate_up, gla_b_gate, gla_out_norm, nsa_w_in, kv_norm, w_kv_shared,
           cmp_pos, cmp_w1, cmp_b1, cmp_w2, cmp_b2, rel_bias, final_norm):
    batch, seq = x.shape[0], x.shape[1]
    t = batch * seq
    h = x.reshape(t, D_MODEL)
    row = lambda v: v.reshape(1, -1).astype(F32)

    wk, wv = w_mem_kv[..., :MEM_W], w_mem_kv[..., MEM_W:]
    w_mkv = jnp.concatenate([_pad_heads(wk, MEM_HEADS, MEM_HEAD_DIM, MEM_DP),
                             _pad_heads(wv, MEM_HEADS, MEM_HEAD_DIM, MEM_DP)], axis=-1).astype(BF16)
    mem_kv_all = _mem_kv_proj(mem.reshape(-1, D_MODEL), norm_mem.reshape(DEPTH, 1, D_MODEL), w_mkv, batch)

    ffn_params = (norm_ffn.reshape(DEPTH, 1, D_MODEL).astype(F32), w_up.astype(BF16), conv_w.astype(F32),
                  conv_b.reshape(DEPTH, 1, 2 * FFN_DIM).astype(F32), w_down.astype(BF16))
    norm_mix3 = norm_mix.reshape(DEPTH, 1, D_MODEL).astype(F32)
    w_o_mem = _pad_head_rows(w_out[:, MAIN_W:], MEM_HEADS, MEM_HEAD_DIM, MEM_DP).astype(BF16)

    c0 = GLA_HEADS * GLA_DK
    c1 = 2 * c0
    c2 = c1 + GLA_HEADS * GLA_DV
    c3 = c2 + GLA_HEADS * GLA_DV
    c4 = c3 + GLA_RANK
    gla_w = jnp.concatenate([gla_w_in[..., :c3], gla_w_in[..., c4:],
                             _pad_heads(gla_w_in[..., c3:c4], 1, GLA_RANK, LANE)], axis=-1).astype(BF16)
    gla_wg = jnp.pad(_pad_heads(gla_w_gate_up, GLA_HEADS, GLA_DK, GLA_DKP),
                     ((0, 0), (0, LANE - GLA_RANK), (0, 0))).astype(BF16)
    gla_bg = _pad_heads(gla_b_gate, GLA_HEADS, GLA_DK, GLA_DKP).reshape(N_A_LAYERS, 1, GLA_QW).astype(F32)
    gla_on = jnp.pad(gla_out_norm, ((0, 0), (0, GLA_DVP - GLA_DV))).reshape(N_A_LAYERS, 1, GLA_DVP).astype(F32)
    gla_wo = _pad_head_rows(w_out[:N_A_LAYERS, :MAIN_W], GLA_HEADS, GLA_DV, GLA_DVP).astype(BF16)

    n0 = NSA_HEADS * NSA_HEAD_DIM
    n1 = n0 + NSA_HEADS * NSA_BRANCHES
    nsa_w = jnp.concatenate([nsa_w_in[..., :n0], _pad_heads(nsa_w_in[..., n0:n1], 1, NSA_HEADS * NSA_BRANCHES, LANE),
                             nsa_w_in[..., n1:]], axis=-1).astype(BF16)
    nsa_wo = w_out[N_A_LAYERS:, :MAIN_W].astype(BF16)

    shared = None
    for i in range(DEPTH):
        if i < N_A_LAYERS:
            q, k, v, go, la, mo = _gla_in_proj(h, i, i, norm_mix3, gla_w, gla_wg, gla_bg, mem_kv_all, batch)
            main = _gla_mixer(q, k, la, v, go, gla_on, i, batch)
            w_o_main, klayer = gla_wo, i
        else:
            if shared is None:
                ncp = seq // CMP_STRIDE
                gr, dh = NSA_GROUPS, NSA_HEAD_DIM
                wkv = w_kv_shared.reshape(D_MODEL, 6, gr, dh)
                pair = lambda a, b: jnp.concatenate([wkv[:, a], wkv[:, b]], axis=-1).reshape(D_MODEL, gr * 2 * dh)
                w_kv = jnp.concatenate([wkv[:, 0].reshape(D_MODEL, KV_NAT), wkv[:, 1].reshape(D_MODEL, KV_NAT),
                                        pair(2, 4), pair(3, 5)], axis=1).astype(BF16)
                x16, ksel, kwin, vsel, vwin = _kv_proj(h, row(kv_norm), w_kv, batch)
                w1 = cmp_w1.reshape(2, 2, CMP_STRIDE, dh, CMP_HIDDEN)
                eye = jnp.eye(gr, dtype=F32)
                w1x = jnp.einsum('jhldc,gk->jhlgdkc', w1, eye).reshape(2, 2, CMP_STRIDE, KV_NAT, gr * CMP_HIDDEN)
                w1x = w1x.astype(BF16)
                pos8 = jnp.broadcast_to(cmp_pos.reshape(2, 1, CMP_BLOCK * dh), (2, SUBLANE, CMP_BLOCK * dh)).astype(BF16)
                w2p = jnp.pad(cmp_w2, ((0, 0), (0, 0), (0, LANE - dh))).astype(BF16)
                b2p = jnp.pad(cmp_b2, ((0, 0), (0, LANE - dh))).reshape(2, 1, LANE).astype(F32)
                cnat, ctr = _compress(x16, w1x[:, 0], w1x[:, 1], pos8, cmp_w1.astype(BF16),
                                      cmp_b1.reshape(2, 1, CMP_HIDDEN).astype(F32), w2p, b2p, batch)
                tz, cb = _bias_tables(rel_bias, ncp)
                ov = _overlap_table(seq, ncp)
                shared = (cnat[0], ctr[1], ksel, kwin, vsel, vwin, ov, tz, cb)
            q, gates, mo = _nsa_in_proj(h, i, i - N_A_LAYERS, norm_mix3, nsa_w, mem_kv_all, batch)
            main = _nsa_attn(q, gates, *shared)
            w_o_main, klayer = nsa_wo, i - N_A_LAYERS
        h = _mix_ffn(h, main, mo, w_o_main, klayer, w_o_mem, i, *ffn_params, row(final_norm), i == DEPTH - 1, batch)
    return h.reshape(batch, seq, D_MODEL)
```

```python
import functools
import math

import numpy as np
import jax
import jax.numpy as jnp
from jax import lax
from jax.experimental import pallas as pl
from jax.experimental.pallas import tpu as pltpu

F32 = jnp.float32
BF16 = jnp.bfloat16

D_MODEL = 1024
DEPTH = 4
N_A_LAYERS = DEPTH // 2
MEM_HEADS = 4
MEM_HEAD_DIM = 64
MEM_W = MEM_HEADS * MEM_HEAD_DIM
MAIN_W = D_MODEL - MEM_W
GLA_HEADS = 4
GLA_DV = MAIN_W // GLA_HEADS
GLA_DK = GLA_DV // 2
GLA_RANK = 16
GLA_GATE_NORM = 16.0
NSA_HEADS = 12
NSA_GROUPS = 4
NSA_HEAD_DIM = MAIN_W // NSA_HEADS
NSA_REP = NSA_HEADS // NSA_GROUPS
NSA_BRANCHES = 3
CMP_BLOCK = 32
CMP_STRIDE = 16
CMP_HIDDEN = 128
SEL_BLOCK = 64
SEL_TOPK = 16
WINDOW = 512
REL_BUCKETS = 32
REL_MAX_DIST = 128
FFN_DIM = 2816
CONV_WIDTH = 3
EPS = 1e-6

LANE = 128
MXU_N = 256
SUBLANE = 8
VMEM_LIMIT = 56 * 1024 * 1024
GLA_DKP = LANE
GLA_DVP = 2 * LANE
MEM_DP = LANE
NEG = -1e30
TQ = 128
ROW_TILE = 1024
GLA_CHUNK = 64
GLA_SAFE_DECAY = 80.0
FFN_TILE = 256
FFN_ROWS = 1024
SEL_TILES = 4
NSA_SEQS = 2
V_ROWS = NSA_HEAD_DIM + 2 * SUBLANE
RANK_UNROLL = 4
LOG2E = math.log2(math.e)

NT = (((1,), (1,)), ((), ()))
TN = (((0,), (0,)), ((), ()))


def _cparams(*sem):
    return pltpu.CompilerParams(dimension_semantics=sem, vmem_limit_bytes=VMEM_LIMIT)


def _rms(x, g):
    return x * lax.rsqrt(jnp.mean(x * x, axis=-1, keepdims=True) + EPS) * g


def _sigmoid(x):
    return 1.0 / (1.0 + jnp.exp(-x))


def _dot(a, b):
    return jnp.dot(a, b, preferred_element_type=F32)


def _resident(a):
    return pl.BlockSpec(a.shape, lambda *_: (0,) * a.ndim, pipeline_mode=pl.Buffered(1))


def _layer_of(a, layer):
    return pl.BlockSpec((None,) + a.shape[1:], lambda *_: (layer,) + (0,) * (a.ndim - 1),
                        pipeline_mode=pl.Buffered(1))


def _pad_heads(w, nh, d, dp):
    lead = w.shape[:-1]
    w = w.reshape(lead + (nh, d))
    w = jnp.pad(w, [(0, 0)] * len(lead) + [(0, 0), (0, dp - d)])
    return w.reshape(lead + (nh * dp,))


def _pad_head_rows(w, nh, d, dp):
    lead, n = w.shape[:-2], w.shape[-1]
    w = jnp.pad(w.reshape(lead + (nh, d, n)), [(0, 0)] * len(lead) + [(0, 0), (0, dp - d), (0, 0)])
    return w.reshape(lead + (nh * dp, n))


def _spread_heads(y, off, nh, d, dp):
    pad = jnp.zeros((y.shape[0], dp - d), F32)
    return jnp.concatenate([x for hd in range(nh) for x in (y[:, off + hd * d:off + (hd + 1) * d], pad)], axis=1)


def _mem_attention(q, kv_ref):
    outs = []
    for hd in range(MEM_HEADS):
        sl = slice(hd * MEM_DP, (hd + 1) * MEM_DP)
        sv = slice(MEM_QW + hd * MEM_DP, MEM_QW + (hd + 1) * MEM_DP)
        s = lax.dot_general(q[:, sl], kv_ref[:, sl], NT, preferred_element_type=F32) * MEM_HEAD_DIM ** -0.5
        p = jnp.exp(s - jnp.max(s, axis=-1, keepdims=True))
        l = jnp.sum(p, axis=-1, keepdims=True)
        outs.append((_dot(p.astype(BF16), kv_ref[:, sv]) / l)[:, 0:MEM_HEAD_DIM])
    return jnp.concatenate(outs, axis=1).astype(BF16)


GLA_QW = GLA_HEADS * GLA_DKP
GLA_VW = GLA_HEADS * GLA_DVP
MEM_QW = MEM_HEADS * MEM_DP
GLA_QK = GLA_HEADS * GLA_DK
GLA_VD = GLA_HEADS * GLA_DV
GLA_OFF_V = 2 * GLA_QK
GLA_OFF_G = GLA_OFF_V + GLA_VD
GLA_OFF_MQ = GLA_OFF_G + GLA_VD
GLA_OFF_LR = GLA_OFF_MQ + MEM_W


def _gla_in_body(h_ref, g_ref, w_ref, wg_ref, bg_ref, kv_ref, q_ref, k_ref, v_ref, go_ref, la_ref, mo_ref):
    xn = _rms(h_ref[...], g_ref[...]).astype(BF16)

    spread = _spread_heads

    def proj(lo, n):
        return _dot(xn, w_ref[:, lo:lo + n])

    qk = proj(0, 2 * GLA_QK)
    q_ref[...] = spread(qk, 0, GLA_HEADS, GLA_DK, GLA_DKP).astype(BF16)
    k_ref[...] = spread(qk, GLA_QK, GLA_HEADS, GLA_DK, GLA_DKP).astype(BF16)
    v_ref[...] = spread(proj(GLA_OFF_V, GLA_VD), 0, GLA_HEADS, GLA_DV, GLA_DVP).astype(BF16)
    go_ref[...] = spread(proj(GLA_OFF_G, GLA_VD), 0, GLA_HEADS, GLA_DV, GLA_DVP).astype(BF16)
    tail = proj(GLA_OFF_MQ, MEM_W + LANE)
    z = _dot(tail[:, MEM_W:].astype(BF16), wg_ref[...]) + bg_ref[...]
    la_ref[...] = (jnp.minimum(z, 0.0) - jnp.log(1.0 + jnp.exp(-jnp.abs(z)))) * (1.0 / GLA_GATE_NORM)
    mo_ref[...] = _mem_attention(spread(tail, 0, MEM_HEADS, MEM_HEAD_DIM, MEM_DP).astype(BF16), kv_ref)


def _mem_kv_spec(mem_kv, layer, t, tm, batch):
    nt = t // batch // tm
    return pl.BlockSpec((None, mem_kv.shape[1] // batch, mem_kv.shape[2]), lambda i: (layer, i // nt, 0))


def _gla_in_proj(h, layer, glayer, g, w, wg, bg, mem_kv, batch):
    t = h.shape[0]
    tm = ROW_TILE
    row = lambda n: pl.BlockSpec((tm, n), lambda i: (i, 0))
    return pl.pallas_call(
        _gla_in_body,
        grid=(t // tm,),
        in_specs=[row(D_MODEL), _layer_of(g, layer), _layer_of(w, glayer), _layer_of(wg, glayer),
                  _layer_of(bg, glayer), _mem_kv_spec(mem_kv, layer, t, tm, batch)],
        out_specs=[row(GLA_QW), row(GLA_QW), row(GLA_VW), row(GLA_VW), row(GLA_QW), row(MEM_W)],
        out_shape=[jax.ShapeDtypeStruct((t, GLA_QW), BF16), jax.ShapeDtypeStruct((t, GLA_QW), BF16),
                   jax.ShapeDtypeStruct((t, GLA_VW), BF16), jax.ShapeDtypeStruct((t, GLA_VW), BF16),
                   jax.ShapeDtypeStruct((t, GLA_QW), F32), jax.ShapeDtypeStruct((t, MEM_W), BF16)],
        compiler_params=_cparams("parallel"),
        name="gla_in_proj",
    )(h, g, w, wg, bg, mem_kv)


def _gla_mix_body(q_ref, k_ref, la_ref, v_ref, go_ref, on_ref, o_ref, st_ref, inter_ref, b_scr, k_scr, v_scr, *,
                  cn, batch):
    @pl.when(pl.program_id(0) == 0)
    def _():
        st_ref[...] = jnp.zeros_like(st_ref)

    row = lax.broadcasted_iota(jnp.int32, (cn, cn), 0)
    col = lax.broadcasted_iota(jnp.int32, (cn, cn), 1)
    causal = row >= col
    tril = jnp.where(causal, 1.0, 0.0).astype(BF16)
    scale = GLA_DK ** -0.5
    chains = [(bi, hd) for bi in range(batch) for hd in range(GLA_HEADS)]
    sk = lambda hd: slice(hd * GLA_DKP, (hd + 1) * GLA_DKP)
    sv = lambda hd: slice(hd * GLA_DVP, (hd + 1) * GLA_DVP)
    dg = lambda x, y, dims: lax.dot_general(x, y, dims, preferred_element_type=F32)

    def cum_decay(bi, hd):
        la = la_ref[bi, :, sk(hd)]
        la1 = la.astype(BF16)
        r1 = la - la1.astype(F32)
        la2 = r1.astype(BF16)
        la3 = (r1 - la2.astype(F32)).astype(BF16)
        return _dot(tril, la1) + _dot(tril, la2) + _dot(tril, la3)

    def finish(i, o):
        bi, hd = chains[i]
        ms = jnp.sum(o * o, axis=-1, keepdims=True) * (1.0 / GLA_DV)
        y = o * lax.rsqrt(ms + EPS) * on_ref[...]
        g = go_ref[bi, :, sv(hd)].astype(F32)
        o_ref[bi, :, sv(hd)] = (y * (g * _sigmoid(g))).astype(BF16)

    bs = [cum_decay(bi, hd) for bi, hd in chains]
    scaled = []
    for (bi, hd), b in zip(chains, bs):
        b_last = b[cn - 1:cn, :]
        q = q_ref[bi, :, sk(hd)].astype(F32)
        k = k_ref[bi, :, sk(hd)].astype(F32)
        scaled.append(((q * jnp.exp(b) * scale).astype(BF16),
                       (k * jnp.exp(jnp.minimum(-b, GLA_SAFE_DECAY))).astype(BF16),
                       (k * jnp.exp(b_last - b)).astype(BF16), jnp.exp(b_last)))
    intra = [jnp.where(causal, dg(qs, ks, NT), 0.0).astype(BF16) for qs, ks, _, _ in scaled]
    outs = []
    for i, (bi, hd) in enumerate(chains):
        qs, _, ko, a_last = scaled[i]
        v = v_ref[bi, :, sv(hd)]
        st = st_ref[i]
        inter = dg(qs, st.astype(BF16), NT)
        inter_ref[i] = inter
        outs.append(_dot(intra[i], v) + inter)
        st_ref[i] = st * a_last + dg(v, ko, TN)
    for i, o in enumerate(outs):
        finish(i, o)

    b_min = bs[0][cn - 1:cn, :]
    for b in bs[1:]:
        b_min = jnp.minimum(b_min, b[cn - 1:cn, :])

    @pl.when(jnp.min(b_min) < -GLA_SAFE_DECAY)
    def _():
        rows = lax.broadcasted_iota(jnp.int32, (cn, 1), 0)
        for i, (bi, hd) in enumerate(chains):
            b = cum_decay(bi, hd)
            qf = q_ref[bi, :, sk(hd)].astype(F32) * scale
            b_scr[...] = b
            k_scr[...] = k_ref[bi, :, sk(hd)].astype(F32)
            v_scr[...] = v_ref[bi, :, sv(hd)].astype(F32)

            def add_key(j, acc):
                decay = jnp.exp(jnp.minimum(b - b_scr[pl.ds(j, 1), :], 0.0))
                a_col = jnp.sum(qf * k_scr[pl.ds(j, 1), :] * decay, axis=-1, keepdims=True)
                return acc + jnp.where(rows >= j, a_col, 0.0) * v_scr[pl.ds(j, 1), :]
            finish(i, lax.fori_loop(0, cn, add_key, inter_ref[i]))


def _gla_mixer(q, k, la, v, go, on, glayer, batch):
    t = q.shape[0]
    s = t // batch
    cn = GLA_CHUNK
    seq = lambda a: a.reshape(batch, s, a.shape[1])
    blk = lambda n: pl.BlockSpec((batch, cn, n), lambda c: (0, c, 0))
    out = pl.pallas_call(
        functools.partial(_gla_mix_body, cn=cn, batch=batch),
        grid=(s // cn,),
        in_specs=[blk(GLA_QW), blk(GLA_QW), blk(GLA_QW), blk(GLA_VW), blk(GLA_VW),
                  _layer_of(on, glayer)],
        out_specs=blk(GLA_VW),
        out_shape=jax.ShapeDtypeStruct((batch, s, GLA_VW), BF16),
        scratch_shapes=[pltpu.VMEM((batch * GLA_HEADS, GLA_DVP, GLA_DKP), F32),
                        pltpu.VMEM((batch * GLA_HEADS, cn, GLA_DVP), F32), pltpu.VMEM((cn, GLA_DKP), F32),
                        pltpu.VMEM((cn, GLA_DKP), F32), pltpu.VMEM((cn, GLA_DVP), F32)],
        compiler_params=_cparams("arbitrary"),
        name="gla_mixer",
    )(seq(q), seq(k), seq(la), seq(v), seq(go), on)
    return out.reshape(t, GLA_VW)


def _mem_kv_body(m_ref, g_ref, w_ref, o_ref):
    xn = _rms(m_ref[...], g_ref[...]).astype(BF16)
    o_ref[...] = _dot(xn, w_ref[...]).astype(BF16)


def _mem_kv_proj(mem2, g, w, batch):
    nl = w.shape[0]
    m = mem2.shape[0] // batch
    n = w.shape[2]
    return pl.pallas_call(
        _mem_kv_body,
        grid=(nl, batch),
        in_specs=[pl.BlockSpec((m, D_MODEL), lambda l, b: (b, 0)),
                  pl.BlockSpec((None, 1, D_MODEL), lambda l, b: (l, 0, 0)),
                  pl.BlockSpec((None, D_MODEL, n), lambda l, b: (l, 0, 0))],
        out_specs=pl.BlockSpec((None, m, n), lambda l, b: (l, b, 0)),
        out_shape=jax.ShapeDtypeStruct((nl, mem2.shape[0], n), BF16),
        compiler_params=_cparams("arbitrary", "arbitrary"),
        name="mem_kv_proj",
    )(mem2, g, w)


BF16_ROWS = 2 * SUBLANE


def _mix_ffn_body(h_ref, hp_ref, a_ref, ap_ref, m_ref, mp_ref, wa_ref, wm_ref, g_ref, wup_ref, cw_ref, cb_ref,
                  wdn_ref, gf_ref, o_ref, act_ref, *, tm, tf, last):
    g = g_ref[...]
    nr = tm // SUBLANE
    interleave = lambda x: x.reshape(SUBLANE, nr, x.shape[-1]).swapaxes(0, 1).reshape(tm, x.shape[-1])
    deinterleave = lambda x: x.reshape(nr, SUBLANE, x.shape[-1]).swapaxes(0, 1).reshape(tm, x.shape[-1])
    h = interleave(h_ref[...] + _dot(a_ref[...], wa_ref[...]) + _dot(m_ref[...], wm_ref[...]))
    x = _rms(h, g).astype(BF16)
    keep = jnp.where(pl.program_id(1) > 0, 1.0, 0.0)
    h_prev = (hp_ref[...] + _dot(ap_ref[...], wa_ref[...]) + _dot(mp_ref[...], wm_ref[...]))[BF16_ROWS - SUBLANE:]
    x_prev = (_rms(h_prev, g) * keep).astype(BF16)
    first = lax.broadcasted_iota(jnp.int32, (SUBLANE, tf), 0) == 0
    nchunk = FFN_DIM // tf

    def up(j):
        cols = [pl.ds(off + j * tf, tf) for off in (0, FFN_DIM)]
        return tuple((_dot(x, wup_ref[:, c]), _dot(x_prev, wup_ref[:, c])) for c in cols)

    def conv(u, u_prev, off):
        w = cw_ref[:, off:off + tf]
        wrap = lambda r, k: jnp.where(first, u_prev[SUBLANE - k:SUBLANE - k + 1, :],
                                      pltpu.roll(u[r * SUBLANE:(r + 1) * SUBLANE, :], 1, 0))
        back1 = jnp.concatenate([wrap(nr - 1, 1), u[0:tm - SUBLANE, :]], axis=0)
        back2 = jnp.concatenate([wrap(nr - 2, 2), wrap(nr - 1, 1), u[0:tm - 2 * SUBLANE, :]], axis=0)
        return cb_ref[:, off:off + tf] + w[0:1, :] * back2 + w[1:2, :] * back1 + w[2:3, :] * u

    u_next = up(0)
    for j in range(nchunk):
        (ua, ua_prev), (ub, ub_prev) = u_next
        if j + 1 < nchunk:
            u_next = up(j + 1)
        a = conv(ua, ua_prev, j * tf)
        b = conv(ub, ub_prev, FFN_DIM + j * tf)
        act_ref[:, j * tf:(j + 1) * tf] = (a * _sigmoid(a) * b).astype(BF16)
    out = h + _dot(act_ref[...], wdn_ref[...])
    o_ref[...] = deinterleave(_rms(out, gf_ref[...]) if last else out)


def _mix_ffn(h, main, mo, wa, klayer, wm, layer, g, wup, cw, cb, wdn, gf, last, batch):
    t = h.shape[0]
    tm = FFN_ROWS
    nt = t // batch // tm
    hb = tm // BF16_ROWS
    cur = lambda n: pl.BlockSpec((tm, n), lambda b, i: (b * nt + i, 0))
    prev = lambda n: pl.BlockSpec((BF16_ROWS, n), lambda b, i: (jnp.maximum((b * nt + i) * hb - 1, 0), 0))
    return pl.pallas_call(
        functools.partial(_mix_ffn_body, tm=tm, tf=FFN_TILE, last=last),
        grid=(batch, nt),
        in_specs=[cur(D_MODEL), prev(D_MODEL), cur(main.shape[1]), prev(main.shape[1]),
                  cur(mo.shape[1]), prev(mo.shape[1]), _layer_of(wa, klayer), _layer_of(wm, layer), _layer_of(g, layer),
                  _layer_of(wup, layer), _layer_of(cw, layer), _layer_of(cb, layer), _layer_of(wdn, layer),
                  _resident(gf)],
        out_specs=cur(D_MODEL),
        out_shape=jax.ShapeDtypeStruct((t, D_MODEL), F32),
        scratch_shapes=[pltpu.VMEM((tm, FFN_DIM), BF16)],
        compiler_params=_cparams("parallel", "parallel"),
        name="mix_ffn",
    )(h, h, main, main, mo, mo, wa, wm, g, wup, cw, cb, wdn, gf)


KV_NAT = NSA_GROUPS * NSA_HEAD_DIM


def _kv_proj_body(h_ref, g_ref, w_ref, cx_ref, ks_ref, kw_ref, vs_ref, vw_ref, *, tm):
    xn = _rms(h_ref[...], g_ref[...]).astype(BF16)
    for j in range(2):
        y = _dot(xn, w_ref[:, j * KV_NAT:(j + 1) * KV_NAT])
        cx_ref[j] = y.reshape(tm // CMP_STRIDE, CMP_STRIDE, KV_NAT).swapaxes(0, 1).astype(BF16)
    key = pl.program_id(1) * tm + lax.broadcasted_iota(jnp.int32, (tm, LANE), 0)
    lane = lax.broadcasted_iota(jnp.int32, (tm, LANE), 1)
    onehot = jnp.where(lax.shift_right_logical(key, int(math.log2(SEL_BLOCK))) == lane - NSA_HEAD_DIM, 1.0, 0.0)
    low = lane < NSA_HEAD_DIM
    ones = jnp.ones((V_ROWS - NSA_HEAD_DIM, tm), BF16)
    slots = lambda n: _dot(xn, w_ref[:, 2 * KV_NAT + n * MXU_N:2 * KV_NAT + (n + 1) * MXU_N])
    for gp in range(NSA_GROUPS // 2):
        kk = slots(gp)
        vv = slots(NSA_GROUPS // 2 + gp)
        for i in range(2):
            kslot = kk[:, i * LANE:(i + 1) * LANE]
            ks_ref[2 * gp + i] = jnp.where(low, kslot, onehot).astype(BF16)
            kw_ref[2 * gp + i] = jnp.where(low, pltpu.roll(kslot, NSA_HEAD_DIM, 1), 0.0).astype(BF16)
            vt = vv[:, i * LANE:(i + 1) * LANE].T.astype(BF16)
            vs_ref[2 * gp + i] = jnp.concatenate([vt[0:NSA_HEAD_DIM], ones], axis=0)
            vw_ref[2 * gp + i] = jnp.concatenate([vt[NSA_HEAD_DIM:], ones], axis=0)


def _kv_proj(h, g, w, batch):
    t = h.shape[0]
    s = t // batch
    tm = ROW_TILE
    nt = s // tm
    gr = NSA_GROUPS
    return pl.pallas_call(
        functools.partial(_kv_proj_body, tm=tm),
        grid=(batch, nt),
        in_specs=[pl.BlockSpec((tm, D_MODEL), lambda b, i: (b * nt + i, 0)),
                  pl.BlockSpec(g.shape, lambda b, i: (0, 0)),
                  pl.BlockSpec(w.shape, lambda b, i: (0, 0))],
        out_specs=[pl.BlockSpec((2, CMP_STRIDE, tm // CMP_STRIDE, KV_NAT), lambda b, i: (0, 0, b * nt + i, 0)),
                   pl.BlockSpec((None, gr, tm, LANE), lambda b, i: (b, 0, i, 0)),
                   pl.BlockSpec((None, gr, tm, LANE), lambda b, i: (b, 0, i, 0)),
                   pl.BlockSpec((None, gr, V_ROWS, tm), lambda b, i: (b, 0, 0, i)),
                   pl.BlockSpec((None, gr, V_ROWS, tm), lambda b, i: (b, 0, 0, i))],
        out_shape=[jax.ShapeDtypeStruct((2, CMP_STRIDE, t // CMP_STRIDE, KV_NAT), BF16),
                   jax.ShapeDtypeStruct((batch, gr, s, LANE), BF16),
                   jax.ShapeDtypeStruct((batch, gr, s, LANE), BF16),
                   jax.ShapeDtypeStruct((batch, gr, V_ROWS, s), BF16),
                   jax.ShapeDtypeStruct((batch, gr, V_ROWS, s), BF16)],
        compiler_params=_cparams("parallel", "parallel"),
        name="nsa_kv_proj",
    )(h, g, w)


def _compress_body(x_ref, wt_ref, wb_ref, pos_ref, w1_ref, b1_ref, w2_ref, b2_ref, on_ref, ot_ref, *, ncp):
    top = _dot(x_ref[0], wt_ref[0])
    bot = _dot(x_ref[0], wb_ref[0])
    for l in range(1, CMP_STRIDE):
        top = top + _dot(x_ref[l], wt_ref[l])
        bot = bot + _dot(x_ref[l], wb_ref[l])
    posb = _dot(pos_ref[...], w1_ref[...])[0:1, :] + b1_ref[...]
    for gi in range(NSA_GROUPS):
        sl = slice(gi * CMP_HIDDEN, (gi + 1) * CMP_HIDDEN)
        hid = top[:, sl] + pltpu.roll(bot[:, sl], ncp - 1, 0) + posb
        hid = (hid * _sigmoid(hid)).astype(BF16)
        out = _dot(hid, w2_ref[...]) + b2_ref[...]
        on_ref[gi] = out.astype(BF16)
        ot_ref[gi] = out.T.astype(BF16)


def _compress(x16, wt, wb, pos, w1, b1, w2, b2, batch):
    ncp = x16.shape[2] // batch
    gr = NSA_GROUPS
    per_j = lambda a: pl.BlockSpec((None,) + a.shape[1:], lambda j, b: (j,) + (0,) * (a.ndim - 1))
    return pl.pallas_call(
        functools.partial(_compress_body, ncp=ncp),
        grid=(2, batch),
        in_specs=[pl.BlockSpec((None, CMP_STRIDE, ncp, KV_NAT), lambda j, b: (j, 0, b, 0)),
                  per_j(wt), per_j(wb), per_j(pos), per_j(w1), per_j(b1), per_j(w2), per_j(b2)],
        out_specs=[pl.BlockSpec((None, None, gr, ncp, LANE), lambda j, b: (j, b, 0, 0, 0)),
                   pl.BlockSpec((None, None, gr, LANE, ncp), lambda j, b: (j, b, 0, 0, 0))],
        out_shape=[jax.ShapeDtypeStruct((2, batch, gr, ncp, LANE), BF16),
                   jax.ShapeDtypeStruct((2, batch, gr, LANE, ncp), BF16)],
        compiler_params=_cparams("arbitrary", "arbitrary"),
        name="nsa_compress",
    )(x16, wt, wb, pos, w1, b1, w2, b2)


NSA_QW = NSA_HEADS * NSA_HEAD_DIM
G3 = NSA_REP * TQ


def _nsa_in_body(h_ref, g_ref, w_ref, kv_ref, qt_ref, gt_ref, mo_ref):
    xn = _rms(h_ref[...], g_ref[...]).astype(BF16)
    for j in range(NSA_QW // MXU_N):
        y = _dot(xn, w_ref[:, j * MXU_N:(j + 1) * MXU_N]) * (NSA_HEAD_DIM ** -0.5 * LOG2E)
        qt_ref[j * MXU_N:(j + 1) * MXU_N, :] = y.T.astype(BF16)
    tail = _dot(xn, w_ref[:, NSA_QW:])
    gt_ref[...] = _sigmoid(tail[:, 0:LANE]).T
    mo_ref[...] = _mem_attention(_spread_heads(tail, LANE, MEM_HEADS, MEM_HEAD_DIM, MEM_DP).astype(BF16), kv_ref)


def _nsa_in_proj(h, layer, nlayer, g, w, mem_kv, batch):
    t = h.shape[0]
    tm = ROW_TILE
    row = lambda n: pl.BlockSpec((tm, n), lambda i: (i, 0))
    nt = t // batch // tm
    col = lambda n: pl.BlockSpec((None, n, tm), lambda i: (i // nt, 0, i % nt))
    return pl.pallas_call(
        _nsa_in_body,
        grid=(t // tm,),
        in_specs=[row(D_MODEL), _layer_of(g, layer), _layer_of(w, nlayer), _mem_kv_spec(mem_kv, layer, t, tm, batch)],
        out_specs=[col(NSA_QW), col(LANE), row(MEM_W)],
        out_shape=[jax.ShapeDtypeStruct((batch, NSA_QW, t // batch), BF16),
                   jax.ShapeDtypeStruct((batch, LANE, t // batch), F32),
                   jax.ShapeDtypeStruct((t, MEM_W), BF16)],
        compiler_params=_cparams("parallel"),
        name="nsa_in_proj",
    )(h, g, w, mem_kv)


def _nsa_attn_body(qt_ref, gt_ref, kc_ref, vct_ref, ks_ref, kw_ref, vs_ref, vw_ref, ovt_ref, tz_ref, cb_ref,
                   o_ref, sc_ref, qa_ref, acc_ref, ot_ref, *s_slots, nsb, ncp, n_sel, nseq):
    n = pl.program_id(1)
    t0 = n * TQ
    dh = NSA_HEAD_DIM
    cstart = pl.multiple_of(ncp - n * (TQ // CMP_STRIDE), SUBLANE)
    sees_any = t0 + (lax.broadcasted_iota(jnp.int32, (1, G3), 1) & (TQ - 1)) >= CMP_BLOCK - 1
    jj = lax.broadcasted_iota(jnp.int32, (nsb, TQ), 0)
    cur = lax.shift_right_logical(t0 + lax.broadcasted_iota(jnp.int32, (nsb, TQ), 1), int(math.log2(SEL_BLOCK)))
    forced = (jj == 0) | (jj == cur) | (jj == cur - 1)
    zpad = jnp.zeros((dh, G3), BF16)
    nwt = WINDOW // TQ
    units = [(bi, gi) for bi in range(nseq) for gi in range(NSA_GROUPS)]

    def scores(k_ref, u, first, count, tz_index, qa):
        bi, gi = units[u]
        koff = pl.multiple_of(first * TQ, TQ)
        s = _dot(k_ref[bi, gi, pl.ds(koff, count * TQ), :], qa)
        return [s[i * TQ:(i + 1) * TQ] + tz_ref[gi, tz_index(n - (first + i))] for i in range(count)]

    def col_max(parts):
        mx = parts[0]
        for x in parts[1:]:
            mx = jnp.maximum(mx, x)
        return jnp.max(mx, axis=0, keepdims=True)

    def probs(parts, m):
        return jnp.concatenate([jnp.exp2(x - m).astype(BF16) for x in parts], axis=0)

    def values(v_ref, u, first, count, p):
        bi, gi = units[u]
        koff = pl.multiple_of(first * TQ, TQ)
        return _dot(v_ref[bi, gi, :, pl.ds(koff, count * TQ)], p)

    sel_index = lambda d: jnp.where(d < 0, 2, jnp.minimum(d, 2))
    win_index = lambda d: jnp.where(d < 0, 4, jnp.where(d == nwt, 3, jnp.minimum(d, 2)))

    groups = range(len(units))
    gate = lambda bi, h, c: gt_ref[bi, pl.ds(h * NSA_BRANCHES + c, 1), :]
    head_lanes = lambda r: slice(r * TQ, (r + 1) * TQ)
    head_rows = lambda bi, h: (bi, slice(h * dh, (h + 1) * dh))

    q3s = [jnp.concatenate([qt_ref[head_rows(bi, gi * NSA_REP + r)] for r in range(NSA_REP)], axis=1)
           for bi, gi in units]
    cs = [_dot(kc_ref[bi, gi], jnp.concatenate([q3s[u], zpad], axis=0)) + cb_ref[gi, pl.ds(cstart, ncp), :]
          for u, (bi, gi) in enumerate(units)]
    cps = [jnp.exp2(s - jnp.max(s, axis=0, keepdims=True)) for s in cs]
    cps = [p * jnp.where(sees_any, 1.0 / jnp.sum(p, axis=0, keepdims=True), 0.0) for p in cps]
    ocs = [_dot(vct_ref[bi, gi], cps[u].astype(BF16)) for u, (bi, gi) in enumerate(units)]
    scs = []
    for gi in groups:
        psum = cps[gi][:, 0:TQ]
        for r in range(1, NSA_REP):
            psum = psum + cps[gi][:, head_lanes(r)]
        p1 = psum.astype(BF16)
        p2 = (psum - p1.astype(F32)).astype(BF16)
        imp = _dot(ovt_ref[...], p1) + _dot(ovt_ref[...], p2)
        score = jnp.where(forced, 1e4, jnp.where(jj <= cur, imp[0:nsb], -1.0))
        scs.append(jnp.where(score < 0.0, -1, lax.bitcast_convert_type(score, jnp.int32)))
    for u, (bi, gi) in enumerate(units):
        sc_ref[u] = scs[u]
        for r in range(NSA_REP):
            h = gi * NSA_REP + r
            ot_ref[head_rows(bi, h)] = gate(bi, h, 0) * ocs[u][0:dh, head_lanes(r)]

    scs1 = [k + 1 for k in scs]

    def rank_step(i4, cnts):
        cnts = list(cnts)
        for u in range(RANK_UNROLL):
            i = i4 * RANK_UNROLL + u
            lower = i < jj
            for gi in groups:
                rowk = sc_ref[gi, pl.ds(i, 1), :]
                before = rowk >= jnp.where(lower, scs[gi], scs1[gi])
                cnts[gi] = cnts[gi] + jnp.where(before, 1, 0)
        return tuple(cnts)
    rank_trips = jnp.minimum((2 * n + 2 + RANK_UNROLL - 1) // RANK_UNROLL, nsb // RANK_UNROLL)
    cnts = lax.fori_loop(0, rank_trips, rank_step, tuple(jnp.zeros((nsb, TQ), jnp.int32) for _ in groups))
    for gi in groups:
        selneg = jnp.where((cnts[gi] < n_sel) & (jj <= cur), 0.0, NEG).astype(BF16)
        if nsb < SEL_BLOCK:
            selneg = jnp.concatenate([selneg, jnp.zeros((SEL_BLOCK - nsb, TQ), BF16)], axis=0)
        qa_ref[gi] = jnp.concatenate([q3s[gi], jnp.concatenate([selneg] * NSA_REP, axis=1)], axis=0)

    acc_ref[...] = jnp.zeros_like(acc_ref)
    half = len(units) // 2
    s_refs = (s_slots[:half], s_slots[half:])

    def score_half(it, hb):
        plist = [scores(ks_ref, hb * half + u, it * SEL_TILES, SEL_TILES, sel_index, qa_ref[hb * half + u])
                 for u in range(half)]
        for u, parts in enumerate(plist):
            for i, x in enumerate(parts):
                s_refs[hb][u][i * TQ:(i + 1) * TQ, :] = x
        return [col_max(parts) for parts in plist]

    def finish_half(it, hb, ms, bms):
        m2s = [jnp.maximum(ms[u], bms[u]) for u in range(half)]
        ps = [probs([s_refs[hb][u][i * TQ:(i + 1) * TQ, :] for i in range(SEL_TILES)], m2s[u]) for u in range(half)]
        vals = [values(vs_ref, hb * half + u, it * SEL_TILES, SEL_TILES, ps[u]) for u in range(half)]
        for u in range(half):
            gi = hb * half + u
            acc_ref[gi] = jnp.exp2(ms[u] - m2s[u]) * acc_ref[gi] + vals[u]
        return m2s

    def sel_trip(it, carry, score_next):
        ms0, ms1, bm0 = carry
        bm1 = score_half(it, 1)
        ms0 = finish_half(it, 0, ms0, bm0)
        if score_next:
            bm0 = score_half(it + 1, 0)
        ms1 = finish_half(it, 1, ms1, bm1)
        return ms0, ms1, bm0

    neg = [jnp.full((1, G3), NEG, F32) for _ in range(half)]
    carry = (neg, neg, score_half(0, 0))

    wfirst = jnp.maximum(n - nwt, 0)
    wparts = [scores(kw_ref, gi, wfirst, nwt + 1, win_index, qa_ref[gi]) for gi in groups]
    wps = [probs(parts, col_max(parts)) for parts in wparts]
    ows = [values(vw_ref, gi, wfirst, nwt + 1, wps[gi]) for gi in groups]
    for u, (bi, gi) in enumerate(units):
        o_w = ows[u][0:dh, :] * (1.0 / ows[u][dh:dh + 1, :])
        for r in range(NSA_REP):
            h = gi * NSA_REP + r
            ot_ref[head_rows(bi, h)] += gate(bi, h, 2) * o_w[:, head_lanes(r)]

    carry = lax.fori_loop(0, n // SEL_TILES, lambda it, c: sel_trip(it, c, True), carry)
    sel_trip(n // SEL_TILES, carry, False)

    for u, (bi, gi) in enumerate(units):
        o_s = acc_ref[u, 0:dh, :] * (1.0 / acc_ref[u, dh:dh + 1, :])
        for r in range(NSA_REP):
            h = gi * NSA_REP + r
            ot_ref[head_rows(bi, h)] += gate(bi, h, 1) * o_s[:, head_lanes(r)]

    for bi in range(nseq):
        for j in range(NSA_QW // LANE):
            o_ref[bi, :, j * LANE:(j + 1) * LANE] = ot_ref[bi, j * LANE:(j + 1) * LANE, :].T.astype(BF16)


def _nsa_attn(qt, gt, kc, vct, ksel, kwin, vsel, vwin, ovt, tz, cb):
    batch, _, s = qt.shape
    nseq = NSA_SEQS if batch % NSA_SEQS == 0 else 1
    nt = s // TQ
    nsb = s // SEL_BLOCK
    ncp = kc.shape[-2]
    nu = nseq * NSA_GROUPS
    per_b = lambda a: pl.BlockSpec((nseq,) + a.shape[1:], lambda b, i: (b,) + (0,) * (a.ndim - 1),
                                   pipeline_mode=pl.Buffered(1))
    out = pl.pallas_call(
        functools.partial(_nsa_attn_body, nsb=nsb, ncp=ncp, n_sel=min(SEL_TOPK, nsb), nseq=nseq),
        grid=(batch // nseq, nt),
        in_specs=[pl.BlockSpec((nseq, NSA_QW, TQ), lambda b, i: (b, 0, i)),
                  pl.BlockSpec((nseq, LANE, TQ), lambda b, i: (b, 0, i)),
                  per_b(kc), per_b(vct), per_b(ksel), per_b(kwin), per_b(vsel), per_b(vwin),
                  _resident(ovt), _resident(tz), _resident(cb)],
        out_specs=pl.BlockSpec((nseq, TQ, NSA_QW), lambda b, i: (b, i, 0)),
        out_shape=jax.ShapeDtypeStruct((batch, s, NSA_QW), BF16),
        scratch_shapes=[pltpu.VMEM((nu, nsb, TQ), jnp.int32), pltpu.VMEM((nu, LANE, G3), BF16),
                        pltpu.VMEM((nu, V_ROWS, G3), F32), pltpu.VMEM((nseq, NSA_QW, TQ), F32),
                        ] + [pltpu.VMEM((SEL_TILES * TQ, G3), F32)] * nu,
        compiler_params=_cparams("parallel", "arbitrary"),
        name="nsa_attn",
    )(qt, gt, kc, vct, ksel, kwin, vsel, vwin, ovt, tz, cb)
    return out.reshape(batch * s, NSA_QW)


def _rel_bucket_np(dist):
    dist = np.maximum(dist, 0)
    max_exact = REL_BUCKETS // 2
    ratio = np.log(np.maximum(dist, 1).astype(np.float32) / np.float32(max_exact)) / np.float32(
        math.log(REL_MAX_DIST / max_exact))
    large = max_exact + (ratio * np.float32(REL_BUCKETS - max_exact)).astype(np.int32)
    large = np.minimum(large, REL_BUCKETS - 1)
    return np.where(dist < max_exact, dist, large).astype(np.int32)


def _group_lanes(a):
    hh, r, c = a.shape
    return a.reshape(NSA_GROUPS, NSA_REP, r, c).transpose(0, 2, 1, 3).reshape(NSA_GROUPS, r, NSA_REP * c)


def _bias_tables(rel_bias, ncp):
    k = np.arange(TQ)[:, None]
    q = np.arange(TQ)[None, :]
    tbl = rel_bias.astype(F32)

    def lookup(idx):
        onehot = (jnp.asarray(idx.reshape(1, -1)) == jnp.arange(REL_BUCKETS)[:, None]).astype(F32)
        out = jnp.dot(tbl.T, onehot, precision=lax.Precision.HIGHEST)
        return out.reshape((NSA_HEADS,) + idx.shape)

    far = jnp.broadcast_to(tbl[REL_BUCKETS - 1][:, None, None], (NSA_HEADS, TQ, TQ))
    t0 = jnp.where(jnp.asarray(k <= q)[None], lookup(_rel_bucket_np(q - k)), NEG)
    t1 = lookup(_rel_bucket_np(TQ + q - k))
    t3 = jnp.where(jnp.asarray(k > q)[None], far, NEG)
    tz = jnp.stack([_group_lanes(x) for x in (t0, t1, far, t3, jnp.full_like(far, NEG))], axis=1)
    m = ncp - np.arange(2 * ncp)[:, None]
    d = CMP_STRIDE * m + q - (CMP_BLOCK - 1)
    idx = np.where((d >= 0) & (d < REL_MAX_DIST), _rel_bucket_np(d), REL_BUCKETS - 1)
    cb = _group_lanes(jnp.where(jnp.asarray(d >= 0)[None], lookup(idx), NEG))
    return tz * LOG2E, cb * LOG2E


def _overlap_table(s, ncp):
    nsb = s // SEL_BLOCK
    nc = (s - CMP_BLOCK) // CMP_STRIDE + 1
    cs = np.arange(ncp) * CMP_STRIDE
    ce = cs + CMP_BLOCK - 1
    ss = np.arange(SEL_BLOCK) * SEL_BLOCK
    ov = (cs[None, :] < ss[:, None] + SEL_BLOCK) & (ce[None, :] >= ss[:, None])
    ov &= (np.arange(ncp) < nc)[None, :] & (np.arange(SEL_BLOCK) < nsb)[:, None]
    return jnp.asarray(ov, dtype=BF16)


def kernel(x, mem, norm_mix, norm_mem, w_mem_kv, w_out, norm_ffn, w_up, conv_w, conv_b, w_down,
           gla_w_in, gla_w_gate_up, gla_b_gate, gla_out_norm, nsa_w_in, kv_norm, w_kv_shared,
           cmp_pos, cmp_w1, cmp_b1, cmp_w2, cmp_b2, rel_bias, final_norm):
    batch, seq = x.shape[0], x.shape[1]
    t = batch * seq
    h = x.reshape(t, D_MODEL)
    row = lambda v: v.reshape(1, -1).astype(F32)

    wk, wv = w_mem_kv[..., :MEM_W], w_mem_kv[..., MEM_W:]
    w_mkv = jnp.concatenate([_pad_heads(wk, MEM_HEADS, MEM_HEAD_DIM, MEM_DP),
                             _pad_heads(wv, MEM_HEADS, MEM_HEAD_DIM, MEM_DP)], axis=-1).astype(BF16)
    mem_kv_all = _mem_kv_proj(mem.reshape(-1, D_MODEL), norm_mem.reshape(DEPTH, 1, D_MODEL), w_mkv, batch)

    ffn_params = (norm_ffn.reshape(DEPTH, 1, D_MODEL).astype(F32), w_up.astype(BF16), conv_w.astype(F32),
                  conv_b.reshape(DEPTH, 1, 2 * FFN_DIM).astype(F32), w_down.astype(BF16))
    norm_mix3 = norm_mix.reshape(DEPTH, 1, D_MODEL).astype(F32)
    w_o_mem = w_out[:, MAIN_W:].astype(BF16)

    c0 = GLA_HEADS * GLA_DK
    c1 = 2 * c0
    c2 = c1 + GLA_HEADS * GLA_DV
    c3 = c2 + GLA_HEADS * GLA_DV
    c4 = c3 + GLA_RANK
    gla_w = jnp.concatenate([gla_w_in[..., :c3], gla_w_in[..., c4:],
                             _pad_heads(gla_w_in[..., c3:c4], 1, GLA_RANK, LANE)], axis=-1).astype(BF16)
    gla_wg = jnp.pad(_pad_heads(gla_w_gate_up, GLA_HEADS, GLA_DK, GLA_DKP),
                     ((0, 0), (0, LANE - GLA_RANK), (0, 0))).astype(BF16)
    gla_bg = _pad_heads(gla_b_gate, GLA_HEADS, GLA_DK, GLA_DKP).reshape(N_A_LAYERS, 1, GLA_QW).astype(F32)
    gla_on = jnp.pad(gla_out_norm, ((0, 0), (0, GLA_DVP - GLA_DV))).reshape(N_A_LAYERS, 1, GLA_DVP).astype(F32)
    gla_wo = _pad_head_rows(w_out[:N_A_LAYERS, :MAIN_W], GLA_HEADS, GLA_DV, GLA_DVP).astype(BF16)

    n0 = NSA_HEADS * NSA_HEAD_DIM
    n1 = n0 + NSA_HEADS * NSA_BRANCHES
    nsa_w = jnp.concatenate([nsa_w_in[..., :n0], _pad_heads(nsa_w_in[..., n0:n1], 1, NSA_HEADS * NSA_BRANCHES, LANE),
                             nsa_w_in[..., n1:]], axis=-1).astype(BF16)
    nsa_wo = w_out[N_A_LAYERS:, :MAIN_W].astype(BF16)

    shared = None
    for i in range(DEPTH):
        if i < N_A_LAYERS:
            q, k, v, go, la, mo = _gla_in_proj(h, i, i, norm_mix3, gla_w, gla_wg, gla_bg, mem_kv_all, batch)
            main = _gla_mixer(q, k, la, v, go, gla_on, i, batch)
            w_o_main, klayer = gla_wo, i
        else:
            if shared is None:
                ncp = seq // CMP_STRIDE
                gr, dh = NSA_GROUPS, NSA_HEAD_DIM
                wkv = w_kv_shared.reshape(D_MODEL, 6, gr, dh)
                pair = lambda a, b: jnp.concatenate([wkv[:, a], wkv[:, b]], axis=-1).reshape(D_MODEL, gr * 2 * dh)
                w_kv = jnp.concatenate([wkv[:, 0].reshape(D_MODEL, KV_NAT), wkv[:, 1].reshape(D_MODEL, KV_NAT),
                                        pair(2, 4), pair(3, 5)], axis=1).astype(BF16)
                x16, ksel, kwin, vsel, vwin = _kv_proj(h, row(kv_norm), w_kv, batch)
                w1 = cmp_w1.reshape(2, 2, CMP_STRIDE, dh, CMP_HIDDEN)
                eye = jnp.eye(gr, dtype=F32)
                w1x = jnp.einsum('jhldc,gk->jhlgdkc', w1, eye).reshape(2, 2, CMP_STRIDE, KV_NAT, gr * CMP_HIDDEN)
                w1x = w1x.astype(BF16)
                pos8 = jnp.broadcast_to(cmp_pos.reshape(2, 1, CMP_BLOCK * dh), (2, SUBLANE, CMP_BLOCK * dh)).astype(BF16)
                w2p = jnp.pad(cmp_w2, ((0, 0), (0, 0), (0, LANE - dh))).astype(BF16)
                b2p = jnp.pad(cmp_b2, ((0, 0), (0, LANE - dh))).reshape(2, 1, LANE).astype(F32)
                cnat, ctr = _compress(x16, w1x[:, 0], w1x[:, 1], pos8, cmp_w1.astype(BF16),
                                      cmp_b1.reshape(2, 1, CMP_HIDDEN).astype(F32), w2p, b2p, batch)
                tz, cb = _bias_tables(rel_bias, ncp)
                ov = _overlap_table(seq, ncp)
                shared = (cnat[0], ctr[1], ksel, kwin, vsel, vwin, ov, tz, cb)
            q, gates, mo = _nsa_in_proj(h, i, i - N_A_LAYERS, norm_mix3, nsa_w, mem_kv_all, batch)
            main = _nsa_attn(q, gates, *shared)
            w_o_main, klayer = nsa_wo, i - N_A_LAYERS
        h = _mix_ffn(h, main, mo, w_o_main, klayer, w_o_mem, i, *ffn_params, row(final_norm), i == DEPTH - 1, batch)
    return h.reshape(batch, seq, D_MODEL)
```

```python
import functools
import math

import numpy as np
import jax
import jax.numpy as jnp
from jax import lax
from jax.experimental import pallas as pl
from jax.experimental.pallas import tpu as pltpu

F32 = jnp.float32
BF16 = jnp.bfloat16

D_MODEL = 1024
DEPTH = 4
N_A_LAYERS = DEPTH // 2
MEM_HEADS = 4
MEM_HEAD_DIM = 64
MEM_W = MEM_HEADS * MEM_HEAD_DIM
MAIN_W = D_MODEL - MEM_W
GLA_HEADS = 4
GLA_DV = MAIN_W // GLA_HEADS
GLA_DK = GLA_DV // 2
GLA_RANK = 16
GLA_GATE_NORM = 16.0
NSA_HEADS = 12
NSA_GROUPS = 4
NSA_HEAD_DIM = MAIN_W // NSA_HEADS
NSA_REP = NSA_HEADS // NSA_GROUPS
NSA_BRANCHES = 3
CMP_BLOCK = 32
CMP_STRIDE = 16
CMP_HIDDEN = 128
SEL_BLOCK = 64
SEL_TOPK = 16
WINDOW = 512
REL_BUCKETS = 32
REL_MAX_DIST = 128
FFN_DIM = 2816
CONV_WIDTH = 3
EPS = 1e-6

LANE = 128
MXU_N = 256
SUBLANE = 8
VMEM_LIMIT = 56 * 1024 * 1024
GLA_DKP = LANE
GLA_DVP = 2 * LANE
MEM_DP = LANE
NEG = -1e30
TQ = 128
ROW_TILE = 1024
GLA_CHUNK = 128
GLA_SAFE_DECAY = 80.0
FFN_TILE = 256
FFN_ROWS = 1024
SEL_TILES = 4
NSA_SEQS = 2
V_ROWS = NSA_HEAD_DIM + 2 * SUBLANE
RANK_UNROLL = 4
LOG2E = math.log2(math.e)

NT = (((1,), (1,)), ((), ()))
TN = (((0,), (0,)), ((), ()))


def _cparams(*sem):
    return pltpu.CompilerParams(dimension_semantics=sem, vmem_limit_bytes=VMEM_LIMIT)


def _rms(x, g):
    return x * lax.rsqrt(jnp.mean(x * x, axis=-1, keepdims=True) + EPS) * g


def _sigmoid(x):
    return 1.0 / (1.0 + jnp.exp(-x))


def _dot(a, b):
    return jnp.dot(a, b, preferred_element_type=F32)


def _resident(a):
    return pl.BlockSpec(a.shape, lambda *_: (0,) * a.ndim, pipeline_mode=pl.Buffered(1))


def _layer_of(a, layer):
    return pl.BlockSpec((None,) + a.shape[1:], lambda *_: (layer,) + (0,) * (a.ndim - 1),
                        pipeline_mode=pl.Buffered(1))


def _pad_heads(w, nh, d, dp):
    lead = w.shape[:-1]
    w = w.reshape(lead + (nh, d))
    w = jnp.pad(w, [(0, 0)] * len(lead) + [(0, 0), (0, dp - d)])
    return w.reshape(lead + (nh * dp,))


def _pad_head_rows(w, nh, d, dp):
    lead, n = w.shape[:-2], w.shape[-1]
    w = jnp.pad(w.reshape(lead + (nh, d, n)), [(0, 0)] * len(lead) + [(0, 0), (0, dp - d), (0, 0)])
    return w.reshape(lead + (nh * dp, n))


def _spread_heads(y, off, nh, d, dp):
    pad = jnp.zeros((y.shape[0], dp - d), F32)
    return jnp.concatenate([x for hd in range(nh) for x in (y[:, off + hd * d:off + (hd + 1) * d], pad)], axis=1)


def _mem_attention(q, kv_ref):
    outs = []
    for hd in range(MEM_HEADS):
        sl = slice(hd * MEM_DP, (hd + 1) * MEM_DP)
        sv = slice(MEM_QW + hd * MEM_DP, MEM_QW + (hd + 1) * MEM_DP)
        s = lax.dot_general(q[:, sl], kv_ref[:, sl], NT, preferred_element_type=F32) * MEM_HEAD_DIM ** -0.5
        p = jnp.exp(s - jnp.max(s, axis=-1, keepdims=True))
        l = jnp.sum(p, axis=-1, keepdims=True)
        outs.append((_dot(p.astype(BF16), kv_ref[:, sv]) / l)[:, 0:MEM_HEAD_DIM])
    return jnp.concatenate(outs, axis=1).astype(BF16)


GLA_QW = GLA_HEADS * GLA_DKP
GLA_VW = GLA_HEADS * GLA_DVP
MEM_QW = MEM_HEADS * MEM_DP
GLA_QK = GLA_HEADS * GLA_DK
GLA_VD = GLA_HEADS * GLA_DV
GLA_OFF_V = 2 * GLA_QK
GLA_OFF_G = GLA_OFF_V + GLA_VD
GLA_OFF_MQ = GLA_OFF_G + GLA_VD
GLA_OFF_LR = GLA_OFF_MQ + MEM_W


def _gla_in_body(h_ref, g_ref, w_ref, wg_ref, bg_ref, kv_ref, q_ref, k_ref, v_ref, go_ref, la_ref, mo_ref):
    xn = _rms(h_ref[...], g_ref[...]).astype(BF16)

    spread = _spread_heads

    def proj(lo, n):
        return _dot(xn, w_ref[:, lo:lo + n])

    qk = proj(0, 2 * GLA_QK)
    q_ref[...] = spread(qk, 0, GLA_HEADS, GLA_DK, GLA_DKP).astype(BF16)
    k_ref[...] = spread(qk, GLA_QK, GLA_HEADS, GLA_DK, GLA_DKP).astype(BF16)
    v_ref[...] = spread(proj(GLA_OFF_V, GLA_VD), 0, GLA_HEADS, GLA_DV, GLA_DVP).astype(BF16)
    go_ref[...] = spread(proj(GLA_OFF_G, GLA_VD), 0, GLA_HEADS, GLA_DV, GLA_DVP).astype(BF16)
    tail = proj(GLA_OFF_MQ, MEM_W + LANE)
    z = _dot(tail[:, MEM_W:].astype(BF16), wg_ref[...]) + bg_ref[...]
    la_ref[...] = (jnp.minimum(z, 0.0) - jnp.log(1.0 + jnp.exp(-jnp.abs(z)))) * (1.0 / GLA_GATE_NORM)
    mo_ref[...] = _mem_attention(spread(tail, 0, MEM_HEADS, MEM_HEAD_DIM, MEM_DP).astype(BF16), kv_ref)


def _mem_kv_spec(mem_kv, layer, t, tm, batch):
    nt = t // batch // tm
    return pl.BlockSpec((None, mem_kv.shape[1] // batch, mem_kv.shape[2]), lambda i: (layer, i // nt, 0))


def _gla_in_proj(h, layer, glayer, g, w, wg, bg, mem_kv, batch):
    t = h.shape[0]
    tm = ROW_TILE
    row = lambda n: pl.BlockSpec((tm, n), lambda i: (i, 0))
    return pl.pallas_call(
        _gla_in_body,
        grid=(t // tm,),
        in_specs=[row(D_MODEL), _layer_of(g, layer), _layer_of(w, glayer), _layer_of(wg, glayer),
                  _layer_of(bg, glayer), _mem_kv_spec(mem_kv, layer, t, tm, batch)],
        out_specs=[row(GLA_QW), row(GLA_QW), row(GLA_VW), row(GLA_VW), row(GLA_QW), row(MEM_W)],
        out_shape=[jax.ShapeDtypeStruct((t, GLA_QW), BF16), jax.ShapeDtypeStruct((t, GLA_QW), BF16),
                   jax.ShapeDtypeStruct((t, GLA_VW), BF16), jax.ShapeDtypeStruct((t, GLA_VW), BF16),
                   jax.ShapeDtypeStruct((t, GLA_QW), F32), jax.ShapeDtypeStruct((t, MEM_W), BF16)],
        compiler_params=_cparams("parallel"),
        name="gla_in_proj",
    )(h, g, w, wg, bg, mem_kv)


def _gla_mix_body(q_ref, k_ref, la_ref, v_ref, go_ref, on_ref, o_ref, st_ref, inter_ref, b_scr, k_scr, v_scr, *,
                  cn, batch):
    @pl.when(pl.program_id(0) == 0)
    def _():
        st_ref[...] = jnp.zeros_like(st_ref)

    row = lax.broadcasted_iota(jnp.int32, (cn, cn), 0)
    col = lax.broadcasted_iota(jnp.int32, (cn, cn), 1)
    causal = row >= col
    tril = jnp.where(causal, 1.0, 0.0).astype(BF16)
    scale = GLA_DK ** -0.5
    chains = [(bi, hd) for bi in range(batch) for hd in range(GLA_HEADS)]
    sk = lambda hd: slice(hd * GLA_DKP, (hd + 1) * GLA_DKP)
    sv = lambda hd: slice(hd * GLA_DVP, (hd + 1) * GLA_DVP)
    dg = lambda x, y, dims: lax.dot_general(x, y, dims, preferred_element_type=F32)

    def cum_decay(bi, hd):
        la = la_ref[bi, :, sk(hd)]
        la1 = la.astype(BF16)
        r1 = la - la1.astype(F32)
        la2 = r1.astype(BF16)
        la3 = (r1 - la2.astype(F32)).astype(BF16)
        return _dot(tril, la1) + _dot(tril, la2) + _dot(tril, la3)

    def finish(i, o):
        bi, hd = chains[i]
        ms = jnp.sum(o * o, axis=-1, keepdims=True) * (1.0 / GLA_DV)
        y = o * lax.rsqrt(ms + EPS) * on_ref[...]
        g = go_ref[bi, :, sv(hd)].astype(F32)
        o_ref[bi, :, sv(hd)] = (y * (g * _sigmoid(g))).astype(BF16)

    bs = [cum_decay(bi, hd) for bi, hd in chains]
    scaled = []
    for (bi, hd), b in zip(chains, bs):
        b_last = b[cn - 1:cn, :]
        q = q_ref[bi, :, sk(hd)].astype(F32)
        k = k_ref[bi, :, sk(hd)].astype(F32)
        scaled.append(((q * jnp.exp(b) * scale).astype(BF16),
                       (k * jnp.exp(jnp.minimum(-b, GLA_SAFE_DECAY))).astype(BF16),
                       (k * jnp.exp(b_last - b)).astype(BF16), jnp.exp(b_last)))
    intra = [jnp.where(causal, dg(qs, ks, NT), 0.0).astype(BF16) for qs, ks, _, _ in scaled]
    outs = []
    for i, (bi, hd) in enumerate(chains):
        qs, _, ko, a_last = scaled[i]
        v = v_ref[bi, :, sv(hd)]
        st = st_ref[i]
        inter = dg(qs, st.astype(BF16), NT)
        inter_ref[i] = inter
        outs.append(_dot(intra[i], v) + inter)
        st_ref[i] = st * a_last + dg(v, ko, TN)
    for i, o in enumerate(outs):
        finish(i, o)

    b_min = bs[0][cn - 1:cn, :]
    for b in bs[1:]:
        b_min = jnp.minimum(b_min, b[cn - 1:cn, :])

    @pl.when(jnp.min(b_min) < -GLA_SAFE_DECAY)
    def _():
        rows = lax.broadcasted_iota(jnp.int32, (cn, 1), 0)
        for i, (bi, hd) in enumerate(chains):
            b = cum_decay(bi, hd)
            qf = q_ref[bi, :, sk(hd)].astype(F32) * scale
            b_scr[...] = b
            k_scr[...] = k_ref[bi, :, sk(hd)].astype(F32)
            v_scr[...] = v_ref[bi, :, sv(hd)].astype(F32)

            def add_key(j, acc):
                decay = jnp.exp(jnp.minimum(b - b_scr[pl.ds(j, 1), :], 0.0))
                a_col = jnp.sum(qf * k_scr[pl.ds(j, 1), :] * decay, axis=-1, keepdims=True)
                return acc + jnp.where(rows >= j, a_col, 0.0) * v_scr[pl.ds(j, 1), :]
            finish(i, lax.fori_loop(0, cn, add_key, inter_ref[i]))


def _gla_mixer(q, k, la, v, go, on, glayer, batch):
    t = q.shape[0]
    s = t // batch
    cn = GLA_CHUNK
    seq = lambda a: a.reshape(batch, s, a.shape[1])
    blk = lambda n: pl.BlockSpec((batch, cn, n), lambda c: (0, c, 0))
    out = pl.pallas_call(
        functools.partial(_gla_mix_body, cn=cn, batch=batch),
        grid=(s // cn,),
        in_specs=[blk(GLA_QW), blk(GLA_QW), blk(GLA_QW), blk(GLA_VW), blk(GLA_VW),
                  _layer_of(on, glayer)],
        out_specs=blk(GLA_VW),
        out_shape=jax.ShapeDtypeStruct((batch, s, GLA_VW), BF16),
        scratch_shapes=[pltpu.VMEM((batch * GLA_HEADS, GLA_DVP, GLA_DKP), F32),
                        pltpu.VMEM((batch * GLA_HEADS, cn, GLA_DVP), F32), pltpu.VMEM((cn, GLA_DKP), F32),
                        pltpu.VMEM((cn, GLA_DKP), F32), pltpu.VMEM((cn, GLA_DVP), F32)],
        compiler_params=_cparams("arbitrary"),
        name="gla_mixer",
    )(seq(q), seq(k), seq(la), seq(v), seq(go), on)
    return out.reshape(t, GLA_VW)


def _mem_kv_body(m_ref, g_ref, w_ref, o_ref):
    xn = _rms(m_ref[...], g_ref[...]).astype(BF16)
    o_ref[...] = _dot(xn, w_ref[...]).astype(BF16)


def _mem_kv_proj(mem2, g, w, batch):
    nl = w.shape[0]
    m = mem2.shape[0] // batch
    n = w.shape[2]
    return pl.pallas_call(
        _mem_kv_body,
        grid=(nl, batch),
        in_specs=[pl.BlockSpec((m, D_MODEL), lambda l, b: (b, 0)),
                  pl.BlockSpec((None, 1, D_MODEL), lambda l, b: (l, 0, 0)),
                  pl.BlockSpec((None, D_MODEL, n), lambda l, b: (l, 0, 0))],
        out_specs=pl.BlockSpec((None, m, n), lambda l, b: (l, b, 0)),
        out_shape=jax.ShapeDtypeStruct((nl, mem2.shape[0], n), BF16),
        compiler_params=_cparams("arbitrary", "arbitrary"),
        name="mem_kv_proj",
    )(mem2, g, w)


BF16_ROWS = 2 * SUBLANE


def _mix_ffn_body(h_ref, hp_ref, a_ref, ap_ref, m_ref, mp_ref, wa_ref, wm_ref, g_ref, wup_ref, cw_ref, cb_ref,
                  wdn_ref, gf_ref, o_ref, act_ref, *, tm, tf, last):
    g = g_ref[...]
    nr = tm // SUBLANE
    interleave = lambda x: x.reshape(SUBLANE, nr, x.shape[-1]).swapaxes(0, 1).reshape(tm, x.shape[-1])
    deinterleave = lambda x: x.reshape(nr, SUBLANE, x.shape[-1]).swapaxes(0, 1).reshape(tm, x.shape[-1])
    h = interleave(h_ref[...] + _dot(a_ref[...], wa_ref[...]) + _dot(m_ref[...], wm_ref[...]))
    x = _rms(h, g).astype(BF16)
    keep = jnp.where(pl.program_id(1) > 0, 1.0, 0.0)
    h_prev = (hp_ref[...] + _dot(ap_ref[...], wa_ref[...]) + _dot(mp_ref[...], wm_ref[...]))[BF16_ROWS - SUBLANE:]
    x_prev = (_rms(h_prev, g) * keep).astype(BF16)
    first = lax.broadcasted_iota(jnp.int32, (SUBLANE, tf), 0) == 0
    nchunk = FFN_DIM // tf

    def up(j):
        cols = [pl.ds(off + j * tf, tf) for off in (0, FFN_DIM)]
        return tuple((_dot(x, wup_ref[:, c]), _dot(x_prev, wup_ref[:, c])) for c in cols)

    def conv(u, u_prev, off):
        w = cw_ref[:, off:off + tf]
        wrap = lambda r, k: jnp.where(first, u_prev[SUBLANE - k:SUBLANE - k + 1, :],
                                      pltpu.roll(u[r * SUBLANE:(r + 1) * SUBLANE, :], 1, 0))
        back1 = jnp.concatenate([wrap(nr - 1, 1), u[0:tm - SUBLANE, :]], axis=0)
        back2 = jnp.concatenate([wrap(nr - 2, 2), wrap(nr - 1, 1), u[0:tm - 2 * SUBLANE, :]], axis=0)
        return cb_ref[:, off:off + tf] + w[0:1, :] * back2 + w[1:2, :] * back1 + w[2:3, :] * u

    u_next = up(0)
    for j in range(nchunk):
        (ua, ua_prev), (ub, ub_prev) = u_next
        if j + 1 < nchunk:
            u_next = up(j + 1)
        a = conv(ua, ua_prev, j * tf)
        b = conv(ub, ub_prev, FFN_DIM + j * tf)
        act_ref[:, j * tf:(j + 1) * tf] = (a * _sigmoid(a) * b).astype(BF16)
    out = h + _dot(act_ref[...], wdn_ref[...])
    o_ref[...] = deinterleave(_rms(out, gf_ref[...]) if last else out)


def _mix_ffn(h, main, mo, wa, klayer, wm, layer, g, wup, cw, cb, wdn, gf, last, batch):
    t = h.shape[0]
    tm = FFN_ROWS
    nt = t // batch // tm
    hb = tm // BF16_ROWS
    cur = lambda n: pl.BlockSpec((tm, n), lambda b, i: (b * nt + i, 0))
    prev = lambda n: pl.BlockSpec((BF16_ROWS, n), lambda b, i: (jnp.maximum((b * nt + i) * hb - 1, 0), 0))
    return pl.pallas_call(
        functools.partial(_mix_ffn_body, tm=tm, tf=FFN_TILE, last=last),
        grid=(batch, nt),
        in_specs=[cur(D_MODEL), prev(D_MODEL), cur(main.shape[1]), prev(main.shape[1]),
                  cur(mo.shape[1]), prev(mo.shape[1]), _layer_of(wa, klayer), _layer_of(wm, layer), _layer_of(g, layer),
                  _layer_of(wup, layer), _layer_of(cw, layer), _layer_of(cb, layer), _layer_of(wdn, layer),
                  _resident(gf)],
        out_specs=cur(D_MODEL),
        out_shape=jax.ShapeDtypeStruct((t, D_MODEL), F32),
        scratch_shapes=[pltpu.VMEM((tm, FFN_DIM), BF16)],
        compiler_params=_cparams("parallel", "parallel"),
        name="mix_ffn",
    )(h, h, main, main, mo, mo, wa, wm, g, wup, cw, cb, wdn, gf)


KV_NAT = NSA_GROUPS * NSA_HEAD_DIM


def _kv_proj_body(h_ref, g_ref, w_ref, cx_ref, ks_ref, kw_ref, vs_ref, vw_ref, *, tm):
    xn = _rms(h_ref[...], g_ref[...]).astype(BF16)
    for j in range(2):
        y = _dot(xn, w_ref[:, j * KV_NAT:(j + 1) * KV_NAT])
        cx_ref[j] = y.reshape(tm // CMP_STRIDE, CMP_STRIDE, KV_NAT).swapaxes(0, 1).astype(BF16)
    key = pl.program_id(1) * tm + lax.broadcasted_iota(jnp.int32, (tm, LANE), 0)
    lane = lax.broadcasted_iota(jnp.int32, (tm, LANE), 1)
    onehot = jnp.where(lax.shift_right_logical(key, int(math.log2(SEL_BLOCK))) == lane - NSA_HEAD_DIM, 1.0, 0.0)
    low = lane < NSA_HEAD_DIM
    ones = jnp.ones((V_ROWS - NSA_HEAD_DIM, tm), BF16)
    slots = lambda n: _dot(xn, w_ref[:, 2 * KV_NAT + n * MXU_N:2 * KV_NAT + (n + 1) * MXU_N])
    for gp in range(NSA_GROUPS // 2):
        kk = slots(gp)
        vv = slots(NSA_GROUPS // 2 + gp)
        for i in range(2):
            kslot = kk[:, i * LANE:(i + 1) * LANE]
            ks_ref[2 * gp + i] = jnp.where(low, kslot, onehot).astype(BF16)
            kw_ref[2 * gp + i] = jnp.where(low, pltpu.roll(kslot, NSA_HEAD_DIM, 1), 0.0).astype(BF16)
            vt = vv[:, i * LANE:(i + 1) * LANE].T.astype(BF16)
            vs_ref[2 * gp + i] = jnp.concatenate([vt[0:NSA_HEAD_DIM], ones], axis=0)
            vw_ref[2 * gp + i] = jnp.concatenate([vt[NSA_HEAD_DIM:], ones], axis=0)


def _kv_proj(h, g, w, batch):
    t = h.shape[0]
    s = t // batch
    tm = ROW_TILE
    nt = s // tm
    gr = NSA_GROUPS
    return pl.pallas_call(
        functools.partial(_kv_proj_body, tm=tm),
        grid=(batch, nt),
        in_specs=[pl.BlockSpec((tm, D_MODEL), lambda b, i: (b * nt + i, 0)),
                  pl.BlockSpec(g.shape, lambda b, i: (0, 0)),
                  pl.BlockSpec(w.shape, lambda b, i: (0, 0))],
        out_specs=[pl.BlockSpec((2, CMP_STRIDE, tm // CMP_STRIDE, KV_NAT), lambda b, i: (0, 0, b * nt + i, 0)),
                   pl.BlockSpec((None, gr, tm, LANE), lambda b, i: (b, 0, i, 0)),
                   pl.BlockSpec((None, gr, tm, LANE), lambda b, i: (b, 0, i, 0)),
                   pl.BlockSpec((None, gr, V_ROWS, tm), lambda b, i: (b, 0, 0, i)),
                   pl.BlockSpec((None, gr, V_ROWS, tm), lambda b, i: (b, 0, 0, i))],
        out_shape=[jax.ShapeDtypeStruct((2, CMP_STRIDE, t // CMP_STRIDE, KV_NAT), BF16),
                   jax.ShapeDtypeStruct((batch, gr, s, LANE), BF16),
                   jax.ShapeDtypeStruct((batch, gr, s, LANE), BF16),
                   jax.ShapeDtypeStruct((batch, gr, V_ROWS, s), BF16),
                   jax.ShapeDtypeStruct((batch, gr, V_ROWS, s), BF16)],
        compiler_params=_cparams("parallel", "parallel"),
        name="nsa_kv_proj",
    )(h, g, w)


def _compress_body(x_ref, wt_ref, wb_ref, pos_ref, w1_ref, b1_ref, w2_ref, b2_ref, on_ref, ot_ref, *, ncp):
    top = _dot(x_ref[0], wt_ref[0])
    bot = _dot(x_ref[0], wb_ref[0])
    for l in range(1, CMP_STRIDE):
        top = top + _dot(x_ref[l], wt_ref[l])
        bot = bot + _dot(x_ref[l], wb_ref[l])
    posb = _dot(pos_ref[...], w1_ref[...])[0:1, :] + b1_ref[...]
    for gi in range(NSA_GROUPS):
        sl = slice(gi * CMP_HIDDEN, (gi + 1) * CMP_HIDDEN)
        hid = top[:, sl] + pltpu.roll(bot[:, sl], ncp - 1, 0) + posb
        hid = (hid * _sigmoid(hid)).astype(BF16)
        out = _dot(hid, w2_ref[...]) + b2_ref[...]
        on_ref[gi] = out.astype(BF16)
        ot_ref[gi] = out.T.astype(BF16)


def _compress(x16, wt, wb, pos, w1, b1, w2, b2, batch):
    ncp = x16.shape[2] // batch
    gr = NSA_GROUPS
    per_j = lambda a: pl.BlockSpec((None,) + a.shape[1:], lambda j, b: (j,) + (0,) * (a.ndim - 1))
    return pl.pallas_call(
        functools.partial(_compress_body, ncp=ncp),
        grid=(2, batch),
        in_specs=[pl.BlockSpec((None, CMP_STRIDE, ncp, KV_NAT), lambda j, b: (j, 0, b, 0)),
                  per_j(wt), per_j(wb), per_j(pos), per_j(w1), per_j(b1), per_j(w2), per_j(b2)],
        out_specs=[pl.BlockSpec((None, None, gr, ncp, LANE), lambda j, b: (j, b, 0, 0, 0)),
                   pl.BlockSpec((None, None, gr, LANE, ncp), lambda j, b: (j, b, 0, 0, 0))],
        out_shape=[jax.ShapeDtypeStruct((2, batch, gr, ncp, LANE), BF16),
                   jax.ShapeDtypeStruct((2, batch, gr, LANE, ncp), BF16)],
        compiler_params=_cparams("arbitrary", "arbitrary"),
        name="nsa_compress",
    )(x16, wt, wb, pos, w1, b1, w2, b2)


NSA_QW = NSA_HEADS * NSA_HEAD_DIM
G3 = NSA_REP * TQ


def _nsa_in_body(h_ref, g_ref, w_ref, kv_ref, qt_ref, gt_ref, mo_ref):
    xn = _rms(h_ref[...], g_ref[...]).astype(BF16)
    for j in range(NSA_QW // MXU_N):
        y = _dot(xn, w_ref[:, j * MXU_N:(j + 1) * MXU_N]) * (NSA_HEAD_DIM ** -0.5 * LOG2E)
        qt_ref[j * MXU_N:(j + 1) * MXU_N, :] = y.T.astype(BF16)
    tail = _dot(xn, w_ref[:, NSA_QW:])
    gt_ref[...] = _sigmoid(tail[:, 0:LANE]).T
    mo_ref[...] = _mem_attention(_spread_heads(tail, LANE, MEM_HEADS, MEM_HEAD_DIM, MEM_DP).astype(BF16), kv_ref)


def _nsa_in_proj(h, layer, nlayer, g, w, mem_kv, batch):
    t = h.shape[0]
    tm = ROW_TILE
    row = lambda n: pl.BlockSpec((tm, n), lambda i: (i, 0))
    nt = t // batch // tm
    col = lambda n: pl.BlockSpec((None, n, tm), lambda i: (i // nt, 0, i % nt))
    return pl.pallas_call(
        _nsa_in_body,
        grid=(t // tm,),
        in_specs=[row(D_MODEL), _layer_of(g, layer), _layer_of(w, nlayer), _mem_kv_spec(mem_kv, layer, t, tm, batch)],
        out_specs=[col(NSA_QW), col(LANE), row(MEM_W)],
        out_shape=[jax.ShapeDtypeStruct((batch, NSA_QW, t // batch), BF16),
                   jax.ShapeDtypeStruct((batch, LANE, t // batch), F32),
                   jax.ShapeDtypeStruct((t, MEM_W), BF16)],
        compiler_params=_cparams("parallel"),
        name="nsa_in_proj",
    )(h, g, w, mem_kv)


def _nsa_attn_body(qt_ref, gt_ref, kc_ref, vct_ref, ks_ref, kw_ref, vs_ref, vw_ref, ovt_ref, tz_ref, cb_ref,
                   o_ref, sc_ref, qa_ref, acc_ref, ot_ref, *s_slots, nsb, ncp, n_sel, nseq):
    n = pl.program_id(1)
    t0 = n * TQ
    dh = NSA_HEAD_DIM
    cstart = pl.multiple_of(ncp - n * (TQ // CMP_STRIDE), SUBLANE)
    sees_any = t0 + (lax.broadcasted_iota(jnp.int32, (1, G3), 1) & (TQ - 1)) >= CMP_BLOCK - 1
    jj = lax.broadcasted_iota(jnp.int32, (nsb, TQ), 0)
    cur = lax.shift_right_logical(t0 + lax.broadcasted_iota(jnp.int32, (nsb, TQ), 1), int(math.log2(SEL_BLOCK)))
    forced = (jj == 0) | (jj == cur) | (jj == cur - 1)
    zpad = jnp.zeros((dh, G3), BF16)
    nwt = WINDOW // TQ
    units = [(bi, gi) for bi in range(nseq) for gi in range(NSA_GROUPS)]

    def scores(k_ref, u, first, count, tz_index, qa):
        bi, gi = units[u]
        koff = pl.multiple_of(first * TQ, TQ)
        s = _dot(k_ref[bi, gi, pl.ds(koff, count * TQ), :], qa)
        return [s[i * TQ:(i + 1) * TQ] + tz_ref[gi, tz_index(n - (first + i))] for i in range(count)]

    def col_max(parts):
        mx = parts[0]
        for x in parts[1:]:
            mx = jnp.maximum(mx, x)
        return jnp.max(mx, axis=0, keepdims=True)

    def probs(parts, m):
        return jnp.concatenate([jnp.exp2(x - m).astype(BF16) for x in parts], axis=0)

    def values(v_ref, u, first, count, p):
        bi, gi = units[u]
        koff = pl.multiple_of(first * TQ, TQ)
        return _dot(v_ref[bi, gi, :, pl.ds(koff, count * TQ)], p)

    sel_index = lambda d: jnp.where(d < 0, 2, jnp.minimum(d, 2))
    win_index = lambda d: jnp.where(d < 0, 4, jnp.where(d == nwt, 3, jnp.minimum(d, 2)))

    groups = range(len(units))
    gate = lambda bi, h, c: gt_ref[bi, pl.ds(h * NSA_BRANCHES + c, 1), :]
    head_lanes = lambda r: slice(r * TQ, (r + 1) * TQ)
    head_rows = lambda bi, h: (bi, slice(h * dh, (h + 1) * dh))

    q3s = [jnp.concatenate([qt_ref[head_rows(bi, gi * NSA_REP + r)] for r in range(NSA_REP)], axis=1)
           for bi, gi in units]
    cs = [_dot(kc_ref[bi, gi], jnp.concatenate([q3s[u], zpad], axis=0)) + cb_ref[gi, pl.ds(cstart, ncp), :]
          for u, (bi, gi) in enumerate(units)]
    cps = [jnp.exp2(s - jnp.max(s, axis=0, keepdims=True)) for s in cs]
    cps = [p * jnp.where(sees_any, 1.0 / jnp.sum(p, axis=0, keepdims=True), 0.0) for p in cps]
    ocs = [_dot(vct_ref[bi, gi], cps[u].astype(BF16)) for u, (bi, gi) in enumerate(units)]
    scs = []
    for gi in groups:
        psum = cps[gi][:, 0:TQ]
        for r in range(1, NSA_REP):
            psum = psum + cps[gi][:, head_lanes(r)]
        p1 = psum.astype(BF16)
        p2 = (psum - p1.astype(F32)).astype(BF16)
        imp = _dot(ovt_ref[...], p1) + _dot(ovt_ref[...], p2)
        score = jnp.where(forced, 1e4, jnp.where(jj <= cur, imp[0:nsb], -1.0))
        scs.append(jnp.where(score < 0.0, -1, lax.bitcast_convert_type(score, jnp.int32)))
    for u, (bi, gi) in enumerate(units):
        sc_ref[u] = scs[u]
        for r in range(NSA_REP):
            h = gi * NSA_REP + r
            ot_ref[head_rows(bi, h)] = gate(bi, h, 0) * ocs[u][0:dh, head_lanes(r)]

    scs1 = [k + 1 for k in scs]

    def rank_step(i4, cnts):
        cnts = list(cnts)
        for u in range(RANK_UNROLL):
            i = i4 * RANK_UNROLL + u
            lower = i < jj
            for gi in groups:
                rowk = sc_ref[gi, pl.ds(i, 1), :]
                before = rowk >= jnp.where(lower, scs[gi], scs1[gi])
                cnts[gi] = cnts[gi] + jnp.where(before, 1, 0)
        return tuple(cnts)
    rank_trips = jnp.minimum((2 * n + 2 + RANK_UNROLL - 1) // RANK_UNROLL, nsb // RANK_UNROLL)
    cnts = lax.fori_loop(0, rank_trips, rank_step, tuple(jnp.zeros((nsb, TQ), jnp.int32) for _ in groups))
    for gi in groups:
        selneg = jnp.where((cnts[gi] < n_sel) & (jj <= cur), 0.0, NEG).astype(BF16)
        if nsb < SEL_BLOCK:
            selneg = jnp.concatenate([selneg, jnp.zeros((SEL_BLOCK - nsb, TQ), BF16)], axis=0)
        qa_ref[gi] = jnp.concatenate([q3s[gi], jnp.concatenate([selneg] * NSA_REP, axis=1)], axis=0)

    acc_ref[...] = jnp.zeros_like(acc_ref)
    half = len(units) // 2
    s_refs = (s_slots[:half], s_slots[half:])

    def score_half(it, hb):
        plist = [scores(ks_ref, hb * half + u, it * SEL_TILES, SEL_TILES, sel_index, qa_ref[hb * half + u])
                 for u in range(half)]
        for u, parts in enumerate(plist):
            for i, x in enumerate(parts):
                s_refs[hb][u][i * TQ:(i + 1) * TQ, :] = x
        return [col_max(parts) for parts in plist]

    def finish_half(it, hb, ms, bms):
        m2s = [jnp.maximum(ms[u], bms[u]) for u in range(half)]
        ps = [probs([s_refs[hb][u][i * TQ:(i + 1) * TQ, :] for i in range(SEL_TILES)], m2s[u]) for u in range(half)]
        vals = [values(vs_ref, hb * half + u, it * SEL_TILES, SEL_TILES, ps[u]) for u in range(half)]
        for u in range(half):
            gi = hb * half + u
            acc_ref[gi] = jnp.exp2(ms[u] - m2s[u]) * acc_ref[gi] + vals[u]
        return m2s

    def sel_trip(it, carry, score_next):
        ms0, ms1, bm0 = carry
        bm1 = score_half(it, 1)
        ms0 = finish_half(it, 0, ms0, bm0)
        if score_next:
            bm0 = score_half(it + 1, 0)
        ms1 = finish_half(it, 1, ms1, bm1)
        return ms0, ms1, bm0

    neg = [jnp.full((1, G3), NEG, F32) for _ in range(half)]
    carry = (neg, neg, score_half(0, 0))

    wfirst = jnp.maximum(n - nwt, 0)
    wparts = [scores(kw_ref, gi, wfirst, nwt + 1, win_index, qa_ref[gi]) for gi in groups]
    wps = [probs(parts, col_max(parts)) for parts in wparts]
    ows = [values(vw_ref, gi, wfirst, nwt + 1, wps[gi]) for gi in groups]
    for u, (bi, gi) in enumerate(units):
        o_w = ows[u][0:dh, :] * (1.0 / ows[u][dh:dh + 1, :])
        for r in range(NSA_REP):
            h = gi * NSA_REP + r
            ot_ref[head_rows(bi, h)] += gate(bi, h, 2) * o_w[:, head_lanes(r)]

    carry = lax.fori_loop(0, n // SEL_TILES, lambda it, c: sel_trip(it, c, True), carry)
    sel_trip(n // SEL_TILES, carry, False)

    for u, (bi, gi) in enumerate(units):
        o_s = acc_ref[u, 0:dh, :] * (1.0 / acc_ref[u, dh:dh + 1, :])
        for r in range(NSA_REP):
            h = gi * NSA_REP + r
            ot_ref[head_rows(bi, h)] += gate(bi, h, 1) * o_s[:, head_lanes(r)]

    for bi in range(nseq):
        for j in range(NSA_QW // LANE):
            o_ref[bi, :, j * LANE:(j + 1) * LANE] = ot_ref[bi, j * LANE:(j + 1) * LANE, :].T.astype(BF16)


def _nsa_attn(qt, gt, kc, vct, ksel, kwin, vsel, vwin, ovt, tz, cb):
    batch, _, s = qt.shape
    nseq = NSA_SEQS if batch % NSA_SEQS == 0 else 1
    nt = s // TQ
    nsb = s // SEL_BLOCK
    ncp = kc.shape[-2]
    nu = nseq * NSA_GROUPS
    per_b = lambda a: pl.BlockSpec((nseq,) + a.shape[1:], lambda b, i: (b,) + (0,) * (a.ndim - 1),
                                   pipeline_mode=pl.Buffered(1))
    out = pl.pallas_call(
        functools.partial(_nsa_attn_body, nsb=nsb, ncp=ncp, n_sel=min(SEL_TOPK, nsb), nseq=nseq),
        grid=(batch // nseq, nt),
        in_specs=[pl.BlockSpec((nseq, NSA_QW, TQ), lambda b, i: (b, 0, i)),
                  pl.BlockSpec((nseq, LANE, TQ), lambda b, i: (b, 0, i)),
                  per_b(kc), per_b(vct), per_b(ksel), per_b(kwin), per_b(vsel), per_b(vwin),
                  _resident(ovt), _resident(tz), _resident(cb)],
        out_specs=pl.BlockSpec((nseq, TQ, NSA_QW), lambda b, i: (b, i, 0)),
        out_shape=jax.ShapeDtypeStruct((batch, s, NSA_QW), BF16),
        scratch_shapes=[pltpu.VMEM((nu, nsb, TQ), jnp.int32), pltpu.VMEM((nu, LANE, G3), BF16),
                        pltpu.VMEM((nu, V_ROWS, G3), F32), pltpu.VMEM((nseq, NSA_QW, TQ), F32),
                        ] + [pltpu.VMEM((SEL_TILES * TQ, G3), F32)] * nu,
        compiler_params=_cparams("parallel", "arbitrary"),
        name="nsa_attn",
    )(qt, gt, kc, vct, ksel, kwin, vsel, vwin, ovt, tz, cb)
    return out.reshape(batch * s, NSA_QW)


def _rel_bucket_np(dist):
    dist = np.maximum(dist, 0)
    max_exact = REL_BUCKETS // 2
    ratio = np.log(np.maximum(dist, 1).astype(np.float32) / np.float32(max_exact)) / np.float32(
        math.log(REL_MAX_DIST / max_exact))
    large = max_exact + (ratio * np.float32(REL_BUCKETS - max_exact)).astype(np.int32)
    large = np.minimum(large, REL_BUCKETS - 1)
    return np.where(dist < max_exact, dist, large).astype(np.int32)


def _group_lanes(a):
    hh, r, c = a.shape
    return a.reshape(NSA_GROUPS, NSA_REP, r, c).transpose(0, 2, 1, 3).reshape(NSA_GROUPS, r, NSA_REP * c)


def _bias_tables(rel_bias, ncp):
    k = np.arange(TQ)[:, None]
    q = np.arange(TQ)[None, :]
    tbl = rel_bias.astype(F32)

    def lookup(idx):
        onehot = (jnp.asarray(idx.reshape(1, -1)) == jnp.arange(REL_BUCKETS)[:, None]).astype(F32)
        out = jnp.dot(tbl.T, onehot, precision=lax.Precision.HIGHEST)
        return out.reshape((NSA_HEADS,) + idx.shape)

    far = jnp.broadcast_to(tbl[REL_BUCKETS - 1][:, None, None], (NSA_HEADS, TQ, TQ))
    t0 = jnp.where(jnp.asarray(k <= q)[None], lookup(_rel_bucket_np(q - k)), NEG)
    t1 = lookup(_rel_bucket_np(TQ + q - k))
    t3 = jnp.where(jnp.asarray(k > q)[None], far, NEG)
    tz = jnp.stack([_group_lanes(x) for x in (t0, t1, far, t3, jnp.full_like(far, NEG))], axis=1)
    m = ncp - np.arange(2 * ncp)[:, None]
    d = CMP_STRIDE * m + q - (CMP_BLOCK - 1)
    idx = np.where((d >= 0) & (d < REL_MAX_DIST), _rel_bucket_np(d), REL_BUCKETS - 1)
    cb = _group_lanes(jnp.where(jnp.asarray(d >= 0)[None], lookup(idx), NEG))
    return tz * LOG2E, cb * LOG2E


def _overlap_table(s, ncp):
    nsb = s // SEL_BLOCK
    nc = (s - CMP_BLOCK) // CMP_STRIDE + 1
    cs = np.arange(ncp) * CMP_STRIDE
    ce = cs + CMP_BLOCK - 1
    ss = np.arange(SEL_BLOCK) * SEL_BLOCK
    ov = (cs[None, :] < ss[:, None] + SEL_BLOCK) & (ce[None, :] >= ss[:, None])
    ov &= (np.arange(ncp) < nc)[None, :] & (np.arange(SEL_BLOCK) < nsb)[:, None]
    return jnp.asarray(ov, dtype=BF16)


def kernel(x, mem, norm_mix, norm_mem, w_mem_kv, w_out, norm_ffn, w_up, conv_w, conv_b, w_down,
           gla_w_in, gla_w_gate_up, gla_b_gate, gla_out_norm, nsa_w_in, kv_norm, w_kv_shared,
           cmp_pos, cmp_w1, cmp_b1, cmp_w2, cmp_b2, rel_bias, final_norm):
    batch, seq = x.shape[0], x.shape[1]
    t = batch * seq
    h = x.reshape(t, D_MODEL)
    row = lambda v: v.reshape(1, -1).astype(F32)

    wk, wv = w_mem_kv[..., :MEM_W], w_mem_kv[..., MEM_W:]
    w_mkv = jnp.concatenate([_pad_heads(wk, MEM_HEADS, MEM_HEAD_DIM, MEM_DP),
                             _pad_heads(wv, MEM_HEADS, MEM_HEAD_DIM, MEM_DP)], axis=-1).astype(BF16)
    mem_kv_all = _mem_kv_proj(mem.reshape(-1, D_MODEL), norm_mem.reshape(DEPTH, 1, D_MODEL), w_mkv, batch)

    ffn_params = (norm_ffn.reshape(DEPTH, 1, D_MODEL).astype(F32), w_up.astype(BF16), conv_w.astype(F32),
                  conv_b.reshape(DEPTH, 1, 2 * FFN_DIM).astype(F32), w_down.astype(BF16))
    norm_mix3 = norm_mix.reshape(DEPTH, 1, D_MODEL).astype(F32)
    w_o_mem = w_out[:, MAIN_W:].astype(BF16)

    c0 = GLA_HEADS * GLA_DK
    c1 = 2 * c0
    c2 = c1 + GLA_HEADS * GLA_DV
    c3 = c2 + GLA_HEADS * GLA_DV
    c4 = c3 + GLA_RANK
    gla_w = jnp.concatenate([gla_w_in[..., :c3], gla_w_in[..., c4:],
                             _pad_heads(gla_w_in[..., c3:c4], 1, GLA_RANK, LANE)], axis=-1).astype(BF16)
    gla_wg = jnp.pad(_pad_heads(gla_w_gate_up, GLA_HEADS, GLA_DK, GLA_DKP),
                     ((0, 0), (0, LANE - GLA_RANK), (0, 0))).astype(BF16)
    gla_bg = _pad_heads(gla_b_gate, GLA_HEADS, GLA_DK, GLA_DKP).reshape(N_A_LAYERS, 1, GLA_QW).astype(F32)
    gla_on = jnp.pad(gla_out_norm, ((0, 0), (0, GLA_DVP - GLA_DV))).reshape(N_A_LAYERS, 1, GLA_DVP).astype(F32)
    gla_wo = _pad_head_rows(w_out[:N_A_LAYERS, :MAIN_W], GLA_HEADS, GLA_DV, GLA_DVP).astype(BF16)

    n0 = NSA_HEADS * NSA_HEAD_DIM
    n1 = n0 + NSA_HEADS * NSA_BRANCHES
    nsa_w = jnp.concatenate([nsa_w_in[..., :n0], _pad_heads(nsa_w_in[..., n0:n1], 1, NSA_HEADS * NSA_BRANCHES, LANE),
                             nsa_w_in[..., n1:]], axis=-1).astype(BF16)
    nsa_wo = w_out[N_A_LAYERS:, :MAIN_W].astype(BF16)

    shared = None
    for i in range(DEPTH):
        if i < N_A_LAYERS:
            q, k, v, go, la, mo = _gla_in_proj(h, i, i, norm_mix3, gla_w, gla_wg, gla_bg, mem_kv_all, batch)
            main = _gla_mixer(q, k, la, v, go, gla_on, i, batch)
            w_o_main, klayer = gla_wo, i
        else:
            if shared is None:
                ncp = seq // CMP_STRIDE
                gr, dh = NSA_GROUPS, NSA_HEAD_DIM
                wkv = w_kv_shared.reshape(D_MODEL, 6, gr, dh)
                pair = lambda a, b: jnp.concatenate([wkv[:, a], wkv[:, b]], axis=-1).reshape(D_MODEL, gr * 2 * dh)
                w_kv = jnp.concatenate([wkv[:, 0].reshape(D_MODEL, KV_NAT), wkv[:, 1].reshape(D_MODEL, KV_NAT),
                                        pair(2, 4), pair(3, 5)], axis=1).astype(BF16)
                x16, ksel, kwin, vsel, vwin = _kv_proj(h, row(kv_norm), w_kv, batch)
                w1 = cmp_w1.reshape(2, 2, CMP_STRIDE, dh, CMP_HIDDEN)
                eye = jnp.eye(gr, dtype=F32)
                w1x = jnp.einsum('jhldc,gk->jhlgdkc', w1, eye).reshape(2, 2, CMP_STRIDE, KV_NAT, gr * CMP_HIDDEN)
                w1x = w1x.astype(BF16)
                pos8 = jnp.broadcast_to(cmp_pos.reshape(2, 1, CMP_BLOCK * dh), (2, SUBLANE, CMP_BLOCK * dh)).astype(BF16)
                w2p = jnp.pad(cmp_w2, ((0, 0), (0, 0), (0, LANE - dh))).astype(BF16)
                b2p = jnp.pad(cmp_b2, ((0, 0), (0, LANE - dh))).reshape(2, 1, LANE).astype(F32)
                cnat, ctr = _compress(x16, w1x[:, 0], w1x[:, 1], pos8, cmp_w1.astype(BF16),
                                      cmp_b1.reshape(2, 1, CMP_HIDDEN).astype(F32), w2p, b2p, batch)
                tz, cb = _bias_tables(rel_bias, ncp)
                ov = _overlap_table(seq, ncp)
                shared = (cnat[0], ctr[1], ksel, kwin, vsel, vwin, ov, tz, cb)
            q, gates, mo = _nsa_in_proj(h, i, i - N_A_LAYERS, norm_mix3, nsa_w, mem_kv_all, batch)
            main = _nsa_attn(q, gates, *shared)
            w_o_main, klayer = nsa_wo, i - N_A_LAYERS
        h = _mix_ffn(h, main, mo, w_o_main, klayer, w_o_mem, i, *ffn_params, row(final_norm), i == DEPTH - 1, batch)
    return h.reshape(batch, seq, D_MODEL)
```
